```python
import functools
import jax, jax.numpy as jnp
from jax import lax
import numpy as np

D_MODEL = 2048
BATCH = 8
SEQ = 2048
DEPTH = 2

GRID_W = 64
CTX_LEN = 256
HEAD_DIM = 128
N_MIX_HEADS = D_MODEL // HEAD_DIM
ATT_Q_HEADS = N_MIX_HEADS // 2
ATT_KV_HEADS = ATT_Q_HEADS // 4
RET_HEADS = N_MIX_HEADS // 4
RET_DIM = HEAD_DIM
GLA_HEADS = N_MIX_HEADS // 4
GLA_DK = HEAD_DIM // 2
GLA_DV = HEAD_DIM
GLA_GATE_RANK = 16
GLA_TAU = 16.0
D_FF = 5632
ROPE_THETA = 10000.0
Q_BLOCK = 128
RET_CHUNK = 128
GLA_CHUNK = 64
N_MOD = 6
EPS = 1e-6
ATT_W = ATT_Q_HEADS * HEAD_DIM
ATT_KV_W = ATT_KV_HEADS * HEAD_DIM
RET_W = RET_HEADS * RET_DIM
GLA_K_W = GLA_HEADS * GLA_DK
GLA_V_W = GLA_HEADS * GLA_DV
IN_SPLITS = (ATT_W, ATT_KV_W, ATT_KV_W, RET_W, RET_W, RET_W, RET_W, GLA_K_W, GLA_K_W, GLA_V_W, GLA_V_W, 2 * GLA_GATE_RANK)
N_IN = ATT_W + 2 * ATT_KV_W + 4 * RET_W + 2 * GLA_K_W + 2 * GLA_V_W + 2 * GLA_GATE_RANK

kernel_name = "hymba_style_diffusion_hybrid_attn_retnet_gla"


def rms_norm(x, g):
    xf = x.astype(jnp.float32)
    y = xf * lax.rsqrt(jnp.mean(xf * xf, axis=-1, keepdims=True) + EPS)
    return (y * g.astype(jnp.float32)).astype(x.dtype)


def modulate(h, shift, scale):
    return h * (1 + scale) + shift


def split_heads(t, n_heads):
    B, L, _ = t.shape
    return t.reshape(B, L, n_heads, -1).transpose(0, 2, 1, 3)


def merge_heads(t):
    B, H, L, d = t.shape
    return t.transpose(0, 2, 1, 3).reshape(B, L, H * d)


def axial_rope(n_tokens):
    rows = n_tokens // GRID_W
    row = jnp.repeat(jnp.arange(rows, dtype=jnp.float32), GRID_W)
    col = jnp.tile(jnp.arange(GRID_W, dtype=jnp.float32), rows)
    n_freq = HEAD_DIM // 4
    inv_freq = ROPE_THETA ** (-jnp.arange(n_freq, dtype=jnp.float32) / n_freq)
    ang = jnp.concatenate([row[:, None] * inv_freq, col[:, None] * inv_freq], axis=-1)
    return jnp.cos(ang), jnp.sin(ang)


def apply_rope(t, cos, sin):
    half = t.shape[-1] // 2
    t1, t2 = t[..., :half], t[..., half:]
    cos = cos.astype(t.dtype)
    sin = sin.astype(t.dtype)
    return jnp.concatenate([t1 * cos - t2 * sin, t1 * sin + t2 * cos], axis=-1)


def project_inputs(h, w_in, q_norm_g, k_norm_g, gla_gate_up, gla_gate_b, rope):
    B, L, _ = h.shape
    f32 = jnp.float32
    z = h @ w_in
    cuts = np.cumsum(IN_SPLITS)[:-1].tolist()
    aq, ak, av, rq, rk, rv, rg, gq, gk, gv, gr, ga = jnp.split(z, cuts, axis=-1)
    aq = rms_norm(split_heads(aq, ATT_Q_HEADS), q_norm_g)
    ak = rms_norm(split_heads(ak, ATT_KV_HEADS), k_norm_g)
    rq = split_heads(rq, RET_HEADS)
    rk = split_heads(rk, RET_HEADS) * (RET_DIM ** -0.5)
    if rope is not None:
        cos, sin = rope
        aq, ak, rq, rk = (apply_rope(t, cos, sin) for t in (aq, ak, rq, rk))
    logit = jnp.einsum('blnr,nrk->nblk', ga.reshape(B, L, 2, GLA_GATE_RANK), gla_gate_up) + gla_gate_b[:, None, None, :]
    log_a = jax.nn.log_sigmoid(logit.astype(f32)) / GLA_TAU
    return dict(
        aq=aq, ak=ak, av=split_heads(av, ATT_KV_HEADS),
        rq=rq.astype(f32), rk=rk.astype(f32), rv=split_heads(rv, RET_HEADS).astype(f32), rg=rg,
        gq=(split_heads(gq, GLA_HEADS) * (GLA_DK ** -0.5)).astype(f32),
        gk=split_heads(gk, GLA_HEADS).astype(f32), gv=split_heads(gv, GLA_HEADS).astype(f32), gr=gr,
        la_f=split_heads(log_a[0], GLA_HEADS), la_b=split_heads(log_a[1], GLA_HEADS))


def softmax_attention(q, k, v):
    B, Hq, Lq, d = q.shape
    Hkv = k.shape[1]
    G = Hq // Hkv
    nb = Lq // Q_BLOCK
    qb = q.reshape(B, Hkv, G, nb, Q_BLOCK, d).transpose(3, 0, 1, 2, 4, 5)
    scale = d ** -0.5

    def block(qi):
        s = jnp.einsum('bkgqd,bksd->bkgqs', qi, k).astype(jnp.float32) * scale
        p = jax.nn.softmax(s, axis=-1).astype(v.dtype)
        return jnp.einsum('bkgqs,bksd->bkgqd', p, v)

    o = lax.map(block, qb)
    return o.transpose(1, 2, 3, 0, 4, 5).reshape(B, Hq, Lq, d)


def retention_scan(log_g, q, k, v, s0):
    B, H, L, _ = q.shape
    C = RET_CHUNK
    n = L // C
    f32 = jnp.float32
    lg = log_g.astype(f32)[:, None]
    idx = jnp.arange(C, dtype=f32)
    rel = idx[:, None] - idx[None, :]
    causal = rel >= 0
    d_in = jnp.where(causal, jnp.exp(lg[:, :, None] * jnp.where(causal, rel, 0.0)), 0.0)
    q_dec = jnp.exp(lg * (idx + 1))[..., None]
    k_dec = jnp.exp(lg * (C - 1 - idx))[..., None]
    c_dec = jnp.exp(lg * C)[..., None]

    def chunks(t):
        return jnp.moveaxis(t.reshape(B, H, n, C, t.shape[-1]), 2, 0)

    def step(S, inp):
        qc, kc, vc = inp
        att = jnp.einsum('bhid,bhjd->bhij', qc, kc) * d_in
        o = jnp.einsum('bhij,bhjv->bhiv', att, vc) + jnp.einsum('bhid,bhdv->bhiv', qc * q_dec, S)
        S = c_dec * S + jnp.einsum('bhjd,bhjv->bhdv', kc * k_dec, vc)
        return S, o

    S, o = lax.scan(step, s0, (chunks(q), chunks(k), chunks(v)))
    return jnp.moveaxis(o, 0, 2).reshape(B, H, L, -1), S


def gla_scan(q, k, v, log_a, s0):
    B, H, L, _ = q.shape
    C = GLA_CHUNK
    n = L // C
    causal = jnp.tril(jnp.ones((C, C), dtype=bool))[:, :, None]

    def chunks(t):
        return jnp.moveaxis(t.reshape(B, H, n, C, t.shape[-1]), 2, 0)

    def step(S, inp):
        qc, kc, vc, ac = inp
        b = jnp.cumsum(ac, axis=2)
        diff = b[:, :, :, None, :] - b[:, :, None, :, :]
        dec = jnp.where(causal, jnp.exp(jnp.where(causal, diff, 0.0)), 0.0)
        att = jnp.einsum('bhid,bhjd,bhijd->bhij', qc, kc, dec)
        o = jnp.einsum('bhij,bhjv->bhiv', att, vc) + jnp.einsum('bhid,bhdv->bhiv', qc * jnp.exp(b), S)
        b_end = b[:, :, -1:, :]
        S = jnp.exp(b_end[:, :, 0, :, None]) * S + jnp.einsum('bhjd,bhjv->bhdv', kc * jnp.exp(b_end - b), vc)
        return S, o

    S, o = lax.scan(step, s0, (chunks(q), chunks(k), chunks(v), chunks(log_a)))
    return jnp.moveaxis(o, 0, 2).reshape(B, H, L, -1), S


def run_bidirectional(scan_f, scan_b, ctx_f, ctx_b, lat_f, lat_b, s0):
    flip = lambda ts: tuple(jnp.flip(t, axis=2) for t in ts)
    o_cf, s_cf = scan_f(*ctx_f, s0)
    o_cb, s_cb = scan_b(*flip(ctx_b), s0)
    o_lf, _ = scan_f(*lat_f, s_cf)
    o_lb, _ = scan_b(*flip(lat_b), s_cb)
    return o_cf + jnp.flip(o_cb, axis=2), o_lf + jnp.flip(o_lb, axis=2)


def mixer_output(o_att, o_ret, g_ret, o_gla, g_gla, ret_norm_g, gla_norm_g, w_out):
    dt = g_ret.dtype
    att = merge_heads(o_att).astype(dt)
    ret = merge_heads(rms_norm(o_ret, ret_norm_g)).astype(dt) * jax.nn.silu(g_ret)
    gla = merge_heads(rms_norm(o_gla, gla_norm_g)).astype(dt) * jax.nn.silu(g_gla)
    return jnp.concatenate([att, ret, gla], axis=-1) @ w_out


def dwconv3(u, w, b):
    up = jnp.pad(u, ((0, 0), (1, 1), (0, 0)))
    return up[:, :-2] * w[0] + up[:, 1:-1] * w[1] + up[:, 2:] * w[2] + b


def conv_glu(h, w_up, conv_w, conv_b, w_down):
    a, v = jnp.split(h @ w_up, 2, axis=-1)
    return (jax.nn.silu(dwconv3(a, conv_w, conv_b)) * v) @ w_down


def setup_inputs(seed: int = 0) -> dict:
    key = jax.random.key(seed)
    ks = jax.random.split(key, 22)
    f32 = jnp.float32

    def nrm(k, shape, scale):
        return jax.random.normal(k, shape, f32) * scale

    def gain(k, shape):
        return 1.0 + 0.02 * jax.random.normal(k, shape, f32)

    decay0 = jnp.log(1.0 - 2.0 ** (-5.0 - jnp.arange(RET_HEADS, dtype=f32)))
    return {
        "x": nrm(ks[0], (BATCH, SEQ, D_MODEL), 1.0),
        "c": nrm(ks[1], (BATCH, D_MODEL), 1.0),
        "ctx": nrm(ks[2], (BATCH, CTX_LEN, D_MODEL), 1.0),
        "c_ctx": nrm(ks[3], (D_MODEL,), 1.0),
        "ada_w": nrm(ks[4], (DEPTH, D_MODEL, N_MOD * D_MODEL), 0.5 * D_MODEL ** -0.5),
        "ada_b": nrm(ks[5], (DEPTH, N_MOD * D_MODEL), 0.01),
        "norm1_g": gain(ks[6], (DEPTH, D_MODEL)),
        "w_in": nrm(ks[7], (DEPTH, D_MODEL, N_IN), D_MODEL ** -0.5),
        "q_norm_g": gain(ks[8], (DEPTH, HEAD_DIM)),
        "k_norm_g": gain(ks[9], (DEPTH, HEAD_DIM)),
        "ret_log_decay": decay0 * (1.0 + 0.05 * jax.random.normal(ks[10], (DEPTH, 2, RET_HEADS), f32)),
        "ret_norm_g": gain(ks[11], (DEPTH, RET_DIM)),
        "gla_gate_up": nrm(ks[12], (DEPTH, 2, GLA_GATE_RANK, GLA_K_W), GLA_GATE_RANK ** -0.5),
        "gla_gate_b": nrm(ks[13], (DEPTH, 2, GLA_K_W), 0.1),
        "gla_norm_g": gain(ks[14], (DEPTH, GLA_DV)),
        "w_out": nrm(ks[15], (DEPTH, D_MODEL, D_MODEL), D_MODEL ** -0.5),
        "norm2_g": gain(ks[16], (DEPTH, D_MODEL)),
        "w_up": nrm(ks[17], (DEPTH, D_MODEL, 2 * D_FF), D_MODEL ** -0.5),
        "conv_w": nrm(ks[18], (DEPTH, 3, D_FF), 3 ** -0.5),
        "conv_b": nrm(ks[19], (DEPTH, D_FF), 0.01),
        "w_down": nrm(ks[20], (DEPTH, D_FF, D_MODEL), D_FF ** -0.5),
        "final_norm_g": gain(ks[21], (D_MODEL,)),
    }


def reference(x, c, ctx, c_ctx, ada_w, ada_b, norm1_g, w_in, q_norm_g, k_norm_g, ret_log_decay, ret_norm_g, gla_gate_up, gla_gate_b, gla_norm_g, w_out, norm2_g, w_up, conv_w, conv_b, w_down, final_norm_g):
    B, L, D = x.shape
    f32 = jnp.float32
    rope = axial_rope(L)
    s0_ret = jnp.zeros((B, RET_HEADS, RET_DIM, RET_DIM), f32)
    s0_gla = jnp.zeros((B, GLA_HEADS, GLA_DK, GLA_DV), f32)
    xc = ctx
    for l in range(DEPTH):
        last = l == DEPTH - 1
        mod = (jax.nn.silu(c) @ ada_w[l] + ada_b[l]).reshape(B, N_MOD, 1, D)
        mod_c = (jax.nn.silu(c_ctx) @ ada_w[l] + ada_b[l]).reshape(N_MOD, 1, 1, D)

        h = modulate(rms_norm(x, norm1_g[l]), mod[:, 0], mod[:, 1])
        hc = modulate(rms_norm(xc, norm1_g[l]), mod_c[0], mod_c[1])
        lat = project_inputs(h, w_in[l], q_norm_g[l], k_norm_g[l], gla_gate_up[l], gla_gate_b[l], rope)
        cx = project_inputs(hc, w_in[l], q_norm_g[l], k_norm_g[l], gla_gate_up[l], gla_gate_b[l], None)

        k_all = jnp.concatenate([cx['ak'], lat['ak']], axis=2)
        v_all = jnp.concatenate([cx['av'], lat['av']], axis=2)
        att_lat = softmax_attention(lat['aq'], k_all, v_all)

        ret_ctx, ret_lat = run_bidirectional(
            functools.partial(retention_scan, ret_log_decay[l, 0]),
            functools.partial(retention_scan, ret_log_decay[l, 1]),
            (cx['rq'], cx['rk'], cx['rv']), (cx['rq'], cx['rk'], cx['rv']),
            (lat['rq'], lat['rk'], lat['rv']), (lat['rq'], lat['rk'], lat['rv']), s0_ret)

        gla_ctx, gla_lat = run_bidirectional(
            gla_scan, gla_scan,
            (cx['gq'], cx['gk'], cx['gv'], cx['la_f']), (cx['gq'], cx['gk'], cx['gv'], cx['la_b']),
            (lat['gq'], lat['gk'], lat['gv'], lat['la_f']), (lat['gq'], lat['gk'], lat['gv'], lat['la_b']), s0_gla)

        y = mixer_output(att_lat, ret_lat, lat['rg'], gla_lat, lat['gr'], ret_norm_g[l], gla_norm_g[l], w_out[l])
        x = x + mod[:, 2] * y

        h2 = modulate(rms_norm(x, norm2_g[l]), mod[:, 3], mod[:, 4])
        x = x + mod[:, 5] * conv_glu(h2, w_up[l], conv_w[l], conv_b[l], w_down[l])

        if not last:
            att_ctx = softmax_attention(cx['aq'], cx['ak'], cx['av'])
            yc = mixer_output(att_ctx, ret_ctx, cx['rg'], gla_ctx, cx['gr'], ret_norm_g[l], gla_norm_g[l], w_out[l])
            xc = xc + mod_c[2] * yc
            hc2 = modulate(rms_norm(xc, norm2_g[l]), mod_c[3], mod_c[4])
            xc = xc + mod_c[5] * conv_glu(hc2, w_up[l], conv_w[l], conv_b[l], w_down[l])
    return rms_norm(x, final_norm_g)
```

```python
import functools

import numpy as np
import jax
import jax.numpy as jnp
from jax import lax
from jax.experimental import pallas as pl
from jax.experimental.pallas import tpu as pltpu

F32 = jnp.float32
BF16 = jnp.bfloat16

D_MODEL = 2048
HEAD_DIM = 128
GRID_W = 64
ATT_Q_HEADS = 8
ATT_KV_HEADS = 2
GQA = ATT_Q_HEADS // ATT_KV_HEADS
RET_HEADS = 4
RET_DIM = 128
RET_CHUNK = 128
GLA_HEADS = 4
GLA_DK = 64
GLA_DV = 128
GLA_GATE_RANK = 16
GLA_TAU = 16.0
GLA_CHUNK = 64
GLA_LEVELS = 6
D_FF = 5632
ROPE_THETA = 10000.0
N_MOD = 6
EPS = 1e-6

LANES = 128
N_IN = 5152
PROJ_TN = 256
N_IN_PAD = 5376
N_SLOTS = N_IN_PAD // LANES
SLOT_AQ, SLOT_AK, SLOT_AV = 0, 8, 10
SLOT_RQ, SLOT_RK, SLOT_RV, SLOT_RG = 12, 16, 20, 24
SLOT_GQ, SLOT_GK, SLOT_GV, SLOT_GR, SLOT_GA = 28, 30, 32, 36, 40

VMEM_LIMIT = 52 * 1024 * 1024

_NT = (((1,), (1,)), ((), ()))
_TN = (((0,), (0,)), ((), ()))


def _cparams(sem):
    return pltpu.CompilerParams(dimension_semantics=sem, vmem_limit_bytes=VMEM_LIMIT)


def _dot(a, b):
    return jnp.dot(a, b, preferred_element_type=F32)


def _dot_nt(a, b):
    return lax.dot_general(a, b, _NT, preferred_element_type=F32)


def _dot_tn(a, b):
    return lax.dot_general(a, b, _TN, preferred_element_type=F32)


def _rows(start, size, align):
    if isinstance(start, int):
        return pl.ds(start, size)
    return pl.ds(pl.multiple_of(start, align), size)


def _rms(t):
    return t * lax.rsqrt(jnp.mean(t * t, axis=-1, keepdims=True) + EPS)


def _silu(t):
    return t * jax.nn.sigmoid(t)


def _ada_kernel(c_ref, w_ref, b_ref, o_ref):
    s = _silu(c_ref[...]).astype(BF16)
    o_ref[0] = _dot(s, w_ref[0].astype(BF16)) + b_ref[0]


def _ada(cc, ada_w, ada_b):
    depth, d, n = ada_w.shape
    rows = cc.shape[0]
    tn = 1024
    return pl.pallas_call(
        _ada_kernel,
        grid=(depth, n // tn),
        in_specs=[
            pl.BlockSpec((rows, d), lambda l, j: (0, 0)),
            pl.BlockSpec((1, d, tn), lambda l, j: (l, 0, j)),
            pl.BlockSpec((1, 1, tn), lambda l, j: (l, 0, j)),
        ],
        out_specs=pl.BlockSpec((1, rows, tn), lambda l, j: (l, 0, j)),
        out_shape=jax.ShapeDtypeStruct((depth, rows, n), F32),
        compiler_params=_cparams(("arbitrary", "arbitrary")),
        name="ada_mod",
    )(cc, ada_w, ada_b.reshape(depth, 1, n))


def _proj_kernel(x_ref, sh_ref, sc_ref, g_ref, w_ref, cos_ref, sin_ref, qg_ref, kg_ref, o_ref, h_ref):
    j = pl.program_id(1)

    @pl.when(j == 0)
    def _():
        y = _rms(x_ref[...]) * g_ref[...]
        h_ref[...] = (y * (1.0 + sc_ref[0]) + sh_ref[0]).astype(BF16)

    z = _dot(h_ref[...], w_ref[...])

    def rope(t):
        return t * cos_ref[...] + pltpu.roll(t, HEAD_DIM // 2, 1) * sin_ref[...]

    def write(f):
        for s in range(PROJ_TN // LANES):
            o_ref[s] = f(z[:, s * LANES:(s + 1) * LANES]).astype(BF16)

    @pl.when(j < 5)
    def _():
        g = jnp.where(j < 4, qg_ref[...] * (HEAD_DIM ** -0.5), kg_ref[...])
        write(lambda t: rope(_rms(t) * g))

    @pl.when((j >= 6) & (j < 10))
    def _():
        s = jnp.where(j >= 8, RET_DIM ** -0.5, 1.0)
        write(lambda t: rope(t * s))

    @pl.when(j == 14)
    def _():
        write(lambda t: t * (GLA_DK ** -0.5))

    @pl.when((j == 5) | ((j >= 10) & (j != 14)))
    def _():
        write(lambda t: t)


def _proj(x2, shift, scale, g, w, cosf, sinf, qg, kg, *, tm, rows_per_mod, rope_tiles):
    m, d = x2.shape
    n_j = N_IN_PAD // PROJ_TN
    return pl.pallas_call(
        _proj_kernel,
        grid=(m // tm, n_j),
        in_specs=[
            pl.BlockSpec((tm, d), lambda i, j: (i, 0)),
            pl.BlockSpec((1, 1, d), lambda i, j: ((i * tm) // rows_per_mod, 0, 0)),
            pl.BlockSpec((1, 1, d), lambda i, j: ((i * tm) // rows_per_mod, 0, 0)),
            pl.BlockSpec((1, d), lambda i, j: (0, 0)),
            pl.BlockSpec((d, PROJ_TN), lambda i, j: (0, j)),
            pl.BlockSpec((tm, LANES), lambda i, j: (i % rope_tiles, 0)),
            pl.BlockSpec((tm, LANES), lambda i, j: (i % rope_tiles, 0)),
            pl.BlockSpec((1, LANES), lambda i, j: (0, 0)),
            pl.BlockSpec((1, LANES), lambda i, j: (0, 0)),
        ],
        out_specs=pl.BlockSpec((PROJ_TN // LANES, tm, LANES), lambda i, j: (j, i, 0)),
        out_shape=jax.ShapeDtypeStruct((N_SLOTS, m, LANES), BF16),
        scratch_shapes=[pltpu.VMEM((tm, d), BF16)],
        compiler_params=_cparams(("arbitrary", "arbitrary")),
        name="norm_proj",
    )(x2, shift, scale, g, w, cosf, sinf, qg, kg)


def _attn_kernel(*refs, tq, n_kv):
    q_ref = refs[0]
    kv_refs = refs[1:1 + 2 * n_kv]
    o_ref = refs[1 + 2 * n_kv]
    q = q_ref[...].reshape(GQA * tq, HEAD_DIM)
    scores = [_dot_nt(q, kv_refs[2 * s][0]) for s in range(n_kv)]
    m = functools.reduce(jnp.maximum, [jnp.max(s, axis=-1, keepdims=True) for s in scores])
    l = 0.0
    o = 0.0
    for s in range(n_kv):
        p = jnp.exp(scores[s] - m)
        l = l + jnp.sum(p, axis=-1, keepdims=True)
        o = o + _dot(p.astype(BF16), kv_refs[2 * s + 1][0])
    o = o / l
    for g in range(GQA):
        o_ref[:, g * HEAD_DIM:(g + 1) * HEAD_DIM] = o[g * tq:(g + 1) * tq].astype(BF16)


def _attn(zq, kv_sources, *, batch, q_len, tq):
    nq = q_len // tq
    in_specs = [pl.BlockSpec((GQA, tq, HEAD_DIM), lambda b, k, i: (k, b * nq + i, 0))]
    args = [zq]
    for z, kv_len in kv_sources:
        in_specs.append(pl.BlockSpec((1, kv_len, HEAD_DIM), lambda b, k, i: (SLOT_AK + k, b, 0)))
        in_specs.append(pl.BlockSpec((1, kv_len, HEAD_DIM), lambda b, k, i: (SLOT_AV + k, b, 0)))
        args += [z, z]
    return pl.pallas_call(
        functools.partial(_attn_kernel, tq=tq, n_kv=len(kv_sources)),
        grid=(batch, ATT_KV_HEADS, nq),
        in_specs=in_specs,
        out_specs=pl.BlockSpec((tq, GQA * HEAD_DIM), lambda b, k, i: (b * nq + i, k)),
        out_shape=jax.ShapeDtypeStruct((batch * q_len, ATT_Q_HEADS * HEAD_DIM), BF16),
        compiler_params=_cparams(("arbitrary", "arbitrary", "arbitrary")),
        name="attention",
    )(*args)


def _ret_kernel(dec_ref, qc_ref, kc_ref, vc_ref, gc_ref, ql_ref, kl_ref, vl_ref, gl_ref, ng_ref,
                oc_ref, ol_ref, accc_ref, accl_ref, s_ref, *, nc_ctx, nc_lat):
    h = pl.program_id(1)
    C = RET_CHUNK
    ii = lax.broadcasted_iota(jnp.int32, (C, C), 0)
    jj = lax.broadcasted_iota(jnp.int32, (C, C), 1)
    row = lax.broadcasted_iota(jnp.int32, (C, 1), 0).astype(F32)

    for d in range(2):
        lg = dec_ref[d, h]
        rel = (ii - jj) if d == 0 else (jj - ii)
        keep = rel >= 0
        dmat = jnp.where(keep, jnp.exp(lg * jnp.where(keep, rel, 0).astype(F32)), 0.0)
        q_dec = jnp.exp(lg * ((row + 1.0) if d == 0 else (C - row)))
        k_dec = jnp.exp(lg * ((C - 1.0 - row) if d == 0 else row))
        c_dec = jnp.exp(lg * C)
        s_ref[...] = jnp.zeros_like(s_ref)

        def chunk(q_ref, k_ref, v_ref, acc_ref, c, d=d, dmat=dmat, q_dec=q_dec, k_dec=k_dec, c_dec=c_dec):
            sl = _rows(c * C, C, C)
            q = q_ref[0, sl, :]
            k = k_ref[0, sl, :]
            v = v_ref[0, sl, :]
            att = _dot_nt(q, k) * dmat
            s = s_ref[...]
            o = _dot(att.astype(BF16), v) + _dot((q.astype(F32) * q_dec).astype(BF16), s.astype(BF16))
            s_ref[...] = c_dec * s + _dot_tn((k.astype(F32) * k_dec).astype(BF16), v)
            if d == 0:
                acc_ref[sl, :] = o
            else:
                acc_ref[sl, :] += o

        for n in range(nc_ctx):
            chunk(qc_ref, kc_ref, vc_ref, accc_ref, n if d == 0 else nc_ctx - 1 - n)

        def body(n, carry, d=d, chunk=chunk):
            chunk(ql_ref, kl_ref, vl_ref, accl_ref, n if d == 0 else nc_lat - 1 - n)
            return carry

        lax.fori_loop(0, nc_lat, body, 0)

    def finish(acc_ref, g_ref, o_ref):
        y = _rms(acc_ref[...]) * ng_ref[...]
        o_ref[...] = (y * _silu(g_ref[0].astype(F32))).astype(BF16)

    finish(accc_ref, gc_ref, oc_ref)
    finish(accl_ref, gl_ref, ol_ref)


def _ret(zc, zl, decay, ng, *, batch, ctx_len, lat_len):
    def slot(base, n):
        return pl.BlockSpec((1, n, RET_DIM), lambda b, h: (base + h, b, 0))

    specs = [pl.BlockSpec(memory_space=pltpu.SMEM)]
    specs += [slot(s, ctx_len) for s in (SLOT_RQ, SLOT_RK, SLOT_RV, SLOT_RG)]
    specs += [slot(s, lat_len) for s in (SLOT_RQ, SLOT_RK, SLOT_RV, SLOT_RG)]
    specs += [pl.BlockSpec((1, RET_DIM), lambda b, h: (0, 0))]
    return pl.pallas_call(
        functools.partial(_ret_kernel, nc_ctx=ctx_len // RET_CHUNK, nc_lat=lat_len // RET_CHUNK),
        grid=(batch, RET_HEADS),
        in_specs=specs,
        out_specs=[pl.BlockSpec((ctx_len, RET_DIM), lambda b, h: (b, h)),
                   pl.BlockSpec((lat_len, RET_DIM), lambda b, h: (b, h))],
        out_shape=[jax.ShapeDtypeStruct((batch * ctx_len, RET_HEADS * RET_DIM), BF16),
                   jax.ShapeDtypeStruct((batch * lat_len, RET_HEADS * RET_DIM), BF16)],
        scratch_shapes=[pltpu.VMEM((ctx_len, RET_DIM), F32), pltpu.VMEM((lat_len, RET_DIM), F32),
                        pltpu.VMEM((RET_DIM, RET_DIM), F32)],
        compiler_params=_cparams(("arbitrary", "arbitrary")),
        name="retention",
    )(decay, zc, zc, zc, zc, zl, zl, zl, zl, ng)


GLA_SLAB = 256


def _gla_level_table():
    i = np.arange(GLA_CHUNK)[:, None]
    j = np.arange(GLA_CHUNK)[None, :]
    x = np.maximum(i ^ j, 1)
    lvl = np.floor(np.log2(x)).astype(np.int32)
    lvl = np.where(i == j, GLA_LEVELS, np.where(i > j, lvl, -1)).astype(np.int32)
    return np.stack([lvl, lvl.T])


def _bcast_block_row(x, blk, r):
    rows = x.shape[0]
    if blk >= 8:
        x3 = x.reshape(rows // blk, blk, LANES)
        return jnp.broadcast_to(x3[:, r:r + 1, :], x3.shape).reshape(rows, LANES)
    pos = lax.broadcasted_iota(jnp.int32, x.shape, 0) & (blk - 1)
    out = x
    for p in range(blk):
        if p != r:
            out = jnp.where(pos == p, pltpu.roll(x, (p - r) % rows, 0), out)
    return out


def _gla_kernel(lvl_ref, up_ref, gb_ref, ng_ref,
                qc_ref, kc_ref, vc_ref, gc_ref, ac_ref, ql_ref, kl_ref, vl_ref, gl_ref, al_ref,
                oc_ref, ol_ref,
                xs_ref, qh_ref, kh_ref, tot_ref, acc_ref, st_ref, *, ctx_len, lat_len):
    C = GLA_CHUNK
    streams = ((qc_ref, kc_ref, vc_ref, ac_ref, 0, ctx_len), (ql_ref, kl_ref, vl_ref, al_ref, ctx_len, lat_len))
    nc_ctx, nc_lat = ctx_len // C, lat_len // C
    pos = lax.broadcasted_iota(jnp.int32, (GLA_SLAB, LANES), 0)
    cpos = pos & (C - 1)

    for d in range(2):
        def prep(i, carry, q_ref, k_ref, a_ref, off, d=d):
            src = _rows(i * GLA_SLAB, GLA_SLAB, GLA_SLAB)
            dst = _rows(off + i * GLA_SLAB, GLA_SLAB, C)
            logit = _dot(a_ref[0, src, :], up_ref[d]) + gb_ref[d]
            la = (jnp.minimum(logit, 0.0) - jnp.log1p(jnp.exp(-jnp.abs(logit)))) * (1.0 / GLA_TAU)
            cum = la
            for k in range(GLA_LEVELS):
                sh = 1 << k
                if d == 0:
                    cum = cum + jnp.where(cpos >= sh, pltpu.roll(cum, sh, 0), 0.0)
                else:
                    cum = cum + jnp.where(cpos < C - sh, pltpu.roll(cum, GLA_SLAB - sh, 0), 0.0)
            tot = _bcast_block_row(cum, C, C - 1 if d == 0 else 0)
            q = q_ref[0, src, :].astype(F32)
            k = k_ref[0, src, :].astype(F32)
            qh_ref[dst, :] = (q * jnp.exp(cum)).astype(BF16)
            kh_ref[dst, :] = (k * jnp.exp(tot - cum)).astype(BF16)
            tot_ref[dst, :] = tot
            for lev in range(GLA_LEVELS):
                s = 1 << lev
                edge = _bcast_block_row(cum, 2 * s, (s - 1) if d == 0 else s)
                late = ((pos >> lev) & 1) == 1
                is_q = late if d == 0 else jnp.logical_not(late)
                xs_ref[lev, dst, :] = (jnp.where(is_q, q, k) * jnp.exp(-jnp.abs(cum - edge))).astype(BF16)
            return carry

        for q_ref, k_ref, _, a_ref, off, n in streams:
            lax.fori_loop(0, n // GLA_SLAB, functools.partial(prep, q_ref=q_ref, k_ref=k_ref, a_ref=a_ref, off=off), 0)

        st_ref[...] = jnp.zeros_like(st_ref)
        lvl = lvl_ref[d]

        def chunk(q_ref, k_ref, v_ref, c, off, d=d, lvl=lvl):
            src = _rows(c * C, C, C)
            dst = _rows(off + c * C, C, C)
            for hh in range(2):
                ls = slice(hh * GLA_DK, (hh + 1) * GLA_DK)
                a = jnp.where(lvl == GLA_LEVELS, _dot_nt(q_ref[0, src, ls], k_ref[0, src, ls]), 0.0)
                for lev in range(GLA_LEVELS):
                    x = xs_ref[lev, dst, ls]
                    a = jnp.where(lvl == lev, _dot_nt(x, x), a)
                v = v_ref[hh, src, :]
                st = st_ref[hh]
                o = _dot(a.astype(BF16), v) + _dot_nt(qh_ref[dst, ls], st.astype(BF16))
                st_ref[hh] = st * jnp.exp(tot_ref[pl.ds(off + c * C, 1), ls]) + _dot_tn(v, kh_ref[dst, ls])
                if d == 0:
                    acc_ref[hh, dst, :] = o
                else:
                    acc_ref[hh, dst, :] += o

        for n in range(nc_ctx):
            chunk(qc_ref, kc_ref, vc_ref, n if d == 0 else nc_ctx - 1 - n, 0)

        def body(n, carry, d=d, chunk=chunk):
            chunk(ql_ref, kl_ref, vl_ref, n if d == 0 else nc_lat - 1 - n, ctx_len)
            return carry

        lax.fori_loop(0, nc_lat, body, 0)

    for hh in range(2):
        for g_ref, o_ref, off, n in ((gc_ref, oc_ref, 0, ctx_len), (gl_ref, ol_ref, ctx_len, lat_len)):
            y = _rms(acc_ref[hh, off:off + n, :]) * ng_ref[...]
            o_ref[:, hh * GLA_DV:(hh + 1) * GLA_DV] = (y * _silu(g_ref[hh].astype(F32))).astype(BF16)


def _gla(zc, zl, up_pad, gate_b, ng, *, batch, ctx_len, lat_len):
    total = ctx_len + lat_len

    def pair(base, n):
        return pl.BlockSpec((1, n, LANES), lambda b, p: (base + p, b, 0))

    def two(base, n):
        return pl.BlockSpec((2, n, LANES), lambda b, p: (base // 2 + p, b, 0))

    def one(n):
        return pl.BlockSpec((1, n, LANES), lambda b, p: (SLOT_GA, b, 0))

    specs = [
        pl.BlockSpec((2, GLA_CHUNK, GLA_CHUNK), lambda b, p: (0, 0, 0)),
        pl.BlockSpec((2, LANES, LANES), lambda b, p: (0, 0, p)),
        pl.BlockSpec((2, 1, LANES), lambda b, p: (0, 0, p)),
        pl.BlockSpec((1, GLA_DV), lambda b, p: (0, 0)),
    ]
    for n in (ctx_len, lat_len):
        specs += [pair(SLOT_GQ, n), pair(SLOT_GK, n), two(SLOT_GV, n), two(SLOT_GR, n), one(n)]
    return pl.pallas_call(
        functools.partial(_gla_kernel, ctx_len=ctx_len, lat_len=lat_len),
        grid=(batch, GLA_HEADS // 2),
        in_specs=specs,
        out_specs=[pl.BlockSpec((ctx_len, 2 * GLA_DV), lambda b, p: (b, p)),
                   pl.BlockSpec((lat_len, 2 * GLA_DV), lambda b, p: (b, p))],
        out_shape=[jax.ShapeDtypeStruct((batch * ctx_len, GLA_HEADS * GLA_DV), BF16),
                   jax.ShapeDtypeStruct((batch * lat_len, GLA_HEADS * GLA_DV), BF16)],
        scratch_shapes=[
            pltpu.VMEM((GLA_LEVELS, total, LANES), BF16),
            pltpu.VMEM((total, LANES), BF16),
            pltpu.VMEM((total, LANES), BF16),
            pltpu.VMEM((total, LANES), F32),
            pltpu.VMEM((2, total, GLA_DV), F32),
            pltpu.VMEM((2, GLA_DV, GLA_DK), F32),
        ],
        compiler_params=_cparams(("arbitrary", "arbitrary")),
        name="gla",
    )(jnp.asarray(_gla_level_table()), up_pad, gate_b.reshape(2, 1, -1), ng,
      zc, zc, zc, zc, zc, zl, zl, zl, zl, zl)


def _outproj_kernel(att_ref, ret_ref, gla_ref, w_ref, x_ref, gate_ref, sh_ref, sc_ref, g_ref, xo_ref, ho_ref):
    na, nr = att_ref.shape[1], ret_ref.shape[1]
    y = (_dot(att_ref[...], w_ref[0:na, :]) + _dot(ret_ref[...], w_ref[na:na + nr, :])
         + _dot(gla_ref[...], w_ref[na + nr:, :]))
    xn = x_ref[...] + gate_ref[0] * y
    xo_ref[...] = xn
    ho_ref[...] = (_rms(xn) * g_ref[...] * (1.0 + sc_ref[0]) + sh_ref[0]).astype(BF16)


def _outproj(att, ret, gla, w, x2, gate, shift, scale, g, *, tm, rows_per_mod):
    m, d = x2.shape

    def rows(n):
        return pl.BlockSpec((tm, n), lambda i: (i, 0))

    def mod():
        return pl.BlockSpec((1, 1, d), lambda i: ((i * tm) // rows_per_mod, 0, 0))

    return pl.pallas_call(
        _outproj_kernel,
        grid=(m // tm,),
        in_specs=[rows(att.shape[1]), rows(ret.shape[1]), rows(gla.shape[1]),
                  pl.BlockSpec(w.shape, lambda i: (0, 0)), rows(d), mod(), mod(), mod(),
                  pl.BlockSpec((1, d), lambda i: (0, 0))],
        out_specs=[rows(d), rows(d)],
        out_shape=[jax.ShapeDtypeStruct((m, d), F32), jax.ShapeDtypeStruct((m, d), BF16)],
        compiler_params=_cparams(("arbitrary",)),
        name="out_proj",
    )(att, ret, gla, w, x2, gate, shift, scale, g)


FFN_HALO = 16


def _ffn_kernel(h_ref, hn_ref, hp_ref, wa_ref, wv_ref, cw_ref, cb_ref, wd_ref, x_ref, gate_ref, fg_ref,
                o_ref, hs_ref, *, tm, seq_len, final_norm):
    i = pl.program_id(0)
    f = pl.program_id(1)
    ext = tm + 2 * FFN_HALO

    @pl.when(f == 0)
    def _():
        hs_ref[0:tm, :] = h_ref[...]
        hs_ref[tm:tm + FFN_HALO, :] = hn_ref[...]
        hs_ref[tm + FFN_HALO:ext, :] = hp_ref[...]
        o_ref[...] = jnp.zeros_like(o_ref)

    a = _dot(hs_ref[...], wa_ref[...])
    pos = (i * tm + lax.broadcasted_iota(jnp.int32, (ext, 1), 0)) % seq_len
    prev = jnp.where(pos == 0, 0.0, pltpu.roll(a, 1, 0))
    nxt = jnp.where(pos == seq_len - 1, 0.0, pltpu.roll(a, ext - 1, 0))
    conv = prev * cw_ref[0:1, :] + a * cw_ref[1:2, :] + nxt * cw_ref[2:3, :] + cb_ref[...]
    v = _dot(h_ref[...], wv_ref[...])
    u = (_silu(conv[0:tm]) * v).astype(BF16)
    o_ref[...] += _dot(u, wd_ref[...])

    @pl.when(f == pl.num_programs(1) - 1)
    def _():
        xn = x_ref[...] + gate_ref[0] * o_ref[...]
        if final_norm:
            xn = _rms(xn) * fg_ref[...]
        o_ref[...] = xn


def _ffn(h2, w_up, conv_w, conv_b, w_down, x2, gate, fg, *, tm, tf, seq_len, rows_per_mod, final_norm):
    m, d = x2.shape
    nf = D_FF // tf
    hb = tm // FFN_HALO
    last = m // FFN_HALO - 1
    return pl.pallas_call(
        functools.partial(_ffn_kernel, tm=tm, seq_len=seq_len, final_norm=final_norm),
        grid=(m // tm, nf),
        in_specs=[
            pl.BlockSpec((tm, d), lambda i, f: (i, 0)),
            pl.BlockSpec((FFN_HALO, d), lambda i, f: (jnp.minimum((i + 1) * hb, last), 0)),
            pl.BlockSpec((FFN_HALO, d), lambda i, f: (jnp.maximum(i * hb - 1, 0), 0)),
            pl.BlockSpec((d, tf), lambda i, f: (0, f)),
            pl.BlockSpec((d, tf), lambda i, f: (0, nf + f)),
            pl.BlockSpec((3, tf), lambda i, f: (0, f)),
            pl.BlockSpec((1, tf), lambda i, f: (0, f)),
            pl.BlockSpec((tf, d), lambda i, f: (f, 0)),
            pl.BlockSpec((tm, d), lambda i, f: (i, 0)),
            pl.BlockSpec((1, 1, d), lambda i, f: ((i * tm) // rows_per_mod, 0, 0)),
            pl.BlockSpec((1, d), lambda i, f: (0, 0)),
        ],
        out_specs=pl.BlockSpec((tm, d), lambda i, f: (i, 0)),
        out_shape=jax.ShapeDtypeStruct((m, d), F32),
        scratch_shapes=[pltpu.VMEM((tm + 2 * FFN_HALO, d), BF16)],
        compiler_params=_cparams(("arbitrary", "arbitrary")),
        name="conv_glu",
    )(h2, h2, h2, w_up, w_up, conv_w, conv_b, w_down, x2, gate, fg)


def _rope_tables(n_tokens):
    rows = n_tokens // GRID_W
    row = jnp.repeat(jnp.arange(rows, dtype=F32), GRID_W)
    col = jnp.tile(jnp.arange(GRID_W, dtype=F32), rows)
    n_freq = HEAD_DIM // 4
    inv_freq = ROPE_THETA ** (-jnp.arange(n_freq, dtype=F32) / n_freq)
    ang = jnp.concatenate([row[:, None] * inv_freq, col[:, None] * inv_freq], axis=-1)
    cos, sin = jnp.cos(ang), jnp.sin(ang)
    return jnp.concatenate([cos, cos], axis=-1), jnp.concatenate([-sin, sin], axis=-1)


def kernel(x, c, ctx, c_ctx, ada_w, ada_b, norm1_g, w_in, q_norm_g, k_norm_g, ret_log_decay, ret_norm_g,
           gla_gate_up, gla_gate_b, gla_norm_g, w_out, norm2_g, w_up, conv_w, conv_b, w_down, final_norm_g):
    batch, lat_len, d = x.shape
    ctx_len = ctx.shape[1]
    depth = ada_w.shape[0]
    mod_rows = 16
    cc = jnp.concatenate([c, c_ctx[None], jnp.zeros((mod_rows - batch - 1, d), F32)], axis=0)
    mod = _ada(cc, ada_w, ada_b).reshape(depth, mod_rows, N_MOD, d)

    cos_l, sin_l = _rope_tables(lat_len)
    proj_tm = 512
    cos_c = jnp.ones((proj_tm, LANES), F32)
    sin_c = jnp.zeros((proj_tm, LANES), F32)

    xl = x.reshape(batch * lat_len, d)
    xc = ctx.reshape(batch * ctx_len, d)
    row = lambda v: v.reshape(1, -1)

    for l in range(depth):
        last = l == depth - 1
        ml = [mod[l, :batch, k].reshape(batch, 1, d) for k in range(N_MOD)]
        mc = [mod[l, batch, k].reshape(1, 1, d) for k in range(N_MOD)]
        w_in_b = jnp.pad(w_in[l].astype(BF16), ((0, 0), (0, N_IN_PAD - N_IN)))
        w_out_b = w_out[l].astype(BF16)
        w_up_b = w_up[l].astype(BF16)
        w_down_b = w_down[l].astype(BF16)
        up_pad = jnp.zeros((2, LANES, GLA_HEADS * GLA_DK), BF16)
        for n in range(2):
            up_pad = up_pad.at[n, n * GLA_GATE_RANK:(n + 1) * GLA_GATE_RANK].set(gla_gate_up[l, n].astype(BF16))

        zl = _proj(xl, ml[0], ml[1], row(norm1_g[l]), w_in_b, cos_l, sin_l, row(q_norm_g[l]), row(k_norm_g[l]),
                   tm=proj_tm, rows_per_mod=lat_len, rope_tiles=lat_len // proj_tm)
        zc = _proj(xc, mc[0], mc[1], row(norm1_g[l]), w_in_b, cos_c, sin_c, row(q_norm_g[l]), row(k_norm_g[l]),
                   tm=proj_tm, rows_per_mod=batch * ctx_len, rope_tiles=1)

        att_l = _attn(zl, [(zc, ctx_len), (zl, lat_len)], batch=batch, q_len=lat_len, tq=256)
        ret_c, ret_l = _ret(zc, zl, ret_log_decay[l], row(ret_norm_g[l]),
                            batch=batch, ctx_len=ctx_len, lat_len=lat_len)
        gla_c, gla_l = _gla(zc, zl, up_pad, gla_gate_b[l], row(gla_norm_g[l]),
                            batch=batch, ctx_len=ctx_len, lat_len=lat_len)

        xl, h2 = _outproj(att_l, ret_l, gla_l, w_out_b, xl, ml[2], ml[3], ml[4], row(norm2_g[l]),
                          tm=256, rows_per_mod=lat_len)
        xl = _ffn(h2, w_up_b, conv_w[l], row(conv_b[l]), w_down_b, xl, ml[5], row(final_norm_g),
                  tm=512, tf=512, seq_len=lat_len, rows_per_mod=lat_len, final_norm=last)

        if not last:
            att_c = _attn(zc, [(zc, ctx_len)], batch=batch, q_len=ctx_len, tq=ctx_len)
            xc, hc2 = _outproj(att_c, ret_c, gla_c, w_out_b, xc, mc[2], mc[3], mc[4], row(norm2_g[l]),
                               tm=256, rows_per_mod=batch * ctx_len)
            xc = _ffn(hc2, w_up_b, conv_w[l], row(conv_b[l]), w_down_b, xc, mc[5], row(final_norm_g),
                      tm=512, tf=512, seq_len=ctx_len, rows_per_mod=batch * ctx_len, final_norm=False)

    return xl.reshape(batch, lat_len, d)
```

```python
import functools

import numpy as np
import jax
import jax.numpy as jnp
from jax import lax
from jax.experimental import pallas as pl
from jax.experimental.pallas import tpu as pltpu

F32 = jnp.float32
BF16 = jnp.bfloat16

D_MODEL = 2048
HEAD_DIM = 128
GRID_W = 64
ATT_Q_HEADS = 8
ATT_KV_HEADS = 2
GQA = ATT_Q_HEADS // ATT_KV_HEADS
RET_HEADS = 4
RET_DIM = 128
RET_CHUNK = 128
GLA_HEADS = 4
GLA_DK = 64
GLA_DV = 128
GLA_GATE_RANK = 16
GLA_TAU = 16.0
GLA_CHUNK = 64
GLA_LEVELS = 6
D_FF = 5632
ROPE_THETA = 10000.0
N_MOD = 6
EPS = 1e-6

LANES = 128
N_IN = 5152
PROJ_TN = 256
N_SLOTS = -(-N_IN // LANES)
SLOT_AQ, SLOT_AK, SLOT_AV = 0, 8, 10
SLOT_RQ, SLOT_RK, SLOT_RV, SLOT_RG = 12, 16, 20, 24
SLOT_GQ, SLOT_GK, SLOT_GV, SLOT_GR, SLOT_GA = 28, 30, 32, 36, 40

VMEM_LIMIT = 52 * 1024 * 1024

_NT = (((1,), (1,)), ((), ()))
_TN = (((0,), (0,)), ((), ()))


def _cparams(sem):
    return pltpu.CompilerParams(dimension_semantics=sem, vmem_limit_bytes=VMEM_LIMIT)


def _dot(a, b):
    return jnp.dot(a, b, preferred_element_type=F32)


def _dot_nt(a, b):
    return lax.dot_general(a, b, _NT, preferred_element_type=F32)


def _dot_tn(a, b):
    return lax.dot_general(a, b, _TN, preferred_element_type=F32)


def _rows(start, size, align):
    if isinstance(start, int):
        return pl.ds(start, size)
    return pl.ds(pl.multiple_of(start, align), size)


def _rms(t):
    return t * lax.rsqrt(jnp.mean(t * t, axis=-1, keepdims=True) + EPS)


def _silu(t):
    return t * jax.nn.sigmoid(t)


def _ada_kernel(c_ref, w_ref, b_ref, o_ref):
    s = _silu(c_ref[...]).astype(BF16)
    o_ref[0] = _dot(s, w_ref[0].astype(BF16)) + b_ref[0]


def _ada(cc, ada_w, ada_b):
    depth, d, n = ada_w.shape
    rows = cc.shape[0]
    tn = 1024
    return pl.pallas_call(
        _ada_kernel,
        grid=(depth, n // tn),
        in_specs=[
            pl.BlockSpec((rows, d), lambda l, j: (0, 0)),
            pl.BlockSpec((1, d, tn), lambda l, j: (l, 0, j)),
            pl.BlockSpec((1, 1, tn), lambda l, j: (l, 0, j)),
        ],
        out_specs=pl.BlockSpec((1, rows, tn), lambda l, j: (l, 0, j)),
        out_shape=jax.ShapeDtypeStruct((depth, rows, n), F32),
        compiler_params=_cparams(("arbitrary", "arbitrary")),
        name="ada_mod",
    )(cc, ada_w, ada_b.reshape(depth, 1, n))


def _proj_kernel(x_ref, sh_ref, sc_ref, g_ref, w_ref, cos_ref, sin_ref, qg_ref, kg_ref, o_ref):
    y = _rms(x_ref[...]) * g_ref[...]
    h = (y * (1.0 + sc_ref[0]) + sh_ref[0]).astype(BF16)
    cos, sin = cos_ref[...], sin_ref[...]
    qg = qg_ref[...] * (HEAD_DIM ** -0.5)
    kg = kg_ref[...]

    def rope(t):
        return t * cos + pltpu.roll(t, HEAD_DIM // 2, 1) * sin

    def transform(slot, t):
        if slot < SLOT_AK:
            return rope(_rms(t) * qg)
        if slot < SLOT_AV:
            return rope(_rms(t) * kg)
        if SLOT_RQ <= slot < SLOT_RK:
            return rope(t)
        if SLOT_RK <= slot < SLOT_RV:
            return rope(t * (RET_DIM ** -0.5))
        if SLOT_GQ <= slot < SLOT_GK:
            return t * (GLA_DK ** -0.5)
        return t

    for c0 in range(0, N_IN, PROJ_TN):
        c1 = min(c0 + PROJ_TN, N_IN)
        z = _dot(h, w_ref[0, :, c0:c1])
        for s0 in range(c0, c1, LANES):
            slot, n = s0 // LANES, min(LANES, c1 - s0)
            t = transform(slot, z[:, s0 - c0:s0 - c0 + n]).astype(BF16)
            if n < LANES:
                o_ref[slot] = jnp.zeros(o_ref.shape[1:], BF16)
                o_ref[slot, :, 0:n] = t
            else:
                o_ref[slot] = t


def _proj(x2, shift, scale, g, w, layer, cosf, sinf, qg, kg, *, tm, rows_per_mod, rope_tiles):
    m, d = x2.shape
    return pl.pallas_call(
        _proj_kernel,
        grid=(m // tm,),
        in_specs=[
            pl.BlockSpec((tm, d), lambda i: (i, 0)),
            pl.BlockSpec((1, 1, d), lambda i: ((i * tm) // rows_per_mod, 0, 0)),
            pl.BlockSpec((1, 1, d), lambda i: ((i * tm) // rows_per_mod, 0, 0)),
            pl.BlockSpec((1, d), lambda i: (0, 0)),
            pl.BlockSpec((1, d, N_IN), lambda i: (layer, 0, 0), pipeline_mode=pl.Buffered(1)),
            pl.BlockSpec((tm, LANES), lambda i: (i % rope_tiles, 0)),
            pl.BlockSpec((tm, LANES), lambda i: (i % rope_tiles, 0)),
            pl.BlockSpec((1, LANES), lambda i: (0, 0)),
            pl.BlockSpec((1, LANES), lambda i: (0, 0)),
        ],
        out_specs=pl.BlockSpec((N_SLOTS, tm, LANES), lambda i: (0, i, 0)),
        out_shape=jax.ShapeDtypeStruct((N_SLOTS, m, LANES), BF16),
        compiler_params=_cparams(("arbitrary",)),
        name="norm_proj",
    )(x2, shift, scale, g, w, cosf, sinf, qg, kg)


def _attn_kernel(*refs, tq, n_kv):
    q_ref = refs[0]
    kv_refs = refs[1:1 + 2 * n_kv]
    o_ref = refs[1 + 2 * n_kv]
    q = q_ref[...].reshape(GQA * tq, HEAD_DIM)
    scores = [_dot_nt(q, kv_refs[2 * s][0]) for s in range(n_kv)]
    m = functools.reduce(jnp.maximum, [jnp.max(s, axis=-1, keepdims=True) for s in scores])
    l = 0.0
    o = 0.0
    for s in range(n_kv):
        p = jnp.exp(scores[s] - m)
        l = l + jnp.sum(p, axis=-1, keepdims=True)
        o = o + _dot(p.astype(BF16), kv_refs[2 * s + 1][0])
    o = o / l
    for g in range(GQA):
        o_ref[:, g * HEAD_DIM:(g + 1) * HEAD_DIM] = o[g * tq:(g + 1) * tq].astype(BF16)


def _attn(zq, kv_sources, *, batch, q_len, tq):
    nq = q_len // tq
    in_specs = [pl.BlockSpec((GQA, tq, HEAD_DIM), lambda b, k, i: (k, b * nq + i, 0))]
    args = [zq]
    for z, kv_len in kv_sources:
        in_specs.append(pl.BlockSpec((1, kv_len, HEAD_DIM), lambda b, k, i: (SLOT_AK + k, b, 0)))
        in_specs.append(pl.BlockSpec((1, kv_len, HEAD_DIM), lambda b, k, i: (SLOT_AV + k, b, 0)))
        args += [z, z]
    return pl.pallas_call(
        functools.partial(_attn_kernel, tq=tq, n_kv=len(kv_sources)),
        grid=(batch, ATT_KV_HEADS, nq),
        in_specs=in_specs,
        out_specs=pl.BlockSpec((tq, GQA * HEAD_DIM), lambda b, k, i: (b * nq + i, k)),
        out_shape=jax.ShapeDtypeStruct((batch * q_len, ATT_Q_HEADS * HEAD_DIM), BF16),
        compiler_params=_cparams(("arbitrary", "arbitrary", "arbitrary")),
        name="attention",
    )(*args)


def _ret_kernel(dec_ref, qc_ref, kc_ref, vc_ref, gc_ref, ql_ref, kl_ref, vl_ref, gl_ref, ng_ref,
                oc_ref, ol_ref, kv_ref, st_ref, *, nc_ctx, nc_lat):
    h = pl.program_id(1)
    C = RET_CHUNK
    lg_f, lg_b = dec_ref[0, h], dec_ref[1, h]
    rel = lax.broadcasted_iota(jnp.int32, (C, C), 0) - lax.broadcasted_iota(jnp.int32, (C, C), 1)
    dmat = (jnp.where(rel >= 0, jnp.exp(lg_f * jnp.maximum(rel, 0).astype(F32)), 0.0)
            + jnp.where(rel <= 0, jnp.exp(lg_b * jnp.maximum(-rel, 0).astype(F32)), 0.0))
    row = lax.broadcasted_iota(jnp.int32, (C, 1), 0).astype(F32)
    qd_f, qd_b = jnp.exp(lg_f * (row + 1.0)), jnp.exp(lg_b * (C - row))
    kd_f, kd_b = jnp.exp(lg_f * (C - 1.0 - row)), jnp.exp(lg_b * row)
    cd_f, cd_b = jnp.exp(lg_f * C), jnp.exp(lg_b * C)

    chunks = [(qc_ref, kc_ref, vc_ref, gc_ref, oc_ref, n) for n in range(nc_ctx)]
    chunks += [(ql_ref, kl_ref, vl_ref, gl_ref, ol_ref, n) for n in range(nc_lat)]
    order_b = list(range(nc_ctx - 1, -1, -1)) + list(range(nc_ctx + nc_lat - 1, nc_ctx - 1, -1))

    def decayed(x_ref, n, dec_f, dec_b):
        x = x_ref[0, n * C:(n + 1) * C, :].astype(F32)
        return jnp.concatenate([(x * dec_f).astype(BF16), (x * dec_b).astype(BF16)], axis=1)

    for c, (_, k_ref, v_ref, _, _, n) in enumerate(chunks):
        kv_ref[c] = _dot_tn(decayed(k_ref, n, kd_f, kd_b), v_ref[0, n * C:(n + 1) * C, :])

    s = jnp.zeros((C, RET_DIM), F32)
    for c in range(len(chunks)):
        st_ref[c, 0:C, :] = s.astype(BF16)
        s = cd_f * s + kv_ref[c, 0:C, :]
    s = jnp.zeros((C, RET_DIM), F32)
    for c in order_b:
        st_ref[c, C:2 * C, :] = s.astype(BF16)
        s = cd_b * s + kv_ref[c, C:2 * C, :]

    for c, (q_ref, k_ref, v_ref, g_ref, o_ref, n) in enumerate(chunks):
        sl = slice(n * C, (n + 1) * C)
        v = v_ref[0, sl, :]
        att = _dot_nt(q_ref[0, sl, :], k_ref[0, sl, :]) * dmat
        o = _dot(att.astype(BF16), v) + _dot(decayed(q_ref, n, qd_f, qd_b), st_ref[c])
        y = _rms(o) * ng_ref[...]
        o_ref[sl, :] = (y * _silu(g_ref[0, sl, :].astype(F32))).astype(BF16)


def _ret(zc, zl, decay, ng, *, batch, ctx_len, lat_len):
    def slot(base, n):
        return pl.BlockSpec((1, n, RET_DIM), lambda b, h: (base + h, b, 0))

    n_chunks = (ctx_len + lat_len) // RET_CHUNK
    specs = [pl.BlockSpec(memory_space=pltpu.SMEM)]
    specs += [slot(s, ctx_len) for s in (SLOT_RQ, SLOT_RK, SLOT_RV, SLOT_RG)]
    specs += [slot(s, lat_len) for s in (SLOT_RQ, SLOT_RK, SLOT_RV, SLOT_RG)]
    specs += [pl.BlockSpec((1, RET_DIM), lambda b, h: (0, 0))]
    return pl.pallas_call(
        functools.partial(_ret_kernel, nc_ctx=ctx_len // RET_CHUNK, nc_lat=lat_len // RET_CHUNK),
        grid=(batch, RET_HEADS),
        in_specs=specs,
        out_specs=[pl.BlockSpec((ctx_len, RET_DIM), lambda b, h: (b, h)),
                   pl.BlockSpec((lat_len, RET_DIM), lambda b, h: (b, h))],
        out_shape=[jax.ShapeDtypeStruct((batch * ctx_len, RET_HEADS * RET_DIM), BF16),
                   jax.ShapeDtypeStruct((batch * lat_len, RET_HEADS * RET_DIM), BF16)],
        scratch_shapes=[pltpu.VMEM((n_chunks, 2 * RET_DIM, RET_DIM), F32),
                        pltpu.VMEM((n_chunks, 2 * RET_DIM, RET_DIM), BF16)],
        compiler_params=_cparams(("arbitrary", "arbitrary")),
        name="retention",
    )(decay, zc, zc, zc, zc, zl, zl, zl, zl, ng)


GLA_SLAB = 256


def _gla_level_table():
    i = np.arange(GLA_CHUNK)[:, None]
    j = np.arange(GLA_CHUNK)[None, :]
    x = np.maximum(i ^ j, 1)
    lvl = np.floor(np.log2(x)).astype(np.int32)
    lvl = np.where(i == j, GLA_LEVELS, np.where(i > j, lvl, -1)).astype(np.int32)
    return np.stack([lvl, lvl.T])


def _bcast_block_row(x, blk, r):
    rows = x.shape[0]
    if blk >= 8:
        x3 = x.reshape(rows // blk, blk, LANES)
        return jnp.broadcast_to(x3[:, r:r + 1, :], x3.shape).reshape(rows, LANES)
    pos = lax.broadcasted_iota(jnp.int32, x.shape, 0) & (blk - 1)
    out = x
    for p in range(blk):
        if p != r:
            out = jnp.where(pos == p, pltpu.roll(x, (p - r) % rows, 0), out)
    return out


def _gla_kernel(lvl_ref, up_ref, gb_ref, ng_ref,
                qc_ref, kc_ref, vc_ref, gc_ref, ac_ref, ql_ref, kl_ref, vl_ref, gl_ref, al_ref,
                oc_ref, ol_ref,
                xs_ref, qh_ref, kh_ref, tot_ref, acc_ref, st_ref, *, ctx_len, lat_len):
    C = GLA_CHUNK
    streams = ((qc_ref, kc_ref, vc_ref, ac_ref, 0, ctx_len), (ql_ref, kl_ref, vl_ref, al_ref, ctx_len, lat_len))
    nc_ctx, nc_lat = ctx_len // C, lat_len // C
    pos = lax.broadcasted_iota(jnp.int32, (GLA_SLAB, LANES), 0)
    cpos = pos & (C - 1)

    for d in range(2):
        def prep(i, carry, q_ref, k_ref, a_ref, off, d=d):
            src = _rows(i * GLA_SLAB, GLA_SLAB, GLA_SLAB)
            dst = _rows(off + i * GLA_SLAB, GLA_SLAB, C)
            logit = _dot(a_ref[0, src, :], up_ref[d]) + gb_ref[d]
            la = (jnp.minimum(logit, 0.0) - jnp.log1p(jnp.exp(-jnp.abs(logit)))) * (1.0 / GLA_TAU)
            cum = la
            for k in range(GLA_LEVELS):
                sh = 1 << k
                if d == 0:
                    cum = cum + jnp.where(cpos >= sh, pltpu.roll(cum, sh, 0), 0.0)
                else:
                    cum = cum + jnp.where(cpos < C - sh, pltpu.roll(cum, GLA_SLAB - sh, 0), 0.0)
            tot = _bcast_block_row(cum, C, C - 1 if d == 0 else 0)
            q = q_ref[0, src, :].astype(F32)
            k = k_ref[0, src, :].astype(F32)
            qh_ref[dst, :] = (q * jnp.exp(cum)).astype(BF16)
            kh_ref[dst, :] = (k * jnp.exp(tot - cum)).astype(BF16)
            tot_ref[dst, :] = tot
            for lev in range(GLA_LEVELS):
                s = 1 << lev
                edge = _bcast_block_row(cum, 2 * s, (s - 1) if d == 0 else s)
                late = ((pos >> lev) & 1) == 1
                is_q = late if d == 0 else jnp.logical_not(late)
                xs_ref[lev, dst, :] = (jnp.where(is_q, q, k) * jnp.exp(-jnp.abs(cum - edge))).astype(BF16)
            return carry

        for q_ref, k_ref, _, a_ref, off, n in streams:
            lax.fori_loop(0, n // GLA_SLAB, functools.partial(prep, q_ref=q_ref, k_ref=k_ref, a_ref=a_ref, off=off), 0)

        st_ref[...] = jnp.zeros_like(st_ref)
        lvl = lvl_ref[d]

        def chunk(q_ref, k_ref, v_ref, c, off, d=d, lvl=lvl):
            src = _rows(c * C, C, C)
            dst = _rows(off + c * C, C, C)
            for hh in range(2):
                ls = slice(hh * GLA_DK, (hh + 1) * GLA_DK)
                a = jnp.where(lvl == GLA_LEVELS, _dot_nt(q_ref[0, src, ls], k_ref[0, src, ls]), 0.0)
                for lev in range(GLA_LEVELS):
                    x = xs_ref[lev, dst, ls]
                    a = jnp.where(lvl == lev, _dot_nt(x, x), a)
                v = v_ref[hh, src, :]
                st = st_ref[hh]
                o = _dot(a.astype(BF16), v) + _dot_nt(qh_ref[dst, ls], st.astype(BF16))
                st_ref[hh] = st * jnp.exp(tot_ref[pl.ds(off + c * C, 1), ls]) + _dot_tn(v, kh_ref[dst, ls])
                if d == 0:
                    acc_ref[hh, dst, :] = o
                else:
                    acc_ref[hh, dst, :] += o

        for n in range(nc_ctx):
            chunk(qc_ref, kc_ref, vc_ref, n if d == 0 else nc_ctx - 1 - n, 0)

        def body(n, carry, d=d, chunk=chunk):
            chunk(ql_ref, kl_ref, vl_ref, n if d == 0 else nc_lat - 1 - n, ctx_len)
            return carry

        lax.fori_loop(0, nc_lat, body, 0)

    for hh in range(2):
        for g_ref, o_ref, off, n in ((gc_ref, oc_ref, 0, ctx_len), (gl_ref, ol_ref, ctx_len, lat_len)):
            y = _rms(acc_ref[hh, off:off + n, :]) * ng_ref[...]
            o_ref[:, hh * GLA_DV:(hh + 1) * GLA_DV] = (y * _silu(g_ref[hh].astype(F32))).astype(BF16)


def _gla(zc, zl, up_pad, gate_b, ng, *, batch, ctx_len, lat_len):
    total = ctx_len + lat_len

    def pair(base, n):
        return pl.BlockSpec((1, n, LANES), lambda b, p: (base + p, b, 0))

    def two(base, n):
        return pl.BlockSpec((2, n, LANES), lambda b, p: (base // 2 + p, b, 0))

    def one(n):
        return pl.BlockSpec((1, n, LANES), lambda b, p: (SLOT_GA, b, 0))

    specs = [
        pl.BlockSpec((2, GLA_CHUNK, GLA_CHUNK), lambda b, p: (0, 0, 0)),
        pl.BlockSpec((2, LANES, LANES), lambda b, p: (0, 0, p)),
        pl.BlockSpec((2, 1, LANES), lambda b, p: (0, 0, p)),
        pl.BlockSpec((1, GLA_DV), lambda b, p: (0, 0)),
    ]
    for n in (ctx_len, lat_len):
        specs += [pair(SLOT_GQ, n), pair(SLOT_GK, n), two(SLOT_GV, n), two(SLOT_GR, n), one(n)]
    return pl.pallas_call(
        functools.partial(_gla_kernel, ctx_len=ctx_len, lat_len=lat_len),
        grid=(batch, GLA_HEADS // 2),
        in_specs=specs,
        out_specs=[pl.BlockSpec((ctx_len, 2 * GLA_DV), lambda b, p: (b, p)),
                   pl.BlockSpec((lat_len, 2 * GLA_DV), lambda b, p: (b, p))],
        out_shape=[jax.ShapeDtypeStruct((batch * ctx_len, GLA_HEADS * GLA_DV), BF16),
                   jax.ShapeDtypeStruct((batch * lat_len, GLA_HEADS * GLA_DV), BF16)],
        scratch_shapes=[
            pltpu.VMEM((GLA_LEVELS, total, LANES), BF16),
            pltpu.VMEM((total, LANES), BF16),
            pltpu.VMEM((total, LANES), BF16),
            pltpu.VMEM((total, LANES), F32),
            pltpu.VMEM((2, total, GLA_DV), F32),
            pltpu.VMEM((2, GLA_DV, GLA_DK), F32),
        ],
        compiler_params=_cparams(("arbitrary", "arbitrary")),
        name="gla",
    )(jnp.asarray(_gla_level_table()), up_pad, gate_b.reshape(2, 1, -1), ng,
      zc, zc, zc, zc, zc, zl, zl, zl, zl, zl)


def _outproj_kernel(att_ref, ret_ref, gla_ref, w_ref, x_ref, gate_ref, sh_ref, sc_ref, g_ref, xo_ref, ho_ref):
    na, nr = att_ref.shape[1], ret_ref.shape[1]
    y = (_dot(att_ref[...], w_ref[0, 0:na, :]) + _dot(ret_ref[...], w_ref[0, na:na + nr, :])
         + _dot(gla_ref[...], w_ref[0, na + nr:, :]))
    xn = x_ref[...] + gate_ref[0] * y
    xo_ref[...] = xn
    ho_ref[...] = (_rms(xn) * g_ref[...] * (1.0 + sc_ref[0]) + sh_ref[0]).astype(BF16)


def _outproj(att, ret, gla, w, layer, x2, gate, shift, scale, g, *, tm, rows_per_mod):
    m, d = x2.shape

    def rows(n):
        return pl.BlockSpec((tm, n), lambda i: (i, 0))

    def mod():
        return pl.BlockSpec((1, 1, d), lambda i: ((i * tm) // rows_per_mod, 0, 0))

    return pl.pallas_call(
        _outproj_kernel,
        grid=(m // tm,),
        in_specs=[rows(att.shape[1]), rows(ret.shape[1]), rows(gla.shape[1]),
                  pl.BlockSpec((1,) + w.shape[1:], lambda i: (layer, 0, 0), pipeline_mode=pl.Buffered(1)),
                  rows(d), mod(), mod(), mod(),
                  pl.BlockSpec((1, d), lambda i: (0, 0))],
        out_specs=[rows(d), rows(d)],
        out_shape=[jax.ShapeDtypeStruct((m, d), F32), jax.ShapeDtypeStruct((m, d), BF16)],
        compiler_params=_cparams(("arbitrary",)),
        name="out_proj",
    )(att, ret, gla, w, x2, gate, shift, scale, g)


FFN_HALO = 16


def _ffn_kernel(h_ref, hn_ref, hp_ref, wa_ref, wv_ref, cw_ref, cb_ref, wd_ref, x_ref, gate_ref, fg_ref,
                o_ref, hs_ref, *, tm, seq_len, final_norm):
    i = pl.program_id(0)
    f = pl.program_id(1)
    ext = tm + 2 * FFN_HALO

    @pl.when(f == 0)
    def _():
        hs_ref[0:tm, :] = h_ref[...]
        hs_ref[tm:tm + FFN_HALO, :] = hn_ref[...]
        hs_ref[tm + FFN_HALO:ext, :] = hp_ref[...]
        o_ref[...] = jnp.zeros_like(o_ref)

    a = _dot(hs_ref[...], wa_ref[0])
    pos = (i * tm + lax.broadcasted_iota(jnp.int32, (ext, 1), 0)) % seq_len
    prev = jnp.where(pos == 0, 0.0, pltpu.roll(a, 1, 0))
    nxt = jnp.where(pos == seq_len - 1, 0.0, pltpu.roll(a, ext - 1, 0))
    conv = prev * cw_ref[0:1, :] + a * cw_ref[1:2, :] + nxt * cw_ref[2:3, :] + cb_ref[...]
    v = _dot(h_ref[...], wv_ref[0])
    u = (_silu(conv[0:tm]) * v).astype(BF16)
    o_ref[...] += _dot(u, wd_ref[0])

    @pl.when(f == pl.num_programs(1) - 1)
    def _():
        xn = x_ref[...] + gate_ref[0] * o_ref[...]
        if final_norm:
            xn = _rms(xn) * fg_ref[...]
        o_ref[...] = xn


def _ffn(h2, w_up, conv_w, conv_b, w_down, layer, x2, gate, fg, *, tm, tf, seq_len, rows_per_mod, final_norm):
    m, d = x2.shape
    nf = D_FF // tf
    hb = tm // FFN_HALO
    last = m // FFN_HALO - 1
    return pl.pallas_call(
        functools.partial(_ffn_kernel, tm=tm, seq_len=seq_len, final_norm=final_norm),
        grid=(m // tm, nf),
        in_specs=[
            pl.BlockSpec((tm, d), lambda i, f: (i, 0)),
            pl.BlockSpec((FFN_HALO, d), lambda i, f: (jnp.minimum((i + 1) * hb, last), 0)),
            pl.BlockSpec((FFN_HALO, d), lambda i, f: (jnp.maximum(i * hb - 1, 0), 0)),
            pl.BlockSpec((1, d, tf), lambda i, f: (layer, 0, f)),
            pl.BlockSpec((1, d, tf), lambda i, f: (layer, 0, nf + f)),
            pl.BlockSpec((3, tf), lambda i, f: (0, f)),
            pl.BlockSpec((1, tf), lambda i, f: (0, f)),
            pl.BlockSpec((1, tf, d), lambda i, f: (layer, f, 0)),
            pl.BlockSpec((tm, d), lambda i, f: (i, 0)),
            pl.BlockSpec((1, 1, d), lambda i, f: ((i * tm) // rows_per_mod, 0, 0)),
            pl.BlockSpec((1, d), lambda i, f: (0, 0)),
        ],
        out_specs=pl.BlockSpec((tm, d), lambda i, f: (i, 0)),
        out_shape=jax.ShapeDtypeStruct((m, d), F32),
        scratch_shapes=[pltpu.VMEM((tm + 2 * FFN_HALO, d), BF16)],
        compiler_params=_cparams(("arbitrary", "arbitrary")),
        name="conv_glu",
    )(h2, h2, h2, w_up, w_up, conv_w, conv_b, w_down, x2, gate, fg)


def _rope_tables(n_tokens):
    rows = n_tokens // GRID_W
    row = jnp.repeat(jnp.arange(rows, dtype=F32), GRID_W)
    col = jnp.tile(jnp.arange(GRID_W, dtype=F32), rows)
    n_freq = HEAD_DIM // 4
    inv_freq = ROPE_THETA ** (-jnp.arange(n_freq, dtype=F32) / n_freq)
    ang = jnp.concatenate([row[:, None] * inv_freq, col[:, None] * inv_freq], axis=-1)
    cos, sin = jnp.cos(ang), jnp.sin(ang)
    return jnp.concatenate([cos, cos], axis=-1), jnp.concatenate([-sin, sin], axis=-1)


def kernel(x, c, ctx, c_ctx, ada_w, ada_b, norm1_g, w_in, q_norm_g, k_norm_g, ret_log_decay, ret_norm_g,
           gla_gate_up, gla_gate_b, gla_norm_g, w_out, norm2_g, w_up, conv_w, conv_b, w_down, final_norm_g):
    batch, lat_len, d = x.shape
    ctx_len = ctx.shape[1]
    depth = ada_w.shape[0]
    mod_rows = 16
    cc = jnp.concatenate([c, c_ctx[None], jnp.zeros((mod_rows - batch - 1, d), F32)], axis=0)
    mod = _ada(cc, ada_w, ada_b).reshape(depth, mod_rows, N_MOD, d)

    cos_l, sin_l = _rope_tables(lat_len)
    proj_tm = 512
    cos_c = jnp.ones((proj_tm, LANES), F32)
    sin_c = jnp.zeros((proj_tm, LANES), F32)

    xl = x.reshape(batch * lat_len, d)
    xc = ctx.reshape(batch * ctx_len, d)
    row = lambda v: v.reshape(1, -1)
    w_in_b, w_out_b, w_up_b, w_down_b = (w.astype(BF16) for w in (w_in, w_out, w_up, w_down))

    for l in range(depth):
        last = l == depth - 1
        ml = [mod[l, :batch, k].reshape(batch, 1, d) for k in range(N_MOD)]
        mc = [mod[l, batch, k].reshape(1, 1, d) for k in range(N_MOD)]
        up_pad = jnp.zeros((2, LANES, GLA_HEADS * GLA_DK), BF16)
        for n in range(2):
            up_pad = up_pad.at[n, n * GLA_GATE_RANK:(n + 1) * GLA_GATE_RANK].set(gla_gate_up[l, n].astype(BF16))

        zl = _proj(xl, ml[0], ml[1], row(norm1_g[l]), w_in_b, l, cos_l, sin_l, row(q_norm_g[l]), row(k_norm_g[l]),
                   tm=proj_tm, rows_per_mod=lat_len, rope_tiles=lat_len // proj_tm)
        zc = _proj(xc, mc[0], mc[1], row(norm1_g[l]), w_in_b, l, cos_c, sin_c, row(q_norm_g[l]), row(k_norm_g[l]),
                   tm=proj_tm, rows_per_mod=batch * ctx_len, rope_tiles=1)

        att_l = _attn(zl, [(zc, ctx_len), (zl, lat_len)], batch=batch, q_len=lat_len, tq=256)
        ret_c, ret_l = _ret(zc, zl, ret_log_decay[l], row(ret_norm_g[l]),
                            batch=batch, ctx_len=ctx_len, lat_len=lat_len)
        gla_c, gla_l = _gla(zc, zl, up_pad, gla_gate_b[l], row(gla_norm_g[l]),
                            batch=batch, ctx_len=ctx_len, lat_len=lat_len)

        xl, h2 = _outproj(att_l, ret_l, gla_l, w_out_b, l, xl, ml[2], ml[3], ml[4], row(norm2_g[l]),
                          tm=256, rows_per_mod=lat_len)
        xl = _ffn(h2, w_up_b, conv_w[l], row(conv_b[l]), w_down_b, l, xl, ml[5], row(final_norm_g),
                  tm=512, tf=512, seq_len=lat_len, rows_per_mod=lat_len, final_norm=last)

        if not last:
            att_c = _attn(zc, [(zc, ctx_len)], batch=batch, q_len=ctx_len, tq=ctx_len)
            xc, hc2 = _outproj(att_c, ret_c, gla_c, w_out_b, l, xc, mc[2], mc[3], mc[4], row(norm2_g[l]),
                               tm=256, rows_per_mod=batch * ctx_len)
            xc = _ffn(hc2, w_up_b, conv_w[l], row(conv_b[l]), w_down_b, l, xc, mc[5], row(final_norm_g),
                      tm=512, tf=512, seq_len=ctx_len, rows_per_mod=batch * ctx_len, final_norm=False)

    return xl.reshape(batch, lat_len, d)
```

```python
import functools

import numpy as np
import jax
import jax.numpy as jnp
from jax import lax
from jax.experimental import pallas as pl
from jax.experimental.pallas import tpu as pltpu

F32 = jnp.float32
BF16 = jnp.bfloat16

D_MODEL = 2048
HEAD_DIM = 128
GRID_W = 64
ATT_Q_HEADS = 8
ATT_KV_HEADS = 2
GQA = ATT_Q_HEADS // ATT_KV_HEADS
RET_HEADS = 4
RET_DIM = 128
RET_CHUNK = 128
GLA_HEADS = 4
GLA_DK = 64
GLA_DV = 128
GLA_GATE_RANK = 16
GLA_TAU = 16.0
GLA_CHUNK = 64
GLA_LEVELS = 6
D_FF = 5632
ROPE_THETA = 10000.0
N_MOD = 6
EPS = 1e-6

LANES = 128
N_IN = 5152
PROJ_TN = 256
N_SLOTS = -(-N_IN // LANES)
SLOT_AQ, SLOT_AK, SLOT_AV = 0, 8, 10
SLOT_RQ, SLOT_RK, SLOT_RV, SLOT_RG = 12, 16, 20, 24
SLOT_GQ, SLOT_GK, SLOT_GV, SLOT_GR, SLOT_GA = 28, 30, 32, 36, 40

VMEM_LIMIT = 52 * 1024 * 1024

_NT = (((1,), (1,)), ((), ()))
_TN = (((0,), (0,)), ((), ()))


def _cparams(sem, vmem_limit=VMEM_LIMIT):
    return pltpu.CompilerParams(dimension_semantics=sem, vmem_limit_bytes=vmem_limit)


def _dot(a, b):
    return jnp.dot(a, b, preferred_element_type=F32)


def _dot_nt(a, b):
    return lax.dot_general(a, b, _NT, preferred_element_type=F32)


def _dot_tn(a, b):
    return lax.dot_general(a, b, _TN, preferred_element_type=F32)


def _rows(start, size, align):
    if isinstance(start, int):
        return pl.ds(start, size)
    return pl.ds(pl.multiple_of(start, align), size)


def _rms(t):
    return t * lax.rsqrt(jnp.mean(t * t, axis=-1, keepdims=True) + EPS)


def _silu(t):
    return t * jax.nn.sigmoid(t)


def _ada_kernel(c_ref, w_ref, b_ref, o_ref):
    s = _silu(c_ref[...]).astype(BF16)
    o_ref[0] = _dot(s, w_ref[0].astype(BF16)) + b_ref[0]


def _ada(cc, ada_w, ada_b):
    depth, d, n = ada_w.shape
    rows = cc.shape[0]
    tn = 1024
    return pl.pallas_call(
        _ada_kernel,
        grid=(depth, n // tn),
        in_specs=[
            pl.BlockSpec((rows, d), lambda l, j: (0, 0)),
            pl.BlockSpec((1, d, tn), lambda l, j: (l, 0, j)),
            pl.BlockSpec((1, 1, tn), lambda l, j: (l, 0, j)),
        ],
        out_specs=pl.BlockSpec((1, rows, tn), lambda l, j: (l, 0, j)),
        out_shape=jax.ShapeDtypeStruct((depth, rows, n), F32),
        compiler_params=_cparams(("arbitrary", "arbitrary")),
        name="ada_mod",
    )(cc, ada_w, ada_b.reshape(depth, 1, n))


def _proj_kernel(x_ref, sh_ref, sc_ref, g_ref, w_ref, cos_ref, sin_ref, qg_ref, kg_ref, o_ref, vt_ref):
    y = _rms(x_ref[...]) * g_ref[...]
    h = (y * (1.0 + sc_ref[0]) + sh_ref[0]).astype(BF16)
    cos, sin = cos_ref[...], sin_ref[...]
    qg = qg_ref[...] * (HEAD_DIM ** -0.5)
    kg = kg_ref[...]

    def rope(t):
        return t * cos + pltpu.roll(t, HEAD_DIM // 2, 1) * sin

    def transform(slot, t):
        if slot < SLOT_AK:
            return rope(_rms(t) * qg)
        if slot < SLOT_AV:
            return rope(_rms(t) * kg)
        if SLOT_RQ <= slot < SLOT_RK:
            return rope(t)
        if SLOT_RK <= slot < SLOT_RV:
            return rope(t * (RET_DIM ** -0.5))
        if SLOT_GQ <= slot < SLOT_GK:
            return t * (GLA_DK ** -0.5)
        return t

    for c0 in range(0, N_IN, PROJ_TN):
        c1 = min(c0 + PROJ_TN, N_IN)
        z = _dot(h, w_ref[0, :, c0:c1])
        for s0 in range(c0, c1, LANES):
            slot, n = s0 // LANES, min(LANES, c1 - s0)
            t = transform(slot, z[:, s0 - c0:s0 - c0 + n])
            if SLOT_AV <= slot < SLOT_RQ:
                vt_ref[slot - SLOT_AV] = t.T.astype(BF16)
            t = t.astype(BF16)
            if n < LANES:
                o_ref[slot] = jnp.zeros(o_ref.shape[1:], BF16)
                o_ref[slot, :, 0:n] = t
            else:
                o_ref[slot] = t


def _proj(x2, shift, scale, g, w, layer, cosf, sinf, qg, kg, *, tm, rows_per_mod, rope_tiles):
    m, d = x2.shape
    return pl.pallas_call(
        _proj_kernel,
        grid=(m // tm,),
        in_specs=[
            pl.BlockSpec((tm, d), lambda i: (i, 0)),
            pl.BlockSpec((1, 1, d), lambda i: ((i * tm) // rows_per_mod, 0, 0)),
            pl.BlockSpec((1, 1, d), lambda i: ((i * tm) // rows_per_mod, 0, 0)),
            pl.BlockSpec((1, d), lambda i: (0, 0)),
            pl.BlockSpec((1, d, N_IN), lambda i: (layer, 0, 0), pipeline_mode=pl.Buffered(1)),
            pl.BlockSpec((tm, LANES), lambda i: (i % rope_tiles, 0)),
            pl.BlockSpec((tm, LANES), lambda i: (i % rope_tiles, 0)),
            pl.BlockSpec((1, LANES), lambda i: (0, 0)),
            pl.BlockSpec((1, LANES), lambda i: (0, 0)),
        ],
        out_specs=[pl.BlockSpec((N_SLOTS, tm, LANES), lambda i: (0, i, 0)),
                   pl.BlockSpec((ATT_KV_HEADS, HEAD_DIM, tm), lambda i: (0, 0, i))],
        out_shape=[jax.ShapeDtypeStruct((N_SLOTS, m, LANES), BF16),
                   jax.ShapeDtypeStruct((ATT_KV_HEADS, HEAD_DIM, m), BF16)],
        compiler_params=_cparams(("arbitrary",)),
        name="norm_proj",
    )(x2, shift, scale, g, w, cosf, sinf, qg, kg)


ATT_KEY_BLOCK = 128


def _attn_kernel(*refs, n_kv):
    q_ref = refs[0]
    kv_refs = refs[1:1 + 2 * n_kv]
    o_ref = refs[1 + 2 * n_kv]
    m, l, acc = [None] * GQA, [None] * GQA, [None] * GQA
    units = [(s, j0, g) for s in range(n_kv) for j0 in range(0, kv_refs[2 * s].shape[1], ATT_KEY_BLOCK)
             for g in range(GQA)]

    def scores(unit):
        s, j0, g = unit
        return _dot_nt(kv_refs[2 * s][0, j0:j0 + ATT_KEY_BLOCK, :], q_ref[g])

    st_next = scores(units[0])
    for u, (s, j0, g) in enumerate(units):
        st = st_next
        if u + 1 < len(units):
            st_next = scores(units[u + 1])
        vt = kv_refs[2 * s + 1][0, :, j0:j0 + ATT_KEY_BLOCK]
        mb = jnp.max(st, axis=0, keepdims=True)
        if m[g] is None:
            m[g] = mb
            p = jnp.exp(st - mb)
            l[g] = jnp.sum(p, axis=0, keepdims=True)
            acc[g] = _dot(vt, p.astype(BF16))
        else:
            m_new = jnp.maximum(m[g], mb)
            alpha = jnp.exp(m[g] - m_new)
            p = jnp.exp(st - m_new)
            l[g] = alpha * l[g] + jnp.sum(p, axis=0, keepdims=True)
            acc[g] = alpha * acc[g] + _dot(vt, p.astype(BF16))
            m[g] = m_new
    for g in range(GQA):
        o_ref[:, g * HEAD_DIM:(g + 1) * HEAD_DIM] = (acc[g] / l[g]).T.astype(BF16)


def _attn(zq, kv_sources, *, batch, q_len, tq):
    nq = q_len // tq
    in_specs = [pl.BlockSpec((GQA, tq, HEAD_DIM), lambda b, k, i: (k, b * nq + i, 0))]
    args = [zq]
    for z, vt, kv_len in kv_sources:
        in_specs.append(pl.BlockSpec((1, kv_len, HEAD_DIM), lambda b, k, i: (SLOT_AK + k, b, 0)))
        in_specs.append(pl.BlockSpec((1, HEAD_DIM, kv_len), lambda b, k, i: (k, 0, b)))
        args += [z, vt]
    return pl.pallas_call(
        functools.partial(_attn_kernel, n_kv=len(kv_sources)),
        grid=(batch, ATT_KV_HEADS, nq),
        in_specs=in_specs,
        out_specs=pl.BlockSpec((tq, GQA * HEAD_DIM), lambda b, k, i: (b * nq + i, k)),
        out_shape=jax.ShapeDtypeStruct((batch * q_len, ATT_Q_HEADS * HEAD_DIM), BF16),
        compiler_params=_cparams(("arbitrary", "arbitrary", "arbitrary")),
        name="attention",
    )(*args)


def _ret_kernel(dec_ref, qc_ref, kc_ref, vc_ref, gc_ref, ql_ref, kl_ref, vl_ref, gl_ref, ng_ref,
                oc_ref, ol_ref, kv_ref, st_ref, *, nc_ctx, nc_lat):
    h = pl.program_id(1)
    C = RET_CHUNK
    lg_f, lg_b = dec_ref[0, h], dec_ref[1, h]
    rel = lax.broadcasted_iota(jnp.int32, (C, C), 0) - lax.broadcasted_iota(jnp.int32, (C, C), 1)
    dmat = (jnp.where(rel >= 0, jnp.exp(lg_f * jnp.maximum(rel, 0).astype(F32)), 0.0)
            + jnp.where(rel <= 0, jnp.exp(lg_b * jnp.maximum(-rel, 0).astype(F32)), 0.0))
    row = lax.broadcasted_iota(jnp.int32, (C, 1), 0).astype(F32)
    qd_f, qd_b = jnp.exp(lg_f * (row + 1.0)), jnp.exp(lg_b * (C - row))
    kd_f, kd_b = jnp.exp(lg_f * (C - 1.0 - row)), jnp.exp(lg_b * row)
    cd_f, cd_b = jnp.exp(lg_f * C), jnp.exp(lg_b * C)

    chunks = [(qc_ref, kc_ref, vc_ref, gc_ref, oc_ref, n) for n in range(nc_ctx)]
    chunks += [(ql_ref, kl_ref, vl_ref, gl_ref, ol_ref, n) for n in range(nc_lat)]
    order_b = list(range(nc_ctx - 1, -1, -1)) + list(range(nc_ctx + nc_lat - 1, nc_ctx - 1, -1))

    def decayed(x_ref, n, dec_f, dec_b):
        x = x_ref[0, n * C:(n + 1) * C, :].astype(F32)
        return jnp.concatenate([(x * dec_f).astype(BF16), (x * dec_b).astype(BF16)], axis=1)

    for c, (_, k_ref, v_ref, _, _, n) in enumerate(chunks):
        kv_ref[c] = _dot_tn(decayed(k_ref, n, kd_f, kd_b), v_ref[0, n * C:(n + 1) * C, :])

    s = jnp.zeros((C, RET_DIM), F32)
    for c in range(len(chunks)):
        st_ref[c, 0:C, :] = s.astype(BF16)
        s = cd_f * s + kv_ref[c, 0:C, :]
    s = jnp.zeros((C, RET_DIM), F32)
    for c in order_b:
        st_ref[c, C:2 * C, :] = s.astype(BF16)
        s = cd_b * s + kv_ref[c, C:2 * C, :]

    for c, (q_ref, k_ref, v_ref, g_ref, o_ref, n) in enumerate(chunks):
        sl = slice(n * C, (n + 1) * C)
        v = v_ref[0, sl, :]
        att = _dot_nt(q_ref[0, sl, :], k_ref[0, sl, :]) * dmat
        o = _dot(att.astype(BF16), v) + _dot(decayed(q_ref, n, qd_f, qd_b), st_ref[c])
        y = _rms(o) * ng_ref[...]
        o_ref[sl, :] = (y * _silu(g_ref[0, sl, :].astype(F32))).astype(BF16)


def _ret(zc, zl, decay, ng, *, batch, ctx_len, lat_len):
    def slot(base, n):
        return pl.BlockSpec((1, n, RET_DIM), lambda b, h: (base + h, b, 0))

    n_chunks = (ctx_len + lat_len) // RET_CHUNK
    specs = [pl.BlockSpec(memory_space=pltpu.SMEM)]
    specs += [slot(s, ctx_len) for s in (SLOT_RQ, SLOT_RK, SLOT_RV, SLOT_RG)]
    specs += [slot(s, lat_len) for s in (SLOT_RQ, SLOT_RK, SLOT_RV, SLOT_RG)]
    specs += [pl.BlockSpec((1, RET_DIM), lambda b, h: (0, 0))]
    return pl.pallas_call(
        functools.partial(_ret_kernel, nc_ctx=ctx_len // RET_CHUNK, nc_lat=lat_len // RET_CHUNK),
        grid=(batch, RET_HEADS),
        in_specs=specs,
        out_specs=[pl.BlockSpec((ctx_len, RET_DIM), lambda b, h: (b, h)),
                   pl.BlockSpec((lat_len, RET_DIM), lambda b, h: (b, h))],
        out_shape=[jax.ShapeDtypeStruct((batch * ctx_len, RET_HEADS * RET_DIM), BF16),
                   jax.ShapeDtypeStruct((batch * lat_len, RET_HEADS * RET_DIM), BF16)],
        scratch_shapes=[pltpu.VMEM((n_chunks, 2 * RET_DIM, RET_DIM), F32),
                        pltpu.VMEM((n_chunks, 2 * RET_DIM, RET_DIM), BF16)],
        compiler_params=_cparams(("arbitrary", "arbitrary")),
        name="retention",
    )(decay, zc, zc, zc, zc, zl, zl, zl, zl, ng)


GLA_SLAB = 256
GLA_UNROLL = 4


def _gla_level_table():
    i = np.arange(GLA_CHUNK)[:, None]
    j = np.arange(GLA_CHUNK)[None, :]
    x = np.maximum(i ^ j, 1)
    lvl = np.floor(np.log2(x)).astype(np.int32)
    lvl = np.where(i == j, GLA_LEVELS, np.where(i > j, lvl, -1)).astype(np.int32)
    return np.concatenate([lvl, lvl.T], axis=1)


def _bcast_block_row(x, blk, r):
    rows = x.shape[0]
    if blk >= 8:
        x3 = x.reshape(rows // blk, blk, LANES)
        return jnp.broadcast_to(x3[:, r:r + 1, :], x3.shape).reshape(rows, LANES)
    pos = lax.broadcasted_iota(jnp.int32, x.shape, 0) & (blk - 1)
    out = x
    for p in range(blk):
        if p != r:
            out = jnp.where(pos == p, pltpu.roll(x, (p - r) % rows, 0), out)
    return out


def _gla_kernel(lvl_ref, up_ref, gb_ref, ng_ref,
                qc_ref, kc_ref, vc_ref, gc_ref, ac_ref, ql_ref, kl_ref, vl_ref, gl_ref, al_ref,
                oc_ref, ol_ref,
                xs_ref, qd_ref, kd_ref, qh_ref, kh_ref, tot_ref, vv_ref, kv_ref, st_ref, *, ctx_len, lat_len):
    C = GLA_CHUNK
    streams = ((qc_ref, kc_ref, vc_ref, ac_ref, 0, ctx_len), (ql_ref, kl_ref, vl_ref, al_ref, ctx_len, lat_len))
    nc_ctx, nc_lat = ctx_len // C, lat_len // C
    n_chunks = nc_ctx + nc_lat
    pos = lax.broadcasted_iota(jnp.int32, (GLA_SLAB, LANES), 0)
    cpos = pos & (C - 1)
    fwd = lax.broadcasted_iota(jnp.int32, (GLA_SLAB, LANES), 1) < GLA_DK
    fwd_c = lax.broadcasted_iota(jnp.int32, (C, LANES), 1) < GLA_DK
    fwd_i = fwd.astype(jnp.int32)
    lvl = lvl_ref[...]

    for hh in range(2):
        own = fwd if hh == 0 else jnp.logical_not(fwd)

        def prep(i, carry, q_ref, k_ref, v_ref, a_ref, off, hh=hh, own=own):
            src = _rows(i * GLA_SLAB, GLA_SLAB, GLA_SLAB)
            dst = _rows(off + i * GLA_SLAB, GLA_SLAB, C)
            logit = _dot(a_ref[0, src, :], up_ref[hh]) + gb_ref[hh]
            la = (jnp.minimum(logit, 0.0) - jnp.log1p(jnp.exp(-jnp.abs(logit)))) * (1.0 / GLA_TAU)
            cum = la
            for k in range(GLA_LEVELS):
                sh = 1 << k
                before = jnp.where(cpos >= sh, pltpu.roll(cum, sh, 0), 0.0)
                after = jnp.where(cpos < C - sh, pltpu.roll(cum, GLA_SLAB - sh, 0), 0.0)
                cum = cum + jnp.where(fwd, before, after)
            tot = jnp.where(fwd, _bcast_block_row(cum, C, C - 1), _bcast_block_row(cum, C, 0))
            q2 = q_ref[0, src, :].astype(F32)
            k2 = k_ref[0, src, :].astype(F32)
            q = jnp.where(own, q2, pltpu.roll(q2, GLA_DK, 1))
            k = jnp.where(own, k2, pltpu.roll(k2, GLA_DK, 1))
            qd_ref[dst, :] = q.astype(BF16)
            kd_ref[dst, :] = k.astype(BF16)
            qh_ref[dst, :] = (q * jnp.exp(cum)).astype(BF16)
            kh_ref[dst, :] = (k * jnp.exp(tot - cum)).astype(BF16)
            tot_ref[dst, :] = tot
            vv_ref[dst, :] = v_ref[hh, src, :]
            for lev in range(GLA_LEVELS):
                s = 1 << lev
                edge = jnp.where(fwd, _bcast_block_row(cum, 2 * s, s - 1), _bcast_block_row(cum, 2 * s, s))
                late = (pos >> lev) & 1
                xs_ref[lev, dst, :] = (jnp.where(late == fwd_i, q, k) * jnp.exp(-jnp.abs(cum - edge))).astype(BF16)
            return carry

        for q_ref, k_ref, v_ref, a_ref, off, n in streams:
            lax.fori_loop(0, n // GLA_SLAB,
                          functools.partial(prep, q_ref=q_ref, k_ref=k_ref, v_ref=v_ref, a_ref=a_ref, off=off), 0)

        def kv_body(c, carry):
            rows = _rows(c * C, C, C)
            kv_ref[c] = _dot_tn(vv_ref[rows, :], kh_ref[rows, :])
            return carry

        lax.fori_loop(0, n_chunks, kv_body, 0, unroll=GLA_UNROLL)

        def scan(order_of, lanes):
            def body(n, s):
                c = order_of(n)
                st_ref[c, :, lanes] = s[:, lanes].astype(BF16)
                return s * jnp.exp(tot_ref[pl.ds(c * C, 1), :]) + kv_ref[c]
            lax.fori_loop(0, n_chunks, body, jnp.zeros((GLA_DV, LANES), F32))

        scan(lambda n: n, slice(0, GLA_DK))
        scan(lambda n: jnp.where(n < nc_ctx, nc_ctx - 1 - n, n_chunks + nc_ctx - 1 - n), slice(GLA_DK, LANES))

        def block_diag(x):
            zero = jnp.zeros_like(x)
            return jnp.concatenate([jnp.where(fwd_c, x, zero), jnp.where(fwd_c, zero, x)], axis=0)

        def out_chunk(c, g_ref, o_ref, local, hh=hh):
            rows = _rows(c * C, C, C)
            a = jnp.where(lvl == GLA_LEVELS, _dot_nt(qd_ref[rows, :], block_diag(kd_ref[rows, :])), 0.0)
            for lev in range(GLA_LEVELS):
                x = xs_ref[lev, rows, :]
                a = jnp.where(lvl == lev, _dot_nt(x, block_diag(x)), a)
            v = vv_ref[rows, :]
            o = _dot(a.astype(BF16), jnp.concatenate([v, v], axis=0)) + _dot_nt(qh_ref[rows, :], st_ref[c])
            y = _rms(o) * ng_ref[...]
            out_rows = _rows(local * C, C, C)
            y = y * _silu(g_ref[hh, out_rows, :].astype(F32))
            o_ref[out_rows, hh * GLA_DV:(hh + 1) * GLA_DV] = y.astype(BF16)

        for n in range(nc_ctx):
            out_chunk(n, gc_ref, oc_ref, n)

        def lat_body(n, carry, out_chunk=out_chunk):
            out_chunk(nc_ctx + n, gl_ref, ol_ref, n)
            return carry

        lax.fori_loop(0, nc_lat, lat_body, 0, unroll=GLA_UNROLL)


def _gla_gate_params(gate_up, gate_b):
    r = GLA_GATE_RANK
    gu = gate_up.astype(BF16).reshape(2, r, GLA_HEADS, GLA_DK).transpose(2, 0, 1, 3)
    up = jnp.zeros((GLA_HEADS, LANES, 2 * GLA_DK), BF16)
    up = up.at[:, 0:r, 0:GLA_DK].set(gu[:, 0]).at[:, r:2 * r, GLA_DK:].set(gu[:, 1])
    gb = gate_b.reshape(2, GLA_HEADS, GLA_DK).transpose(1, 0, 2).reshape(GLA_HEADS, 1, 2 * GLA_DK)
    return up, gb


def _gla(zc, zl, up_heads, gate_b_heads, ng, *, batch, ctx_len, lat_len):
    total = ctx_len + lat_len
    n_chunks = total // GLA_CHUNK

    def pair(base, n):
        return pl.BlockSpec((1, n, LANES), lambda b, p: (base + p, b, 0))

    def two(base, n):
        return pl.BlockSpec((2, n, LANES), lambda b, p: (base // 2 + p, b, 0))

    def one(n):
        return pl.BlockSpec((1, n, LANES), lambda b, p: (SLOT_GA, b, 0))

    specs = [
        pl.BlockSpec((GLA_CHUNK, 2 * GLA_CHUNK), lambda b, p: (0, 0)),
        pl.BlockSpec((2, LANES, LANES), lambda b, p: (p, 0, 0)),
        pl.BlockSpec((2, 1, LANES), lambda b, p: (p, 0, 0)),
        pl.BlockSpec((1, GLA_DV), lambda b, p: (0, 0)),
    ]
    for n in (ctx_len, lat_len):
        specs += [pair(SLOT_GQ, n), pair(SLOT_GK, n), two(SLOT_GV, n), two(SLOT_GR, n), one(n)]
    return pl.pallas_call(
        functools.partial(_gla_kernel, ctx_len=ctx_len, lat_len=lat_len),
        grid=(batch, GLA_HEADS // 2),
        in_specs=specs,
        out_specs=[pl.BlockSpec((ctx_len, 2 * GLA_DV), lambda b, p: (b, p)),
                   pl.BlockSpec((lat_len, 2 * GLA_DV), lambda b, p: (b, p))],
        out_shape=[jax.ShapeDtypeStruct((batch * ctx_len, GLA_HEADS * GLA_DV), BF16),
                   jax.ShapeDtypeStruct((batch * lat_len, GLA_HEADS * GLA_DV), BF16)],
        scratch_shapes=[
            pltpu.VMEM((GLA_LEVELS, total, LANES), BF16),
            pltpu.VMEM((total, LANES), BF16),
            pltpu.VMEM((total, LANES), BF16),
            pltpu.VMEM((total, LANES), BF16),
            pltpu.VMEM((total, LANES), BF16),
            pltpu.VMEM((total, LANES), F32),
            pltpu.VMEM((total, GLA_DV), BF16),
            pltpu.VMEM((n_chunks, GLA_DV, LANES), F32),
            pltpu.VMEM((n_chunks, GLA_DV, LANES), BF16),
        ],
        compiler_params=_cparams(("arbitrary", "arbitrary")),
        name="gla",
    )(jnp.asarray(_gla_level_table()), up_heads, gate_b_heads, ng,
      zc, zc, zc, zc, zc, zl, zl, zl, zl, zl)


def _outproj_kernel(att_ref, ret_ref, gla_ref, w_ref, x_ref, gate_ref, sh_ref, sc_ref, g_ref, xo_ref, ho_ref):
    na, nr = att_ref.shape[1], ret_ref.shape[1]
    y = (_dot(att_ref[...], w_ref[0, 0:na, :]) + _dot(ret_ref[...], w_ref[0, na:na + nr, :])
         + _dot(gla_ref[...], w_ref[0, na + nr:, :]))
    xn = x_ref[...] + gate_ref[0] * y
    xo_ref[...] = xn
    ho_ref[...] = (_rms(xn) * g_ref[...] * (1.0 + sc_ref[0]) + sh_ref[0]).astype(BF16)


def _outproj(att, ret, gla, w, layer, x2, gate, shift, scale, g, *, tm, rows_per_mod):
    m, d = x2.shape

    def rows(n):
        return pl.BlockSpec((tm, n), lambda i: (i, 0))

    def mod():
        return pl.BlockSpec((1, 1, d), lambda i: ((i * tm) // rows_per_mod, 0, 0))

    return pl.pallas_call(
        _outproj_kernel,
        grid=(m // tm,),
        in_specs=[rows(att.shape[1]), rows(ret.shape[1]), rows(gla.shape[1]),
                  pl.BlockSpec((1,) + w.shape[1:], lambda i: (layer, 0, 0), pipeline_mode=pl.Buffered(1)),
                  rows(d), mod(), mod(), mod(),
                  pl.BlockSpec((1, d), lambda i: (0, 0))],
        out_specs=[rows(d), rows(d)],
        out_shape=[jax.ShapeDtypeStruct((m, d), F32), jax.ShapeDtypeStruct((m, d), BF16)],
        compiler_params=_cparams(("arbitrary",)),
        name="out_proj",
    )(att, ret, gla, w, x2, gate, shift, scale, g)


FFN_HALO = 16
FFN_TF = 512
FFN_TM = 1024
FFN_VMEM_LIMIT = 58 * 1024 * 1024


def _ffn_kernel(h_ref, hn_ref, hp_ref, wa_ref, wv_ref, cw_ref, cb_ref, wd_ref, x_ref, gate_ref, fg_ref,
                o_ref, hs_ref, *, tm, seq_len, final_norm):
    i = pl.program_id(0)
    f = pl.program_id(1)
    ext = tm + 2 * FFN_HALO

    @pl.when(f == 0)
    def _():
        hs_ref[0:tm, :] = h_ref[...]
        hs_ref[tm:tm + FFN_HALO, :] = hn_ref[...]
        hs_ref[tm + FFN_HALO:ext, :] = hp_ref[...]
        o_ref[...] = jnp.zeros_like(o_ref)

    a = _dot(hs_ref[...], wa_ref[0])
    pos = (i * tm + lax.broadcasted_iota(jnp.int32, (ext, 1), 0)) % seq_len
    prev = jnp.where(pos == 0, 0.0, pltpu.roll(a, 1, 0))
    nxt = jnp.where(pos == seq_len - 1, 0.0, pltpu.roll(a, ext - 1, 0))
    conv = prev * cw_ref[0:1, :] + a * cw_ref[1:2, :] + nxt * cw_ref[2:3, :] + cb_ref[...]
    v = _dot(h_ref[...], wv_ref[0])
    u = (_silu(conv[0:tm]) * v).astype(BF16)
    o_ref[...] += _dot(u, wd_ref[0])

    @pl.when(f == pl.num_programs(1) - 1)
    def _():
        xn = x_ref[...] + gate_ref[0] * o_ref[...]
        if final_norm:
            xn = _rms(xn) * fg_ref[...]
        o_ref[...] = xn


def _ffn(h2, w_up, conv_w, conv_b, w_down, layer, x2, gate, fg, *, tm, seq_len, rows_per_mod, final_norm):
    m, d = x2.shape
    tf = FFN_TF
    nf = D_FF // tf
    hb = tm // FFN_HALO
    last = m // FFN_HALO - 1
    return pl.pallas_call(
        functools.partial(_ffn_kernel, tm=tm, seq_len=seq_len, final_norm=final_norm),
        grid=(m // tm, nf),
        in_specs=[
            pl.BlockSpec((tm, d), lambda i, f: (i, 0), pipeline_mode=pl.Buffered(1)),
            pl.BlockSpec((FFN_HALO, d), lambda i, f: (jnp.minimum((i + 1) * hb, last), 0)),
            pl.BlockSpec((FFN_HALO, d), lambda i, f: (jnp.maximum(i * hb - 1, 0), 0)),
            pl.BlockSpec((1, d, tf), lambda i, f: (layer, 0, f)),
            pl.BlockSpec((1, d, tf), lambda i, f: (layer, 0, nf + f)),
            pl.BlockSpec((3, tf), lambda i, f: (0, f)),
            pl.BlockSpec((1, tf), lambda i, f: (0, f)),
            pl.BlockSpec((1, tf, d), lambda i, f: (layer, f, 0)),
            pl.BlockSpec((tm, d), lambda i, f: (i, 0), pipeline_mode=pl.Buffered(1)),
            pl.BlockSpec((1, 1, d), lambda i, f: ((i * tm) // rows_per_mod, 0, 0)),
            pl.BlockSpec((1, d), lambda i, f: (0, 0)),
        ],
        out_specs=pl.BlockSpec((tm, d), lambda i, f: (i, 0)),
        out_shape=jax.ShapeDtypeStruct((m, d), F32),
        scratch_shapes=[pltpu.VMEM((tm + 2 * FFN_HALO, d), BF16)],
        compiler_params=_cparams(("arbitrary", "arbitrary"), FFN_VMEM_LIMIT),
        name="conv_glu",
    )(h2, h2, h2, w_up, w_up, conv_w, conv_b, w_down, x2, gate, fg)


def _rope_tables(n_tokens):
    rows = n_tokens // GRID_W
    row = jnp.repeat(jnp.arange(rows, dtype=F32), GRID_W)
    col = jnp.tile(jnp.arange(GRID_W, dtype=F32), rows)
    n_freq = HEAD_DIM // 4
    inv_freq = ROPE_THETA ** (-jnp.arange(n_freq, dtype=F32) / n_freq)
    ang = jnp.concatenate([row[:, None] * inv_freq, col[:, None] * inv_freq], axis=-1)
    cos, sin = jnp.cos(ang), jnp.sin(ang)
    return jnp.concatenate([cos, cos], axis=-1), jnp.concatenate([-sin, sin], axis=-1)


def kernel(x, c, ctx, c_ctx, ada_w, ada_b, norm1_g, w_in, q_norm_g, k_norm_g, ret_log_decay, ret_norm_g,
           gla_gate_up, gla_gate_b, gla_norm_g, w_out, norm2_g, w_up, conv_w, conv_b, w_down, final_norm_g):
    batch, lat_len, d = x.shape
    ctx_len = ctx.shape[1]
    depth = ada_w.shape[0]
    mod_rows = 16
    cc = jnp.concatenate([c, c_ctx[None], jnp.zeros((mod_rows - batch - 1, d), F32)], axis=0)
    mod = _ada(cc, ada_w, ada_b).reshape(depth, mod_rows, N_MOD, d)

    cos_l, sin_l = _rope_tables(lat_len)
    proj_tm = 512
    cos_c = jnp.ones((proj_tm, LANES), F32)
    sin_c = jnp.zeros((proj_tm, LANES), F32)

    xl = x.reshape(batch * lat_len, d)
    xc = ctx.reshape(batch * ctx_len, d)
    row = lambda v: v.reshape(1, -1)
    w_in_b, w_out_b, w_up_b, w_down_b = (w.astype(BF16) for w in (w_in, w_out, w_up, w_down))

    for l in range(depth):
        last = l == depth - 1
        ml = [mod[l, :batch, k].reshape(batch, 1, d) for k in range(N_MOD)]
        mc = [mod[l, batch, k].reshape(1, 1, d) for k in range(N_MOD)]
        up_heads, gate_b_heads = _gla_gate_params(gla_gate_up[l], gla_gate_b[l])

        zl, vtl = _proj(xl, ml[0], ml[1], row(norm1_g[l]), w_in_b, l, cos_l, sin_l, row(q_norm_g[l]), row(k_norm_g[l]),
                   tm=proj_tm, rows_per_mod=lat_len, rope_tiles=lat_len // proj_tm)
        zc, vtc = _proj(xc, mc[0], mc[1], row(norm1_g[l]), w_in_b, l, cos_c, sin_c, row(q_norm_g[l]), row(k_norm_g[l]),
                   tm=proj_tm, rows_per_mod=batch * ctx_len, rope_tiles=1)

        att_l = _attn(zl, [(zc, vtc, ctx_len), (zl, vtl, lat_len)], batch=batch, q_len=lat_len, tq=256)
        ret_c, ret_l = _ret(zc, zl, ret_log_decay[l], row(ret_norm_g[l]),
                            batch=batch, ctx_len=ctx_len, lat_len=lat_len)
        gla_c, gla_l = _gla(zc, zl, up_heads, gate_b_heads, row(gla_norm_g[l]),
                            batch=batch, ctx_len=ctx_len, lat_len=lat_len)

        xl, h2 = _outproj(att_l, ret_l, gla_l, w_out_b, l, xl, ml[2], ml[3], ml[4], row(norm2_g[l]),
                          tm=256, rows_per_mod=lat_len)
        xl = _ffn(h2, w_up_b, conv_w[l], row(conv_b[l]), w_down_b, l, xl, ml[5], row(final_norm_g),
                  tm=FFN_TM, seq_len=lat_len, rows_per_mod=lat_len, final_norm=last)

        if not last:
            att_c = _attn(zc, [(zc, vtc, ctx_len)], batch=batch, q_len=ctx_len, tq=ctx_len)
            xc, hc2 = _outproj(att_c, ret_c, gla_c, w_out_b, l, xc, mc[2], mc[3], mc[4], row(norm2_g[l]),
                               tm=256, rows_per_mod=batch * ctx_len)
            xc = _ffn(hc2, w_up_b, conv_w[l], row(conv_b[l]), w_down_b, l, xc, mc[5], row(final_norm_g),
                      tm=FFN_TM, seq_len=ctx_len, rows_per_mod=batch * ctx_len, final_norm=False)

    return xl.reshape(batch, lat_len, d)
```

```python
import functools

import numpy as np
import jax
import jax.numpy as jnp
from jax import lax
from jax.experimental import pallas as pl
from jax.experimental.pallas import tpu as pltpu

F32 = jnp.float32
BF16 = jnp.bfloat16

D_MODEL = 2048
HEAD_DIM = 128
GRID_W = 64
ATT_Q_HEADS = 8
ATT_KV_HEADS = 2
GQA = ATT_Q_HEADS // ATT_KV_HEADS
RET_HEADS = 4
RET_DIM = 128
RET_CHUNK = 128
GLA_HEADS = 4
GLA_DK = 64
GLA_DV = 128
GLA_GATE_RANK = 16
GLA_TAU = 16.0
GLA_CHUNK = 64
GLA_LEVELS = 6
D_FF = 5632
ROPE_THETA = 10000.0
N_MOD = 6
EPS = 1e-6

LOG2_E = 1.4426950408889634

LANES = 128
N_IN = 5152
PROJ_TN = 256
N_SLOTS = -(-N_IN // LANES)
SLOT_AQ, SLOT_AK, SLOT_AV = 0, 8, 10
SLOT_RQ, SLOT_RK, SLOT_RV, SLOT_RG = 12, 16, 20, 24
SLOT_GQ, SLOT_GK, SLOT_GV, SLOT_GR, SLOT_GA = 28, 30, 32, 36, 40

VMEM_LIMIT = 52 * 1024 * 1024

_NT = (((1,), (1,)), ((), ()))
_TN = (((0,), (0,)), ((), ()))


def _cparams(sem, vmem_limit=VMEM_LIMIT):
    return pltpu.CompilerParams(dimension_semantics=sem, vmem_limit_bytes=vmem_limit)


def _dot(a, b):
    return jnp.dot(a, b, preferred_element_type=F32)


def _dot_nt(a, b):
    return lax.dot_general(a, b, _NT, preferred_element_type=F32)


def _dot_tn(a, b):
    return lax.dot_general(a, b, _TN, preferred_element_type=F32)


def _rows(start, size, align):
    if isinstance(start, int):
        return pl.ds(start, size)
    return pl.ds(pl.multiple_of(start, align), size)


def _rms(t):
    return t * lax.rsqrt(jnp.mean(t * t, axis=-1, keepdims=True) + EPS)


def _silu(t):
    return t * jax.nn.sigmoid(t)


def _ada_kernel(c_ref, w_ref, b_ref, o_ref):
    s = _silu(c_ref[...]).astype(BF16)
    o_ref[0] = _dot(s, w_ref[0].astype(BF16)) + b_ref[0]


def _ada(cc, ada_w, ada_b):
    depth, d, n = ada_w.shape
    rows = cc.shape[0]
    tn = 1024
    return pl.pallas_call(
        _ada_kernel,
        grid=(depth, n // tn),
        in_specs=[
            pl.BlockSpec((rows, d), lambda l, j: (0, 0)),
            pl.BlockSpec((1, d, tn), lambda l, j: (l, 0, j)),
            pl.BlockSpec((1, 1, tn), lambda l, j: (l, 0, j)),
        ],
        out_specs=pl.BlockSpec((1, rows, tn), lambda l, j: (l, 0, j)),
        out_shape=jax.ShapeDtypeStruct((depth, rows, n), F32),
        compiler_params=_cparams(("arbitrary", "arbitrary")),
        name="ada_mod",
    )(cc, ada_w, ada_b.reshape(depth, 1, n))


def _proj_kernel(x_ref, sh_ref, sc_ref, g_ref, w_ref, cos_ref, sin_ref, qg_ref, kg_ref, o_ref, vt_ref):
    y = _rms(x_ref[...]) * g_ref[...]
    h = (y * (1.0 + sc_ref[0]) + sh_ref[0]).astype(BF16)
    cos, sin = cos_ref[...], sin_ref[...]
    qg = qg_ref[...] * (HEAD_DIM ** -0.5 * LOG2_E)
    kg = kg_ref[...]

    def rope(t):
        return t * cos + pltpu.roll(t, HEAD_DIM // 2, 1) * sin

    def transform(slot, t):
        if slot < SLOT_AK:
            return rope(_rms(t) * qg)
        if slot < SLOT_AV:
            return rope(_rms(t) * kg)
        if SLOT_RQ <= slot < SLOT_RK:
            return rope(t)
        if SLOT_RK <= slot < SLOT_RV:
            return rope(t * (RET_DIM ** -0.5))
        if SLOT_GQ <= slot < SLOT_GK:
            return t * (GLA_DK ** -0.5)
        return t

    for c0 in range(0, N_IN, PROJ_TN):
        c1 = min(c0 + PROJ_TN, N_IN)
        z = _dot(h, w_ref[0, :, c0:c1])
        for s0 in range(c0, c1, LANES):
            slot, n = s0 // LANES, min(LANES, c1 - s0)
            t = transform(slot, z[:, s0 - c0:s0 - c0 + n])
            if SLOT_AV <= slot < SLOT_RQ:
                vt_ref[slot - SLOT_AV] = t.T.astype(BF16)
            t = t.astype(BF16)
            if n < LANES:
                o_ref[slot] = jnp.zeros(o_ref.shape[1:], BF16)
                o_ref[slot, :, 0:n] = t
            else:
                o_ref[slot] = t


def _proj(x2, shift, scale, g, w, layer, cosf, sinf, qg, kg, *, tm, rows_per_mod, rope_tiles):
    m, d = x2.shape
    return pl.pallas_call(
        _proj_kernel,
        grid=(m // tm,),
        in_specs=[
            pl.BlockSpec((tm, d), lambda i: (i, 0)),
            pl.BlockSpec((1, 1, d), lambda i: ((i * tm) // rows_per_mod, 0, 0)),
            pl.BlockSpec((1, 1, d), lambda i: ((i * tm) // rows_per_mod, 0, 0)),
            pl.BlockSpec((1, d), lambda i: (0, 0)),
            pl.BlockSpec((1, d, N_IN), lambda i: (layer, 0, 0), pipeline_mode=pl.Buffered(1)),
            pl.BlockSpec((tm, LANES), lambda i: (i % rope_tiles, 0)),
            pl.BlockSpec((tm, LANES), lambda i: (i % rope_tiles, 0)),
            pl.BlockSpec((1, LANES), lambda i: (0, 0)),
            pl.BlockSpec((1, LANES), lambda i: (0, 0)),
        ],
        out_specs=[pl.BlockSpec((N_SLOTS, tm, LANES), lambda i: (0, i, 0)),
                   pl.BlockSpec((ATT_KV_HEADS, HEAD_DIM, tm), lambda i: (0, 0, i))],
        out_shape=[jax.ShapeDtypeStruct((N_SLOTS, m, LANES), BF16),
                   jax.ShapeDtypeStruct((ATT_KV_HEADS, HEAD_DIM, m), BF16)],
        compiler_params=_cparams(("arbitrary",)),
        name="norm_proj",
    )(x2, shift, scale, g, w, cosf, sinf, qg, kg)


ATT_KEY_BLOCK = 128


def _attn_kernel(*refs, n_kv):
    q_ref = refs[0]
    kv_refs = refs[1:1 + 2 * n_kv]
    o_ref = refs[1 + 2 * n_kv]
    m, l, acc = [None] * GQA, [None] * GQA, [None] * GQA
    units = [(s, j0, g) for s in range(n_kv) for j0 in range(0, kv_refs[2 * s].shape[1], ATT_KEY_BLOCK)
             for g in range(GQA)]

    def scores(unit):
        s, j0, g = unit
        return _dot_nt(kv_refs[2 * s][0, j0:j0 + ATT_KEY_BLOCK, :], q_ref[g])

    st_next = scores(units[0])
    for u, (s, j0, g) in enumerate(units):
        st = st_next
        if u + 1 < len(units):
            st_next = scores(units[u + 1])
        vt = kv_refs[2 * s + 1][0, :, j0:j0 + ATT_KEY_BLOCK]
        mb = jnp.max(st, axis=0, keepdims=True)
        if m[g] is None:
            m[g] = mb
            p = jnp.exp2(st - mb)
            l[g] = jnp.sum(p, axis=0, keepdims=True)
            acc[g] = _dot(vt, p.astype(BF16))
        else:
            m_new = jnp.maximum(m[g], mb)
            alpha = jnp.exp2(m[g] - m_new)
            p = jnp.exp2(st - m_new)
            l[g] = alpha * l[g] + jnp.sum(p, axis=0, keepdims=True)
            acc[g] = alpha * acc[g] + _dot(vt, p.astype(BF16))
            m[g] = m_new
    for g in range(GQA):
        o_ref[:, g * HEAD_DIM:(g + 1) * HEAD_DIM] = (acc[g] / l[g]).T.astype(BF16)


def _attn(zq, kv_sources, *, batch, q_len, tq):
    nq = q_len // tq
    in_specs = [pl.BlockSpec((GQA, tq, HEAD_DIM), lambda b, k, i: (k, b * nq + i, 0))]
    args = [zq]
    for z, vt, kv_len in kv_sources:
        in_specs.append(pl.BlockSpec((1, kv_len, HEAD_DIM), lambda b, k, i: (SLOT_AK + k, b, 0)))
        in_specs.append(pl.BlockSpec((1, HEAD_DIM, kv_len), lambda b, k, i: (k, 0, b)))
        args += [z, vt]
    return pl.pallas_call(
        functools.partial(_attn_kernel, n_kv=len(kv_sources)),
        grid=(batch, ATT_KV_HEADS, nq),
        in_specs=in_specs,
        out_specs=pl.BlockSpec((tq, GQA * HEAD_DIM), lambda b, k, i: (b * nq + i, k)),
        out_shape=jax.ShapeDtypeStruct((batch * q_len, ATT_Q_HEADS * HEAD_DIM), BF16),
        compiler_params=_cparams(("arbitrary", "arbitrary", "arbitrary")),
        name="attention",
    )(*args)


def _ret_kernel(dec_ref, qc_ref, kc_ref, vc_ref, gc_ref, ql_ref, kl_ref, vl_ref, gl_ref, ng_ref,
                oc_ref, ol_ref, kv_ref, st_ref, *, nc_ctx, nc_lat):
    h = pl.program_id(1)
    C = RET_CHUNK
    lg_f, lg_b = dec_ref[0, h], dec_ref[1, h]
    rel = lax.broadcasted_iota(jnp.int32, (C, C), 0) - lax.broadcasted_iota(jnp.int32, (C, C), 1)
    dmat = (jnp.where(rel >= 0, jnp.exp(lg_f * jnp.maximum(rel, 0).astype(F32)), 0.0)
            + jnp.where(rel <= 0, jnp.exp(lg_b * jnp.maximum(-rel, 0).astype(F32)), 0.0))
    row = lax.broadcasted_iota(jnp.int32, (C, 1), 0).astype(F32)
    qd_f, qd_b = jnp.exp(lg_f * (row + 1.0)), jnp.exp(lg_b * (C - row))
    kd_f, kd_b = jnp.exp(lg_f * (C - 1.0 - row)), jnp.exp(lg_b * row)
    cd_f, cd_b = jnp.exp(lg_f * C), jnp.exp(lg_b * C)

    chunks = [(qc_ref, kc_ref, vc_ref, gc_ref, oc_ref, n) for n in range(nc_ctx)]
    chunks += [(ql_ref, kl_ref, vl_ref, gl_ref, ol_ref, n) for n in range(nc_lat)]
    order_b = list(range(nc_ctx - 1, -1, -1)) + list(range(nc_ctx + nc_lat - 1, nc_ctx - 1, -1))

    def decayed(x_ref, n, dec_f, dec_b):
        x = x_ref[0, n * C:(n + 1) * C, :].astype(F32)
        return jnp.concatenate([(x * dec_f).astype(BF16), (x * dec_b).astype(BF16)], axis=1)

    for c, (_, k_ref, v_ref, _, _, n) in enumerate(chunks):
        kv_ref[c] = _dot_tn(decayed(k_ref, n, kd_f, kd_b), v_ref[0, n * C:(n + 1) * C, :])

    s = jnp.zeros((C, RET_DIM), F32)
    for c in range(len(chunks)):
        st_ref[c, 0:C, :] = s.astype(BF16)
        s = cd_f * s + kv_ref[c, 0:C, :]
    s = jnp.zeros((C, RET_DIM), F32)
    for c in order_b:
        st_ref[c, C:2 * C, :] = s.astype(BF16)
        s = cd_b * s + kv_ref[c, C:2 * C, :]

    for c, (q_ref, k_ref, v_ref, g_ref, o_ref, n) in enumerate(chunks):
        sl = slice(n * C, (n + 1) * C)
        v = v_ref[0, sl, :]
        att = _dot_nt(q_ref[0, sl, :], k_ref[0, sl, :]) * dmat
        o = _dot(att.astype(BF16), v) + _dot(decayed(q_ref, n, qd_f, qd_b), st_ref[c])
        y = _rms(o) * ng_ref[...]
        o_ref[sl, :] = (y * _silu(g_ref[0, sl, :].astype(F32))).astype(BF16)


def _ret(zc, zl, decay, ng, *, batch, ctx_len, lat_len):
    def slot(base, n):
        return pl.BlockSpec((1, n, RET_DIM), lambda b, h: (base + h, b, 0))

    n_chunks = (ctx_len + lat_len) // RET_CHUNK
    specs = [pl.BlockSpec(memory_space=pltpu.SMEM)]
    specs += [slot(s, ctx_len) for s in (SLOT_RQ, SLOT_RK, SLOT_RV, SLOT_RG)]
    specs += [slot(s, lat_len) for s in (SLOT_RQ, SLOT_RK, SLOT_RV, SLOT_RG)]
    specs += [pl.BlockSpec((1, RET_DIM), lambda b, h: (0, 0))]
    return pl.pallas_call(
        functools.partial(_ret_kernel, nc_ctx=ctx_len // RET_CHUNK, nc_lat=lat_len // RET_CHUNK),
        grid=(batch, RET_HEADS),
        in_specs=specs,
        out_specs=[pl.BlockSpec((ctx_len, RET_DIM), lambda b, h: (b, h)),
                   pl.BlockSpec((lat_len, RET_DIM), lambda b, h: (b, h))],
        out_shape=[jax.ShapeDtypeStruct((batch * ctx_len, RET_HEADS * RET_DIM), BF16),
                   jax.ShapeDtypeStruct((batch * lat_len, RET_HEADS * RET_DIM), BF16)],
        scratch_shapes=[pltpu.VMEM((n_chunks, 2 * RET_DIM, RET_DIM), F32),
                        pltpu.VMEM((n_chunks, 2 * RET_DIM, RET_DIM), BF16)],
        compiler_params=_cparams(("arbitrary", "arbitrary")),
        name="retention",
    )(decay, zc, zc, zc, zc, zl, zl, zl, zl, ng)


GLA_SLAB = 256
GLA_UNROLL = 4


def _gla_level_table():
    i = np.arange(GLA_CHUNK)[:, None]
    j = np.arange(GLA_CHUNK)[None, :]
    x = np.maximum(i ^ j, 1)
    lvl = np.floor(np.log2(x)).astype(np.int32)
    lvl = np.where(i == j, GLA_LEVELS, np.where(i > j, lvl, -1)).astype(np.int32)
    return np.concatenate([lvl, lvl.T], axis=1)


def _gla_scan_tables():
    t = np.arange(GLA_SLAB)
    same = (t[:, None] // GLA_CHUNK) == (t[None, :] // GLA_CHUNK)
    tri = np.stack([same & (t[None, :] <= t[:, None]), same & (t[None, :] >= t[:, None])]).astype(np.float32)
    fwd = np.arange(LANES)[None, :] < GLA_DK
    role = np.stack([np.where(((t[:, None] >> lev) & 1).astype(bool) == fwd, LOG2_E, -LOG2_E)
                     for lev in range(GLA_LEVELS)]).astype(np.float32)
    return tri, role


def _bcast_block_row(x, blk, r):
    rows = x.shape[0]
    if blk >= 8:
        x3 = x.reshape(rows // blk, blk, LANES)
        return jnp.broadcast_to(x3[:, r:r + 1, :], x3.shape).reshape(rows, LANES)
    pos = lax.broadcasted_iota(jnp.int32, x.shape, 0) & (blk - 1)
    out = x
    for p in range(blk):
        if p != r:
            out = jnp.where(pos == p, pltpu.roll(x, (p - r) % rows, 0), out)
    return out


def _gla_kernel(lvl_ref, tri_ref, role_ref, up_ref, gb_ref, ng_ref,
                qc_ref, kc_ref, vc_ref, gc_ref, ac_ref, ql_ref, kl_ref, vl_ref, gl_ref, al_ref,
                oc_ref, ol_ref,
                xs_ref, qd_ref, kd_ref, qh_ref, kh_ref, tot_ref, vv_ref, kv_ref, st_ref, *, ctx_len, lat_len):
    C = GLA_CHUNK
    streams = ((qc_ref, kc_ref, vc_ref, ac_ref, 0, ctx_len), (ql_ref, kl_ref, vl_ref, al_ref, ctx_len, lat_len))
    nc_ctx, nc_lat = ctx_len // C, lat_len // C
    n_chunks = nc_ctx + nc_lat
    fwd = lax.broadcasted_iota(jnp.int32, (GLA_SLAB, LANES), 1) < GLA_DK
    fwd_c = lax.broadcasted_iota(jnp.int32, (C, LANES), 1) < GLA_DK
    lvl = lvl_ref[...]

    for hh in range(2):
        own = fwd if hh == 0 else jnp.logical_not(fwd)

        def prep(i, carry, q_ref, k_ref, v_ref, a_ref, off, hh=hh, own=own):
            src = _rows(i * GLA_SLAB, GLA_SLAB, GLA_SLAB)
            dst = _rows(off + i * GLA_SLAB, GLA_SLAB, C)
            logit = _dot(a_ref[0, src, :], up_ref[hh]) + gb_ref[hh]
            la = (jnp.minimum(logit, 0.0) - jnp.log1p(jnp.exp(-jnp.abs(logit)))) * (1.0 / GLA_TAU)
            la_hi = la.astype(BF16)
            la_lo = (la - la_hi.astype(F32)).astype(BF16)
            cum = jnp.where(fwd, _dot(tri_ref[0], la_hi) + _dot(tri_ref[0], la_lo),
                            _dot(tri_ref[1], la_hi) + _dot(tri_ref[1], la_lo))
            tot = jnp.where(fwd, _bcast_block_row(cum, C, C - 1), _bcast_block_row(cum, C, 0))
            q2 = q_ref[0, src, :].astype(F32)
            k2 = k_ref[0, src, :].astype(F32)
            q = jnp.where(own, q2, pltpu.roll(q2, GLA_DK, 1))
            k = jnp.where(own, k2, pltpu.roll(k2, GLA_DK, 1))
            qd_ref[dst, :] = q.astype(BF16)
            kd_ref[dst, :] = k.astype(BF16)
            qh_ref[dst, :] = (q * jnp.exp(cum)).astype(BF16)
            kh_ref[dst, :] = (k * jnp.exp(tot - cum)).astype(BF16)
            tot_ref[dst, :] = tot
            vv_ref[dst, :] = v_ref[hh, src, :]
            for lev in range(GLA_LEVELS):
                s = 1 << lev
                edge = jnp.where(fwd, _bcast_block_row(cum, 2 * s, s - 1), _bcast_block_row(cum, 2 * s, s))
                role = role_ref[lev]
                xs_ref[lev, dst, :] = (jnp.where(role > 0, q, k) * jnp.exp2((cum - edge) * role)).astype(BF16)
            return carry

        for q_ref, k_ref, v_ref, a_ref, off, n in streams:
            lax.fori_loop(0, n // GLA_SLAB,
                          functools.partial(prep, q_ref=q_ref, k_ref=k_ref, v_ref=v_ref, a_ref=a_ref, off=off), 0,
                          unroll=2 if n // GLA_SLAB % 2 == 0 else 1)

        def kv_body(c, carry):
            rows = _rows(c * C, C, C)
            kv_ref[c] = _dot_tn(vv_ref[rows, :], kh_ref[rows, :])
            return carry

        lax.fori_loop(0, n_chunks, kv_body, 0, unroll=GLA_UNROLL)

        def scan(order_of, lanes):
            def body(n, s):
                c = order_of(n)
                st_ref[c, :, lanes] = s[:, lanes].astype(BF16)
                return s * jnp.exp(tot_ref[pl.ds(c * C, 1), :]) + kv_ref[c]
            lax.fori_loop(0, n_chunks, body, jnp.zeros((GLA_DV, LANES), F32))

        scan(lambda n: n, slice(0, GLA_DK))
        scan(lambda n: jnp.where(n < nc_ctx, nc_ctx - 1 - n, n_chunks + nc_ctx - 1 - n), slice(GLA_DK, LANES))

        def block_diag(x):
            zero = jnp.zeros_like(x)
            return jnp.concatenate([jnp.where(fwd_c, x, zero), jnp.where(fwd_c, zero, x)], axis=0)

        def out_chunk(c, g_ref, o_ref, local, hh=hh):
            rows = _rows(c * C, C, C)
            a = jnp.where(lvl == GLA_LEVELS, _dot_nt(qd_ref[rows, :], block_diag(kd_ref[rows, :])), 0.0)
            for lev in range(GLA_LEVELS):
                x = xs_ref[lev, rows, :]
                a = jnp.where(lvl == lev, _dot_nt(x, block_diag(x)), a)
            v = vv_ref[rows, :]
            o = _dot(a.astype(BF16), jnp.concatenate([v, v], axis=0)) + _dot_nt(qh_ref[rows, :], st_ref[c])
            y = _rms(o) * ng_ref[...]
            out_rows = _rows(local * C, C, C)
            y = y * _silu(g_ref[hh, out_rows, :].astype(F32))
            o_ref[out_rows, hh * GLA_DV:(hh + 1) * GLA_DV] = y.astype(BF16)

        for n in range(nc_ctx):
            out_chunk(n, gc_ref, oc_ref, n)

        def lat_body(n, carry, out_chunk=out_chunk):
            out_chunk(nc_ctx + n, gl_ref, ol_ref, n)
            return carry

        lax.fori_loop(0, nc_lat, lat_body, 0, unroll=GLA_UNROLL)


def _gla_gate_params(gate_up, gate_b):
    r = GLA_GATE_RANK
    gu = gate_up.astype(BF16).reshape(2, r, GLA_HEADS, GLA_DK).transpose(2, 0, 1, 3)
    up = jnp.zeros((GLA_HEADS, LANES, 2 * GLA_DK), BF16)
    up = up.at[:, 0:r, 0:GLA_DK].set(gu[:, 0]).at[:, r:2 * r, GLA_DK:].set(gu[:, 1])
    gb = gate_b.reshape(2, GLA_HEADS, GLA_DK).transpose(1, 0, 2).reshape(GLA_HEADS, 1, 2 * GLA_DK)
    return up, gb


def _gla(zc, zl, up_heads, gate_b_heads, ng, *, batch, ctx_len, lat_len):
    total = ctx_len + lat_len
    n_chunks = total // GLA_CHUNK

    def pair(base, n):
        return pl.BlockSpec((1, n, LANES), lambda b, p: (base + p, b, 0))

    def two(base, n):
        return pl.BlockSpec((2, n, LANES), lambda b, p: (base // 2 + p, b, 0))

    def one(n):
        return pl.BlockSpec((1, n, LANES), lambda b, p: (SLOT_GA, b, 0))

    tri, role = _gla_scan_tables()
    specs = [
        pl.BlockSpec((GLA_CHUNK, 2 * GLA_CHUNK), lambda b, p: (0, 0)),
        pl.BlockSpec(tri.shape, lambda b, p: (0, 0, 0)),
        pl.BlockSpec(role.shape, lambda b, p: (0, 0, 0)),
        pl.BlockSpec((2, LANES, LANES), lambda b, p: (p, 0, 0)),
        pl.BlockSpec((2, 1, LANES), lambda b, p: (p, 0, 0)),
        pl.BlockSpec((1, GLA_DV), lambda b, p: (0, 0)),
    ]
    for n in (ctx_len, lat_len):
        specs += [pair(SLOT_GQ, n), pair(SLOT_GK, n), two(SLOT_GV, n), two(SLOT_GR, n), one(n)]
    return pl.pallas_call(
        functools.partial(_gla_kernel, ctx_len=ctx_len, lat_len=lat_len),
        grid=(batch, GLA_HEADS // 2),
        in_specs=specs,
        out_specs=[pl.BlockSpec((ctx_len, 2 * GLA_DV), lambda b, p: (b, p)),
                   pl.BlockSpec((lat_len, 2 * GLA_DV), lambda b, p: (b, p))],
        out_shape=[jax.ShapeDtypeStruct((batch * ctx_len, GLA_HEADS * GLA_DV), BF16),
                   jax.ShapeDtypeStruct((batch * lat_len, GLA_HEADS * GLA_DV), BF16)],
        scratch_shapes=[
            pltpu.VMEM((GLA_LEVELS, total, LANES), BF16),
            pltpu.VMEM((total, LANES), BF16),
            pltpu.VMEM((total, LANES), BF16),
            pltpu.VMEM((total, LANES), BF16),
            pltpu.VMEM((total, LANES), BF16),
            pltpu.VMEM((total, LANES), F32),
            pltpu.VMEM((total, GLA_DV), BF16),
            pltpu.VMEM((n_chunks, GLA_DV, LANES), F32),
            pltpu.VMEM((n_chunks, GLA_DV, LANES), BF16),
        ],
        compiler_params=_cparams(("arbitrary", "arbitrary")),
        name="gla",
    )(jnp.asarray(_gla_level_table()), jnp.asarray(tri, BF16), jnp.asarray(role), up_heads, gate_b_heads, ng,
      zc, zc, zc, zc, zc, zl, zl, zl, zl, zl)


def _outproj_kernel(att_ref, ret_ref, gla_ref, w_ref, x_ref, gate_ref, sh_ref, sc_ref, g_ref, xo_ref, ho_ref):
    na, nr = att_ref.shape[1], ret_ref.shape[1]
    y = (_dot(att_ref[...], w_ref[0, 0:na, :]) + _dot(ret_ref[...], w_ref[0, na:na + nr, :])
         + _dot(gla_ref[...], w_ref[0, na + nr:, :]))
    xn = x_ref[...] + gate_ref[0] * y
    xo_ref[...] = xn
    ho_ref[...] = (_rms(xn) * g_ref[...] * (1.0 + sc_ref[0]) + sh_ref[0]).astype(BF16)


def _outproj(att, ret, gla, w, layer, x2, gate, shift, scale, g, *, tm, rows_per_mod):
    m, d = x2.shape

    def rows(n):
        return pl.BlockSpec((tm, n), lambda i: (i, 0))

    def mod():
        return pl.BlockSpec((1, 1, d), lambda i: ((i * tm) // rows_per_mod, 0, 0))

    return pl.pallas_call(
        _outproj_kernel,
        grid=(m // tm,),
        in_specs=[rows(att.shape[1]), rows(ret.shape[1]), rows(gla.shape[1]),
                  pl.BlockSpec((1,) + w.shape[1:], lambda i: (layer, 0, 0), pipeline_mode=pl.Buffered(1)),
                  rows(d), mod(), mod(), mod(),
                  pl.BlockSpec((1, d), lambda i: (0, 0))],
        out_specs=[rows(d), rows(d)],
        out_shape=[jax.ShapeDtypeStruct((m, d), F32), jax.ShapeDtypeStruct((m, d), BF16)],
        compiler_params=_cparams(("arbitrary",)),
        name="out_proj",
    )(att, ret, gla, w, x2, gate, shift, scale, g)


FFN_HALO = 16
FFN_TF = 512
FFN_TM = 512


def _ffn_kernel(h_ref, hn_ref, hp_ref, wa_ref, wv_ref, cw_ref, cb_ref, wd_ref, x_ref, gate_ref, fg_ref,
                o_ref, hs_ref, *, tm, seq_len, final_norm):
    i = pl.program_id(0)
    f = pl.program_id(1)
    ext = tm + 2 * FFN_HALO

    @pl.when(f == 0)
    def _():
        hs_ref[0:tm, :] = h_ref[...]
        hs_ref[tm:tm + FFN_HALO, :] = hn_ref[...]
        hs_ref[tm + FFN_HALO:ext, :] = hp_ref[...]
        o_ref[...] = jnp.zeros_like(o_ref)

    a = _dot(hs_ref[...], wa_ref[0])
    pos = (i * tm + lax.broadcasted_iota(jnp.int32, (ext, 1), 0)) % seq_len
    prev = jnp.where(pos == 0, 0.0, pltpu.roll(a, 1, 0))
    nxt = jnp.where(pos == seq_len - 1, 0.0, pltpu.roll(a, ext - 1, 0))
    conv = prev * cw_ref[0:1, :] + a * cw_ref[1:2, :] + nxt * cw_ref[2:3, :] + cb_ref[...]
    v = _dot(h_ref[...], wv_ref[0])
    u = (_silu(conv[0:tm]) * v).astype(BF16)
    o_ref[...] += _dot(u, wd_ref[0])

    @pl.when(f == pl.num_programs(1) - 1)
    def _():
        xn = x_ref[...] + gate_ref[0] * o_ref[...]
        if final_norm:
            xn = _rms(xn) * fg_ref[...]
        o_ref[...] = xn


def _ffn(h2, w_up, conv_w, conv_b, w_down, layer, x2, gate, fg, *, tm, seq_len, rows_per_mod, final_norm):
    m, d = x2.shape
    tf = FFN_TF
    nf = D_FF // tf
    hb = tm // FFN_HALO
    last = m // FFN_HALO - 1
    return pl.pallas_call(
        functools.partial(_ffn_kernel, tm=tm, seq_len=seq_len, final_norm=final_norm),
        grid=(m // tm, nf),
        in_specs=[
            pl.BlockSpec((tm, d), lambda i, f: (i, 0)),
            pl.BlockSpec((FFN_HALO, d), lambda i, f: (jnp.minimum((i + 1) * hb, last), 0)),
            pl.BlockSpec((FFN_HALO, d), lambda i, f: (jnp.maximum(i * hb - 1, 0), 0)),
            pl.BlockSpec((1, d, tf), lambda i, f: (layer, 0, f)),
            pl.BlockSpec((1, d, tf), lambda i, f: (layer, 0, nf + f)),
            pl.BlockSpec((3, tf), lambda i, f: (0, f)),
            pl.BlockSpec((1, tf), lambda i, f: (0, f)),
            pl.BlockSpec((1, tf, d), lambda i, f: (layer, f, 0)),
            pl.BlockSpec((tm, d), lambda i, f: (i, 0)),
            pl.BlockSpec((1, 1, d), lambda i, f: ((i * tm) // rows_per_mod, 0, 0)),
            pl.BlockSpec((1, d), lambda i, f: (0, 0)),
        ],
        out_specs=pl.BlockSpec((tm, d), lambda i, f: (i, 0)),
        out_shape=jax.ShapeDtypeStruct((m, d), F32),
        scratch_shapes=[pltpu.VMEM((tm + 2 * FFN_HALO, d), BF16)],
        compiler_params=_cparams(("arbitrary", "arbitrary")),
        name="conv_glu",
    )(h2, h2, h2, w_up, w_up, conv_w, conv_b, w_down, x2, gate, fg)


def _rope_tables(n_tokens):
    rows = n_tokens // GRID_W
    row = jnp.repeat(jnp.arange(rows, dtype=F32), GRID_W)
    col = jnp.tile(jnp.arange(GRID_W, dtype=F32), rows)
    n_freq = HEAD_DIM // 4
    inv_freq = ROPE_THETA ** (-jnp.arange(n_freq, dtype=F32) / n_freq)
    ang = jnp.concatenate([row[:, None] * inv_freq, col[:, None] * inv_freq], axis=-1)
    cos, sin = jnp.cos(ang), jnp.sin(ang)
    return jnp.concatenate([cos, cos], axis=-1), jnp.concatenate([-sin, sin], axis=-1)


def kernel(x, c, ctx, c_ctx, ada_w, ada_b, norm1_g, w_in, q_norm_g, k_norm_g, ret_log_decay, ret_norm_g,
           gla_gate_up, gla_gate_b, gla_norm_g, w_out, norm2_g, w_up, conv_w, conv_b, w_down, final_norm_g):
    batch, lat_len, d = x.shape
    ctx_len = ctx.shape[1]
    depth = ada_w.shape[0]
    mod_rows = 16
    cc = jnp.concatenate([c, c_ctx[None], jnp.zeros((mod_rows - batch - 1, d), F32)], axis=0)
    mod = _ada(cc, ada_w, ada_b).reshape(depth, mod_rows, N_MOD, d)

    cos_l, sin_l = _rope_tables(lat_len)
    proj_tm = 512
    cos_c = jnp.ones((proj_tm, LANES), F32)
    sin_c = jnp.zeros((proj_tm, LANES), F32)

    xl = x.reshape(batch * lat_len, d)
    xc = ctx.reshape(batch * ctx_len, d)
    row = lambda v: v.reshape(1, -1)
    w_in_b, w_out_b, w_up_b, w_down_b = (w.astype(BF16) for w in (w_in, w_out, w_up, w_down))

    for l in range(depth):
        last = l == depth - 1
        ml = [mod[l, :batch, k].reshape(batch, 1, d) for k in range(N_MOD)]
        mc = [mod[l, batch, k].reshape(1, 1, d) for k in range(N_MOD)]
        up_heads, gate_b_heads = _gla_gate_params(gla_gate_up[l], gla_gate_b[l])

        zl, vtl = _proj(xl, ml[0], ml[1], row(norm1_g[l]), w_in_b, l, cos_l, sin_l, row(q_norm_g[l]), row(k_norm_g[l]),
                   tm=proj_tm, rows_per_mod=lat_len, rope_tiles=lat_len // proj_tm)
        zc, vtc = _proj(xc, mc[0], mc[1], row(norm1_g[l]), w_in_b, l, cos_c, sin_c, row(q_norm_g[l]), row(k_norm_g[l]),
                   tm=proj_tm, rows_per_mod=batch * ctx_len, rope_tiles=1)

        att_l = _attn(zl, [(zc, vtc, ctx_len), (zl, vtl, lat_len)], batch=batch, q_len=lat_len, tq=256)
        ret_c, ret_l = _ret(zc, zl, ret_log_decay[l], row(ret_norm_g[l]),
                            batch=batch, ctx_len=ctx_len, lat_len=lat_len)
        gla_c, gla_l = _gla(zc, zl, up_heads, gate_b_heads, row(gla_norm_g[l]),
                            batch=batch, ctx_len=ctx_len, lat_len=lat_len)

        xl, h2 = _outproj(att_l, ret_l, gla_l, w_out_b, l, xl, ml[2], ml[3], ml[4], row(norm2_g[l]),
                          tm=256, rows_per_mod=lat_len)
        xl = _ffn(h2, w_up_b, conv_w[l], row(conv_b[l]), w_down_b, l, xl, ml[5], row(final_norm_g),
                  tm=FFN_TM, seq_len=lat_len, rows_per_mod=lat_len, final_norm=last)

        if not last:
            att_c = _attn(zc, [(zc, vtc, ctx_len)], batch=batch, q_len=ctx_len, tq=ctx_len)
            xc, hc2 = _outproj(att_c, ret_c, gla_c, w_out_b, l, xc, mc[2], mc[3], mc[4], row(norm2_g[l]),
                               tm=256, rows_per_mod=batch * ctx_len)
            xc = _ffn(hc2, w_up_b, conv_w[l], row(conv_b[l]), w_down_b, l, xc, mc[5], row(final_norm_g),
                      tm=FFN_TM, seq_len=ctx_len, rows_per_mod=batch * ctx_len, final_norm=False)

    return xl.reshape(batch, lat_len, d)
```

```python
import functools

import numpy as np
import jax
import jax.numpy as jnp
from jax import lax
from jax.experimental import pallas as pl
from jax.experimental.pallas import tpu as pltpu

F32 = jnp.float32
BF16 = jnp.bfloat16

D_MODEL = 2048
HEAD_DIM = 128
GRID_W = 64
ATT_Q_HEADS = 8
ATT_KV_HEADS = 2
GQA = ATT_Q_HEADS // ATT_KV_HEADS
RET_HEADS = 4
RET_DIM = 128
RET_CHUNK = 128
GLA_HEADS = 4
GLA_DK = 64
GLA_DV = 128
GLA_GATE_RANK = 16
GLA_TAU = 16.0
GLA_CHUNK = 64
GLA_LEVELS = 6
D_FF = 5632
ROPE_THETA = 10000.0
N_MOD = 6
EPS = 1e-6

LOG2_E = 1.4426950408889634

LANES = 128
N_IN = 5152
PROJ_TN = 256
N_SLOTS = -(-N_IN // LANES)
SLOT_AQ, SLOT_AK, SLOT_AV = 0, 8, 10
SLOT_RQ, SLOT_RK, SLOT_RV, SLOT_RG = 12, 16, 20, 24
SLOT_GQ, SLOT_GK, SLOT_GV, SLOT_GR, SLOT_GA = 28, 30, 32, 36, 40

VMEM_LIMIT = 52 * 1024 * 1024

_NT = (((1,), (1,)), ((), ()))
_TN = (((0,), (0,)), ((), ()))


def _cparams(sem, vmem_limit=VMEM_LIMIT):
    return pltpu.CompilerParams(dimension_semantics=sem, vmem_limit_bytes=vmem_limit)


def _dot(a, b):
    return jnp.dot(a, b, preferred_element_type=F32)


def _dot_nt(a, b):
    return lax.dot_general(a, b, _NT, preferred_element_type=F32)


def _dot_tn(a, b):
    return lax.dot_general(a, b, _TN, preferred_element_type=F32)


def _rows(start, size, align):
    if isinstance(start, int):
        return pl.ds(start, size)
    return pl.ds(pl.multiple_of(start, align), size)


def _rms(t):
    return t * lax.rsqrt(jnp.mean(t * t, axis=-1, keepdims=True) + EPS)


def _silu(t):
    return t * jax.nn.sigmoid(t)


def _ada_kernel(c_ref, w_ref, b_ref, o_ref):
    s = _silu(c_ref[...]).astype(BF16)
    o_ref[0] = _dot(s, w_ref[0].astype(BF16)) + b_ref[0]


def _ada(cc, ada_w, ada_b):
    depth, d, n = ada_w.shape
    rows = cc.shape[0]
    tn = 1024
    return pl.pallas_call(
        _ada_kernel,
        grid=(depth, n // tn),
        in_specs=[
            pl.BlockSpec((rows, d), lambda l, j: (0, 0)),
            pl.BlockSpec((1, d, tn), lambda l, j: (l, 0, j)),
            pl.BlockSpec((1, 1, tn), lambda l, j: (l, 0, j)),
        ],
        out_specs=pl.BlockSpec((1, rows, tn), lambda l, j: (l, 0, j)),
        out_shape=jax.ShapeDtypeStruct((depth, rows, n), F32),
        compiler_params=_cparams(("arbitrary", "arbitrary")),
        name="ada_mod",
    )(cc, ada_w, ada_b.reshape(depth, 1, n))


def _proj_kernel(x_ref, sh_ref, sc_ref, g_ref, w_ref, cos_ref, sin_ref, qg_ref, kg_ref, o_ref, vt_ref):
    y = _rms(x_ref[...]) * g_ref[...]
    h = (y * (1.0 + sc_ref[0]) + sh_ref[0]).astype(BF16)
    cos, sin = cos_ref[...], sin_ref[...]
    qg = qg_ref[...] * (HEAD_DIM ** -0.5 * LOG2_E)
    kg = kg_ref[...]

    def rope(t):
        return t * cos + pltpu.roll(t, HEAD_DIM // 2, 1) * sin

    def transform(slot, t):
        if slot < SLOT_AK:
            return rope(_rms(t) * qg)
        if slot < SLOT_AV:
            return rope(_rms(t) * kg)
        if SLOT_RQ <= slot < SLOT_RK:
            return rope(t)
        if SLOT_RK <= slot < SLOT_RV:
            return rope(t * (RET_DIM ** -0.5))
        if SLOT_GQ <= slot < SLOT_GK:
            return t * (GLA_DK ** -0.5)
        return t

    for c0 in range(0, N_IN, PROJ_TN):
        c1 = min(c0 + PROJ_TN, N_IN)
        z = _dot(h, w_ref[0, :, c0:c1])
        for s0 in range(c0, c1, LANES):
            slot, n = s0 // LANES, min(LANES, c1 - s0)
            t = transform(slot, z[:, s0 - c0:s0 - c0 + n])
            if SLOT_AV <= slot < SLOT_RQ:
                vt_ref[slot - SLOT_AV] = t.T.astype(BF16)
            t = t.astype(BF16)
            if n < LANES:
                o_ref[slot] = jnp.zeros(o_ref.shape[1:], BF16)
                o_ref[slot, :, 0:n] = t
            else:
                o_ref[slot] = t


def _proj(x2, shift, scale, g, w, layer, cosf, sinf, qg, kg, *, tm, rows_per_mod, rope_tiles):
    m, d = x2.shape
    return pl.pallas_call(
        _proj_kernel,
        grid=(m // tm,),
        in_specs=[
            pl.BlockSpec((tm, d), lambda i: (i, 0)),
            pl.BlockSpec((1, 1, d), lambda i: ((i * tm) // rows_per_mod, 0, 0)),
            pl.BlockSpec((1, 1, d), lambda i: ((i * tm) // rows_per_mod, 0, 0)),
            pl.BlockSpec((1, d), lambda i: (0, 0)),
            pl.BlockSpec((1, d, N_IN), lambda i: (layer, 0, 0), pipeline_mode=pl.Buffered(1)),
            pl.BlockSpec((tm, LANES), lambda i: (i % rope_tiles, 0)),
            pl.BlockSpec((tm, LANES), lambda i: (i % rope_tiles, 0)),
            pl.BlockSpec((1, LANES), lambda i: (0, 0)),
            pl.BlockSpec((1, LANES), lambda i: (0, 0)),
        ],
        out_specs=[pl.BlockSpec((N_SLOTS, tm, LANES), lambda i: (0, i, 0)),
                   pl.BlockSpec((ATT_KV_HEADS, HEAD_DIM, tm), lambda i: (0, 0, i))],
        out_shape=[jax.ShapeDtypeStruct((N_SLOTS, m, LANES), BF16),
                   jax.ShapeDtypeStruct((ATT_KV_HEADS, HEAD_DIM, m), BF16)],
        compiler_params=_cparams(("arbitrary",)),
        name="norm_proj",
    )(x2, shift, scale, g, w, cosf, sinf, qg, kg)


ATT_KEY_BLOCK = 128


def _attn_kernel(*refs, n_kv):
    q_ref = refs[0]
    kv_refs = refs[1:1 + 2 * n_kv]
    o_ref = refs[1 + 2 * n_kv]
    m, l, acc = [None] * GQA, [None] * GQA, [None] * GQA
    units = [(s, j0, g) for s in range(n_kv) for j0 in range(0, kv_refs[2 * s].shape[1], ATT_KEY_BLOCK)
             for g in range(GQA)]

    def scores(unit):
        s, j0, g = unit
        return _dot_nt(kv_refs[2 * s][0, j0:j0 + ATT_KEY_BLOCK, :], q_ref[g])

    st_next = scores(units[0])
    for u, (s, j0, g) in enumerate(units):
        st = st_next
        if u + 1 < len(units):
            st_next = scores(units[u + 1])
        vt = kv_refs[2 * s + 1][0, :, j0:j0 + ATT_KEY_BLOCK]
        mb = jnp.max(st, axis=0, keepdims=True)
        if m[g] is None:
            m[g] = mb
            p = jnp.exp2(st - mb)
            l[g] = jnp.sum(p, axis=0, keepdims=True)
            acc[g] = _dot(vt, p.astype(BF16))
        else:
            m_new = jnp.maximum(m[g], mb)
            alpha = jnp.exp2(m[g] - m_new)
            p = jnp.exp2(st - m_new)
            l[g] = alpha * l[g] + jnp.sum(p, axis=0, keepdims=True)
            acc[g] = alpha * acc[g] + _dot(vt, p.astype(BF16))
            m[g] = m_new
    for g in range(GQA):
        o_ref[:, g * HEAD_DIM:(g + 1) * HEAD_DIM] = (acc[g] / l[g]).T.astype(BF16)


def _attn(zq, kv_sources, *, batch, q_len, tq):
    nq = q_len // tq
    in_specs = [pl.BlockSpec((GQA, tq, HEAD_DIM), lambda b, k, i: (k, b * nq + i, 0))]
    args = [zq]
    for z, vt, kv_len in kv_sources:
        in_specs.append(pl.BlockSpec((1, kv_len, HEAD_DIM), lambda b, k, i: (SLOT_AK + k, b, 0)))
        in_specs.append(pl.BlockSpec((1, HEAD_DIM, kv_len), lambda b, k, i: (k, 0, b)))
        args += [z, vt]
    return pl.pallas_call(
        functools.partial(_attn_kernel, n_kv=len(kv_sources)),
        grid=(batch, ATT_KV_HEADS, nq),
        in_specs=in_specs,
        out_specs=pl.BlockSpec((tq, GQA * HEAD_DIM), lambda b, k, i: (b * nq + i, k)),
        out_shape=jax.ShapeDtypeStruct((batch * q_len, ATT_Q_HEADS * HEAD_DIM), BF16),
        compiler_params=_cparams(("arbitrary", "arbitrary", "arbitrary")),
        name="attention",
    )(*args)


def _ret_kernel(dec_ref, qc_ref, kc_ref, vc_ref, gc_ref, ql_ref, kl_ref, vl_ref, gl_ref, ng_ref,
                oc_ref, ol_ref, kv_ref, st_ref, *, nc_ctx, nc_lat):
    h = pl.program_id(1)
    C = RET_CHUNK
    lg_f, lg_b = dec_ref[0, h], dec_ref[1, h]
    rel = lax.broadcasted_iota(jnp.int32, (C, C), 0) - lax.broadcasted_iota(jnp.int32, (C, C), 1)
    dmat = (jnp.where(rel >= 0, jnp.exp(lg_f * jnp.maximum(rel, 0).astype(F32)), 0.0)
            + jnp.where(rel <= 0, jnp.exp(lg_b * jnp.maximum(-rel, 0).astype(F32)), 0.0))
    row = lax.broadcasted_iota(jnp.int32, (C, 1), 0).astype(F32)
    qd_f, qd_b = jnp.exp(lg_f * (row + 1.0)), jnp.exp(lg_b * (C - row))
    kd_f, kd_b = jnp.exp(lg_f * (C - 1.0 - row)), jnp.exp(lg_b * row)
    cd_f, cd_b = jnp.exp(lg_f * C), jnp.exp(lg_b * C)

    chunks = [(qc_ref, kc_ref, vc_ref, gc_ref, oc_ref, n) for n in range(nc_ctx)]
    chunks += [(ql_ref, kl_ref, vl_ref, gl_ref, ol_ref, n) for n in range(nc_lat)]
    order_b = list(range(nc_ctx - 1, -1, -1)) + list(range(nc_ctx + nc_lat - 1, nc_ctx - 1, -1))

    def decayed(x_ref, n, dec_f, dec_b):
        x = x_ref[0, n * C:(n + 1) * C, :].astype(F32)
        return jnp.concatenate([(x * dec_f).astype(BF16), (x * dec_b).astype(BF16)], axis=1)

    for c, (_, k_ref, v_ref, _, _, n) in enumerate(chunks):
        kv_ref[c] = _dot_tn(decayed(k_ref, n, kd_f, kd_b), v_ref[0, n * C:(n + 1) * C, :])

    s = jnp.zeros((C, RET_DIM), F32)
    for c in range(len(chunks)):
        st_ref[c, 0:C, :] = s.astype(BF16)
        s = cd_f * s + kv_ref[c, 0:C, :]
    s = jnp.zeros((C, RET_DIM), F32)
    for c in order_b:
        st_ref[c, C:2 * C, :] = s.astype(BF16)
        s = cd_b * s + kv_ref[c, C:2 * C, :]

    for c, (q_ref, k_ref, v_ref, g_ref, o_ref, n) in enumerate(chunks):
        sl = slice(n * C, (n + 1) * C)
        v = v_ref[0, sl, :]
        att = _dot_nt(q_ref[0, sl, :], k_ref[0, sl, :]) * dmat
        o = _dot(att.astype(BF16), v) + _dot(decayed(q_ref, n, qd_f, qd_b), st_ref[c])
        y = _rms(o) * ng_ref[...]
        o_ref[sl, :] = (y * _silu(g_ref[0, sl, :].astype(F32))).astype(BF16)


def _ret(zc, zl, decay, ng, *, batch, ctx_len, lat_len):
    def slot(base, n):
        return pl.BlockSpec((1, n, RET_DIM), lambda b, h: (base + h, b, 0))

    n_chunks = (ctx_len + lat_len) // RET_CHUNK
    specs = [pl.BlockSpec(memory_space=pltpu.SMEM)]
    specs += [slot(s, ctx_len) for s in (SLOT_RQ, SLOT_RK, SLOT_RV, SLOT_RG)]
    specs += [slot(s, lat_len) for s in (SLOT_RQ, SLOT_RK, SLOT_RV, SLOT_RG)]
    specs += [pl.BlockSpec((1, RET_DIM), lambda b, h: (0, 0))]
    return pl.pallas_call(
        functools.partial(_ret_kernel, nc_ctx=ctx_len // RET_CHUNK, nc_lat=lat_len // RET_CHUNK),
        grid=(batch, RET_HEADS),
        in_specs=specs,
        out_specs=[pl.BlockSpec((ctx_len, RET_DIM), lambda b, h: (b, h)),
                   pl.BlockSpec((lat_len, RET_DIM), lambda b, h: (b, h))],
        out_shape=[jax.ShapeDtypeStruct((batch * ctx_len, RET_HEADS * RET_DIM), BF16),
                   jax.ShapeDtypeStruct((batch * lat_len, RET_HEADS * RET_DIM), BF16)],
        scratch_shapes=[pltpu.VMEM((n_chunks, 2 * RET_DIM, RET_DIM), F32),
                        pltpu.VMEM((n_chunks, 2 * RET_DIM, RET_DIM), BF16)],
        compiler_params=_cparams(("arbitrary", "arbitrary")),
        name="retention",
    )(decay, zc, zc, zc, zc, zl, zl, zl, zl, ng)


GLA_SLAB = 256
GLA_UNROLL = 4


def _gla_level_table():
    i = np.arange(GLA_CHUNK)[:, None]
    j = np.arange(GLA_CHUNK)[None, :]
    x = np.maximum(i ^ j, 1)
    lvl = np.floor(np.log2(x)).astype(np.int32)
    lvl = np.where(i == j, GLA_LEVELS, np.where(i > j, lvl, -1)).astype(np.int32)
    return np.concatenate([lvl, lvl.T], axis=1)


def _gla_role_table():
    t = np.arange(GLA_SLAB)
    fwd = np.arange(LANES)[None, :] < GLA_DK
    return np.stack([np.where(((t[:, None] >> lev) & 1).astype(bool) == fwd, LOG2_E, -LOG2_E)
                     for lev in range(GLA_LEVELS)]).astype(np.float32)


def _bcast_block_row(x, blk, r):
    rows = x.shape[0]
    if blk >= 8:
        x3 = x.reshape(rows // blk, blk, LANES)
        return jnp.broadcast_to(x3[:, r:r + 1, :], x3.shape).reshape(rows, LANES)
    pos = lax.broadcasted_iota(jnp.int32, x.shape, 0) & (blk - 1)
    out = x
    for p in range(blk):
        if p != r:
            out = jnp.where(pos == p, pltpu.roll(x, (p - r) % rows, 0), out)
    return out


def _gla_kernel(lvl_ref, role_ref, up_ref, gb_ref, ng_ref,
                qc_ref, kc_ref, vc_ref, gc_ref, ac_ref, ql_ref, kl_ref, vl_ref, gl_ref, al_ref,
                oc_ref, ol_ref, *scratch, ctx_len, lat_len):
    C = GLA_CHUNK
    streams = ((qc_ref, kc_ref, vc_ref, ac_ref, 0, ctx_len), (ql_ref, kl_ref, vl_ref, al_ref, ctx_len, lat_len))
    nc_ctx, nc_lat = ctx_len // C, lat_len // C
    n_chunks = nc_ctx + nc_lat
    cpos = lax.broadcasted_iota(jnp.int32, (GLA_SLAB, LANES), 0) & (C - 1)
    fwd = lax.broadcasted_iota(jnp.int32, (GLA_SLAB, LANES), 1) < GLA_DK
    fwd_c = lax.broadcasted_iota(jnp.int32, (C, LANES), 1) < GLA_DK
    lvl = lvl_ref[...]

    passes = []
    for hh in range(2):
        own = fwd if hh == 0 else jnp.logical_not(fwd)
        xs_ref, qd_ref, kd_ref, qh_ref, kh_ref, tot_ref, vv_ref, kv_ref, st_ref = (r.at[hh] for r in scratch)

        def prep(i, between, q_ref, k_ref, v_ref, a_ref, off, hh=hh, own=own, xs_ref=xs_ref, qd_ref=qd_ref,
                 kd_ref=kd_ref, qh_ref=qh_ref, kh_ref=kh_ref, tot_ref=tot_ref, vv_ref=vv_ref):
            between = list(between)
            src = _rows(i * GLA_SLAB, GLA_SLAB, GLA_SLAB)
            dst = _rows(off + i * GLA_SLAB, GLA_SLAB, C)
            logit = _dot(a_ref[0, src, :], up_ref[hh]) + gb_ref[hh]
            la = (jnp.minimum(logit, 0.0) - jnp.log1p(jnp.exp(-jnp.abs(logit)))) * (1.0 / GLA_TAU)
            cum = la
            for step in range(GLA_LEVELS):
                sh = 1 << step
                before = jnp.where(cpos >= sh, pltpu.roll(cum, sh, 0), 0.0)
                after = jnp.where(cpos < C - sh, pltpu.roll(cum, GLA_SLAB - sh, 0), 0.0)
                cum = cum + jnp.where(fwd, before, after)
            tot = jnp.where(fwd, _bcast_block_row(cum, C, C - 1), _bcast_block_row(cum, C, 0))
            q2 = q_ref[0, src, :].astype(F32)
            k2 = k_ref[0, src, :].astype(F32)
            q = jnp.where(own, q2, pltpu.roll(q2, GLA_DK, 1))
            k = jnp.where(own, k2, pltpu.roll(k2, GLA_DK, 1))
            qd_ref[dst, :] = q.astype(BF16)
            kd_ref[dst, :] = k.astype(BF16)
            qh_ref[dst, :] = (q * jnp.exp(cum)).astype(BF16)
            kh_ref[dst, :] = (k * jnp.exp(tot - cum)).astype(BF16)
            tot_ref[dst, :] = tot
            vv_ref[dst, :] = v_ref[hh, src, :]
            for lev in range(GLA_LEVELS):
                if between and lev % 2 == 0:
                    between.pop(0)()
                s = 1 << lev
                edge = jnp.where(fwd, _bcast_block_row(cum, 2 * s, s - 1), _bcast_block_row(cum, 2 * s, s))
                role = role_ref[lev]
                xs_ref[lev, dst, :] = (jnp.where(role > 0, q, k) * jnp.exp2((cum - edge) * role)).astype(BF16)
            for item in between:
                item()

        def states(kv_ref=kv_ref, st_ref=st_ref, vv_ref=vv_ref, kh_ref=kh_ref, tot_ref=tot_ref):
            def kv_body(c, carry):
                rows = _rows(c * C, C, C)
                kv_ref[c] = _dot_tn(vv_ref[rows, :], kh_ref[rows, :])
                return carry

            lax.fori_loop(0, n_chunks, kv_body, 0, unroll=GLA_UNROLL)

            def scan(order_of, lanes):
                def body(n, s):
                    c = order_of(n)
                    st_ref[c, :, lanes] = s[:, lanes].astype(BF16)
                    return s * jnp.exp(tot_ref[pl.ds(c * C, 1), :]) + kv_ref[c]
                lax.fori_loop(0, n_chunks, body, jnp.zeros((GLA_DV, LANES), F32))

            scan(lambda n: n, slice(0, GLA_DK))
            scan(lambda n: jnp.where(n < nc_ctx, nc_ctx - 1 - n, n_chunks + nc_ctx - 1 - n), slice(GLA_DK, LANES))

        def block_diag(x):
            zero = jnp.zeros_like(x)
            return jnp.concatenate([jnp.where(fwd_c, x, zero), jnp.where(fwd_c, zero, x)], axis=0)

        def out_chunk(c, g_ref, o_ref, local, hh=hh, xs_ref=xs_ref, qd_ref=qd_ref, kd_ref=kd_ref, qh_ref=qh_ref,
                      vv_ref=vv_ref, st_ref=st_ref):
            rows = _rows(c * C, C, C)
            a = jnp.where(lvl == GLA_LEVELS, _dot_nt(qd_ref[rows, :], block_diag(kd_ref[rows, :])), 0.0)
            for lev in range(GLA_LEVELS):
                x = xs_ref[lev, rows, :]
                a = jnp.where(lvl == lev, _dot_nt(x, block_diag(x)), a)
            v = vv_ref[rows, :]
            o = _dot(a.astype(BF16), jnp.concatenate([v, v], axis=0)) + _dot_nt(qh_ref[rows, :], st_ref[c])
            y = _rms(o) * ng_ref[...]
            out_rows = _rows(local * C, C, C)
            y = y * _silu(g_ref[hh, out_rows, :].astype(F32))
            o_ref[out_rows, hh * GLA_DV:(hh + 1) * GLA_DV] = y.astype(BF16)

        passes.append((prep, states, out_chunk))

    per_slab = GLA_SLAB // C
    (prep0, states0, out0), (prep1, states1, out1) = passes

    def slab(prep, out, stream, i, g_ref, o_ref):
        q_ref, k_ref, v_ref, a_ref, off, _ = stream
        chunks = []
        if out is not None:
            chunks = [functools.partial(out, off // C + i * per_slab + j, g_ref, o_ref, i * per_slab + j)
                      for j in range(per_slab)]
        if prep is not None:
            prep(i, chunks, q_ref=q_ref, k_ref=k_ref, v_ref=v_ref, a_ref=a_ref, off=off)
        else:
            for item in chunks:
                item()

    def run(prep, out):
        for stream, g_ref, o_ref in ((streams[0], gc_ref, oc_ref), (streams[1], gl_ref, ol_ref)):
            n_slabs = stream[5] // GLA_SLAB
            if n_slabs == 1:
                slab(prep, out, stream, 0, g_ref, o_ref)
            else:
                def body(i, carry, stream=stream, g_ref=g_ref, o_ref=o_ref):
                    slab(prep, out, stream, i, g_ref, o_ref)
                    return carry
                lax.fori_loop(0, n_slabs, body, 0, unroll=2 if (out is None and n_slabs % 2 == 0) else 1)

    run(prep0, None)
    states0()
    run(prep1, out0)
    states1()
    run(None, out1)


def _gla_gate_params(gate_up, gate_b):
    r = GLA_GATE_RANK
    gu = gate_up.astype(BF16).reshape(2, r, GLA_HEADS, GLA_DK).transpose(2, 0, 1, 3)
    up = jnp.zeros((GLA_HEADS, LANES, 2 * GLA_DK), BF16)
    up = up.at[:, 0:r, 0:GLA_DK].set(gu[:, 0]).at[:, r:2 * r, GLA_DK:].set(gu[:, 1])
    gb = gate_b.reshape(2, GLA_HEADS, GLA_DK).transpose(1, 0, 2).reshape(GLA_HEADS, 1, 2 * GLA_DK)
    return up, gb


def _gla(zc, zl, up_heads, gate_b_heads, ng, *, batch, ctx_len, lat_len):
    total = ctx_len + lat_len
    n_chunks = total // GLA_CHUNK

    def pair(base, n):
        return pl.BlockSpec((1, n, LANES), lambda b, p: (base + p, b, 0))

    def two(base, n):
        return pl.BlockSpec((2, n, LANES), lambda b, p: (base // 2 + p, b, 0))

    def one(n):
        return pl.BlockSpec((1, n, LANES), lambda b, p: (SLOT_GA, b, 0))

    role = _gla_role_table()
    specs = [
        pl.BlockSpec((GLA_CHUNK, 2 * GLA_CHUNK), lambda b, p: (0, 0)),
        pl.BlockSpec(role.shape, lambda b, p: (0, 0, 0)),
        pl.BlockSpec((2, LANES, LANES), lambda b, p: (p, 0, 0)),
        pl.BlockSpec((2, 1, LANES), lambda b, p: (p, 0, 0)),
        pl.BlockSpec((1, GLA_DV), lambda b, p: (0, 0)),
    ]
    for n in (ctx_len, lat_len):
        specs += [pair(SLOT_GQ, n), pair(SLOT_GK, n), two(SLOT_GV, n), two(SLOT_GR, n), one(n)]
    return pl.pallas_call(
        functools.partial(_gla_kernel, ctx_len=ctx_len, lat_len=lat_len),
        grid=(batch, GLA_HEADS // 2),
        in_specs=specs,
        out_specs=[pl.BlockSpec((ctx_len, 2 * GLA_DV), lambda b, p: (b, p)),
                   pl.BlockSpec((lat_len, 2 * GLA_DV), lambda b, p: (b, p))],
        out_shape=[jax.ShapeDtypeStruct((batch * ctx_len, GLA_HEADS * GLA_DV), BF16),
                   jax.ShapeDtypeStruct((batch * lat_len, GLA_HEADS * GLA_DV), BF16)],
        scratch_shapes=[
            pltpu.VMEM((2, GLA_LEVELS, total, LANES), BF16),
            pltpu.VMEM((2, total, LANES), BF16),
            pltpu.VMEM((2, total, LANES), BF16),
            pltpu.VMEM((2, total, LANES), BF16),
            pltpu.VMEM((2, total, LANES), BF16),
            pltpu.VMEM((2, total, LANES), F32),
            pltpu.VMEM((2, total, GLA_DV), BF16),
            pltpu.VMEM((2, n_chunks, GLA_DV, LANES), F32),
            pltpu.VMEM((2, n_chunks, GLA_DV, LANES), BF16),
        ],
        compiler_params=_cparams(("arbitrary", "arbitrary")),
        name="gla",
    )(jnp.asarray(_gla_level_table()), jnp.asarray(role), up_heads, gate_b_heads, ng,
      zc, zc, zc, zc, zc, zl, zl, zl, zl, zl)


def _outproj_kernel(att_ref, ret_ref, gla_ref, w_ref, x_ref, gate_ref, sh_ref, sc_ref, g_ref, xo_ref, ho_ref):
    na, nr = att_ref.shape[1], ret_ref.shape[1]
    y = (_dot(att_ref[...], w_ref[0, 0:na, :]) + _dot(ret_ref[...], w_ref[0, na:na + nr, :])
         + _dot(gla_ref[...], w_ref[0, na + nr:, :]))
    xn = x_ref[...] + gate_ref[0] * y
    xo_ref[...] = xn
    ho_ref[...] = (_rms(xn) * g_ref[...] * (1.0 + sc_ref[0]) + sh_ref[0]).astype(BF16)


def _outproj(att, ret, gla, w, layer, x2, gate, shift, scale, g, *, tm, rows_per_mod):
    m, d = x2.shape

    def rows(n):
        return pl.BlockSpec((tm, n), lambda i: (i, 0))

    def mod():
        return pl.BlockSpec((1, 1, d), lambda i: ((i * tm) // rows_per_mod, 0, 0))

    return pl.pallas_call(
        _outproj_kernel,
        grid=(m // tm,),
        in_specs=[rows(att.shape[1]), rows(ret.shape[1]), rows(gla.shape[1]),
                  pl.BlockSpec((1,) + w.shape[1:], lambda i: (layer, 0, 0), pipeline_mode=pl.Buffered(1)),
                  rows(d), mod(), mod(), mod(),
                  pl.BlockSpec((1, d), lambda i: (0, 0))],
        out_specs=[rows(d), rows(d)],
        out_shape=[jax.ShapeDtypeStruct((m, d), F32), jax.ShapeDtypeStruct((m, d), BF16)],
        compiler_params=_cparams(("arbitrary",)),
        name="out_proj",
    )(att, ret, gla, w, x2, gate, shift, scale, g)


FFN_HALO = 16
FFN_TF = 512
FFN_TM = 512


def _ffn_kernel(h_ref, hn_ref, hp_ref, wa_ref, wv_ref, cw_ref, cb_ref, wd_ref, x_ref, gate_ref, fg_ref,
                o_ref, hs_ref, *, tm, seq_len, final_norm):
    i = pl.program_id(0)
    f = pl.program_id(1)
    ext = tm + 2 * FFN_HALO

    @pl.when(f == 0)
    def _():
        hs_ref[0:tm, :] = h_ref[...]
        hs_ref[tm:tm + FFN_HALO, :] = hn_ref[...]
        hs_ref[tm + FFN_HALO:ext, :] = hp_ref[...]
        o_ref[...] = jnp.zeros_like(o_ref)

    a = _dot(hs_ref[...], wa_ref[0])
    pos = (i * tm + lax.broadcasted_iota(jnp.int32, (ext, 1), 0)) % seq_len
    prev = jnp.where(pos == 0, 0.0, pltpu.roll(a, 1, 0))
    nxt = jnp.where(pos == seq_len - 1, 0.0, pltpu.roll(a, ext - 1, 0))
    conv = prev * cw_ref[0:1, :] + a * cw_ref[1:2, :] + nxt * cw_ref[2:3, :] + cb_ref[...]
    v = _dot(h_ref[...], wv_ref[0])
    u = (_silu(conv[0:tm]) * v).astype(BF16)
    o_ref[...] += _dot(u, wd_ref[0])

    @pl.when(f == pl.num_programs(1) - 1)
    def _():
        xn = x_ref[...] + gate_ref[0] * o_ref[...]
        if final_norm:
            xn = _rms(xn) * fg_ref[...]
        o_ref[...] = xn


def _ffn(h2, w_up, conv_w, conv_b, w_down, layer, x2, gate, fg, *, tm, seq_len, rows_per_mod, final_norm):
    m, d = x2.shape
    tf = FFN_TF
    nf = D_FF // tf
    hb = tm // FFN_HALO
    last = m // FFN_HALO - 1
    return pl.pallas_call(
        functools.partial(_ffn_kernel, tm=tm, seq_len=seq_len, final_norm=final_norm),
        grid=(m // tm, nf),
        in_specs=[
            pl.BlockSpec((tm, d), lambda i, f: (i, 0)),
            pl.BlockSpec((FFN_HALO, d), lambda i, f: (jnp.minimum((i + 1) * hb, last), 0)),
            pl.BlockSpec((FFN_HALO, d), lambda i, f: (jnp.maximum(i * hb - 1, 0), 0)),
            pl.BlockSpec((1, d, tf), lambda i, f: (layer, 0, f)),
            pl.BlockSpec((1, d, tf), lambda i, f: (layer, 0, nf + f)),
            pl.BlockSpec((3, tf), lambda i, f: (0, f)),
            pl.BlockSpec((1, tf), lambda i, f: (0, f)),
            pl.BlockSpec((1, tf, d), lambda i, f: (layer, f, 0)),
            pl.BlockSpec((tm, d), lambda i, f: (i, 0)),
            pl.BlockSpec((1, 1, d), lambda i, f: ((i * tm) // rows_per_mod, 0, 0)),
            pl.BlockSpec((1, d), lambda i, f: (0, 0)),
        ],
        out_specs=pl.BlockSpec((tm, d), lambda i, f: (i, 0)),
        out_shape=jax.ShapeDtypeStruct((m, d), F32),
        scratch_shapes=[pltpu.VMEM((tm + 2 * FFN_HALO, d), BF16)],
        compiler_params=_cparams(("arbitrary", "arbitrary")),
        name="conv_glu",
    )(h2, h2, h2, w_up, w_up, conv_w, conv_b, w_down, x2, gate, fg)


def _rope_tables(n_tokens):
    rows = n_tokens // GRID_W
    row = jnp.repeat(jnp.arange(rows, dtype=F32), GRID_W)
    col = jnp.tile(jnp.arange(GRID_W, dtype=F32), rows)
    n_freq = HEAD_DIM // 4
    inv_freq = ROPE_THETA ** (-jnp.arange(n_freq, dtype=F32) / n_freq)
    ang = jnp.concatenate([row[:, None] * inv_freq, col[:, None] * inv_freq], axis=-1)
    cos, sin = jnp.cos(ang), jnp.sin(ang)
    return jnp.concatenate([cos, cos], axis=-1), jnp.concatenate([-sin, sin], axis=-1)


def kernel(x, c, ctx, c_ctx, ada_w, ada_b, norm1_g, w_in, q_norm_g, k_norm_g, ret_log_decay, ret_norm_g,
           gla_gate_up, gla_gate_b, gla_norm_g, w_out, norm2_g, w_up, conv_w, conv_b, w_down, final_norm_g):
    batch, lat_len, d = x.shape
    ctx_len = ctx.shape[1]
    depth = ada_w.shape[0]
    mod_rows = 16
    cc = jnp.concatenate([c, c_ctx[None], jnp.zeros((mod_rows - batch - 1, d), F32)], axis=0)
    mod = _ada(cc, ada_w, ada_b).reshape(depth, mod_rows, N_MOD, d)

    cos_l, sin_l = _rope_tables(lat_len)
    proj_tm = 512
    cos_c = jnp.ones((proj_tm, LANES), F32)
    sin_c = jnp.zeros((proj_tm, LANES), F32)

    xl = x.reshape(batch * lat_len, d)
    xc = ctx.reshape(batch * ctx_len, d)
    row = lambda v: v.reshape(1, -1)
    w_in_b, w_out_b, w_up_b, w_down_b = (w.astype(BF16) for w in (w_in, w_out, w_up, w_down))

    for l in range(depth):
        last = l == depth - 1
        ml = [mod[l, :batch, k].reshape(batch, 1, d) for k in range(N_MOD)]
        mc = [mod[l, batch, k].reshape(1, 1, d) for k in range(N_MOD)]
        up_heads, gate_b_heads = _gla_gate_params(gla_gate_up[l], gla_gate_b[l])

        zl, vtl = _proj(xl, ml[0], ml[1], row(norm1_g[l]), w_in_b, l, cos_l, sin_l, row(q_norm_g[l]), row(k_norm_g[l]),
                   tm=proj_tm, rows_per_mod=lat_len, rope_tiles=lat_len // proj_tm)
        zc, vtc = _proj(xc, mc[0], mc[1], row(norm1_g[l]), w_in_b, l, cos_c, sin_c, row(q_norm_g[l]), row(k_norm_g[l]),
                   tm=proj_tm, rows_per_mod=batch * ctx_len, rope_tiles=1)

        att_l = _attn(zl, [(zc, vtc, ctx_len), (zl, vtl, lat_len)], batch=batch, q_len=lat_len, tq=256)
        ret_c, ret_l = _ret(zc, zl, ret_log_decay[l], row(ret_norm_g[l]),
                            batch=batch, ctx_len=ctx_len, lat_len=lat_len)
        gla_c, gla_l = _gla(zc, zl, up_heads, gate_b_heads, row(gla_norm_g[l]),
                            batch=batch, ctx_len=ctx_len, lat_len=lat_len)

        xl, h2 = _outproj(att_l, ret_l, gla_l, w_out_b, l, xl, ml[2], ml[3], ml[4], row(norm2_g[l]),
                          tm=256, rows_per_mod=lat_len)
        xl = _ffn(h2, w_up_b, conv_w[l], row(conv_b[l]), w_down_b, l, xl, ml[5], row(final_norm_g),
                  tm=FFN_TM, seq_len=lat_len, rows_per_mod=lat_len, final_norm=last)

        if not last:
            att_c = _attn(zc, [(zc, vtc, ctx_len)], batch=batch, q_len=ctx_len, tq=ctx_len)
            xc, hc2 = _outproj(att_c, ret_c, gla_c, w_out_b, l, xc, mc[2], mc[3], mc[4], row(norm2_g[l]),
                               tm=256, rows_per_mod=batch * ctx_len)
            xc = _ffn(hc2, w_up_b, conv_w[l], row(conv_b[l]), w_down_b, l, xc, mc[5], row(final_norm_g),
                      tm=FFN_TM, seq_len=ctx_len, rows_per_mod=batch * ctx_len, final_norm=False)

    return xl.reshape(batch, lat_len, d)
```

```python
import functools

import numpy as np
import jax
import jax.numpy as jnp
from jax import lax
from jax.experimental import pallas as pl
from jax.experimental.pallas import tpu as pltpu

F32 = jnp.float32
BF16 = jnp.bfloat16

D_MODEL = 2048
HEAD_DIM = 128
GRID_W = 64
ATT_Q_HEADS = 8
ATT_KV_HEADS = 2
GQA = ATT_Q_HEADS // ATT_KV_HEADS
RET_HEADS = 4
RET_DIM = 128
RET_CHUNK = 128
GLA_HEADS = 4
GLA_DK = 64
GLA_DV = 128
GLA_GATE_RANK = 16
GLA_TAU = 16.0
GLA_CHUNK = 64
GLA_LEVELS = 6
D_FF = 5632
ROPE_THETA = 10000.0
N_MOD = 6
EPS = 1e-6

LOG2_E = 1.4426950408889634

LANES = 128
N_IN = 5152
PROJ_TN = 256
N_SLOTS = -(-N_IN // LANES)
SLOT_AQ, SLOT_AK, SLOT_AV = 0, 8, 10
SLOT_RQ, SLOT_RK, SLOT_RV, SLOT_RG = 12, 16, 20, 24
SLOT_GQ, SLOT_GK, SLOT_GV, SLOT_GR, SLOT_GA = 28, 30, 32, 36, 40

VMEM_LIMIT = 52 * 1024 * 1024

_NT = (((1,), (1,)), ((), ()))
_TN = (((0,), (0,)), ((), ()))


def _cparams(sem, vmem_limit=VMEM_LIMIT):
    return pltpu.CompilerParams(dimension_semantics=sem, vmem_limit_bytes=vmem_limit)


def _dot(a, b):
    return jnp.dot(a, b, preferred_element_type=F32)


def _dot_nt(a, b):
    return lax.dot_general(a, b, _NT, preferred_element_type=F32)


def _dot_tn(a, b):
    return lax.dot_general(a, b, _TN, preferred_element_type=F32)


def _rows(start, size, align):
    if isinstance(start, int):
        return pl.ds(start, size)
    return pl.ds(pl.multiple_of(start, align), size)


def _rms(t):
    return t * lax.rsqrt(jnp.mean(t * t, axis=-1, keepdims=True) + EPS)


def _silu(t):
    return t * jax.nn.sigmoid(t)


def _ada_kernel(c_ref, w_ref, b_ref, o_ref):
    s = _silu(c_ref[...]).astype(BF16)
    o_ref[0] = _dot(s, w_ref[0].astype(BF16)) + b_ref[0]


def _ada(cc, ada_w, ada_b):
    depth, d, n = ada_w.shape
    rows = cc.shape[0]
    tn = 1024
    return pl.pallas_call(
        _ada_kernel,
        grid=(depth, n // tn),
        in_specs=[
            pl.BlockSpec((rows, d), lambda l, j: (0, 0)),
            pl.BlockSpec((1, d, tn), lambda l, j: (l, 0, j)),
            pl.BlockSpec((1, 1, tn), lambda l, j: (l, 0, j)),
        ],
        out_specs=pl.BlockSpec((1, rows, tn), lambda l, j: (l, 0, j)),
        out_shape=jax.ShapeDtypeStruct((depth, rows, n), F32),
        compiler_params=_cparams(("arbitrary", "arbitrary")),
        name="ada_mod",
    )(cc, ada_w, ada_b.reshape(depth, 1, n))


def _proj_kernel(x_ref, sh_ref, sc_ref, g_ref, w_ref, cos_ref, sin_ref, qg_ref, kg_ref, o_ref, vt_ref):
    y = _rms(x_ref[...]) * g_ref[...]
    h = (y * (1.0 + sc_ref[0]) + sh_ref[0]).astype(BF16)
    cos, sin = cos_ref[...], sin_ref[...]
    qg = qg_ref[...] * (HEAD_DIM ** -0.5 * LOG2_E)
    kg = kg_ref[...]

    def rope(t):
        return t * cos + pltpu.roll(t, HEAD_DIM // 2, 1) * sin

    def transform(slot, t):
        if slot < SLOT_AK:
            return rope(_rms(t) * qg)
        if slot < SLOT_AV:
            return rope(_rms(t) * kg)
        if SLOT_RQ <= slot < SLOT_RK:
            return rope(t)
        if SLOT_RK <= slot < SLOT_RV:
            return rope(t * (RET_DIM ** -0.5))
        if SLOT_GQ <= slot < SLOT_GK:
            return t * (GLA_DK ** -0.5)
        return t

    for c0 in range(0, N_IN, PROJ_TN):
        c1 = min(c0 + PROJ_TN, N_IN)
        z = _dot(h, w_ref[0, :, c0:c1])
        for s0 in range(c0, c1, LANES):
            slot, n = s0 // LANES, min(LANES, c1 - s0)
            t = transform(slot, z[:, s0 - c0:s0 - c0 + n])
            if SLOT_AV <= slot < SLOT_RQ:
                vt_ref[slot - SLOT_AV] = t.T.astype(BF16)
            t = t.astype(BF16)
            if n < LANES:
                o_ref[slot] = jnp.zeros(o_ref.shape[1:], BF16)
                o_ref[slot, :, 0:n] = t
            else:
                o_ref[slot] = t


def _proj(x2, shift, scale, g, w, layer, cosf, sinf, qg, kg, *, tm, rows_per_mod, rope_tiles):
    m, d = x2.shape
    return pl.pallas_call(
        _proj_kernel,
        grid=(m // tm,),
        in_specs=[
            pl.BlockSpec((tm, d), lambda i: (i, 0)),
            pl.BlockSpec((1, 1, d), lambda i: ((i * tm) // rows_per_mod, 0, 0)),
            pl.BlockSpec((1, 1, d), lambda i: ((i * tm) // rows_per_mod, 0, 0)),
            pl.BlockSpec((1, d), lambda i: (0, 0)),
            pl.BlockSpec((1, d, N_IN), lambda i: (layer, 0, 0), pipeline_mode=pl.Buffered(1)),
            pl.BlockSpec((tm, LANES), lambda i: (i % rope_tiles, 0)),
            pl.BlockSpec((tm, LANES), lambda i: (i % rope_tiles, 0)),
            pl.BlockSpec((1, LANES), lambda i: (0, 0)),
            pl.BlockSpec((1, LANES), lambda i: (0, 0)),
        ],
        out_specs=[pl.BlockSpec((N_SLOTS, tm, LANES), lambda i: (0, i, 0)),
                   pl.BlockSpec((ATT_KV_HEADS, HEAD_DIM, tm), lambda i: (0, 0, i))],
        out_shape=[jax.ShapeDtypeStruct((N_SLOTS, m, LANES), BF16),
                   jax.ShapeDtypeStruct((ATT_KV_HEADS, HEAD_DIM, m), BF16)],
        compiler_params=_cparams(("arbitrary",)),
        name="norm_proj",
    )(x2, shift, scale, g, w, cosf, sinf, qg, kg)


ATT_KEY_BLOCK = 128


def _attn_kernel(*refs, n_kv, reps):
    n_cast = len(reps)
    q_ref = refs[0]
    kv_refs = refs[1:1 + 2 * n_kv]
    w_refs = refs[1 + 2 * n_kv:1 + 2 * n_kv + n_cast]
    o_ref = refs[1 + 2 * n_kv + n_cast]
    wo_refs = refs[2 + 2 * n_kv + n_cast:]
    step = (pl.program_id(0) * pl.num_programs(1) + pl.program_id(1)) * pl.num_programs(2) + pl.program_id(2)

    def convert(w_ref, wo_ref):
        if len(wo_ref.shape) == 3:
            tn = wo_ref.shape[2]
            for t in range(wo_ref.shape[0]):
                wo_ref[t] = w_ref[:, t * tn:(t + 1) * tn].astype(BF16)
        else:
            wo_ref[...] = w_ref[...].astype(BF16)

    for w_ref, wo_ref, rep in zip(w_refs, wo_refs, reps):
        if rep == 1:
            convert(w_ref, wo_ref)
        else:
            pl.when(step % rep == 0)(functools.partial(convert, w_ref, wo_ref))
    m, l, acc = [None] * GQA, [None] * GQA, [None] * GQA
    units = [(s, j0, g) for s in range(n_kv) for j0 in range(0, kv_refs[2 * s].shape[1], ATT_KEY_BLOCK)
             for g in range(GQA)]

    def scores(unit):
        s, j0, g = unit
        return _dot_nt(kv_refs[2 * s][0, j0:j0 + ATT_KEY_BLOCK, :], q_ref[g])

    st_next = scores(units[0])
    for u, (s, j0, g) in enumerate(units):
        st = st_next
        if u + 1 < len(units):
            st_next = scores(units[u + 1])
        vt = kv_refs[2 * s + 1][0, :, j0:j0 + ATT_KEY_BLOCK]
        mb = jnp.max(st, axis=0, keepdims=True)
        if m[g] is None:
            m[g] = mb
            p = jnp.exp2(st - mb)
            l[g] = jnp.sum(p, axis=0, keepdims=True)
            acc[g] = _dot(vt, p.astype(BF16))
        else:
            m_new = jnp.maximum(m[g], mb)
            alpha = jnp.exp2(m[g] - m_new)
            p = jnp.exp2(st - m_new)
            l[g] = alpha * l[g] + jnp.sum(p, axis=0, keepdims=True)
            acc[g] = alpha * acc[g] + _dot(vt, p.astype(BF16))
            m[g] = m_new
    for g in range(GQA):
        o_ref[:, g * HEAD_DIM:(g + 1) * HEAD_DIM] = (acc[g] / l[g]).T.astype(BF16)


def _attn(zq, kv_sources, *, batch, q_len, tq, casts=()):
    nq = q_len // tq
    steps = batch * ATT_KV_HEADS * nq
    in_specs = [pl.BlockSpec((GQA, tq, HEAD_DIM), lambda b, k, i: (k, b * nq + i, 0))]
    args = [zq]
    for z, vt, kv_len in kv_sources:
        in_specs.append(pl.BlockSpec((1, kv_len, HEAD_DIM), lambda b, k, i: (SLOT_AK + k, b, 0)))
        in_specs.append(pl.BlockSpec((1, HEAD_DIM, kv_len), lambda b, k, i: (k, 0, b)))
        args += [z, vt]
    out_specs = [pl.BlockSpec((tq, GQA * HEAD_DIM), lambda b, k, i: (b * nq + i, k))]
    out_shape = [jax.ShapeDtypeStruct((batch * q_len, ATT_Q_HEADS * HEAD_DIM), BF16)]
    step = lambda b, k, i: (b * ATT_KV_HEADS + k) * nq + i
    reps = []
    for w, tile in casts:
        rows, cols = w.shape
        rep = next(r for r in (1, 2, 4, 8, 16) if rows % (steps // r) == 0 and rows // (steps // r) % 16 == 0)
        slab = rows // (steps // rep)
        reps.append(rep)
        in_specs.append(pl.BlockSpec((slab, cols), lambda b, k, i, rep=rep: (step(b, k, i) // rep, 0)))
        args.append(w)
        if tile is None:
            out_specs.append(pl.BlockSpec((slab, cols), lambda b, k, i, rep=rep: (step(b, k, i) // rep, 0)))
            out_shape.append(jax.ShapeDtypeStruct((rows, cols), BF16))
        else:
            out_specs.append(pl.BlockSpec((cols // tile, slab, tile),
                                          lambda b, k, i, rep=rep: (0, step(b, k, i) // rep, 0)))
            out_shape.append(jax.ShapeDtypeStruct((cols // tile, rows, tile), BF16))
    outs = pl.pallas_call(
        functools.partial(_attn_kernel, n_kv=len(kv_sources), reps=tuple(reps)),
        grid=(batch, ATT_KV_HEADS, nq),
        in_specs=in_specs,
        out_specs=out_specs,
        out_shape=out_shape,
        compiler_params=_cparams(("arbitrary", "arbitrary", "arbitrary")),
        name="attention",
    )(*args)
    return outs[0], list(outs[1:])


def _ret_kernel(dec_ref, qc_ref, kc_ref, vc_ref, gc_ref, ql_ref, kl_ref, vl_ref, gl_ref, ng_ref,
                oc_ref, ol_ref, kv_ref, st_ref, *, nc_ctx, nc_lat):
    h = pl.program_id(1)
    C = RET_CHUNK
    lg_f, lg_b = dec_ref[0, h], dec_ref[1, h]
    rel = lax.broadcasted_iota(jnp.int32, (C, C), 0) - lax.broadcasted_iota(jnp.int32, (C, C), 1)
    dmat = (jnp.where(rel >= 0, jnp.exp(lg_f * jnp.maximum(rel, 0).astype(F32)), 0.0)
            + jnp.where(rel <= 0, jnp.exp(lg_b * jnp.maximum(-rel, 0).astype(F32)), 0.0))
    row = lax.broadcasted_iota(jnp.int32, (C, 1), 0).astype(F32)
    qd_f, qd_b = jnp.exp(lg_f * (row + 1.0)), jnp.exp(lg_b * (C - row))
    kd_f, kd_b = jnp.exp(lg_f * (C - 1.0 - row)), jnp.exp(lg_b * row)
    cd_f, cd_b = jnp.exp(lg_f * C), jnp.exp(lg_b * C)

    chunks = [(qc_ref, kc_ref, vc_ref, gc_ref, oc_ref, n) for n in range(nc_ctx)]
    chunks += [(ql_ref, kl_ref, vl_ref, gl_ref, ol_ref, n) for n in range(nc_lat)]
    order_b = list(range(nc_ctx - 1, -1, -1)) + list(range(nc_ctx + nc_lat - 1, nc_ctx - 1, -1))

    def decayed(x_ref, n, dec_f, dec_b):
        x = x_ref[0, n * C:(n + 1) * C, :].astype(F32)
        return jnp.concatenate([(x * dec_f).astype(BF16), (x * dec_b).astype(BF16)], axis=1)

    for c, (_, k_ref, v_ref, _, _, n) in enumerate(chunks):
        kv_ref[c] = _dot_tn(decayed(k_ref, n, kd_f, kd_b), v_ref[0, n * C:(n + 1) * C, :])

    s = jnp.zeros((C, RET_DIM), F32)
    for c in range(len(chunks)):
        st_ref[c, 0:C, :] = s.astype(BF16)
        s = cd_f * s + kv_ref[c, 0:C, :]
    s = jnp.zeros((C, RET_DIM), F32)
    for c in order_b:
        st_ref[c, C:2 * C, :] = s.astype(BF16)
        s = cd_b * s + kv_ref[c, C:2 * C, :]

    for c, (q_ref, k_ref, v_ref, g_ref, o_ref, n) in enumerate(chunks):
        sl = slice(n * C, (n + 1) * C)
        v = v_ref[0, sl, :]
        att = _dot_nt(q_ref[0, sl, :], k_ref[0, sl, :]) * dmat
        o = _dot(att.astype(BF16), v) + _dot(decayed(q_ref, n, qd_f, qd_b), st_ref[c])
        y = _rms(o) * ng_ref[...]
        o_ref[sl, :] = (y * _silu(g_ref[0, sl, :].astype(F32))).astype(BF16)


def _ret(zc, zl, decay, ng, *, batch, ctx_len, lat_len):
    def slot(base, n):
        return pl.BlockSpec((1, n, RET_DIM), lambda b, h: (base + h, b, 0))

    n_chunks = (ctx_len + lat_len) // RET_CHUNK
    specs = [pl.BlockSpec(memory_space=pltpu.SMEM)]
    specs += [slot(s, ctx_len) for s in (SLOT_RQ, SLOT_RK, SLOT_RV, SLOT_RG)]
    specs += [slot(s, lat_len) for s in (SLOT_RQ, SLOT_RK, SLOT_RV, SLOT_RG)]
    specs += [pl.BlockSpec((1, RET_DIM), lambda b, h: (0, 0))]
    return pl.pallas_call(
        functools.partial(_ret_kernel, nc_ctx=ctx_len // RET_CHUNK, nc_lat=lat_len // RET_CHUNK),
        grid=(batch, RET_HEADS),
        in_specs=specs,
        out_specs=[pl.BlockSpec((ctx_len, RET_DIM), lambda b, h: (b, h)),
                   pl.BlockSpec((lat_len, RET_DIM), lambda b, h: (b, h))],
        out_shape=[jax.ShapeDtypeStruct((batch * ctx_len, RET_HEADS * RET_DIM), BF16),
                   jax.ShapeDtypeStruct((batch * lat_len, RET_HEADS * RET_DIM), BF16)],
        scratch_shapes=[pltpu.VMEM((n_chunks, 2 * RET_DIM, RET_DIM), F32),
                        pltpu.VMEM((n_chunks, 2 * RET_DIM, RET_DIM), BF16)],
        compiler_params=_cparams(("arbitrary", "arbitrary")),
        name="retention",
    )(decay, zc, zc, zc, zc, zl, zl, zl, zl, ng)


GLA_SLAB = 256
GLA_UNROLL = 4


def _gla_level_table():
    i = np.arange(GLA_CHUNK)[:, None]
    j = np.arange(GLA_CHUNK)[None, :]
    x = np.maximum(i ^ j, 1)
    lvl = np.floor(np.log2(x)).astype(np.int32)
    lvl = np.where(i == j, GLA_LEVELS, np.where(i > j, lvl, -1)).astype(np.int32)
    return np.concatenate([lvl, lvl.T], axis=1)


def _gla_role_table():
    t = np.arange(GLA_SLAB)
    fwd = np.arange(LANES)[None, :] < GLA_DK
    return np.stack([np.where(((t[:, None] >> lev) & 1).astype(bool) == fwd, LOG2_E, -LOG2_E)
                     for lev in range(GLA_LEVELS)]).astype(np.float32)


def _bcast_block_row(x, blk, r):
    rows = x.shape[0]
    if blk >= 8:
        x3 = x.reshape(rows // blk, blk, LANES)
        return jnp.broadcast_to(x3[:, r:r + 1, :], x3.shape).reshape(rows, LANES)
    pos = lax.broadcasted_iota(jnp.int32, x.shape, 0) & (blk - 1)
    out = x
    for p in range(blk):
        if p != r:
            out = jnp.where(pos == p, pltpu.roll(x, (p - r) % rows, 0), out)
    return out


def _gla_kernel(lvl_ref, role_ref, up_ref, gb_ref, ng_ref,
                qc_ref, kc_ref, vc_ref, gc_ref, ac_ref, ql_ref, kl_ref, vl_ref, gl_ref, al_ref,
                oc_ref, ol_ref, *scratch, ctx_len, lat_len):
    C = GLA_CHUNK
    streams = ((qc_ref, kc_ref, vc_ref, ac_ref, 0, ctx_len), (ql_ref, kl_ref, vl_ref, al_ref, ctx_len, lat_len))
    nc_ctx, nc_lat = ctx_len // C, lat_len // C
    n_chunks = nc_ctx + nc_lat
    cpos = lax.broadcasted_iota(jnp.int32, (GLA_SLAB, LANES), 0) & (C - 1)
    fwd = lax.broadcasted_iota(jnp.int32, (GLA_SLAB, LANES), 1) < GLA_DK
    fwd_c = lax.broadcasted_iota(jnp.int32, (C, LANES), 1) < GLA_DK
    lvl = lvl_ref[...]

    passes = []
    for hh in range(2):
        own = fwd if hh == 0 else jnp.logical_not(fwd)
        xs_ref, qd_ref, kd_ref, qh_ref, kh_ref, tot_ref, vv_ref, kv_ref, st_ref = (r.at[hh] for r in scratch)

        def prep(i, between, q_ref, k_ref, v_ref, a_ref, off, hh=hh, own=own, xs_ref=xs_ref, qd_ref=qd_ref,
                 kd_ref=kd_ref, qh_ref=qh_ref, kh_ref=kh_ref, tot_ref=tot_ref, vv_ref=vv_ref):
            between = list(between)
            src = _rows(i * GLA_SLAB, GLA_SLAB, GLA_SLAB)
            dst = _rows(off + i * GLA_SLAB, GLA_SLAB, C)
            logit = _dot(a_ref[0, src, :], up_ref[hh]) + gb_ref[hh]
            la = (jnp.minimum(logit, 0.0) - jnp.log1p(jnp.exp(-jnp.abs(logit)))) * (1.0 / GLA_TAU)
            cum = la
            for step in range(GLA_LEVELS):
                sh = 1 << step
                before = jnp.where(cpos >= sh, pltpu.roll(cum, sh, 0), 0.0)
                after = jnp.where(cpos < C - sh, pltpu.roll(cum, GLA_SLAB - sh, 0), 0.0)
                cum = cum + jnp.where(fwd, before, after)
            tot = jnp.where(fwd, _bcast_block_row(cum, C, C - 1), _bcast_block_row(cum, C, 0))
            q2 = q_ref[0, src, :].astype(F32)
            k2 = k_ref[0, src, :].astype(F32)
            q = jnp.where(own, q2, pltpu.roll(q2, GLA_DK, 1))
            k = jnp.where(own, k2, pltpu.roll(k2, GLA_DK, 1))
            qd_ref[dst, :] = q.astype(BF16)
            kd_ref[dst, :] = k.astype(BF16)
            qh_ref[dst, :] = (q * jnp.exp(cum)).astype(BF16)
            kh_ref[dst, :] = (k * jnp.exp(tot - cum)).astype(BF16)
            tot_ref[dst, :] = tot
            vv_ref[dst, :] = v_ref[hh, src, :]
            for lev in range(GLA_LEVELS):
                if between and lev % 2 == 0:
                    between.pop(0)()
                s = 1 << lev
                edge = jnp.where(fwd, _bcast_block_row(cum, 2 * s, s - 1), _bcast_block_row(cum, 2 * s, s))
                role = role_ref[lev]
                xs_ref[lev, dst, :] = (jnp.where(role > 0, q, k) * jnp.exp2((cum - edge) * role)).astype(BF16)
            for item in between:
                item()

        def states(kv_ref=kv_ref, st_ref=st_ref, vv_ref=vv_ref, kh_ref=kh_ref, tot_ref=tot_ref):
            def kv_body(c, carry):
                rows = _rows(c * C, C, C)
                kv_ref[c] = _dot_tn(vv_ref[rows, :], kh_ref[rows, :])
                return carry

            lax.fori_loop(0, n_chunks, kv_body, 0, unroll=GLA_UNROLL)

            def scan(order_of, lanes):
                def body(n, s):
                    c = order_of(n)
                    st_ref[c, :, lanes] = s[:, lanes].astype(BF16)
                    return s * jnp.exp(tot_ref[pl.ds(c * C, 1), :]) + kv_ref[c]
                lax.fori_loop(0, n_chunks, body, jnp.zeros((GLA_DV, LANES), F32))

            scan(lambda n: n, slice(0, GLA_DK))
            scan(lambda n: jnp.where(n < nc_ctx, nc_ctx - 1 - n, n_chunks + nc_ctx - 1 - n), slice(GLA_DK, LANES))

        def block_diag(x):
            zero = jnp.zeros_like(x)
            return jnp.concatenate([jnp.where(fwd_c, x, zero), jnp.where(fwd_c, zero, x)], axis=0)

        def out_chunk(c, g_ref, o_ref, local, hh=hh, xs_ref=xs_ref, qd_ref=qd_ref, kd_ref=kd_ref, qh_ref=qh_ref,
                      vv_ref=vv_ref, st_ref=st_ref):
            rows = _rows(c * C, C, C)
            a = jnp.where(lvl == GLA_LEVELS, _dot_nt(qd_ref[rows, :], block_diag(kd_ref[rows, :])), 0.0)
            for lev in range(GLA_LEVELS):
                x = xs_ref[lev, rows, :]
                a = jnp.where(lvl == lev, _dot_nt(x, block_diag(x)), a)
            v = vv_ref[rows, :]
            o = _dot(a.astype(BF16), jnp.concatenate([v, v], axis=0)) + _dot_nt(qh_ref[rows, :], st_ref[c])
            y = _rms(o) * ng_ref[...]
            out_rows = _rows(local * C, C, C)
            y = y * _silu(g_ref[hh, out_rows, :].astype(F32))
            o_ref[out_rows, hh * GLA_DV:(hh + 1) * GLA_DV] = y.astype(BF16)

        passes.append((prep, states, out_chunk))

    per_slab = GLA_SLAB // C
    (prep0, states0, out0), (prep1, states1, out1) = passes

    def slab(prep, out, stream, i, g_ref, o_ref):
        q_ref, k_ref, v_ref, a_ref, off, _ = stream
        chunks = []
        if out is not None:
            chunks = [functools.partial(out, off // C + i * per_slab + j, g_ref, o_ref, i * per_slab + j)
                      for j in range(per_slab)]
        if prep is not None:
            prep(i, chunks, q_ref=q_ref, k_ref=k_ref, v_ref=v_ref, a_ref=a_ref, off=off)
        else:
            for item in chunks:
                item()

    def run(prep, out):
        for stream, g_ref, o_ref in ((streams[0], gc_ref, oc_ref), (streams[1], gl_ref, ol_ref)):
            n_slabs = stream[5] // GLA_SLAB
            if n_slabs == 1:
                slab(prep, out, stream, 0, g_ref, o_ref)
            else:
                def body(i, carry, stream=stream, g_ref=g_ref, o_ref=o_ref):
                    slab(prep, out, stream, i, g_ref, o_ref)
                    return carry
                lax.fori_loop(0, n_slabs, body, 0, unroll=2 if (out is None and n_slabs % 2 == 0) else 1)

    run(prep0, None)
    states0()
    run(prep1, out0)
    states1()
    run(None, out1)


def _gla_gate_params(gate_up, gate_b):
    r = GLA_GATE_RANK
    gu = gate_up.astype(BF16).reshape(2, r, GLA_HEADS, GLA_DK).transpose(2, 0, 1, 3)
    up = jnp.zeros((GLA_HEADS, LANES, 2 * GLA_DK), BF16)
    up = up.at[:, 0:r, 0:GLA_DK].set(gu[:, 0]).at[:, r:2 * r, GLA_DK:].set(gu[:, 1])
    gb = gate_b.reshape(2, GLA_HEADS, GLA_DK).transpose(1, 0, 2).reshape(GLA_HEADS, 1, 2 * GLA_DK)
    return up, gb


def _gla(zc, zl, up_heads, gate_b_heads, ng, *, batch, ctx_len, lat_len):
    total = ctx_len + lat_len
    n_chunks = total // GLA_CHUNK

    def pair(base, n):
        return pl.BlockSpec((1, n, LANES), lambda b, p: (base + p, b, 0))

    def two(base, n):
        return pl.BlockSpec((2, n, LANES), lambda b, p: (base // 2 + p, b, 0))

    def one(n):
        return pl.BlockSpec((1, n, LANES), lambda b, p: (SLOT_GA, b, 0))

    role = _gla_role_table()
    specs = [
        pl.BlockSpec((GLA_CHUNK, 2 * GLA_CHUNK), lambda b, p: (0, 0)),
        pl.BlockSpec(role.shape, lambda b, p: (0, 0, 0)),
        pl.BlockSpec((2, LANES, LANES), lambda b, p: (p, 0, 0)),
        pl.BlockSpec((2, 1, LANES), lambda b, p: (p, 0, 0)),
        pl.BlockSpec((1, GLA_DV), lambda b, p: (0, 0)),
    ]
    for n in (ctx_len, lat_len):
        specs += [pair(SLOT_GQ, n), pair(SLOT_GK, n), two(SLOT_GV, n), two(SLOT_GR, n), one(n)]
    return pl.pallas_call(
        functools.partial(_gla_kernel, ctx_len=ctx_len, lat_len=lat_len),
        grid=(batch, GLA_HEADS // 2),
        in_specs=specs,
        out_specs=[pl.BlockSpec((ctx_len, 2 * GLA_DV), lambda b, p: (b, p)),
                   pl.BlockSpec((lat_len, 2 * GLA_DV), lambda b, p: (b, p))],
        out_shape=[jax.ShapeDtypeStruct((batch * ctx_len, GLA_HEADS * GLA_DV), BF16),
                   jax.ShapeDtypeStruct((batch * lat_len, GLA_HEADS * GLA_DV), BF16)],
        scratch_shapes=[
            pltpu.VMEM((2, GLA_LEVELS, total, LANES), BF16),
            pltpu.VMEM((2, total, LANES), BF16),
            pltpu.VMEM((2, total, LANES), BF16),
            pltpu.VMEM((2, total, LANES), BF16),
            pltpu.VMEM((2, total, LANES), BF16),
            pltpu.VMEM((2, total, LANES), F32),
            pltpu.VMEM((2, total, GLA_DV), BF16),
            pltpu.VMEM((2, n_chunks, GLA_DV, LANES), F32),
            pltpu.VMEM((2, n_chunks, GLA_DV, LANES), BF16),
        ],
        compiler_params=_cparams(("arbitrary", "arbitrary")),
        name="gla",
    )(jnp.asarray(_gla_level_table()), jnp.asarray(role), up_heads, gate_b_heads, ng,
      zc, zc, zc, zc, zc, zl, zl, zl, zl, zl)


def _outproj_kernel(att_ref, ret_ref, gla_ref, w_ref, x_ref, gate_ref, sh_ref, sc_ref, g_ref, xo_ref, ho_ref):
    na, nr = att_ref.shape[1], ret_ref.shape[1]
    y = (_dot(att_ref[...], w_ref[0, 0:na, :]) + _dot(ret_ref[...], w_ref[0, na:na + nr, :])
         + _dot(gla_ref[...], w_ref[0, na + nr:, :]))
    xn = x_ref[...] + gate_ref[0] * y
    xo_ref[...] = xn
    ho_ref[...] = (_rms(xn) * g_ref[...] * (1.0 + sc_ref[0]) + sh_ref[0]).astype(BF16)


def _outproj(att, ret, gla, w, layer, x2, gate, shift, scale, g, *, tm, rows_per_mod):
    m, d = x2.shape

    def rows(n):
        return pl.BlockSpec((tm, n), lambda i: (i, 0))

    def mod():
        return pl.BlockSpec((1, 1, d), lambda i: ((i * tm) // rows_per_mod, 0, 0))

    return pl.pallas_call(
        _outproj_kernel,
        grid=(m // tm,),
        in_specs=[rows(att.shape[1]), rows(ret.shape[1]), rows(gla.shape[1]),
                  pl.BlockSpec((1,) + w.shape[1:], lambda i: (layer, 0, 0), pipeline_mode=pl.Buffered(1)),
                  rows(d), mod(), mod(), mod(),
                  pl.BlockSpec((1, d), lambda i: (0, 0))],
        out_specs=[rows(d), rows(d)],
        out_shape=[jax.ShapeDtypeStruct((m, d), F32), jax.ShapeDtypeStruct((m, d), BF16)],
        compiler_params=_cparams(("arbitrary",)),
        name="out_proj",
    )(att, ret, gla, w, x2, gate, shift, scale, g)


FFN_HALO = 16
FFN_TF = 512
FFN_TM = 512


def _ffn_kernel(h_ref, hn_ref, hp_ref, wa_ref, wv_ref, cw_ref, cb_ref, wd_ref, x_ref, gate_ref, fg_ref,
                o_ref, hs_ref, *, tm, seq_len, final_norm):
    i = pl.program_id(0)
    f = pl.program_id(1)
    ext = tm + 2 * FFN_HALO

    @pl.when(f == 0)
    def _():
        hs_ref[0:tm, :] = h_ref[...]
        hs_ref[tm:tm + FFN_HALO, :] = hn_ref[...]
        hs_ref[tm + FFN_HALO:ext, :] = hp_ref[...]
        o_ref[...] = jnp.zeros_like(o_ref)

    a = _dot(hs_ref[...], wa_ref[0])
    pos = (i * tm + lax.broadcasted_iota(jnp.int32, (ext, 1), 0)) % seq_len
    prev = jnp.where(pos == 0, 0.0, pltpu.roll(a, 1, 0))
    nxt = jnp.where(pos == seq_len - 1, 0.0, pltpu.roll(a, ext - 1, 0))
    conv = prev * cw_ref[0:1, :] + a * cw_ref[1:2, :] + nxt * cw_ref[2:3, :] + cb_ref[...]
    v = _dot(h_ref[...], wv_ref[0])
    u = (_silu(conv[0:tm]) * v).astype(BF16)
    o_ref[...] += _dot(u, wd_ref[0])

    @pl.when(f == pl.num_programs(1) - 1)
    def _():
        xn = x_ref[...] + gate_ref[0] * o_ref[...]
        if final_norm:
            xn = _rms(xn) * fg_ref[...]
        o_ref[...] = xn


def _ffn(h2, w_up, conv_w, conv_b, w_down, x2, gate, fg, *, tm, seq_len, rows_per_mod, final_norm):
    m, d = x2.shape
    tf = w_up.shape[2]
    nf = D_FF // tf
    hb = tm // FFN_HALO
    last = m // FFN_HALO - 1
    return pl.pallas_call(
        functools.partial(_ffn_kernel, tm=tm, seq_len=seq_len, final_norm=final_norm),
        grid=(m // tm, nf),
        in_specs=[
            pl.BlockSpec((tm, d), lambda i, f: (i, 0)),
            pl.BlockSpec((FFN_HALO, d), lambda i, f: (jnp.minimum((i + 1) * hb, last), 0)),
            pl.BlockSpec((FFN_HALO, d), lambda i, f: (jnp.maximum(i * hb - 1, 0), 0)),
            pl.BlockSpec((1, d, tf), lambda i, f: (f, 0, 0)),
            pl.BlockSpec((1, d, tf), lambda i, f: (nf + f, 0, 0)),
            pl.BlockSpec((3, tf), lambda i, f: (0, f)),
            pl.BlockSpec((1, tf), lambda i, f: (0, f)),
            pl.BlockSpec((1, tf, d), lambda i, f: (0, f, 0)),
            pl.BlockSpec((tm, d), lambda i, f: (i, 0)),
            pl.BlockSpec((1, 1, d), lambda i, f: ((i * tm) // rows_per_mod, 0, 0)),
            pl.BlockSpec((1, d), lambda i, f: (0, 0)),
        ],
        out_specs=pl.BlockSpec((tm, d), lambda i, f: (i, 0)),
        out_shape=jax.ShapeDtypeStruct((m, d), F32),
        scratch_shapes=[pltpu.VMEM((tm + 2 * FFN_HALO, d), BF16)],
        compiler_params=_cparams(("arbitrary", "arbitrary")),
        name="conv_glu",
    )(h2, h2, h2, w_up, w_up, conv_w, conv_b, w_down, x2, gate, fg)


def _rope_tables(n_tokens):
    rows = n_tokens // GRID_W
    row = jnp.repeat(jnp.arange(rows, dtype=F32), GRID_W)
    col = jnp.tile(jnp.arange(GRID_W, dtype=F32), rows)
    n_freq = HEAD_DIM // 4
    inv_freq = ROPE_THETA ** (-jnp.arange(n_freq, dtype=F32) / n_freq)
    ang = jnp.concatenate([row[:, None] * inv_freq, col[:, None] * inv_freq], axis=-1)
    cos, sin = jnp.cos(ang), jnp.sin(ang)
    return jnp.concatenate([cos, cos], axis=-1), jnp.concatenate([-sin, sin], axis=-1)


def kernel(x, c, ctx, c_ctx, ada_w, ada_b, norm1_g, w_in, q_norm_g, k_norm_g, ret_log_decay, ret_norm_g,
           gla_gate_up, gla_gate_b, gla_norm_g, w_out, norm2_g, w_up, conv_w, conv_b, w_down, final_norm_g):
    batch, lat_len, d = x.shape
    ctx_len = ctx.shape[1]
    depth = ada_w.shape[0]
    mod_rows = 16
    cc = jnp.concatenate([c, c_ctx[None], jnp.zeros((mod_rows - batch - 1, d), F32)], axis=0)
    mod = _ada(cc, ada_w, ada_b).reshape(depth, mod_rows, N_MOD, d)

    cos_l, sin_l = _rope_tables(lat_len)
    proj_tm = 512
    cos_c = jnp.ones((proj_tm, LANES), F32)
    sin_c = jnp.zeros((proj_tm, LANES), F32)

    xl = x.reshape(batch * lat_len, d)
    xc = ctx.reshape(batch * ctx_len, d)
    row = lambda v: v.reshape(1, -1)
    w_in_b = w_in[0].astype(BF16)[None]
    w_out_b = w_out[0].astype(BF16)[None]

    for l in range(depth):
        last = l == depth - 1
        ml = [mod[l, :batch, k].reshape(batch, 1, d) for k in range(N_MOD)]
        mc = [mod[l, batch, k].reshape(1, 1, d) for k in range(N_MOD)]
        up_heads, gate_b_heads = _gla_gate_params(gla_gate_up[l], gla_gate_b[l])

        zl, vtl = _proj(xl, ml[0], ml[1], row(norm1_g[l]), w_in_b, 0, cos_l, sin_l, row(q_norm_g[l]), row(k_norm_g[l]),
                        tm=proj_tm, rows_per_mod=lat_len, rope_tiles=lat_len // proj_tm)
        zc, vtc = _proj(xc, mc[0], mc[1], row(norm1_g[l]), w_in_b, 0, cos_c, sin_c, row(q_norm_g[l]), row(k_norm_g[l]),
                        tm=proj_tm, rows_per_mod=batch * ctx_len, rope_tiles=1)

        casts = [(w_up[l], FFN_TF), (w_down[l], None)]
        if not last:
            casts += [(w_in[l + 1], None), (w_out[l + 1], None)]
        att_l, cast = _attn(zl, [(zc, vtc, ctx_len), (zl, vtl, lat_len)], batch=batch, q_len=lat_len, tq=256,
                            casts=casts)
        w_up_b, w_down_b = cast[0], cast[1][None]
        ret_c, ret_l = _ret(zc, zl, ret_log_decay[l], row(ret_norm_g[l]),
                            batch=batch, ctx_len=ctx_len, lat_len=lat_len)
        gla_c, gla_l = _gla(zc, zl, up_heads, gate_b_heads, row(gla_norm_g[l]),
                            batch=batch, ctx_len=ctx_len, lat_len=lat_len)

        xl, h2 = _outproj(att_l, ret_l, gla_l, w_out_b, 0, xl, ml[2], ml[3], ml[4], row(norm2_g[l]),
                          tm=256, rows_per_mod=lat_len)
        xl = _ffn(h2, w_up_b, conv_w[l], row(conv_b[l]), w_down_b, xl, ml[5], row(final_norm_g),
                  tm=FFN_TM, seq_len=lat_len, rows_per_mod=lat_len, final_norm=last)

        if not last:
            att_c, _ = _attn(zc, [(zc, vtc, ctx_len)], batch=batch, q_len=ctx_len, tq=ctx_len)
            xc, hc2 = _outproj(att_c, ret_c, gla_c, w_out_b, 0, xc, mc[2], mc[3], mc[4], row(norm2_g[l]),
                               tm=256, rows_per_mod=batch * ctx_len)
            xc = _ffn(hc2, w_up_b, conv_w[l], row(conv_b[l]), w_down_b, xc, mc[5], row(final_norm_g),
                      tm=FFN_TM, seq_len=ctx_len, rows_per_mod=batch * ctx_len, final_norm=False)
            w_in_b, w_out_b = cast[2][None], cast[3][None]

    return xl.reshape(batch, lat_len, d)
```

```python
import functools

import numpy as np
import jax
import jax.numpy as jnp
from jax import lax
from jax.experimental import pallas as pl
from jax.experimental.pallas import tpu as pltpu

F32 = jnp.float32
BF16 = jnp.bfloat16

D_MODEL = 2048
HEAD_DIM = 128
GRID_W = 64
ATT_Q_HEADS = 8
ATT_KV_HEADS = 2
GQA = ATT_Q_HEADS // ATT_KV_HEADS
RET_HEADS = 4
RET_DIM = 128
RET_CHUNK = 128
GLA_HEADS = 4
GLA_DK = 64
GLA_DV = 128
GLA_GATE_RANK = 16
GLA_TAU = 16.0
GLA_CHUNK = 64
GLA_LEVELS = 6
D_FF = 5632
ROPE_THETA = 10000.0
N_MOD = 6
EPS = 1e-6

LOG2_E = 1.4426950408889634

LANES = 128
N_IN = 5152
PROJ_TN = 256
N_SLOTS = -(-N_IN // LANES)
SLOT_AQ, SLOT_AK, SLOT_AV = 0, 8, 10
SLOT_RQ, SLOT_RK, SLOT_RV, SLOT_RG = 12, 16, 20, 24
SLOT_GQ, SLOT_GK, SLOT_GV, SLOT_GR, SLOT_GA = 28, 30, 32, 36, 40

VMEM_LIMIT = 52 * 1024 * 1024

_NT = (((1,), (1,)), ((), ()))
_TN = (((0,), (0,)), ((), ()))


def _cparams(sem, vmem_limit=VMEM_LIMIT):
    return pltpu.CompilerParams(dimension_semantics=sem, vmem_limit_bytes=vmem_limit)


def _dot(a, b):
    return jnp.dot(a, b, preferred_element_type=F32)


def _dot_nt(a, b):
    return lax.dot_general(a, b, _NT, preferred_element_type=F32)


def _dot_tn(a, b):
    return lax.dot_general(a, b, _TN, preferred_element_type=F32)


def _rows(start, size, align):
    if isinstance(start, int):
        return pl.ds(start, size)
    return pl.ds(pl.multiple_of(start, align), size)


def _rms(t):
    return t * lax.rsqrt(jnp.mean(t * t, axis=-1, keepdims=True) + EPS)


def _silu(t):
    return t * jax.nn.sigmoid(t)


def _ada_kernel(c_ref, w_ref, b_ref, o_ref):
    s = _silu(c_ref[...]).astype(BF16)
    o_ref[0] = _dot(s, w_ref[0].astype(BF16)) + b_ref[0]


def _ada(cc, ada_w, ada_b):
    depth, d, n = ada_w.shape
    rows = cc.shape[0]
    tn = 1024
    return pl.pallas_call(
        _ada_kernel,
        grid=(depth, n // tn),
        in_specs=[
            pl.BlockSpec((rows, d), lambda l, j: (0, 0)),
            pl.BlockSpec((1, d, tn), lambda l, j: (l, 0, j)),
            pl.BlockSpec((1, 1, tn), lambda l, j: (l, 0, j)),
        ],
        out_specs=pl.BlockSpec((1, rows, tn), lambda l, j: (l, 0, j)),
        out_shape=jax.ShapeDtypeStruct((depth, rows, n), F32),
        compiler_params=_cparams(("arbitrary", "arbitrary")),
        name="ada_mod",
    )(cc, ada_w, ada_b.reshape(depth, 1, n))


def _proj_kernel(x_ref, sh_ref, sc_ref, g_ref, w_ref, cos_ref, sin_ref, qg_ref, kg_ref, o_ref, vt_ref):
    y = _rms(x_ref[...]) * g_ref[...]
    h = (y * (1.0 + sc_ref[0]) + sh_ref[0]).astype(BF16)
    cos, sin = cos_ref[...], sin_ref[...]
    qg = qg_ref[...] * (HEAD_DIM ** -0.5 * LOG2_E)
    kg = kg_ref[...]

    def rope(t):
        return t * cos + pltpu.roll(t, HEAD_DIM // 2, 1) * sin

    def transform(slot, t):
        if slot < SLOT_AK:
            return rope(_rms(t) * qg)
        if slot < SLOT_AV:
            return rope(_rms(t) * kg)
        if SLOT_RQ <= slot < SLOT_RK:
            return rope(t)
        if SLOT_RK <= slot < SLOT_RV:
            return rope(t * (RET_DIM ** -0.5))
        if SLOT_GQ <= slot < SLOT_GK:
            return t * (GLA_DK ** -0.5)
        return t

    for c0 in range(0, N_IN, PROJ_TN):
        c1 = min(c0 + PROJ_TN, N_IN)
        z = _dot(h, w_ref[0, :, c0:c1])
        for s0 in range(c0, c1, LANES):
            slot, n = s0 // LANES, min(LANES, c1 - s0)
            t = transform(slot, z[:, s0 - c0:s0 - c0 + n])
            if SLOT_AV <= slot < SLOT_RQ:
                vt_ref[slot - SLOT_AV] = t.T.astype(BF16)
            t = t.astype(BF16)
            if n < LANES:
                o_ref[slot] = jnp.zeros(o_ref.shape[1:], BF16)
                o_ref[slot, :, 0:n] = t
            else:
                o_ref[slot] = t


def _proj(x2, shift, scale, g, w, layer, cosf, sinf, qg, kg, *, tm, rows_per_mod, rope_tiles):
    m, d = x2.shape
    return pl.pallas_call(
        _proj_kernel,
        grid=(m // tm,),
        in_specs=[
            pl.BlockSpec((tm, d), lambda i: (i, 0)),
            pl.BlockSpec((1, 1, d), lambda i: ((i * tm) // rows_per_mod, 0, 0)),
            pl.BlockSpec((1, 1, d), lambda i: ((i * tm) // rows_per_mod, 0, 0)),
            pl.BlockSpec((1, d), lambda i: (0, 0)),
            pl.BlockSpec((1, d, N_IN), lambda i: (layer, 0, 0), pipeline_mode=pl.Buffered(1)),
            pl.BlockSpec((tm, LANES), lambda i: (i % rope_tiles, 0)),
            pl.BlockSpec((tm, LANES), lambda i: (i % rope_tiles, 0)),
            pl.BlockSpec((1, LANES), lambda i: (0, 0)),
            pl.BlockSpec((1, LANES), lambda i: (0, 0)),
        ],
        out_specs=[pl.BlockSpec((N_SLOTS, tm, LANES), lambda i: (0, i, 0)),
                   pl.BlockSpec((ATT_KV_HEADS, HEAD_DIM, tm), lambda i: (0, 0, i))],
        out_shape=[jax.ShapeDtypeStruct((N_SLOTS, m, LANES), BF16),
                   jax.ShapeDtypeStruct((ATT_KV_HEADS, HEAD_DIM, m), BF16)],
        compiler_params=_cparams(("arbitrary",)),
        name="norm_proj",
    )(x2, shift, scale, g, w, cosf, sinf, qg, kg)


ATT_KEY_BLOCK = 128


def _attn_kernel(*refs, n_kv, reps):
    n_cast = len(reps)
    q_ref = refs[0]
    kv_refs = refs[1:1 + 2 * n_kv]
    w_refs = refs[1 + 2 * n_kv:1 + 2 * n_kv + n_cast]
    o_ref = refs[1 + 2 * n_kv + n_cast]
    wo_refs = refs[2 + 2 * n_kv + n_cast:]
    step = (pl.program_id(0) * pl.num_programs(1) + pl.program_id(1)) * pl.num_programs(2) + pl.program_id(2)

    def convert(w_ref, wo_ref):
        if len(wo_ref.shape) == 3:
            tn = wo_ref.shape[2]
            for t in range(wo_ref.shape[0]):
                wo_ref[t] = w_ref[0, :, t * tn:(t + 1) * tn].astype(BF16)
        else:
            wo_ref[...] = w_ref[0].astype(BF16)

    for w_ref, wo_ref, rep in zip(w_refs, wo_refs, reps):
        if rep == 1:
            convert(w_ref, wo_ref)
        else:
            pl.when(step % rep == 0)(functools.partial(convert, w_ref, wo_ref))
    m, l, acc = [None] * GQA, [None] * GQA, [None] * GQA
    units = [(s, j0, g) for s in range(n_kv) for j0 in range(0, kv_refs[2 * s].shape[1], ATT_KEY_BLOCK)
             for g in range(GQA)]

    def scores(unit):
        s, j0, g = unit
        return _dot_nt(kv_refs[2 * s][0, j0:j0 + ATT_KEY_BLOCK, :], q_ref[g])

    st_next = scores(units[0])
    for u, (s, j0, g) in enumerate(units):
        st = st_next
        if u + 1 < len(units):
            st_next = scores(units[u + 1])
        vt = kv_refs[2 * s + 1][0, :, j0:j0 + ATT_KEY_BLOCK]
        mb = jnp.max(st, axis=0, keepdims=True)
        if m[g] is None:
            m[g] = mb
            p = jnp.exp2(st - mb)
            l[g] = jnp.sum(p, axis=0, keepdims=True)
            acc[g] = _dot(vt, p.astype(BF16))
        else:
            m_new = jnp.maximum(m[g], mb)
            alpha = jnp.exp2(m[g] - m_new)
            p = jnp.exp2(st - m_new)
            l[g] = alpha * l[g] + jnp.sum(p, axis=0, keepdims=True)
            acc[g] = alpha * acc[g] + _dot(vt, p.astype(BF16))
            m[g] = m_new
    for g in range(GQA):
        o_ref[:, g * HEAD_DIM:(g + 1) * HEAD_DIM] = (acc[g] / l[g]).T.astype(BF16)


def _attn(zq, kv_sources, *, batch, q_len, tq, casts=()):
    nq = q_len // tq
    steps = batch * ATT_KV_HEADS * nq
    in_specs = [pl.BlockSpec((GQA, tq, HEAD_DIM), lambda b, k, i: (k, b * nq + i, 0))]
    args = [zq]
    for z, vt, kv_len in kv_sources:
        in_specs.append(pl.BlockSpec((1, kv_len, HEAD_DIM), lambda b, k, i: (SLOT_AK + k, b, 0)))
        in_specs.append(pl.BlockSpec((1, HEAD_DIM, kv_len), lambda b, k, i: (k, 0, b)))
        args += [z, vt]
    out_specs = [pl.BlockSpec((tq, GQA * HEAD_DIM), lambda b, k, i: (b * nq + i, k))]
    out_shape = [jax.ShapeDtypeStruct((batch * q_len, ATT_Q_HEADS * HEAD_DIM), BF16)]
    step = lambda b, k, i: (b * ATT_KV_HEADS + k) * nq + i
    reps = []
    for w, layer, tile in casts:
        _, rows, cols = w.shape
        rep = next(r for r in (1, 2, 4, 8, 16) if rows % (steps // r) == 0 and rows // (steps // r) % 16 == 0)
        slab = rows // (steps // rep)
        reps.append(rep)
        in_specs.append(pl.BlockSpec((1, slab, cols),
                                     lambda b, k, i, rep=rep, layer=layer: (layer, step(b, k, i) // rep, 0)))
        args.append(w)
        if tile is None:
            out_specs.append(pl.BlockSpec((slab, cols), lambda b, k, i, rep=rep: (step(b, k, i) // rep, 0)))
            out_shape.append(jax.ShapeDtypeStruct((rows, cols), BF16))
        else:
            out_specs.append(pl.BlockSpec((cols // tile, slab, tile),
                                          lambda b, k, i, rep=rep: (0, step(b, k, i) // rep, 0)))
            out_shape.append(jax.ShapeDtypeStruct((cols // tile, rows, tile), BF16))
    outs = pl.pallas_call(
        functools.partial(_attn_kernel, n_kv=len(kv_sources), reps=tuple(reps)),
        grid=(batch, ATT_KV_HEADS, nq),
        in_specs=in_specs,
        out_specs=out_specs,
        out_shape=out_shape,
        compiler_params=_cparams(("arbitrary", "arbitrary", "arbitrary")),
        name="attention",
    )(*args)
    return outs[0], list(outs[1:])


def _ret_kernel(dec_ref, qc_ref, kc_ref, vc_ref, gc_ref, ql_ref, kl_ref, vl_ref, gl_ref, ng_ref,
                oc_ref, ol_ref, kv_ref, st_ref, *, nc_ctx, nc_lat):
    h = pl.program_id(1)
    C = RET_CHUNK
    lg_f, lg_b = dec_ref[0, h], dec_ref[1, h]
    rel = lax.broadcasted_iota(jnp.int32, (C, C), 0) - lax.broadcasted_iota(jnp.int32, (C, C), 1)
    dmat = (jnp.where(rel >= 0, jnp.exp(lg_f * jnp.maximum(rel, 0).astype(F32)), 0.0)
            + jnp.where(rel <= 0, jnp.exp(lg_b * jnp.maximum(-rel, 0).astype(F32)), 0.0))
    row = lax.broadcasted_iota(jnp.int32, (C, 1), 0).astype(F32)
    qd_f, qd_b = jnp.exp(lg_f * (row + 1.0)), jnp.exp(lg_b * (C - row))
    kd_f, kd_b = jnp.exp(lg_f * (C - 1.0 - row)), jnp.exp(lg_b * row)
    cd_f, cd_b = jnp.exp(lg_f * C), jnp.exp(lg_b * C)

    chunks = [(qc_ref, kc_ref, vc_ref, gc_ref, oc_ref, n) for n in range(nc_ctx)]
    chunks += [(ql_ref, kl_ref, vl_ref, gl_ref, ol_ref, n) for n in range(nc_lat)]
    order_b = list(range(nc_ctx - 1, -1, -1)) + list(range(nc_ctx + nc_lat - 1, nc_ctx - 1, -1))

    def decayed(x_ref, n, dec_f, dec_b):
        x = x_ref[0, n * C:(n + 1) * C, :].astype(F32)
        return jnp.concatenate([(x * dec_f).astype(BF16), (x * dec_b).astype(BF16)], axis=1)

    for c, (_, k_ref, v_ref, _, _, n) in enumerate(chunks):
        kv_ref[c] = _dot_tn(decayed(k_ref, n, kd_f, kd_b), v_ref[0, n * C:(n + 1) * C, :])

    s = jnp.zeros((C, RET_DIM), F32)
    for c in range(len(chunks)):
        st_ref[c, 0:C, :] = s.astype(BF16)
        s = cd_f * s + kv_ref[c, 0:C, :]
    s = jnp.zeros((C, RET_DIM), F32)
    for c in order_b:
        st_ref[c, C:2 * C, :] = s.astype(BF16)
        s = cd_b * s + kv_ref[c, C:2 * C, :]

    for c, (q_ref, k_ref, v_ref, g_ref, o_ref, n) in enumerate(chunks):
        sl = slice(n * C, (n + 1) * C)
        v = v_ref[0, sl, :]
        att = _dot_nt(q_ref[0, sl, :], k_ref[0, sl, :]) * dmat
        o = _dot(att.astype(BF16), v) + _dot(decayed(q_ref, n, qd_f, qd_b), st_ref[c])
        y = _rms(o) * ng_ref[...]
        o_ref[sl, :] = (y * _silu(g_ref[0, sl, :].astype(F32))).astype(BF16)


def _ret(zc, zl, decay, ng, *, batch, ctx_len, lat_len):
    def slot(base, n):
        return pl.BlockSpec((1, n, RET_DIM), lambda b, h: (base + h, b, 0))

    n_chunks = (ctx_len + lat_len) // RET_CHUNK
    specs = [pl.BlockSpec(memory_space=pltpu.SMEM)]
    specs += [slot(s, ctx_len) for s in (SLOT_RQ, SLOT_RK, SLOT_RV, SLOT_RG)]
    specs += [slot(s, lat_len) for s in (SLOT_RQ, SLOT_RK, SLOT_RV, SLOT_RG)]
    specs += [pl.BlockSpec((1, RET_DIM), lambda b, h: (0, 0))]
    return pl.pallas_call(
        functools.partial(_ret_kernel, nc_ctx=ctx_len // RET_CHUNK, nc_lat=lat_len // RET_CHUNK),
        grid=(batch, RET_HEADS),
        in_specs=specs,
        out_specs=[pl.BlockSpec((ctx_len, RET_DIM), lambda b, h: (b, h)),
                   pl.BlockSpec((lat_len, RET_DIM), lambda b, h: (b, h))],
        out_shape=[jax.ShapeDtypeStruct((batch * ctx_len, RET_HEADS * RET_DIM), BF16),
                   jax.ShapeDtypeStruct((batch * lat_len, RET_HEADS * RET_DIM), BF16)],
        scratch_shapes=[pltpu.VMEM((n_chunks, 2 * RET_DIM, RET_DIM), F32),
                        pltpu.VMEM((n_chunks, 2 * RET_DIM, RET_DIM), BF16)],
        compiler_params=_cparams(("arbitrary", "arbitrary")),
        name="retention",
    )(decay, zc, zc, zc, zc, zl, zl, zl, zl, ng)


GLA_SLAB = 256
GLA_UNROLL = 4


def _gla_level_table():
    i = np.arange(GLA_CHUNK)[:, None]
    j = np.arange(GLA_CHUNK)[None, :]
    x = np.maximum(i ^ j, 1)
    lvl = np.floor(np.log2(x)).astype(np.int32)
    lvl = np.where(i == j, GLA_LEVELS, np.where(i > j, lvl, -1)).astype(np.int32)
    return np.concatenate([lvl, lvl.T], axis=1)


def _gla_role_table():
    t = np.arange(GLA_SLAB)
    fwd = np.arange(LANES)[None, :] < GLA_DK
    return np.stack([np.where(((t[:, None] >> lev) & 1).astype(bool) == fwd, LOG2_E, -LOG2_E)
                     for lev in range(GLA_LEVELS)]).astype(np.float32)


def _bcast_block_row(x, blk, r):
    rows = x.shape[0]
    if blk >= 8:
        x3 = x.reshape(rows // blk, blk, LANES)
        return jnp.broadcast_to(x3[:, r:r + 1, :], x3.shape).reshape(rows, LANES)
    pos = lax.broadcasted_iota(jnp.int32, x.shape, 0) & (blk - 1)
    out = x
    for p in range(blk):
        if p != r:
            out = jnp.where(pos == p, pltpu.roll(x, (p - r) % rows, 0), out)
    return out


def _gla_kernel(lvl_ref, role_ref, up_ref, gb_ref, ng_ref,
                qc_ref, kc_ref, vc_ref, gc_ref, ac_ref, ql_ref, kl_ref, vl_ref, gl_ref, al_ref,
                oc_ref, ol_ref, *scratch, ctx_len, lat_len):
    C = GLA_CHUNK
    streams = ((qc_ref, kc_ref, vc_ref, ac_ref, 0, ctx_len), (ql_ref, kl_ref, vl_ref, al_ref, ctx_len, lat_len))
    nc_ctx, nc_lat = ctx_len // C, lat_len // C
    n_chunks = nc_ctx + nc_lat
    cpos = lax.broadcasted_iota(jnp.int32, (GLA_SLAB, LANES), 0) & (C - 1)
    fwd = lax.broadcasted_iota(jnp.int32, (GLA_SLAB, LANES), 1) < GLA_DK
    fwd_c = lax.broadcasted_iota(jnp.int32, (C, LANES), 1) < GLA_DK
    lvl = lvl_ref[...]

    passes = []
    for hh in range(2):
        own = fwd if hh == 0 else jnp.logical_not(fwd)
        xs_ref, qd_ref, kd_ref, qh_ref, kh_ref, tot_ref, vv_ref, kv_ref, st_ref = (r.at[hh] for r in scratch)

        def prep(i, between, q_ref, k_ref, v_ref, a_ref, off, hh=hh, own=own, xs_ref=xs_ref, qd_ref=qd_ref,
                 kd_ref=kd_ref, qh_ref=qh_ref, kh_ref=kh_ref, tot_ref=tot_ref, vv_ref=vv_ref):
            between = list(between)
            src = _rows(i * GLA_SLAB, GLA_SLAB, GLA_SLAB)
            dst = _rows(off + i * GLA_SLAB, GLA_SLAB, C)
            logit = _dot(a_ref[0, src, :], up_ref[hh]) + gb_ref[hh]
            la = (jnp.minimum(logit, 0.0) - jnp.log1p(jnp.exp(-jnp.abs(logit)))) * (1.0 / GLA_TAU)
            cum = la
            for step in range(GLA_LEVELS):
                sh = 1 << step
                before = jnp.where(cpos >= sh, pltpu.roll(cum, sh, 0), 0.0)
                after = jnp.where(cpos < C - sh, pltpu.roll(cum, GLA_SLAB - sh, 0), 0.0)
                cum = cum + jnp.where(fwd, before, after)
            tot = jnp.where(fwd, _bcast_block_row(cum, C, C - 1), _bcast_block_row(cum, C, 0))
            q2 = q_ref[0, src, :].astype(F32)
            k2 = k_ref[0, src, :].astype(F32)
            q = jnp.where(own, q2, pltpu.roll(q2, GLA_DK, 1))
            k = jnp.where(own, k2, pltpu.roll(k2, GLA_DK, 1))
            qd_ref[dst, :] = q.astype(BF16)
            kd_ref[dst, :] = k.astype(BF16)
            qh_ref[dst, :] = (q * jnp.exp(cum)).astype(BF16)
            kh_ref[dst, :] = (k * jnp.exp(tot - cum)).astype(BF16)
            tot_ref[dst, :] = tot
            vv_ref[dst, :] = v_ref[hh, src, :]
            for lev in range(GLA_LEVELS):
                if between and lev % 2 == 0:
                    between.pop(0)()
                s = 1 << lev
                edge = jnp.where(fwd, _bcast_block_row(cum, 2 * s, s - 1), _bcast_block_row(cum, 2 * s, s))
                role = role_ref[lev]
                xs_ref[lev, dst, :] = (jnp.where(role > 0, q, k) * jnp.exp2((cum - edge) * role)).astype(BF16)
            for item in between:
                item()

        def states(kv_ref=kv_ref, st_ref=st_ref, vv_ref=vv_ref, kh_ref=kh_ref, tot_ref=tot_ref):
            def kv_body(c, carry):
                rows = _rows(c * C, C, C)
                kv_ref[c] = _dot_tn(vv_ref[rows, :], kh_ref[rows, :])
                return carry

            lax.fori_loop(0, n_chunks, kv_body, 0, unroll=GLA_UNROLL)

            def scan(order_of, lanes):
                def body(n, s):
                    c = order_of(n)
                    st_ref[c, :, lanes] = s[:, lanes].astype(BF16)
                    return s * jnp.exp(tot_ref[pl.ds(c * C, 1), :]) + kv_ref[c]
                lax.fori_loop(0, n_chunks, body, jnp.zeros((GLA_DV, LANES), F32))

            scan(lambda n: n, slice(0, GLA_DK))
            scan(lambda n: jnp.where(n < nc_ctx, nc_ctx - 1 - n, n_chunks + nc_ctx - 1 - n), slice(GLA_DK, LANES))

        def block_diag(x):
            zero = jnp.zeros_like(x)
            return jnp.concatenate([jnp.where(fwd_c, x, zero), jnp.where(fwd_c, zero, x)], axis=0)

        def out_chunk(c, g_ref, o_ref, local, hh=hh, xs_ref=xs_ref, qd_ref=qd_ref, kd_ref=kd_ref, qh_ref=qh_ref,
                      vv_ref=vv_ref, st_ref=st_ref):
            rows = _rows(c * C, C, C)
            a = jnp.where(lvl == GLA_LEVELS, _dot_nt(qd_ref[rows, :], block_diag(kd_ref[rows, :])), 0.0)
            for lev in range(GLA_LEVELS):
                x = xs_ref[lev, rows, :]
                a = jnp.where(lvl == lev, _dot_nt(x, block_diag(x)), a)
            v = vv_ref[rows, :]
            o = _dot(a.astype(BF16), jnp.concatenate([v, v], axis=0)) + _dot_nt(qh_ref[rows, :], st_ref[c])
            y = _rms(o) * ng_ref[...]
            out_rows = _rows(local * C, C, C)
            y = y * _silu(g_ref[hh, out_rows, :].astype(F32))
            o_ref[out_rows, hh * GLA_DV:(hh + 1) * GLA_DV] = y.astype(BF16)

        passes.append((prep, states, out_chunk))

    per_slab = GLA_SLAB // C
    (prep0, states0, out0), (prep1, states1, out1) = passes

    def slab(prep, out, stream, i, g_ref, o_ref):
        q_ref, k_ref, v_ref, a_ref, off, _ = stream
        chunks = []
        if out is not None:
            chunks = [functools.partial(out, off // C + i * per_slab + j, g_ref, o_ref, i * per_slab + j)
                      for j in range(per_slab)]
        if prep is not None:
            prep(i, chunks, q_ref=q_ref, k_ref=k_ref, v_ref=v_ref, a_ref=a_ref, off=off)
        else:
            for item in chunks:
                item()

    def run(prep, out):
        for stream, g_ref, o_ref in ((streams[0], gc_ref, oc_ref), (streams[1], gl_ref, ol_ref)):
            n_slabs = stream[5] // GLA_SLAB
            if n_slabs == 1:
                slab(prep, out, stream, 0, g_ref, o_ref)
            else:
                def body(i, carry, stream=stream, g_ref=g_ref, o_ref=o_ref):
                    slab(prep, out, stream, i, g_ref, o_ref)
                    return carry
                lax.fori_loop(0, n_slabs, body, 0, unroll=2 if (out is None and n_slabs % 2 == 0) else 1)

    run(prep0, None)
    states0()
    run(prep1, out0)
    states1()
    run(None, out1)


def _gla_gate_params(gate_up, gate_b):
    r = GLA_GATE_RANK
    gu = gate_up.astype(BF16).reshape(2, r, GLA_HEADS, GLA_DK).transpose(2, 0, 1, 3)
    up = jnp.zeros((GLA_HEADS, LANES, 2 * GLA_DK), BF16)
    up = up.at[:, 0:r, 0:GLA_DK].set(gu[:, 0]).at[:, r:2 * r, GLA_DK:].set(gu[:, 1])
    gb = gate_b.reshape(2, GLA_HEADS, GLA_DK).transpose(1, 0, 2).reshape(GLA_HEADS, 1, 2 * GLA_DK)
    return up, gb


def _gla(zc, zl, up_heads, gate_b_heads, ng, *, batch, ctx_len, lat_len):
    total = ctx_len + lat_len
    n_chunks = total // GLA_CHUNK

    def pair(base, n):
        return pl.BlockSpec((1, n, LANES), lambda b, p: (base + p, b, 0))

    def two(base, n):
        return pl.BlockSpec((2, n, LANES), lambda b, p: (base // 2 + p, b, 0))

    def one(n):
        return pl.BlockSpec((1, n, LANES), lambda b, p: (SLOT_GA, b, 0))

    role = _gla_role_table()
    specs = [
        pl.BlockSpec((GLA_CHUNK, 2 * GLA_CHUNK), lambda b, p: (0, 0)),
        pl.BlockSpec(role.shape, lambda b, p: (0, 0, 0)),
        pl.BlockSpec((2, LANES, LANES), lambda b, p: (p, 0, 0)),
        pl.BlockSpec((2, 1, LANES), lambda b, p: (p, 0, 0)),
        pl.BlockSpec((1, GLA_DV), lambda b, p: (0, 0)),
    ]
    for n in (ctx_len, lat_len):
        specs += [pair(SLOT_GQ, n), pair(SLOT_GK, n), two(SLOT_GV, n), two(SLOT_GR, n), one(n)]
    return pl.pallas_call(
        functools.partial(_gla_kernel, ctx_len=ctx_len, lat_len=lat_len),
        grid=(batch, GLA_HEADS // 2),
        in_specs=specs,
        out_specs=[pl.BlockSpec((ctx_len, 2 * GLA_DV), lambda b, p: (b, p)),
                   pl.BlockSpec((lat_len, 2 * GLA_DV), lambda b, p: (b, p))],
        out_shape=[jax.ShapeDtypeStruct((batch * ctx_len, GLA_HEADS * GLA_DV), BF16),
                   jax.ShapeDtypeStruct((batch * lat_len, GLA_HEADS * GLA_DV), BF16)],
        scratch_shapes=[
            pltpu.VMEM((2, GLA_LEVELS, total, LANES), BF16),
            pltpu.VMEM((2, total, LANES), BF16),
            pltpu.VMEM((2, total, LANES), BF16),
            pltpu.VMEM((2, total, LANES), BF16),
            pltpu.VMEM((2, total, LANES), BF16),
            pltpu.VMEM((2, total, LANES), F32),
            pltpu.VMEM((2, total, GLA_DV), BF16),
            pltpu.VMEM((2, n_chunks, GLA_DV, LANES), F32),
            pltpu.VMEM((2, n_chunks, GLA_DV, LANES), BF16),
        ],
        compiler_params=_cparams(("arbitrary", "arbitrary")),
        name="gla",
    )(jnp.asarray(_gla_level_table()), jnp.asarray(role), up_heads, gate_b_heads, ng,
      zc, zc, zc, zc, zc, zl, zl, zl, zl, zl)


def _outproj_kernel(att_ref, ret_ref, gla_ref, w_ref, x_ref, gate_ref, sh_ref, sc_ref, g_ref, xo_ref, ho_ref):
    na, nr = att_ref.shape[1], ret_ref.shape[1]
    y = (_dot(att_ref[...], w_ref[0, 0:na, :]) + _dot(ret_ref[...], w_ref[0, na:na + nr, :])
         + _dot(gla_ref[...], w_ref[0, na + nr:, :]))
    xn = x_ref[...] + gate_ref[0] * y
    xo_ref[...] = xn
    ho_ref[...] = (_rms(xn) * g_ref[...] * (1.0 + sc_ref[0]) + sh_ref[0]).astype(BF16)


def _outproj(att, ret, gla, w, layer, x2, gate, shift, scale, g, *, tm, rows_per_mod):
    m, d = x2.shape

    def rows(n):
        return pl.BlockSpec((tm, n), lambda i: (i, 0))

    def mod():
        return pl.BlockSpec((1, 1, d), lambda i: ((i * tm) // rows_per_mod, 0, 0))

    return pl.pallas_call(
        _outproj_kernel,
        grid=(m // tm,),
        in_specs=[rows(att.shape[1]), rows(ret.shape[1]), rows(gla.shape[1]),
                  pl.BlockSpec((1,) + w.shape[1:], lambda i: (layer, 0, 0), pipeline_mode=pl.Buffered(1)),
                  rows(d), mod(), mod(), mod(),
                  pl.BlockSpec((1, d), lambda i: (0, 0))],
        out_specs=[rows(d), rows(d)],
        out_shape=[jax.ShapeDtypeStruct((m, d), F32), jax.ShapeDtypeStruct((m, d), BF16)],
        compiler_params=_cparams(("arbitrary",)),
        name="out_proj",
    )(att, ret, gla, w, x2, gate, shift, scale, g)


FFN_HALO = 16
FFN_TF = 512
FFN_TM = 512


def _ffn_kernel(h_ref, hn_ref, hp_ref, wa_ref, wv_ref, cw_ref, cb_ref, wd_ref, x_ref, gate_ref, fg_ref,
                o_ref, hs_ref, *, tm, seq_len, final_norm):
    i = pl.program_id(0)
    f = pl.program_id(1)
    ext = tm + 2 * FFN_HALO

    @pl.when(f == 0)
    def _():
        hs_ref[0:tm, :] = h_ref[...]
        hs_ref[tm:tm + FFN_HALO, :] = hn_ref[...]
        hs_ref[tm + FFN_HALO:ext, :] = hp_ref[...]
        o_ref[...] = jnp.zeros_like(o_ref)

    a = _dot(hs_ref[...], wa_ref[0])
    pos = (i * tm + lax.broadcasted_iota(jnp.int32, (ext, 1), 0)) % seq_len
    prev = jnp.where(pos == 0, 0.0, pltpu.roll(a, 1, 0))
    nxt = jnp.where(pos == seq_len - 1, 0.0, pltpu.roll(a, ext - 1, 0))
    conv = prev * cw_ref[0:1, :] + a * cw_ref[1:2, :] + nxt * cw_ref[2:3, :] + cb_ref[...]
    v = _dot(h_ref[...], wv_ref[0])
    u = (_silu(conv[0:tm]) * v).astype(BF16)
    o_ref[...] += _dot(u, wd_ref[0])

    @pl.when(f == pl.num_programs(1) - 1)
    def _():
        xn = x_ref[...] + gate_ref[0] * o_ref[...]
        if final_norm:
            xn = _rms(xn) * fg_ref[...]
        o_ref[...] = xn


def _ffn(h2, w_up, conv_w, conv_b, w_down, x2, gate, fg, *, tm, seq_len, rows_per_mod, final_norm):
    m, d = x2.shape
    tf = w_up.shape[2]
    nf = D_FF // tf
    hb = tm // FFN_HALO
    last = m // FFN_HALO - 1
    return pl.pallas_call(
        functools.partial(_ffn_kernel, tm=tm, seq_len=seq_len, final_norm=final_norm),
        grid=(m // tm, nf),
        in_specs=[
            pl.BlockSpec((tm, d), lambda i, f: (i, 0)),
            pl.BlockSpec((FFN_HALO, d), lambda i, f: (jnp.minimum((i + 1) * hb, last), 0)),
            pl.BlockSpec((FFN_HALO, d), lambda i, f: (jnp.maximum(i * hb - 1, 0), 0)),
            pl.BlockSpec((1, d, tf), lambda i, f: (f, 0, 0)),
            pl.BlockSpec((1, d, tf), lambda i, f: (nf + f, 0, 0)),
            pl.BlockSpec((3, tf), lambda i, f: (0, f)),
            pl.BlockSpec((1, tf), lambda i, f: (0, f)),
            pl.BlockSpec((1, tf, d), lambda i, f: (0, f, 0)),
            pl.BlockSpec((tm, d), lambda i, f: (i, 0)),
            pl.BlockSpec((1, 1, d), lambda i, f: ((i * tm) // rows_per_mod, 0, 0)),
            pl.BlockSpec((1, d), lambda i, f: (0, 0)),
        ],
        out_specs=pl.BlockSpec((tm, d), lambda i, f: (i, 0)),
        out_shape=jax.ShapeDtypeStruct((m, d), F32),
        scratch_shapes=[pltpu.VMEM((tm + 2 * FFN_HALO, d), BF16)],
        compiler_params=_cparams(("arbitrary", "arbitrary")),
        name="conv_glu",
    )(h2, h2, h2, w_up, w_up, conv_w, conv_b, w_down, x2, gate, fg)


def _rope_tables(n_tokens):
    rows = n_tokens // GRID_W
    row = jnp.repeat(jnp.arange(rows, dtype=F32), GRID_W)
    col = jnp.tile(jnp.arange(GRID_W, dtype=F32), rows)
    n_freq = HEAD_DIM // 4
    inv_freq = ROPE_THETA ** (-jnp.arange(n_freq, dtype=F32) / n_freq)
    ang = jnp.concatenate([row[:, None] * inv_freq, col[:, None] * inv_freq], axis=-1)
    cos, sin = jnp.cos(ang), jnp.sin(ang)
    return jnp.concatenate([cos, cos], axis=-1), jnp.concatenate([-sin, sin], axis=-1)


def kernel(x, c, ctx, c_ctx, ada_w, ada_b, norm1_g, w_in, q_norm_g, k_norm_g, ret_log_decay, ret_norm_g,
           gla_gate_up, gla_gate_b, gla_norm_g, w_out, norm2_g, w_up, conv_w, conv_b, w_down, final_norm_g):
    batch, lat_len, d = x.shape
    ctx_len = ctx.shape[1]
    depth = ada_w.shape[0]
    mod_rows = 16
    cc = jnp.concatenate([c, c_ctx[None], jnp.zeros((mod_rows - batch - 1, d), F32)], axis=0)
    mod = _ada(cc, ada_w, ada_b).reshape(depth, mod_rows, N_MOD, d)

    cos_l, sin_l = _rope_tables(lat_len)
    proj_tm = 512
    cos_c = jnp.ones((proj_tm, LANES), F32)
    sin_c = jnp.zeros((proj_tm, LANES), F32)

    xl = x.reshape(batch * lat_len, d)
    xc = ctx.reshape(batch * ctx_len, d)
    row = lambda v: v.reshape(1, -1)
    w_in_b = w_in.astype(BF16)
    w_out_b = w_out[0].astype(BF16)[None]

    for l in range(depth):
        last = l == depth - 1
        ml = [mod[l, :batch, k].reshape(batch, 1, d) for k in range(N_MOD)]
        mc = [mod[l, batch, k].reshape(1, 1, d) for k in range(N_MOD)]
        up_heads, gate_b_heads = _gla_gate_params(gla_gate_up[l], gla_gate_b[l])

        zl, vtl = _proj(xl, ml[0], ml[1], row(norm1_g[l]), w_in_b, l, cos_l, sin_l, row(q_norm_g[l]), row(k_norm_g[l]),
                        tm=proj_tm, rows_per_mod=lat_len, rope_tiles=lat_len // proj_tm)
        zc, vtc = _proj(xc, mc[0], mc[1], row(norm1_g[l]), w_in_b, l, cos_c, sin_c, row(q_norm_g[l]), row(k_norm_g[l]),
                        tm=proj_tm, rows_per_mod=batch * ctx_len, rope_tiles=1)

        casts = [(w_up, l, FFN_TF), (w_down, l, None)]
        if not last:
            casts += [(w_out, l + 1, None)]
        att_l, cast = _attn(zl, [(zc, vtc, ctx_len), (zl, vtl, lat_len)], batch=batch, q_len=lat_len, tq=256,
                            casts=casts)
        w_up_b, w_down_b = cast[0], cast[1][None]
        ret_c, ret_l = _ret(zc, zl, ret_log_decay[l], row(ret_norm_g[l]),
                            batch=batch, ctx_len=ctx_len, lat_len=lat_len)
        gla_c, gla_l = _gla(zc, zl, up_heads, gate_b_heads, row(gla_norm_g[l]),
                            batch=batch, ctx_len=ctx_len, lat_len=lat_len)

        xl, h2 = _outproj(att_l, ret_l, gla_l, w_out_b, 0, xl, ml[2], ml[3], ml[4], row(norm2_g[l]),
                          tm=256, rows_per_mod=lat_len)
        xl = _ffn(h2, w_up_b, conv_w[l], row(conv_b[l]), w_down_b, xl, ml[5], row(final_norm_g),
                  tm=FFN_TM, seq_len=lat_len, rows_per_mod=lat_len, final_norm=last)

        if not last:
            att_c, _ = _attn(zc, [(zc, vtc, ctx_len)], batch=batch, q_len=ctx_len, tq=ctx_len)
            xc, hc2 = _outproj(att_c, ret_c, gla_c, w_out_b, 0, xc, mc[2], mc[3], mc[4], row(norm2_g[l]),
                               tm=256, rows_per_mod=batch * ctx_len)
            xc = _ffn(hc2, w_up_b, conv_w[l], row(conv_b[l]), w_down_b, xc, mc[5], row(final_norm_g),
                      tm=FFN_TM, seq_len=ctx_len, rows_per_mod=batch * ctx_len, final_norm=False)
            w_out_b = cast[2][None]

    return xl.reshape(batch, lat_len, d)
```

```python
import functools

import numpy as np
import jax
import jax.numpy as jnp
from jax import lax
from jax.experimental import pallas as pl
from jax.experimental.pallas import tpu as pltpu

F32 = jnp.float32
BF16 = jnp.bfloat16

D_MODEL = 2048
HEAD_DIM = 128
GRID_W = 64
ATT_Q_HEADS = 8
ATT_KV_HEADS = 2
GQA = ATT_Q_HEADS // ATT_KV_HEADS
RET_HEADS = 4
RET_DIM = 128
RET_CHUNK = 128
GLA_HEADS = 4
GLA_DK = 64
GLA_DV = 128
GLA_GATE_RANK = 16
GLA_TAU = 16.0
GLA_CHUNK = 64
GLA_LEVELS = 6
D_FF = 5632
ROPE_THETA = 10000.0
N_MOD = 6
EPS = 1e-6

LOG2_E = 1.4426950408889634

LANES = 128
N_IN = 5152
PROJ_TN = 256
N_SLOTS = -(-N_IN // LANES)
SLOT_AQ, SLOT_AK, SLOT_AV = 0, 8, 10
SLOT_RQ, SLOT_RK, SLOT_RV, SLOT_RG = 12, 16, 20, 24
SLOT_GQ, SLOT_GK, SLOT_GV, SLOT_GR, SLOT_GA = 28, 30, 32, 36, 40

VMEM_LIMIT = 52 * 1024 * 1024

_NT = (((1,), (1,)), ((), ()))
_TN = (((0,), (0,)), ((), ()))


def _cparams(sem, vmem_limit=VMEM_LIMIT):
    return pltpu.CompilerParams(dimension_semantics=sem, vmem_limit_bytes=vmem_limit)


def _dot(a, b):
    return jnp.dot(a, b, preferred_element_type=F32)


def _dot_nt(a, b):
    return lax.dot_general(a, b, _NT, preferred_element_type=F32)


def _dot_tn(a, b):
    return lax.dot_general(a, b, _TN, preferred_element_type=F32)


def _rows(start, size, align):
    if isinstance(start, int):
        return pl.ds(start, size)
    return pl.ds(pl.multiple_of(start, align), size)


def _rms(t):
    return t * lax.rsqrt(jnp.mean(t * t, axis=-1, keepdims=True) + EPS)


def _silu(t):
    return t * jax.nn.sigmoid(t)


def _ada_kernel(c_ref, w_ref, b_ref, o_ref):
    s = _silu(c_ref[...]).astype(BF16)
    o_ref[0] = _dot(s, w_ref[0].astype(BF16)) + b_ref[0]


def _ada(cc, ada_w, ada_b):
    depth, d, n = ada_w.shape
    rows = cc.shape[0]
    tn = 1024
    return pl.pallas_call(
        _ada_kernel,
        grid=(depth, n // tn),
        in_specs=[
            pl.BlockSpec((rows, d), lambda l, j: (0, 0)),
            pl.BlockSpec((1, d, tn), lambda l, j: (l, 0, j)),
            pl.BlockSpec((1, 1, tn), lambda l, j: (l, 0, j)),
        ],
        out_specs=pl.BlockSpec((1, rows, tn), lambda l, j: (l, 0, j)),
        out_shape=jax.ShapeDtypeStruct((depth, rows, n), F32),
        compiler_params=_cparams(("arbitrary", "arbitrary")),
        name="ada_mod",
    )(cc, ada_w, ada_b.reshape(depth, 1, n))


def _proj_kernel(x_ref, sh_ref, sc_ref, g_ref, w_ref, cos_ref, sin_ref, qg_ref, kg_ref, o_ref, vt_ref):
    y = _rms(x_ref[...]) * g_ref[...]
    h = (y * (1.0 + sc_ref[0]) + sh_ref[0]).astype(BF16)
    cos, sin = cos_ref[...], sin_ref[...]
    qg = qg_ref[...] * (HEAD_DIM ** -0.5 * LOG2_E)
    kg = kg_ref[...]

    def rope(t):
        return t * cos + pltpu.roll(t, HEAD_DIM // 2, 1) * sin

    def transform(slot, t):
        if slot < SLOT_AK:
            return rope(_rms(t) * qg)
        if slot < SLOT_AV:
            return rope(_rms(t) * kg)
        if SLOT_RQ <= slot < SLOT_RK:
            return rope(t)
        if SLOT_RK <= slot < SLOT_RV:
            return rope(t * (RET_DIM ** -0.5))
        if SLOT_GQ <= slot < SLOT_GK:
            return t * (GLA_DK ** -0.5)
        return t

    for c0 in range(0, N_IN, PROJ_TN):
        c1 = min(c0 + PROJ_TN, N_IN)
        z = _dot(h, w_ref[0, :, c0:c1])
        for s0 in range(c0, c1, LANES):
            slot, n = s0 // LANES, min(LANES, c1 - s0)
            t = transform(slot, z[:, s0 - c0:s0 - c0 + n])
            if slot < SLOT_AK:
                vt_ref[slot] = t.T.astype(BF16)
            elif SLOT_AV <= slot < SLOT_RQ:
                vt_ref[ATT_Q_HEADS + slot - SLOT_AV] = t.T.astype(BF16)
            t = t.astype(BF16)
            if n < LANES:
                o_ref[slot] = jnp.zeros(o_ref.shape[1:], BF16)
                o_ref[slot, :, 0:n] = t
            else:
                o_ref[slot] = t


def _proj(x2, shift, scale, g, w, layer, cosf, sinf, qg, kg, *, tm, rows_per_mod, rope_tiles):
    m, d = x2.shape
    return pl.pallas_call(
        _proj_kernel,
        grid=(m // tm,),
        in_specs=[
            pl.BlockSpec((tm, d), lambda i: (i, 0)),
            pl.BlockSpec((1, 1, d), lambda i: ((i * tm) // rows_per_mod, 0, 0)),
            pl.BlockSpec((1, 1, d), lambda i: ((i * tm) // rows_per_mod, 0, 0)),
            pl.BlockSpec((1, d), lambda i: (0, 0)),
            pl.BlockSpec((1, d, N_IN), lambda i: (layer, 0, 0), pipeline_mode=pl.Buffered(1)),
            pl.BlockSpec((tm, LANES), lambda i: (i % rope_tiles, 0)),
            pl.BlockSpec((tm, LANES), lambda i: (i % rope_tiles, 0)),
            pl.BlockSpec((1, LANES), lambda i: (0, 0)),
            pl.BlockSpec((1, LANES), lambda i: (0, 0)),
        ],
        out_specs=[pl.BlockSpec((N_SLOTS, tm, LANES), lambda i: (0, i, 0)),
                   pl.BlockSpec((ATT_Q_HEADS + ATT_KV_HEADS, HEAD_DIM, tm), lambda i: (0, 0, i))],
        out_shape=[jax.ShapeDtypeStruct((N_SLOTS, m, LANES), BF16),
                   jax.ShapeDtypeStruct((ATT_Q_HEADS + ATT_KV_HEADS, HEAD_DIM, m), BF16)],
        compiler_params=_cparams(("arbitrary",)),
        name="norm_proj",
    )(x2, shift, scale, g, w, cosf, sinf, qg, kg)


ATT_KEY_BLOCK = 128


def _attn_kernel(*refs, n_kv, reps):
    n_cast = len(reps)
    q_ref = refs[0]
    kv_refs = refs[1:1 + 2 * n_kv]
    w_refs = refs[1 + 2 * n_kv:1 + 2 * n_kv + n_cast]
    o_ref = refs[1 + 2 * n_kv + n_cast]
    wo_refs = refs[2 + 2 * n_kv + n_cast:]
    step = (pl.program_id(0) * pl.num_programs(1) + pl.program_id(1)) * pl.num_programs(2) + pl.program_id(2)

    def convert(w_ref, wo_ref):
        if len(wo_ref.shape) == 3:
            tn = wo_ref.shape[2]
            for t in range(wo_ref.shape[0]):
                wo_ref[t] = w_ref[0, :, t * tn:(t + 1) * tn].astype(BF16)
        else:
            wo_ref[...] = w_ref[0].astype(BF16)

    for w_ref, wo_ref, rep in zip(w_refs, wo_refs, reps):
        if rep == 1:
            convert(w_ref, wo_ref)
        else:
            pl.when(step % rep == 0)(functools.partial(convert, w_ref, wo_ref))
    m, l, acc = [None] * GQA, [None] * GQA, [None] * GQA
    units = [(s, j0, g) for s in range(n_kv) for j0 in range(0, kv_refs[2 * s].shape[1], ATT_KEY_BLOCK)
             for g in range(GQA)]

    def scores(unit):
        s, j0, g = unit
        return _dot(kv_refs[2 * s][0, j0:j0 + ATT_KEY_BLOCK, :], q_ref[g])

    st_next = scores(units[0])
    for u, (s, j0, g) in enumerate(units):
        st = st_next
        if u + 1 < len(units):
            st_next = scores(units[u + 1])
        vt = kv_refs[2 * s + 1][0, :, j0:j0 + ATT_KEY_BLOCK]
        mb = jnp.max(st, axis=0, keepdims=True)
        if m[g] is None:
            m[g] = mb
            p = jnp.exp2(st - mb)
            l[g] = jnp.sum(p, axis=0, keepdims=True)
            acc[g] = _dot(vt, p.astype(BF16))
        else:
            m_new = jnp.maximum(m[g], mb)
            alpha = jnp.exp2(m[g] - m_new)
            p = jnp.exp2(st - m_new)
            l[g] = alpha * l[g] + jnp.sum(p, axis=0, keepdims=True)
            acc[g] = alpha * acc[g] + _dot(vt, p.astype(BF16))
            m[g] = m_new
    for g in range(GQA):
        o_ref[:, g * HEAD_DIM:(g + 1) * HEAD_DIM] = (acc[g] / l[g]).T.astype(BF16)


def _attn(qt, kv_sources, *, batch, q_len, tq, casts=()):
    nq = q_len // tq
    steps = batch * ATT_KV_HEADS * nq
    in_specs = [pl.BlockSpec((GQA, HEAD_DIM, tq), lambda b, k, i: (k, 0, b * nq + i))]
    args = [qt]
    for z, vt, kv_len in kv_sources:
        in_specs.append(pl.BlockSpec((1, kv_len, HEAD_DIM), lambda b, k, i: (SLOT_AK + k, b, 0)))
        in_specs.append(pl.BlockSpec((1, HEAD_DIM, kv_len), lambda b, k, i: (ATT_Q_HEADS + k, 0, b)))
        args += [z, vt]
    out_specs = [pl.BlockSpec((tq, GQA * HEAD_DIM), lambda b, k, i: (b * nq + i, k))]
    out_shape = [jax.ShapeDtypeStruct((batch * q_len, ATT_Q_HEADS * HEAD_DIM), BF16)]
    step = lambda b, k, i: (b * ATT_KV_HEADS + k) * nq + i
    reps = []
    for w, layer, tile in casts:
        _, rows, cols = w.shape
        rep = next(r for r in (1, 2, 4, 8, 16) if rows % (steps // r) == 0 and rows // (steps // r) % 16 == 0)
        slab = rows // (steps // rep)
        reps.append(rep)
        in_specs.append(pl.BlockSpec((1, slab, cols),
                                     lambda b, k, i, rep=rep, layer=layer: (layer, step(b, k, i) // rep, 0)))
        args.append(w)
        if tile is None:
            out_specs.append(pl.BlockSpec((slab, cols), lambda b, k, i, rep=rep: (step(b, k, i) // rep, 0)))
            out_shape.append(jax.ShapeDtypeStruct((rows, cols), BF16))
        else:
            out_specs.append(pl.BlockSpec((cols // tile, slab, tile),
                                          lambda b, k, i, rep=rep: (0, step(b, k, i) // rep, 0)))
            out_shape.append(jax.ShapeDtypeStruct((cols // tile, rows, tile), BF16))
    outs = pl.pallas_call(
        functools.partial(_attn_kernel, n_kv=len(kv_sources), reps=tuple(reps)),
        grid=(batch, ATT_KV_HEADS, nq),
        in_specs=in_specs,
        out_specs=out_specs,
        out_shape=out_shape,
        compiler_params=_cparams(("arbitrary", "arbitrary", "arbitrary")),
        name="attention",
    )(*args)
    return outs[0], list(outs[1:])


def _ret_kernel(dec_ref, qc_ref, kc_ref, vc_ref, gc_ref, ql_ref, kl_ref, vl_ref, gl_ref, ng_ref,
                oc_ref, ol_ref, kv_ref, st_ref, *, nc_ctx, nc_lat):
    h = pl.program_id(1)
    C = RET_CHUNK
    lg_f, lg_b = dec_ref[0, h], dec_ref[1, h]
    rel = lax.broadcasted_iota(jnp.int32, (C, C), 0) - lax.broadcasted_iota(jnp.int32, (C, C), 1)
    dmat = (jnp.where(rel >= 0, jnp.exp(lg_f * jnp.maximum(rel, 0).astype(F32)), 0.0)
            + jnp.where(rel <= 0, jnp.exp(lg_b * jnp.maximum(-rel, 0).astype(F32)), 0.0))
    row = lax.broadcasted_iota(jnp.int32, (C, 1), 0).astype(F32)
    qd_f, qd_b = jnp.exp(lg_f * (row + 1.0)), jnp.exp(lg_b * (C - row))
    kd_f, kd_b = jnp.exp(lg_f * (C - 1.0 - row)), jnp.exp(lg_b * row)
    cd_f, cd_b = jnp.exp(lg_f * C), jnp.exp(lg_b * C)

    chunks = [(qc_ref, kc_ref, vc_ref, gc_ref, oc_ref, n) for n in range(nc_ctx)]
    chunks += [(ql_ref, kl_ref, vl_ref, gl_ref, ol_ref, n) for n in range(nc_lat)]
    order_b = list(range(nc_ctx - 1, -1, -1)) + list(range(nc_ctx + nc_lat - 1, nc_ctx - 1, -1))

    def decayed(x_ref, n, dec_f, dec_b):
        x = x_ref[0, n * C:(n + 1) * C, :].astype(F32)
        return jnp.concatenate([(x * dec_f).astype(BF16), (x * dec_b).astype(BF16)], axis=1)

    for c, (_, k_ref, v_ref, _, _, n) in enumerate(chunks):
        kv_ref[c] = _dot_tn(decayed(k_ref, n, kd_f, kd_b), v_ref[0, n * C:(n + 1) * C, :])

    s = jnp.zeros((C, RET_DIM), F32)
    for c in range(len(chunks)):
        st_ref[c, 0:C, :] = s.astype(BF16)
        s = cd_f * s + kv_ref[c, 0:C, :]
    s = jnp.zeros((C, RET_DIM), F32)
    for c in order_b:
        st_ref[c, C:2 * C, :] = s.astype(BF16)
        s = cd_b * s + kv_ref[c, C:2 * C, :]

    for c, (q_ref, k_ref, v_ref, g_ref, o_ref, n) in enumerate(chunks):
        sl = slice(n * C, (n + 1) * C)
        v = v_ref[0, sl, :]
        att = _dot_nt(q_ref[0, sl, :], k_ref[0, sl, :]) * dmat
        o = _dot(att.astype(BF16), v) + _dot(decayed(q_ref, n, qd_f, qd_b), st_ref[c])
        y = _rms(o) * ng_ref[...]
        o_ref[sl, :] = (y * _silu(g_ref[0, sl, :].astype(F32))).astype(BF16)


def _ret(zc, zl, decay, ng, *, batch, ctx_len, lat_len):
    def slot(base, n):
        return pl.BlockSpec((1, n, RET_DIM), lambda b, h: (base + h, b, 0))

    n_chunks = (ctx_len + lat_len) // RET_CHUNK
    specs = [pl.BlockSpec(memory_space=pltpu.SMEM)]
    specs += [slot(s, ctx_len) for s in (SLOT_RQ, SLOT_RK, SLOT_RV, SLOT_RG)]
    specs += [slot(s, lat_len) for s in (SLOT_RQ, SLOT_RK, SLOT_RV, SLOT_RG)]
    specs += [pl.BlockSpec((1, RET_DIM), lambda b, h: (0, 0))]
    return pl.pallas_call(
        functools.partial(_ret_kernel, nc_ctx=ctx_len // RET_CHUNK, nc_lat=lat_len // RET_CHUNK),
        grid=(batch, RET_HEADS),
        in_specs=specs,
        out_specs=[pl.BlockSpec((ctx_len, RET_DIM), lambda b, h: (b, h)),
                   pl.BlockSpec((lat_len, RET_DIM), lambda b, h: (b, h))],
        out_shape=[jax.ShapeDtypeStruct((batch * ctx_len, RET_HEADS * RET_DIM), BF16),
                   jax.ShapeDtypeStruct((batch * lat_len, RET_HEADS * RET_DIM), BF16)],
        scratch_shapes=[pltpu.VMEM((n_chunks, 2 * RET_DIM, RET_DIM), F32),
                        pltpu.VMEM((n_chunks, 2 * RET_DIM, RET_DIM), BF16)],
        compiler_params=_cparams(("arbitrary", "arbitrary")),
        name="retention",
    )(decay, zc, zc, zc, zc, zl, zl, zl, zl, ng)


GLA_SLAB = 256
GLA_UNROLL = 4


def _gla_level_table():
    i = np.arange(GLA_CHUNK)[:, None]
    j = np.arange(GLA_CHUNK)[None, :]
    x = np.maximum(i ^ j, 1)
    lvl = np.floor(np.log2(x)).astype(np.int32)
    lvl = np.where(i == j, GLA_LEVELS, np.where(i > j, lvl, -1)).astype(np.int32)
    return np.concatenate([lvl, lvl.T], axis=1)


def _gla_role_table():
    t = np.arange(GLA_SLAB)
    fwd = np.arange(LANES)[None, :] < GLA_DK
    return np.stack([np.where(((t[:, None] >> lev) & 1).astype(bool) == fwd, LOG2_E, -LOG2_E)
                     for lev in range(GLA_LEVELS)]).astype(np.float32)


def _bcast_block_row(x, blk, r):
    rows = x.shape[0]
    if blk >= 8:
        x3 = x.reshape(rows // blk, blk, LANES)
        return jnp.broadcast_to(x3[:, r:r + 1, :], x3.shape).reshape(rows, LANES)
    pos = lax.broadcasted_iota(jnp.int32, x.shape, 0) & (blk - 1)
    out = x
    for p in range(blk):
        if p != r:
            out = jnp.where(pos == p, pltpu.roll(x, (p - r) % rows, 0), out)
    return out


def _gla_kernel(lvl_ref, role_ref, up_ref, gb_ref, ng_ref,
                qc_ref, kc_ref, vc_ref, gc_ref, ac_ref, ql_ref, kl_ref, vl_ref, gl_ref, al_ref,
                oc_ref, ol_ref, *scratch, ctx_len, lat_len):
    C = GLA_CHUNK
    streams = ((qc_ref, kc_ref, vc_ref, ac_ref, 0, ctx_len), (ql_ref, kl_ref, vl_ref, al_ref, ctx_len, lat_len))
    nc_ctx, nc_lat = ctx_len // C, lat_len // C
    n_chunks = nc_ctx + nc_lat
    cpos = lax.broadcasted_iota(jnp.int32, (GLA_SLAB, LANES), 0) & (C - 1)
    fwd = lax.broadcasted_iota(jnp.int32, (GLA_SLAB, LANES), 1) < GLA_DK
    lvl = lvl_ref[...]

    passes = []
    for hh in range(2):
        own = fwd if hh == 0 else jnp.logical_not(fwd)
        xs_ref, qd_ref, xst_ref, qh_ref, kh_ref, tot_ref, vv_ref, kv_ref, st_ref = (r.at[hh] for r in scratch)

        def prep(i, between, q_ref, k_ref, v_ref, a_ref, off, hh=hh, own=own, xs_ref=xs_ref, qd_ref=qd_ref,
                 xst_ref=xst_ref, qh_ref=qh_ref, kh_ref=kh_ref, tot_ref=tot_ref, vv_ref=vv_ref):
            between = list(between)
            src = _rows(i * GLA_SLAB, GLA_SLAB, GLA_SLAB)
            dst = _rows(off + i * GLA_SLAB, GLA_SLAB, C)
            chunk0 = off // C + i * (GLA_SLAB // C)

            def store_keys(lev, x):
                zero = jnp.zeros_like(x)
                xf, xb = jnp.where(fwd, x, zero), jnp.where(fwd, zero, x)
                y = jnp.concatenate([p[c * C:(c + 1) * C] for c in range(GLA_SLAB // C) for p in (xf, xb)], axis=0)
                yt = y.T.astype(BF16)
                for c in range(GLA_SLAB // C):
                    xst_ref[lev, chunk0 + c] = yt[:, c * 2 * C:(c + 1) * 2 * C]

            logit = _dot(a_ref[0, src, :], up_ref[hh]) + gb_ref[hh]
            la = (jnp.minimum(logit, 0.0) - jnp.log1p(jnp.exp(-jnp.abs(logit)))) * (1.0 / GLA_TAU)
            cum = la
            for step in range(GLA_LEVELS):
                sh = 1 << step
                before = jnp.where(cpos >= sh, pltpu.roll(cum, sh, 0), 0.0)
                after = jnp.where(cpos < C - sh, pltpu.roll(cum, GLA_SLAB - sh, 0), 0.0)
                cum = cum + jnp.where(fwd, before, after)
            tot = jnp.where(fwd, _bcast_block_row(cum, C, C - 1), _bcast_block_row(cum, C, 0))
            q2 = q_ref[0, src, :].astype(F32)
            k2 = k_ref[0, src, :].astype(F32)
            q = jnp.where(own, q2, pltpu.roll(q2, GLA_DK, 1))
            k = jnp.where(own, k2, pltpu.roll(k2, GLA_DK, 1))
            qd_ref[dst, :] = q.astype(BF16)
            store_keys(GLA_LEVELS, k)
            qh_ref[dst, :] = (q * jnp.exp(cum)).astype(BF16)
            kh_ref[dst, :] = (k * jnp.exp(tot - cum)).astype(BF16)
            tot_ref[dst, :] = tot
            vv_ref[dst, :] = v_ref[hh, src, :]
            for lev in range(GLA_LEVELS):
                if between and lev % 2 == 0:
                    between.pop(0)()
                s = 1 << lev
                edge = jnp.where(fwd, _bcast_block_row(cum, 2 * s, s - 1), _bcast_block_row(cum, 2 * s, s))
                role = role_ref[lev]
                x = jnp.where(role > 0, q, k) * jnp.exp2((cum - edge) * role)
                xs_ref[lev, dst, :] = x.astype(BF16)
                store_keys(lev, x)
            for item in between:
                item()

        def states(kv_ref=kv_ref, st_ref=st_ref, vv_ref=vv_ref, kh_ref=kh_ref, tot_ref=tot_ref):
            def kv_body(c, carry):
                rows = _rows(c * C, C, C)
                kv_ref[c] = _dot_tn(vv_ref[rows, :], kh_ref[rows, :])
                return carry

            lax.fori_loop(0, n_chunks, kv_body, 0, unroll=GLA_UNROLL)

            def scan(order_of, lanes):
                def body(n, s):
                    c = order_of(n)
                    st_ref[c, :, lanes] = s[:, lanes].astype(BF16)
                    return s * jnp.exp(tot_ref[pl.ds(c * C, 1), :]) + kv_ref[c]
                lax.fori_loop(0, n_chunks, body, jnp.zeros((GLA_DV, LANES), F32))

            scan(lambda n: n, slice(0, GLA_DK))
            scan(lambda n: jnp.where(n < nc_ctx, nc_ctx - 1 - n, n_chunks + nc_ctx - 1 - n), slice(GLA_DK, LANES))

        def out_chunk(c, g_ref, o_ref, local, hh=hh, xs_ref=xs_ref, qd_ref=qd_ref, xst_ref=xst_ref, qh_ref=qh_ref,
                      vv_ref=vv_ref, st_ref=st_ref):
            rows = _rows(c * C, C, C)
            a = jnp.where(lvl == GLA_LEVELS, _dot(qd_ref[rows, :], xst_ref[GLA_LEVELS, c]), 0.0)
            for lev in range(GLA_LEVELS):
                a = jnp.where(lvl == lev, _dot(xs_ref[lev, rows, :], xst_ref[lev, c]), a)
            v = vv_ref[rows, :]
            o = _dot(a.astype(BF16), jnp.concatenate([v, v], axis=0)) + _dot_nt(qh_ref[rows, :], st_ref[c])
            y = _rms(o) * ng_ref[...]
            out_rows = _rows(local * C, C, C)
            y = y * _silu(g_ref[hh, out_rows, :].astype(F32))
            o_ref[out_rows, hh * GLA_DV:(hh + 1) * GLA_DV] = y.astype(BF16)

        passes.append((prep, states, out_chunk))

    per_slab = GLA_SLAB // C
    (prep0, states0, out0), (prep1, states1, out1) = passes

    def slab(prep, out, stream, i, g_ref, o_ref):
        q_ref, k_ref, v_ref, a_ref, off, _ = stream
        chunks = []
        if out is not None:
            chunks = [functools.partial(out, off // C + i * per_slab + j, g_ref, o_ref, i * per_slab + j)
                      for j in range(per_slab)]
        if prep is not None:
            prep(i, chunks, q_ref=q_ref, k_ref=k_ref, v_ref=v_ref, a_ref=a_ref, off=off)
        else:
            for item in chunks:
                item()

    def run(prep, out):
        for stream, g_ref, o_ref in ((streams[0], gc_ref, oc_ref), (streams[1], gl_ref, ol_ref)):
            n_slabs = stream[5] // GLA_SLAB
            if n_slabs == 1:
                slab(prep, out, stream, 0, g_ref, o_ref)
            else:
                def body(i, carry, stream=stream, g_ref=g_ref, o_ref=o_ref):
                    slab(prep, out, stream, i, g_ref, o_ref)
                    return carry
                lax.fori_loop(0, n_slabs, body, 0, unroll=2 if (out is None and n_slabs % 2 == 0) else 1)

    run(prep0, None)
    states0()
    run(prep1, out0)
    states1()
    run(None, out1)


def _gla_gate_params(gate_up, gate_b):
    r = GLA_GATE_RANK
    gu = gate_up.astype(BF16).reshape(2, r, GLA_HEADS, GLA_DK).transpose(2, 0, 1, 3)
    up = jnp.zeros((GLA_HEADS, LANES, 2 * GLA_DK), BF16)
    up = up.at[:, 0:r, 0:GLA_DK].set(gu[:, 0]).at[:, r:2 * r, GLA_DK:].set(gu[:, 1])
    gb = gate_b.reshape(2, GLA_HEADS, GLA_DK).transpose(1, 0, 2).reshape(GLA_HEADS, 1, 2 * GLA_DK)
    return up, gb


def _gla(zc, zl, up_heads, gate_b_heads, ng, *, batch, ctx_len, lat_len):
    total = ctx_len + lat_len
    n_chunks = total // GLA_CHUNK

    def pair(base, n):
        return pl.BlockSpec((1, n, LANES), lambda b, p: (base + p, b, 0))

    def two(base, n):
        return pl.BlockSpec((2, n, LANES), lambda b, p: (base // 2 + p, b, 0))

    def one(n):
        return pl.BlockSpec((1, n, LANES), lambda b, p: (SLOT_GA, b, 0))

    role = _gla_role_table()
    specs = [
        pl.BlockSpec((GLA_CHUNK, 2 * GLA_CHUNK), lambda b, p: (0, 0)),
        pl.BlockSpec(role.shape, lambda b, p: (0, 0, 0)),
        pl.BlockSpec((2, LANES, LANES), lambda b, p: (p, 0, 0)),
        pl.BlockSpec((2, 1, LANES), lambda b, p: (p, 0, 0)),
        pl.BlockSpec((1, GLA_DV), lambda b, p: (0, 0)),
    ]
    for n in (ctx_len, lat_len):
        specs += [pair(SLOT_GQ, n), pair(SLOT_GK, n), two(SLOT_GV, n), two(SLOT_GR, n), one(n)]
    return pl.pallas_call(
        functools.partial(_gla_kernel, ctx_len=ctx_len, lat_len=lat_len),
        grid=(batch, GLA_HEADS // 2),
        in_specs=specs,
        out_specs=[pl.BlockSpec((ctx_len, 2 * GLA_DV), lambda b, p: (b, p)),
                   pl.BlockSpec((lat_len, 2 * GLA_DV), lambda b, p: (b, p))],
        out_shape=[jax.ShapeDtypeStruct((batch * ctx_len, GLA_HEADS * GLA_DV), BF16),
                   jax.ShapeDtypeStruct((batch * lat_len, GLA_HEADS * GLA_DV), BF16)],
        scratch_shapes=[
            pltpu.VMEM((2, GLA_LEVELS, total, LANES), BF16),
            pltpu.VMEM((2, total, LANES), BF16),
            pltpu.VMEM((2, GLA_LEVELS + 1, n_chunks, LANES, 2 * GLA_CHUNK), BF16),
            pltpu.VMEM((2, total, LANES), BF16),
            pltpu.VMEM((2, total, LANES), BF16),
            pltpu.VMEM((2, total, LANES), F32),
            pltpu.VMEM((2, total, GLA_DV), BF16),
            pltpu.VMEM((2, n_chunks, GLA_DV, LANES), F32),
            pltpu.VMEM((2, n_chunks, GLA_DV, LANES), BF16),
        ],
        compiler_params=_cparams(("arbitrary", "arbitrary")),
        name="gla",
    )(jnp.asarray(_gla_level_table()), jnp.asarray(role), up_heads, gate_b_heads, ng,
      zc, zc, zc, zc, zc, zl, zl, zl, zl, zl)


def _outproj_kernel(att_ref, ret_ref, gla_ref, w_ref, x_ref, gate_ref, sh_ref, sc_ref, g_ref, xo_ref, ho_ref):
    na, nr = att_ref.shape[1], ret_ref.shape[1]
    y = (_dot(att_ref[...], w_ref[0, 0:na, :]) + _dot(ret_ref[...], w_ref[0, na:na + nr, :])
         + _dot(gla_ref[...], w_ref[0, na + nr:, :]))
    xn = x_ref[...] + gate_ref[0] * y
    xo_ref[...] = xn
    ho_ref[...] = (_rms(xn) * g_ref[...] * (1.0 + sc_ref[0]) + sh_ref[0]).astype(BF16)


def _outproj(att, ret, gla, w, layer, x2, gate, shift, scale, g, *, tm, rows_per_mod):
    m, d = x2.shape

    def rows(n):
        return pl.BlockSpec((tm, n), lambda i: (i, 0))

    def mod():
        return pl.BlockSpec((1, 1, d), lambda i: ((i * tm) // rows_per_mod, 0, 0))

    return pl.pallas_call(
        _outproj_kernel,
        grid=(m // tm,),
        in_specs=[rows(att.shape[1]), rows(ret.shape[1]), rows(gla.shape[1]),
                  pl.BlockSpec((1,) + w.shape[1:], lambda i: (layer, 0, 0), pipeline_mode=pl.Buffered(1)),
                  rows(d), mod(), mod(), mod(),
                  pl.BlockSpec((1, d), lambda i: (0, 0))],
        out_specs=[rows(d), rows(d)],
        out_shape=[jax.ShapeDtypeStruct((m, d), F32), jax.ShapeDtypeStruct((m, d), BF16)],
        compiler_params=_cparams(("arbitrary",)),
        name="out_proj",
    )(att, ret, gla, w, x2, gate, shift, scale, g)


FFN_HALO = 16
FFN_TF = 512
FFN_TM = 512


def _ffn_kernel(h_ref, hn_ref, hp_ref, wa_ref, wv_ref, cw_ref, cb_ref, wd_ref, x_ref, gate_ref, fg_ref,
                o_ref, hs_ref, *, tm, seq_len, final_norm):
    i = pl.program_id(0)
    f = pl.program_id(1)
    ext = tm + 2 * FFN_HALO

    @pl.when(f == 0)
    def _():
        hs_ref[0:tm, :] = h_ref[...]
        hs_ref[tm:tm + FFN_HALO, :] = hn_ref[...]
        hs_ref[tm + FFN_HALO:ext, :] = hp_ref[...]
        o_ref[...] = jnp.zeros_like(o_ref)

    a = _dot(hs_ref[...], wa_ref[0])
    pos = (i * tm + lax.broadcasted_iota(jnp.int32, (ext, 1), 0)) % seq_len
    prev = jnp.where(pos == 0, 0.0, pltpu.roll(a, 1, 0))
    nxt = jnp.where(pos == seq_len - 1, 0.0, pltpu.roll(a, ext - 1, 0))
    conv = prev * cw_ref[0:1, :] + a * cw_ref[1:2, :] + nxt * cw_ref[2:3, :] + cb_ref[...]
    v = _dot(h_ref[...], wv_ref[0])
    u = (_silu(conv[0:tm]) * v).astype(BF16)
    o_ref[...] += _dot(u, wd_ref[0])

    @pl.when(f == pl.num_programs(1) - 1)
    def _():
        xn = x_ref[...] + gate_ref[0] * o_ref[...]
        if final_norm:
            xn = _rms(xn) * fg_ref[...]
        o_ref[...] = xn


def _ffn(h2, w_up, conv_w, conv_b, w_down, x2, gate, fg, *, tm, seq_len, rows_per_mod, final_norm):
    m, d = x2.shape
    tf = w_up.shape[2]
    nf = D_FF // tf
    hb = tm // FFN_HALO
    last = m // FFN_HALO - 1
    return pl.pallas_call(
        functools.partial(_ffn_kernel, tm=tm, seq_len=seq_len, final_norm=final_norm),
        grid=(m // tm, nf),
        in_specs=[
            pl.BlockSpec((tm, d), lambda i, f: (i, 0)),
            pl.BlockSpec((FFN_HALO, d), lambda i, f: (jnp.minimum((i + 1) * hb, last), 0)),
            pl.BlockSpec((FFN_HALO, d), lambda i, f: (jnp.maximum(i * hb - 1, 0), 0)),
            pl.BlockSpec((1, d, tf), lambda i, f: (f, 0, 0)),
            pl.BlockSpec((1, d, tf), lambda i, f: (nf + f, 0, 0)),
            pl.BlockSpec((3, tf), lambda i, f: (0, f)),
            pl.BlockSpec((1, tf), lambda i, f: (0, f)),
            pl.BlockSpec((1, tf, d), lambda i, f: (0, f, 0)),
            pl.BlockSpec((tm, d), lambda i, f: (i, 0)),
            pl.BlockSpec((1, 1, d), lambda i, f: ((i * tm) // rows_per_mod, 0, 0)),
            pl.BlockSpec((1, d), lambda i, f: (0, 0)),
        ],
        out_specs=pl.BlockSpec((tm, d), lambda i, f: (i, 0)),
        out_shape=jax.ShapeDtypeStruct((m, d), F32),
        scratch_shapes=[pltpu.VMEM((tm + 2 * FFN_HALO, d), BF16)],
        compiler_params=_cparams(("arbitrary", "arbitrary")),
        name="conv_glu",
    )(h2, h2, h2, w_up, w_up, conv_w, conv_b, w_down, x2, gate, fg)


def _rope_tables(n_tokens):
    rows = n_tokens // GRID_W
    row = jnp.repeat(jnp.arange(rows, dtype=F32), GRID_W)
    col = jnp.tile(jnp.arange(GRID_W, dtype=F32), rows)
    n_freq = HEAD_DIM // 4
    inv_freq = ROPE_THETA ** (-jnp.arange(n_freq, dtype=F32) / n_freq)
    ang = jnp.concatenate([row[:, None] * inv_freq, col[:, None] * inv_freq], axis=-1)
    cos, sin = jnp.cos(ang), jnp.sin(ang)
    return jnp.concatenate([cos, cos], axis=-1), jnp.concatenate([-sin, sin], axis=-1)


def kernel(x, c, ctx, c_ctx, ada_w, ada_b, norm1_g, w_in, q_norm_g, k_norm_g, ret_log_decay, ret_norm_g,
           gla_gate_up, gla_gate_b, gla_norm_g, w_out, norm2_g, w_up, conv_w, conv_b, w_down, final_norm_g):
    batch, lat_len, d = x.shape
    ctx_len = ctx.shape[1]
    depth = ada_w.shape[0]
    mod_rows = 16
    cc = jnp.concatenate([c, c_ctx[None], jnp.zeros((mod_rows - batch - 1, d), F32)], axis=0)
    mod = _ada(cc, ada_w, ada_b).reshape(depth, mod_rows, N_MOD, d)

    cos_l, sin_l = _rope_tables(lat_len)
    proj_tm = 512
    cos_c = jnp.ones((proj_tm, LANES), F32)
    sin_c = jnp.zeros((proj_tm, LANES), F32)

    xl = x.reshape(batch * lat_len, d)
    xc = ctx.reshape(batch * ctx_len, d)
    row = lambda v: v.reshape(1, -1)
    w_in_b = w_in.astype(BF16)
    w_out_b = w_out[0].astype(BF16)[None]

    for l in range(depth):
        last = l == depth - 1
        ml = [mod[l, :batch, k].reshape(batch, 1, d) for k in range(N_MOD)]
        mc = [mod[l, batch, k].reshape(1, 1, d) for k in range(N_MOD)]
        up_heads, gate_b_heads = _gla_gate_params(gla_gate_up[l], gla_gate_b[l])

        zl, vtl = _proj(xl, ml[0], ml[1], row(norm1_g[l]), w_in_b, l, cos_l, sin_l, row(q_norm_g[l]), row(k_norm_g[l]),
                        tm=proj_tm, rows_per_mod=lat_len, rope_tiles=lat_len // proj_tm)
        zc, vtc = _proj(xc, mc[0], mc[1], row(norm1_g[l]), w_in_b, l, cos_c, sin_c, row(q_norm_g[l]), row(k_norm_g[l]),
                        tm=proj_tm, rows_per_mod=batch * ctx_len, rope_tiles=1)

        casts = [(w_up, l, FFN_TF), (w_down, l, None)]
        if not last:
            casts += [(w_out, l + 1, None)]
        att_l, cast = _attn(vtl, [(zc, vtc, ctx_len), (zl, vtl, lat_len)], batch=batch, q_len=lat_len, tq=256,
                            casts=casts)
        w_up_b, w_down_b = cast[0], cast[1][None]
        ret_c, ret_l = _ret(zc, zl, ret_log_decay[l], row(ret_norm_g[l]),
                            batch=batch, ctx_len=ctx_len, lat_len=lat_len)
        gla_c, gla_l = _gla(zc, zl, up_heads, gate_b_heads, row(gla_norm_g[l]),
                            batch=batch, ctx_len=ctx_len, lat_len=lat_len)

        xl, h2 = _outproj(att_l, ret_l, gla_l, w_out_b, 0, xl, ml[2], ml[3], ml[4], row(norm2_g[l]),
                          tm=256, rows_per_mod=lat_len)
        xl = _ffn(h2, w_up_b, conv_w[l], row(conv_b[l]), w_down_b, xl, ml[5], row(final_norm_g),
                  tm=FFN_TM, seq_len=lat_len, rows_per_mod=lat_len, final_norm=last)

        if not last:
            att_c, _ = _attn(vtc, [(zc, vtc, ctx_len)], batch=batch, q_len=ctx_len, tq=ctx_len)
            xc, hc2 = _outproj(att_c, ret_c, gla_c, w_out_b, 0, xc, mc[2], mc[3], mc[4], row(norm2_g[l]),
                               tm=256, rows_per_mod=batch * ctx_len)
            xc = _ffn(hc2, w_up_b, conv_w[l], row(conv_b[l]), w_down_b, xc, mc[5], row(final_norm_g),
                      tm=FFN_TM, seq_len=ctx_len, rows_per_mod=batch * ctx_len, final_norm=False)
            w_out_b = cast[2][None]

    return xl.reshape(batch, lat_len, d)
```

```python
import functools

import numpy as np
import jax
import jax.numpy as jnp
from jax import lax
from jax.experimental import pallas as pl
from jax.experimental.pallas import tpu as pltpu

F32 = jnp.float32
BF16 = jnp.bfloat16

HEAD_DIM = 128
GRID_W = 64
ATT_Q_HEADS = 8
ATT_KV_HEADS = 2
GQA = ATT_Q_HEADS // ATT_KV_HEADS
RET_HEADS = 4
RET_DIM = 128
RET_CHUNK = 128
GLA_HEADS = 4
GLA_DK = 64
GLA_DV = 128
GLA_GATE_RANK = 16
GLA_TAU = 16.0
GLA_CHUNK = 64
GLA_LEVELS = 6
D_FF = 5632
ROPE_THETA = 10000.0
N_MOD = 6
EPS = 1e-6

LOG2_E = 1.4426950408889634

LANES = 128
N_IN = 5152
PROJ_TN = 256
N_SLOTS = -(-N_IN // LANES)
SLOT_AQ, SLOT_AK, SLOT_AV = 0, 8, 10
SLOT_RQ, SLOT_RK, SLOT_RV, SLOT_RG = 12, 16, 20, 24
SLOT_GQ, SLOT_GK, SLOT_GV, SLOT_GR, SLOT_GA = 28, 30, 32, 36, 40

VMEM_LIMIT = 52 * 1024 * 1024
ADA_TN = 1024
PROJ_TM = 512
ATT_TQ = 256
OUTPROJ_TM = 256

_NT = (((1,), (1,)), ((), ()))
_TN = (((0,), (0,)), ((), ()))


def _cparams(sem, vmem_limit=VMEM_LIMIT):
    return pltpu.CompilerParams(dimension_semantics=sem, vmem_limit_bytes=vmem_limit)


def _dot(a, b):
    return jnp.dot(a, b, preferred_element_type=F32)


def _dot_nt(a, b):
    return lax.dot_general(a, b, _NT, preferred_element_type=F32)


def _dot_tn(a, b):
    return lax.dot_general(a, b, _TN, preferred_element_type=F32)


def _rows(start, size, align):
    if isinstance(start, int):
        return pl.ds(start, size)
    return pl.ds(pl.multiple_of(start, align), size)


def _rms(t):
    return t * lax.rsqrt(jnp.mean(t * t, axis=-1, keepdims=True) + EPS)


def _silu(t):
    return t * jax.nn.sigmoid(t)


def _ada_kernel(c_ref, w_ref, b_ref, o_ref):
    s = _silu(c_ref[...]).astype(BF16)
    o_ref[0] = _dot(s, w_ref[0].astype(BF16)) + b_ref[0]


def _ada(cc, ada_w, ada_b):
    depth, d, n = ada_w.shape
    rows = cc.shape[0]
    tn = ADA_TN
    return pl.pallas_call(
        _ada_kernel,
        grid=(depth, n // tn),
        in_specs=[
            pl.BlockSpec((rows, d), lambda l, j: (0, 0)),
            pl.BlockSpec((1, d, tn), lambda l, j: (l, 0, j)),
            pl.BlockSpec((1, 1, tn), lambda l, j: (l, 0, j)),
        ],
        out_specs=pl.BlockSpec((1, rows, tn), lambda l, j: (l, 0, j)),
        out_shape=jax.ShapeDtypeStruct((depth, rows, n), F32),
        compiler_params=_cparams(("arbitrary", "arbitrary")),
        name="ada_mod",
    )(cc, ada_w, ada_b.reshape(depth, 1, n))


def _proj_kernel(x_ref, sh_ref, sc_ref, g_ref, w_ref, cos_ref, sin_ref, qg_ref, kg_ref, o_ref, vt_ref):
    y = _rms(x_ref[...]) * g_ref[...]
    h = (y * (1.0 + sc_ref[0]) + sh_ref[0]).astype(BF16)
    cos, sin = cos_ref[...], sin_ref[...]
    qg = qg_ref[...] * (HEAD_DIM ** -0.5 * LOG2_E)
    kg = kg_ref[...]

    def rope(t):
        return t * cos + pltpu.roll(t, HEAD_DIM // 2, 1) * sin

    def transform(slot, t):
        if slot < SLOT_AK:
            return rope(_rms(t) * qg)
        if slot < SLOT_AV:
            return rope(_rms(t) * kg)
        if SLOT_RQ <= slot < SLOT_RK:
            return rope(t)
        if SLOT_RK <= slot < SLOT_RV:
            return rope(t * (RET_DIM ** -0.5))
        if SLOT_GQ <= slot < SLOT_GK:
            return t * (GLA_DK ** -0.5)
        return t

    for c0 in range(0, N_IN, PROJ_TN):
        c1 = min(c0 + PROJ_TN, N_IN)
        z = _dot(h, w_ref[0, :, c0:c1])
        for s0 in range(c0, c1, LANES):
            slot, n = s0 // LANES, min(LANES, c1 - s0)
            t = transform(slot, z[:, s0 - c0:s0 - c0 + n])
            if slot < SLOT_AK:
                vt_ref[slot] = t.T.astype(BF16)
            elif SLOT_AV <= slot < SLOT_RQ:
                vt_ref[ATT_Q_HEADS + slot - SLOT_AV] = t.T.astype(BF16)
            t = t.astype(BF16)
            if n < LANES:
                o_ref[slot] = jnp.zeros(o_ref.shape[1:], BF16)
                o_ref[slot, :, 0:n] = t
            else:
                o_ref[slot] = t


def _proj(x2, shift, scale, g, w, layer, cosf, sinf, qg, kg, *, tm, rows_per_mod, rope_tiles):
    m, d = x2.shape
    return pl.pallas_call(
        _proj_kernel,
        grid=(m // tm,),
        in_specs=[
            pl.BlockSpec((tm, d), lambda i: (i, 0)),
            pl.BlockSpec((1, 1, d), lambda i: ((i * tm) // rows_per_mod, 0, 0)),
            pl.BlockSpec((1, 1, d), lambda i: ((i * tm) // rows_per_mod, 0, 0)),
            pl.BlockSpec((1, d), lambda i: (0, 0)),
            pl.BlockSpec((1, d, N_IN), lambda i: (layer, 0, 0), pipeline_mode=pl.Buffered(1)),
            pl.BlockSpec((tm, LANES), lambda i: (i % rope_tiles, 0)),
            pl.BlockSpec((tm, LANES), lambda i: (i % rope_tiles, 0)),
            pl.BlockSpec((1, LANES), lambda i: (0, 0)),
            pl.BlockSpec((1, LANES), lambda i: (0, 0)),
        ],
        out_specs=[pl.BlockSpec((N_SLOTS, tm, LANES), lambda i: (0, i, 0)),
                   pl.BlockSpec((ATT_Q_HEADS + ATT_KV_HEADS, HEAD_DIM, tm), lambda i: (0, 0, i))],
        out_shape=[jax.ShapeDtypeStruct((N_SLOTS, m, LANES), BF16),
                   jax.ShapeDtypeStruct((ATT_Q_HEADS + ATT_KV_HEADS, HEAD_DIM, m), BF16)],
        compiler_params=_cparams(("arbitrary",)),
        name="norm_proj",
    )(x2, shift, scale, g, w, cosf, sinf, qg, kg)


ATT_KEY_BLOCK = 128


def _attn_kernel(*refs, n_kv, reps):
    n_cast = len(reps)
    q_ref = refs[0]
    kv_refs = refs[1:1 + 2 * n_kv]
    w_refs = refs[1 + 2 * n_kv:1 + 2 * n_kv + n_cast]
    o_ref = refs[1 + 2 * n_kv + n_cast]
    wo_refs = refs[2 + 2 * n_kv + n_cast:]
    step = (pl.program_id(0) * pl.num_programs(1) + pl.program_id(1)) * pl.num_programs(2) + pl.program_id(2)

    def convert(w_ref, wo_ref):
        if len(wo_ref.shape) == 3:
            tn = wo_ref.shape[2]
            for t in range(wo_ref.shape[0]):
                wo_ref[t] = w_ref[0, :, t * tn:(t + 1) * tn].astype(BF16)
        else:
            wo_ref[...] = w_ref[0].astype(BF16)

    for w_ref, wo_ref, rep in zip(w_refs, wo_refs, reps):
        if rep == 1:
            convert(w_ref, wo_ref)
        else:
            pl.when(step % rep == 0)(functools.partial(convert, w_ref, wo_ref))
    m, l, acc = [None] * GQA, [None] * GQA, [None] * GQA
    units = [(s, j0, g) for s in range(n_kv) for j0 in range(0, kv_refs[2 * s].shape[1], ATT_KEY_BLOCK)
             for g in range(GQA)]

    def scores(unit):
        s, j0, g = unit
        return _dot(kv_refs[2 * s][0, j0:j0 + ATT_KEY_BLOCK, :], q_ref[g])

    st_next = scores(units[0])
    for u, (s, j0, g) in enumerate(units):
        st = st_next
        if u + 1 < len(units):
            st_next = scores(units[u + 1])
        vt = kv_refs[2 * s + 1][0, :, j0:j0 + ATT_KEY_BLOCK]
        mb = jnp.max(st, axis=0, keepdims=True)
        if m[g] is None:
            m[g] = mb
            p = jnp.exp2(st - mb)
            l[g] = jnp.sum(p, axis=0, keepdims=True)
            acc[g] = _dot(vt, p.astype(BF16))
        else:
            m_new = jnp.maximum(m[g], mb)
            alpha = jnp.exp2(m[g] - m_new)
            p = jnp.exp2(st - m_new)
            l[g] = alpha * l[g] + jnp.sum(p, axis=0, keepdims=True)
            acc[g] = alpha * acc[g] + _dot(vt, p.astype(BF16))
            m[g] = m_new
    for g in range(GQA):
        o_ref[:, g * HEAD_DIM:(g + 1) * HEAD_DIM] = (acc[g] / l[g]).T.astype(BF16)


def _attn(qt, kv_sources, *, batch, q_len, tq, casts=()):
    nq = q_len // tq
    steps = batch * ATT_KV_HEADS * nq
    in_specs = [pl.BlockSpec((GQA, HEAD_DIM, tq), lambda b, k, i: (k, 0, b * nq + i))]
    args = [qt]
    for z, vt, kv_len in kv_sources:
        in_specs.append(pl.BlockSpec((1, kv_len, HEAD_DIM), lambda b, k, i: (SLOT_AK + k, b, 0)))
        in_specs.append(pl.BlockSpec((1, HEAD_DIM, kv_len), lambda b, k, i: (ATT_Q_HEADS + k, 0, b)))
        args += [z, vt]
    out_specs = [pl.BlockSpec((tq, GQA * HEAD_DIM), lambda b, k, i: (b * nq + i, k))]
    out_shape = [jax.ShapeDtypeStruct((batch * q_len, ATT_Q_HEADS * HEAD_DIM), BF16)]
    step = lambda b, k, i: (b * ATT_KV_HEADS + k) * nq + i
    reps = []
    for w, layer, tile in casts:
        _, rows, cols = w.shape
        rep = next(r for r in (1, 2, 4, 8, 16) if rows % (steps // r) == 0 and rows // (steps // r) % 16 == 0)
        slab = rows // (steps // rep)
        reps.append(rep)
        in_specs.append(pl.BlockSpec((1, slab, cols),
                                     lambda b, k, i, rep=rep, layer=layer: (layer, step(b, k, i) // rep, 0)))
        args.append(w)
        if tile is None:
            out_specs.append(pl.BlockSpec((slab, cols), lambda b, k, i, rep=rep: (step(b, k, i) // rep, 0)))
            out_shape.append(jax.ShapeDtypeStruct((rows, cols), BF16))
        else:
            out_specs.append(pl.BlockSpec((cols // tile, slab, tile),
                                          lambda b, k, i, rep=rep: (0, step(b, k, i) // rep, 0)))
            out_shape.append(jax.ShapeDtypeStruct((cols // tile, rows, tile), BF16))
    outs = pl.pallas_call(
        functools.partial(_attn_kernel, n_kv=len(kv_sources), reps=tuple(reps)),
        grid=(batch, ATT_KV_HEADS, nq),
        in_specs=in_specs,
        out_specs=out_specs,
        out_shape=out_shape,
        compiler_params=_cparams(("arbitrary", "arbitrary", "arbitrary")),
        name="attention",
    )(*args)
    return outs[0], list(outs[1:])


def _ret_kernel(dec_ref, qc_ref, kc_ref, vc_ref, gc_ref, ql_ref, kl_ref, vl_ref, gl_ref, ng_ref,
                oc_ref, ol_ref, kv_ref, st_ref, *, nc_ctx, nc_lat):
    h = pl.program_id(1)
    C = RET_CHUNK
    lg_f, lg_b = dec_ref[0, h], dec_ref[1, h]
    rel = lax.broadcasted_iota(jnp.int32, (C, C), 0) - lax.broadcasted_iota(jnp.int32, (C, C), 1)
    dmat = (jnp.where(rel >= 0, jnp.exp(lg_f * jnp.maximum(rel, 0).astype(F32)), 0.0)
            + jnp.where(rel <= 0, jnp.exp(lg_b * jnp.maximum(-rel, 0).astype(F32)), 0.0))
    row = lax.broadcasted_iota(jnp.int32, (C, 1), 0).astype(F32)
    qd_f, qd_b = jnp.exp(lg_f * (row + 1.0)), jnp.exp(lg_b * (C - row))
    kd_f, kd_b = jnp.exp(lg_f * (C - 1.0 - row)), jnp.exp(lg_b * row)
    cd_f, cd_b = jnp.exp(lg_f * C), jnp.exp(lg_b * C)

    chunks = [(qc_ref, kc_ref, vc_ref, gc_ref, oc_ref, n) for n in range(nc_ctx)]
    chunks += [(ql_ref, kl_ref, vl_ref, gl_ref, ol_ref, n) for n in range(nc_lat)]
    order_b = list(range(nc_ctx - 1, -1, -1)) + list(range(nc_ctx + nc_lat - 1, nc_ctx - 1, -1))

    def decayed(x_ref, n, dec_f, dec_b):
        x = x_ref[0, n * C:(n + 1) * C, :].astype(F32)
        return jnp.concatenate([(x * dec_f).astype(BF16), (x * dec_b).astype(BF16)], axis=1)

    for c, (_, k_ref, v_ref, _, _, n) in enumerate(chunks):
        kv_ref[c] = _dot_tn(decayed(k_ref, n, kd_f, kd_b), v_ref[0, n * C:(n + 1) * C, :])

    s = jnp.zeros((C, RET_DIM), F32)
    for c in range(len(chunks)):
        st_ref[c, 0:C, :] = s.astype(BF16)
        s = cd_f * s + kv_ref[c, 0:C, :]
    s = jnp.zeros((C, RET_DIM), F32)
    for c in order_b:
        st_ref[c, C:2 * C, :] = s.astype(BF16)
        s = cd_b * s + kv_ref[c, C:2 * C, :]

    for c, (q_ref, k_ref, v_ref, g_ref, o_ref, n) in enumerate(chunks):
        sl = slice(n * C, (n + 1) * C)
        v = v_ref[0, sl, :]
        att = _dot_nt(q_ref[0, sl, :], k_ref[0, sl, :]) * dmat
        o = _dot(att.astype(BF16), v) + _dot(decayed(q_ref, n, qd_f, qd_b), st_ref[c])
        y = _rms(o) * ng_ref[...]
        o_ref[sl, :] = (y * _silu(g_ref[0, sl, :].astype(F32))).astype(BF16)


def _ret(zc, zl, decay, ng, *, batch, ctx_len, lat_len):
    def slot(base, n):
        return pl.BlockSpec((1, n, RET_DIM), lambda b, h: (base + h, b, 0))

    n_chunks = (ctx_len + lat_len) // RET_CHUNK
    specs = [pl.BlockSpec(memory_space=pltpu.SMEM)]
    specs += [slot(s, ctx_len) for s in (SLOT_RQ, SLOT_RK, SLOT_RV, SLOT_RG)]
    specs += [slot(s, lat_len) for s in (SLOT_RQ, SLOT_RK, SLOT_RV, SLOT_RG)]
    specs += [pl.BlockSpec((1, RET_DIM), lambda b, h: (0, 0))]
    return pl.pallas_call(
        functools.partial(_ret_kernel, nc_ctx=ctx_len // RET_CHUNK, nc_lat=lat_len // RET_CHUNK),
        grid=(batch, RET_HEADS),
        in_specs=specs,
        out_specs=[pl.BlockSpec((ctx_len, RET_DIM), lambda b, h: (b, h)),
                   pl.BlockSpec((lat_len, RET_DIM), lambda b, h: (b, h))],
        out_shape=[jax.ShapeDtypeStruct((batch * ctx_len, RET_HEADS * RET_DIM), BF16),
                   jax.ShapeDtypeStruct((batch * lat_len, RET_HEADS * RET_DIM), BF16)],
        scratch_shapes=[pltpu.VMEM((n_chunks, 2 * RET_DIM, RET_DIM), F32),
                        pltpu.VMEM((n_chunks, 2 * RET_DIM, RET_DIM), BF16)],
        compiler_params=_cparams(("arbitrary", "arbitrary")),
        name="retention",
    )(decay, zc, zc, zc, zc, zl, zl, zl, zl, ng)


GLA_SLAB = 256
GLA_UNROLL = 4


def _gla_level_table():
    i = np.arange(GLA_CHUNK)[:, None]
    j = np.arange(GLA_CHUNK)[None, :]
    x = np.maximum(i ^ j, 1)
    lvl = np.floor(np.log2(x)).astype(np.int32)
    lvl = np.where(i == j, GLA_LEVELS, np.where(i > j, lvl, -1)).astype(np.int32)
    return np.concatenate([lvl, lvl.T], axis=1)


def _gla_role_table():
    t = np.arange(GLA_SLAB)
    fwd = np.arange(LANES)[None, :] < GLA_DK
    return np.stack([np.where(((t[:, None] >> lev) & 1).astype(bool) == fwd, LOG2_E, -LOG2_E)
                     for lev in range(GLA_LEVELS)]).astype(np.float32)


def _bcast_block_row(x, blk, r):
    rows = x.shape[0]
    if blk >= 8:
        x3 = x.reshape(rows // blk, blk, LANES)
        return jnp.broadcast_to(x3[:, r:r + 1, :], x3.shape).reshape(rows, LANES)
    pos = lax.broadcasted_iota(jnp.int32, x.shape, 0) & (blk - 1)
    out = x
    for p in range(blk):
        if p != r:
            out = jnp.where(pos == p, pltpu.roll(x, (p - r) % rows, 0), out)
    return out


def _gla_kernel(lvl_ref, role_ref, up_ref, gb_ref, ng_ref,
                qc_ref, kc_ref, vc_ref, gc_ref, ac_ref, ql_ref, kl_ref, vl_ref, gl_ref, al_ref,
                oc_ref, ol_ref, *scratch, ctx_len, lat_len):
    C = GLA_CHUNK
    streams = ((qc_ref, kc_ref, vc_ref, ac_ref, 0, ctx_len), (ql_ref, kl_ref, vl_ref, al_ref, ctx_len, lat_len))
    nc_ctx, nc_lat = ctx_len // C, lat_len // C
    n_chunks = nc_ctx + nc_lat
    cpos = lax.broadcasted_iota(jnp.int32, (GLA_SLAB, LANES), 0) & (C - 1)
    fwd = lax.broadcasted_iota(jnp.int32, (GLA_SLAB, LANES), 1) < GLA_DK
    lvl = lvl_ref[...]

    passes = []
    for hh in range(2):
        own = fwd if hh == 0 else jnp.logical_not(fwd)
        xs_ref, qd_ref, xst_ref, qh_ref, kh_ref, tot_ref, vv_ref, kv_ref, st_ref = (r.at[hh] for r in scratch)

        def prep(i, between, q_ref, k_ref, v_ref, a_ref, off, hh=hh, own=own, xs_ref=xs_ref, qd_ref=qd_ref,
                 xst_ref=xst_ref, qh_ref=qh_ref, kh_ref=kh_ref, tot_ref=tot_ref, vv_ref=vv_ref):
            between = list(between)
            src = _rows(i * GLA_SLAB, GLA_SLAB, GLA_SLAB)
            dst = _rows(off + i * GLA_SLAB, GLA_SLAB, C)
            chunk0 = off // C + i * (GLA_SLAB // C)

            def store_keys(lev, x):
                zero = jnp.zeros_like(x)
                xf, xb = jnp.where(fwd, x, zero), jnp.where(fwd, zero, x)
                y = jnp.concatenate([p[c * C:(c + 1) * C] for c in range(GLA_SLAB // C) for p in (xf, xb)], axis=0)
                yt = y.T.astype(BF16)
                for c in range(GLA_SLAB // C):
                    xst_ref[lev, chunk0 + c] = yt[:, c * 2 * C:(c + 1) * 2 * C]

            logit = _dot(a_ref[0, src, :], up_ref[hh]) + gb_ref[hh]
            la = (jnp.minimum(logit, 0.0) - jnp.log1p(jnp.exp(-jnp.abs(logit)))) * (1.0 / GLA_TAU)
            pre = la
            for step in range(GLA_LEVELS):
                sh = 1 << step
                pre = pre + jnp.where(cpos >= sh, pltpu.roll(pre, sh, 0), 0.0)
            tot = _bcast_block_row(pre, C, C - 1)
            cum = jnp.where(fwd, pre, tot - pre + la)
            q2 = q_ref[0, src, :].astype(F32)
            k2 = k_ref[0, src, :].astype(F32)
            q = jnp.where(own, q2, pltpu.roll(q2, GLA_DK, 1))
            k = jnp.where(own, k2, pltpu.roll(k2, GLA_DK, 1))
            qd_ref[dst, :] = q.astype(BF16)
            store_keys(GLA_LEVELS, k)
            qh_ref[dst, :] = (q * jnp.exp(cum)).astype(BF16)
            kh_ref[dst, :] = (k * jnp.exp(tot - cum)).astype(BF16)
            tot_ref[dst, :] = tot
            vv_ref[dst, :] = v_ref[hh, src, :]
            for lev in range(GLA_LEVELS):
                if between and lev % 2 == 0:
                    between.pop(0)()
                s = 1 << lev
                edge = jnp.where(fwd, _bcast_block_row(cum, 2 * s, s - 1), _bcast_block_row(cum, 2 * s, s))
                role = role_ref[lev]
                x = jnp.where(role > 0, q, k) * jnp.exp2((cum - edge) * role)
                xs_ref[lev, dst, :] = x.astype(BF16)
                store_keys(lev, x)
            for item in between:
                item()

        def states(kv_ref=kv_ref, st_ref=st_ref, vv_ref=vv_ref, kh_ref=kh_ref, tot_ref=tot_ref):
            def kv_body(c, carry):
                rows = _rows(c * C, C, C)
                kv_ref[c] = _dot_tn(vv_ref[rows, :], kh_ref[rows, :])
                return carry

            lax.fori_loop(0, n_chunks, kv_body, 0, unroll=GLA_UNROLL)

            def scan(order_of, lanes):
                def body(n, s):
                    c = order_of(n)
                    st_ref[c, :, lanes] = s[:, lanes].astype(BF16)
                    return s * jnp.exp(tot_ref[pl.ds(c * C, 1), :]) + kv_ref[c]
                lax.fori_loop(0, n_chunks, body, jnp.zeros((GLA_DV, LANES), F32))

            scan(lambda n: n, slice(0, GLA_DK))
            scan(lambda n: jnp.where(n < nc_ctx, nc_ctx - 1 - n, n_chunks + nc_ctx - 1 - n), slice(GLA_DK, LANES))

        def out_chunk(c, g_ref, o_ref, local, hh=hh, xs_ref=xs_ref, qd_ref=qd_ref, xst_ref=xst_ref, qh_ref=qh_ref,
                      vv_ref=vv_ref, st_ref=st_ref):
            rows = _rows(c * C, C, C)
            a = jnp.where(lvl == GLA_LEVELS, _dot(qd_ref[rows, :], xst_ref[GLA_LEVELS, c]), 0.0)
            for lev in range(GLA_LEVELS):
                a = jnp.where(lvl == lev, _dot(xs_ref[lev, rows, :], xst_ref[lev, c]), a)
            v = vv_ref[rows, :]
            o = _dot(a.astype(BF16), jnp.concatenate([v, v], axis=0)) + _dot_nt(qh_ref[rows, :], st_ref[c])
            y = _rms(o) * ng_ref[...]
            out_rows = _rows(local * C, C, C)
            y = y * _silu(g_ref[hh, out_rows, :].astype(F32))
            o_ref[out_rows, hh * GLA_DV:(hh + 1) * GLA_DV] = y.astype(BF16)

        passes.append((prep, states, out_chunk))

    per_slab = GLA_SLAB // C
    (prep0, states0, out0), (prep1, states1, out1) = passes

    def slab(prep, out, stream, i, g_ref, o_ref):
        q_ref, k_ref, v_ref, a_ref, off, _ = stream
        chunks = []
        if out is not None:
            chunks = [functools.partial(out, off // C + i * per_slab + j, g_ref, o_ref, i * per_slab + j)
                      for j in range(per_slab)]
        if prep is not None:
            prep(i, chunks, q_ref=q_ref, k_ref=k_ref, v_ref=v_ref, a_ref=a_ref, off=off)
        else:
            for item in chunks:
                item()

    def run(prep, out):
        for stream, g_ref, o_ref in ((streams[0], gc_ref, oc_ref), (streams[1], gl_ref, ol_ref)):
            n_slabs = stream[5] // GLA_SLAB
            if n_slabs == 1:
                slab(prep, out, stream, 0, g_ref, o_ref)
            else:
                def body(i, carry, stream=stream, g_ref=g_ref, o_ref=o_ref):
                    slab(prep, out, stream, i, g_ref, o_ref)
                    return carry
                lax.fori_loop(0, n_slabs, body, 0, unroll=2 if (out is None and n_slabs % 2 == 0) else 1)

    run(prep0, None)
    states0()
    run(prep1, out0)
    states1()
    run(None, out1)


def _gla_gate_params(gate_up, gate_b):
    r = GLA_GATE_RANK
    gu = gate_up.astype(BF16).reshape(2, r, GLA_HEADS, GLA_DK).transpose(2, 0, 1, 3)
    up = jnp.zeros((GLA_HEADS, LANES, 2 * GLA_DK), BF16)
    up = up.at[:, 0:r, 0:GLA_DK].set(gu[:, 0]).at[:, r:2 * r, GLA_DK:].set(gu[:, 1])
    gb = gate_b.reshape(2, GLA_HEADS, GLA_DK).transpose(1, 0, 2).reshape(GLA_HEADS, 1, 2 * GLA_DK)
    return up, gb


def _gla(zc, zl, up_heads, gate_b_heads, ng, *, batch, ctx_len, lat_len):
    total = ctx_len + lat_len
    n_chunks = total // GLA_CHUNK

    def pair(base, n):
        return pl.BlockSpec((1, n, LANES), lambda b, p: (base + p, b, 0))

    def two(base, n):
        return pl.BlockSpec((2, n, LANES), lambda b, p: (base // 2 + p, b, 0))

    def one(n):
        return pl.BlockSpec((1, n, LANES), lambda b, p: (SLOT_GA, b, 0))

    role = _gla_role_table()
    specs = [
        pl.BlockSpec((GLA_CHUNK, 2 * GLA_CHUNK), lambda b, p: (0, 0)),
        pl.BlockSpec(role.shape, lambda b, p: (0, 0, 0)),
        pl.BlockSpec((2, LANES, LANES), lambda b, p: (p, 0, 0)),
        pl.BlockSpec((2, 1, LANES), lambda b, p: (p, 0, 0)),
        pl.BlockSpec((1, GLA_DV), lambda b, p: (0, 0)),
    ]
    for n in (ctx_len, lat_len):
        specs += [pair(SLOT_GQ, n), pair(SLOT_GK, n), two(SLOT_GV, n), two(SLOT_GR, n), one(n)]
    return pl.pallas_call(
        functools.partial(_gla_kernel, ctx_len=ctx_len, lat_len=lat_len),
        grid=(batch, GLA_HEADS // 2),
        in_specs=specs,
        out_specs=[pl.BlockSpec((ctx_len, 2 * GLA_DV), lambda b, p: (b, p)),
                   pl.BlockSpec((lat_len, 2 * GLA_DV), lambda b, p: (b, p))],
        out_shape=[jax.ShapeDtypeStruct((batch * ctx_len, GLA_HEADS * GLA_DV), BF16),
                   jax.ShapeDtypeStruct((batch * lat_len, GLA_HEADS * GLA_DV), BF16)],
        scratch_shapes=[
            pltpu.VMEM((2, GLA_LEVELS, total, LANES), BF16),
            pltpu.VMEM((2, total, LANES), BF16),
            pltpu.VMEM((2, GLA_LEVELS + 1, n_chunks, LANES, 2 * GLA_CHUNK), BF16),
            pltpu.VMEM((2, total, LANES), BF16),
            pltpu.VMEM((2, total, LANES), BF16),
            pltpu.VMEM((2, total, LANES), F32),
            pltpu.VMEM((2, total, GLA_DV), BF16),
            pltpu.VMEM((2, n_chunks, GLA_DV, LANES), F32),
            pltpu.VMEM((2, n_chunks, GLA_DV, LANES), BF16),
        ],
        compiler_params=_cparams(("arbitrary", "arbitrary")),
        name="gla",
    )(jnp.asarray(_gla_level_table()), jnp.asarray(role), up_heads, gate_b_heads, ng,
      zc, zc, zc, zc, zc, zl, zl, zl, zl, zl)


def _outproj_kernel(att_ref, ret_ref, gla_ref, w_ref, x_ref, gate_ref, sh_ref, sc_ref, g_ref, xo_ref, ho_ref):
    na, nr = att_ref.shape[1], ret_ref.shape[1]
    y = (_dot(att_ref[...], w_ref[0, 0:na, :]) + _dot(ret_ref[...], w_ref[0, na:na + nr, :])
         + _dot(gla_ref[...], w_ref[0, na + nr:, :]))
    xn = x_ref[...] + gate_ref[0] * y
    xo_ref[...] = xn
    ho_ref[...] = (_rms(xn) * g_ref[...] * (1.0 + sc_ref[0]) + sh_ref[0]).astype(BF16)


def _outproj(att, ret, gla, w, layer, x2, gate, shift, scale, g, *, tm, rows_per_mod):
    m, d = x2.shape

    def rows(n):
        return pl.BlockSpec((tm, n), lambda i: (i, 0))

    def mod():
        return pl.BlockSpec((1, 1, d), lambda i: ((i * tm) // rows_per_mod, 0, 0))

    return pl.pallas_call(
        _outproj_kernel,
        grid=(m // tm,),
        in_specs=[rows(att.shape[1]), rows(ret.shape[1]), rows(gla.shape[1]),
                  pl.BlockSpec((1,) + w.shape[1:], lambda i: (layer, 0, 0), pipeline_mode=pl.Buffered(1)),
                  rows(d), mod(), mod(), mod(),
                  pl.BlockSpec((1, d), lambda i: (0, 0))],
        out_specs=[rows(d), rows(d)],
        out_shape=[jax.ShapeDtypeStruct((m, d), F32), jax.ShapeDtypeStruct((m, d), BF16)],
        compiler_params=_cparams(("arbitrary",)),
        name="out_proj",
    )(att, ret, gla, w, x2, gate, shift, scale, g)


FFN_HALO = 16
FFN_TF = 512
FFN_TM = 512


def _ffn_kernel(h_ref, hn_ref, hp_ref, wa_ref, wv_ref, cw_ref, cb_ref, wd_ref, x_ref, gate_ref, fg_ref,
                o_ref, hs_ref, *, tm, seq_len, final_norm):
    i = pl.program_id(0)
    f = pl.program_id(1)
    ext = tm + 2 * FFN_HALO

    @pl.when(f == 0)
    def _():
        hs_ref[0:tm, :] = h_ref[...]
        hs_ref[tm:tm + FFN_HALO, :] = hn_ref[...]
        hs_ref[tm + FFN_HALO:ext, :] = hp_ref[...]
        o_ref[...] = jnp.zeros_like(o_ref)

    a = _dot(hs_ref[...], wa_ref[0])
    pos = (i * tm + lax.broadcasted_iota(jnp.int32, (ext, 1), 0)) % seq_len
    prev = jnp.where(pos == 0, 0.0, pltpu.roll(a, 1, 0))
    nxt = jnp.where(pos == seq_len - 1, 0.0, pltpu.roll(a, ext - 1, 0))
    conv = prev * cw_ref[0:1, :] + a * cw_ref[1:2, :] + nxt * cw_ref[2:3, :] + cb_ref[...]
    v = _dot(h_ref[...], wv_ref[0])
    u = (_silu(conv[0:tm]) * v).astype(BF16)
    o_ref[...] += _dot(u, wd_ref[0])

    @pl.when(f == pl.num_programs(1) - 1)
    def _():
        xn = x_ref[...] + gate_ref[0] * o_ref[...]
        if final_norm:
            xn = _rms(xn) * fg_ref[...]
        o_ref[...] = xn


def _ffn(h2, w_up, conv_w, conv_b, w_down, x2, gate, fg, *, tm, seq_len, rows_per_mod, final_norm):
    m, d = x2.shape
    tf = w_up.shape[2]
    nf = D_FF // tf
    hb = tm // FFN_HALO
    last = m // FFN_HALO - 1
    return pl.pallas_call(
        functools.partial(_ffn_kernel, tm=tm, seq_len=seq_len, final_norm=final_norm),
        grid=(m // tm, nf),
        in_specs=[
            pl.BlockSpec((tm, d), lambda i, f: (i, 0)),
            pl.BlockSpec((FFN_HALO, d), lambda i, f: (jnp.minimum((i + 1) * hb, last), 0)),
            pl.BlockSpec((FFN_HALO, d), lambda i, f: (jnp.maximum(i * hb - 1, 0), 0)),
            pl.BlockSpec((1, d, tf), lambda i, f: (f, 0, 0)),
            pl.BlockSpec((1, d, tf), lambda i, f: (nf + f, 0, 0)),
            pl.BlockSpec((3, tf), lambda i, f: (0, f)),
            pl.BlockSpec((1, tf), lambda i, f: (0, f)),
            pl.BlockSpec((1, tf, d), lambda i, f: (0, f, 0)),
            pl.BlockSpec((tm, d), lambda i, f: (i, 0)),
            pl.BlockSpec((1, 1, d), lambda i, f: ((i * tm) // rows_per_mod, 0, 0)),
            pl.BlockSpec((1, d), lambda i, f: (0, 0)),
        ],
        out_specs=pl.BlockSpec((tm, d), lambda i, f: (i, 0)),
        out_shape=jax.ShapeDtypeStruct((m, d), F32),
        scratch_shapes=[pltpu.VMEM((tm + 2 * FFN_HALO, d), BF16)],
        compiler_params=_cparams(("arbitrary", "arbitrary")),
        name="conv_glu",
    )(h2, h2, h2, w_up, w_up, conv_w, conv_b, w_down, x2, gate, fg)


def _rope_tables(n_tokens):
    rows = n_tokens // GRID_W
    row = jnp.repeat(jnp.arange(rows, dtype=F32), GRID_W)
    col = jnp.tile(jnp.arange(GRID_W, dtype=F32), rows)
    n_freq = HEAD_DIM // 4
    inv_freq = ROPE_THETA ** (-jnp.arange(n_freq, dtype=F32) / n_freq)
    ang = jnp.concatenate([row[:, None] * inv_freq, col[:, None] * inv_freq], axis=-1)
    cos, sin = jnp.cos(ang), jnp.sin(ang)
    return jnp.concatenate([cos, cos], axis=-1), jnp.concatenate([-sin, sin], axis=-1)


def kernel(x, c, ctx, c_ctx, ada_w, ada_b, norm1_g, w_in, q_norm_g, k_norm_g, ret_log_decay, ret_norm_g,
           gla_gate_up, gla_gate_b, gla_norm_g, w_out, norm2_g, w_up, conv_w, conv_b, w_down, final_norm_g):
    batch, lat_len, d = x.shape
    ctx_len = ctx.shape[1]
    depth = ada_w.shape[0]
    mod_rows = 16
    cc = jnp.concatenate([c, c_ctx[None], jnp.zeros((mod_rows - batch - 1, d), F32)], axis=0)
    mod = _ada(cc, ada_w, ada_b).reshape(depth, mod_rows, N_MOD, d)

    cos_l, sin_l = _rope_tables(lat_len)
    proj_tm = PROJ_TM
    cos_c = jnp.ones((proj_tm, LANES), F32)
    sin_c = jnp.zeros((proj_tm, LANES), F32)

    xl = x.reshape(batch * lat_len, d)
    xc = ctx.reshape(batch * ctx_len, d)
    row = lambda v: v.reshape(1, -1)
    w_in_b = w_in.astype(BF16)
    w_out_b = w_out[0].astype(BF16)[None]

    for l in range(depth):
        last = l == depth - 1
        ml = [mod[l, :batch, k].reshape(batch, 1, d) for k in range(N_MOD)]
        mc = [mod[l, batch, k].reshape(1, 1, d) for k in range(N_MOD)]
        up_heads, gate_b_heads = _gla_gate_params(gla_gate_up[l], gla_gate_b[l])

        zl, vtl = _proj(xl, ml[0], ml[1], row(norm1_g[l]), w_in_b, l, cos_l, sin_l, row(q_norm_g[l]), row(k_norm_g[l]),
                        tm=proj_tm, rows_per_mod=lat_len, rope_tiles=lat_len // proj_tm)
        zc, vtc = _proj(xc, mc[0], mc[1], row(norm1_g[l]), w_in_b, l, cos_c, sin_c, row(q_norm_g[l]), row(k_norm_g[l]),
                        tm=proj_tm, rows_per_mod=batch * ctx_len, rope_tiles=1)

        casts = [(w_up, l, FFN_TF), (w_down, l, None)]
        if not last:
            casts += [(w_out, l + 1, None)]
        att_l, cast = _attn(vtl, [(zc, vtc, ctx_len), (zl, vtl, lat_len)], batch=batch, q_len=lat_len, tq=ATT_TQ,
                            casts=casts)
        w_up_b, w_down_b = cast[0], cast[1][None]
        ret_c, ret_l = _ret(zc, zl, ret_log_decay[l], row(ret_norm_g[l]),
                            batch=batch, ctx_len=ctx_len, lat_len=lat_len)
        gla_c, gla_l = _gla(zc, zl, up_heads, gate_b_heads, row(gla_norm_g[l]),
                            batch=batch, ctx_len=ctx_len, lat_len=lat_len)

        xl, h2 = _outproj(att_l, ret_l, gla_l, w_out_b, 0, xl, ml[2], ml[3], ml[4], row(norm2_g[l]),
                          tm=OUTPROJ_TM, rows_per_mod=lat_len)
        xl = _ffn(h2, w_up_b, conv_w[l], row(conv_b[l]), w_down_b, xl, ml[5], row(final_norm_g),
                  tm=FFN_TM, seq_len=lat_len, rows_per_mod=lat_len, final_norm=last)

        if not last:
            att_c, _ = _attn(vtc, [(zc, vtc, ctx_len)], batch=batch, q_len=ctx_len, tq=ctx_len)
            xc, hc2 = _outproj(att_c, ret_c, gla_c, w_out_b, 0, xc, mc[2], mc[3], mc[4], row(norm2_g[l]),
                               tm=OUTPROJ_TM, rows_per_mod=batch * ctx_len)
            xc = _ffn(hc2, w_up_b, conv_w[l], row(conv_b[l]), w_down_b, xc, mc[5], row(final_norm_g),
                      tm=FFN_TM, seq_len=ctx_len, rows_per_mod=batch * ctx_len, final_norm=False)
            w_out_b = cast[2][None]

    return xl.reshape(batch, lat_len, d)
```

```python
import functools

import numpy as np
import jax
import jax.numpy as jnp
from jax import lax
from jax.experimental import pallas as pl
from jax.experimental.pallas import tpu as pltpu

F32 = jnp.float32
BF16 = jnp.bfloat16

HEAD_DIM = 128
GRID_W = 64
ATT_Q_HEADS = 8
ATT_KV_HEADS = 2
GQA = ATT_Q_HEADS // ATT_KV_HEADS
RET_HEADS = 4
RET_DIM = 128
RET_CHUNK = 128
GLA_HEADS = 4
GLA_DK = 64
GLA_DV = 128
GLA_GATE_RANK = 16
GLA_TAU = 16.0
GLA_CHUNK = 64
GLA_LEVELS = 6
D_FF = 5632
ROPE_THETA = 10000.0
N_MOD = 6
EPS = 1e-6

LOG2_E = 1.4426950408889634

LANES = 128
N_IN = 5152
PROJ_TN = 256
N_SLOTS = -(-N_IN // LANES)
SLOT_AQ, SLOT_AK, SLOT_AV = 0, 8, 10
SLOT_RQ, SLOT_RK, SLOT_RV, SLOT_RG = 12, 16, 20, 24
SLOT_GQ, SLOT_GK, SLOT_GV, SLOT_GR, SLOT_GA = 28, 30, 32, 36, 40

VMEM_LIMIT = 52 * 1024 * 1024
ADA_TN = 1024
PROJ_TM = 512
ATT_TQ = 256
OUTPROJ_TM = 256

_NT = (((1,), (1,)), ((), ()))
_TN = (((0,), (0,)), ((), ()))


def _cparams(sem, vmem_limit=VMEM_LIMIT):
    return pltpu.CompilerParams(dimension_semantics=sem, vmem_limit_bytes=vmem_limit)


def _dot(a, b):
    return jnp.dot(a, b, preferred_element_type=F32)


def _dot_nt(a, b):
    return lax.dot_general(a, b, _NT, preferred_element_type=F32)


def _dot_tn(a, b):
    return lax.dot_general(a, b, _TN, preferred_element_type=F32)


def _rows(start, size, align):
    if isinstance(start, int):
        return pl.ds(start, size)
    return pl.ds(pl.multiple_of(start, align), size)


def _rms(t):
    return t * lax.rsqrt(jnp.mean(t * t, axis=-1, keepdims=True) + EPS)


def _silu(t):
    return t * jax.nn.sigmoid(t)


def _ada_kernel(c_ref, w_ref, b_ref, o_ref):
    s = _silu(c_ref[...]).astype(BF16)
    o_ref[0] = _dot(s, w_ref[0].astype(BF16)) + b_ref[0]


def _ada(cc, ada_w, ada_b):
    depth, d, n = ada_w.shape
    rows = cc.shape[0]
    tn = ADA_TN
    return pl.pallas_call(
        _ada_kernel,
        grid=(depth, n // tn),
        in_specs=[
            pl.BlockSpec((rows, d), lambda l, j: (0, 0)),
            pl.BlockSpec((1, d, tn), lambda l, j: (l, 0, j)),
            pl.BlockSpec((1, 1, tn), lambda l, j: (l, 0, j)),
        ],
        out_specs=pl.BlockSpec((1, rows, tn), lambda l, j: (l, 0, j)),
        out_shape=jax.ShapeDtypeStruct((depth, rows, n), F32),
        compiler_params=_cparams(("arbitrary", "arbitrary")),
        name="ada_mod",
    )(cc, ada_w, ada_b.reshape(depth, 1, n))


def _proj_kernel(x_ref, sh_ref, sc_ref, g_ref, w_ref, cos_ref, sin_ref, qg_ref, kg_ref, o_ref, vt_ref):
    y = _rms(x_ref[...]) * g_ref[...]
    h = (y * (1.0 + sc_ref[0]) + sh_ref[0]).astype(BF16)
    cos, sin = cos_ref[...], sin_ref[...]
    qg = qg_ref[...] * (HEAD_DIM ** -0.5 * LOG2_E)
    kg = kg_ref[...]

    def rope(t):
        return t * cos + pltpu.roll(t, HEAD_DIM // 2, 1) * sin

    def transform(slot, t):
        if slot < SLOT_AK:
            return rope(_rms(t) * qg)
        if slot < SLOT_AV:
            return rope(_rms(t) * kg)
        if SLOT_RQ <= slot < SLOT_RK:
            return rope(t)
        if SLOT_RK <= slot < SLOT_RV:
            return rope(t * (RET_DIM ** -0.5))
        if SLOT_GQ <= slot < SLOT_GK:
            return t * (GLA_DK ** -0.5)
        return t

    for c0 in range(0, N_IN, PROJ_TN):
        c1 = min(c0 + PROJ_TN, N_IN)
        z = _dot(h, w_ref[0, :, c0:c1])
        for s0 in range(c0, c1, LANES):
            slot, n = s0 // LANES, min(LANES, c1 - s0)
            t = transform(slot, z[:, s0 - c0:s0 - c0 + n])
            if slot < SLOT_AK:
                vt_ref[slot] = t.T.astype(BF16)
            elif SLOT_AV <= slot < SLOT_RQ:
                vt_ref[ATT_Q_HEADS + slot - SLOT_AV] = t.T.astype(BF16)
            t = t.astype(BF16)
            if n < LANES:
                o_ref[slot] = jnp.zeros(o_ref.shape[1:], BF16)
                o_ref[slot, :, 0:n] = t
            else:
                o_ref[slot] = t


def _proj(x2, shift, scale, g, w, layer, cosf, sinf, qg, kg, *, tm, rows_per_mod, rope_tiles):
    m, d = x2.shape
    return pl.pallas_call(
        _proj_kernel,
        grid=(m // tm,),
        in_specs=[
            pl.BlockSpec((tm, d), lambda i: (i, 0)),
            pl.BlockSpec((1, 1, d), lambda i: ((i * tm) // rows_per_mod, 0, 0)),
            pl.BlockSpec((1, 1, d), lambda i: ((i * tm) // rows_per_mod, 0, 0)),
            pl.BlockSpec((1, d), lambda i: (0, 0)),
            pl.BlockSpec((1, d, N_IN), lambda i: (layer, 0, 0), pipeline_mode=pl.Buffered(1)),
            pl.BlockSpec((tm, LANES), lambda i: (i % rope_tiles, 0)),
            pl.BlockSpec((tm, LANES), lambda i: (i % rope_tiles, 0)),
            pl.BlockSpec((1, LANES), lambda i: (0, 0)),
            pl.BlockSpec((1, LANES), lambda i: (0, 0)),
        ],
        out_specs=[pl.BlockSpec((N_SLOTS, tm, LANES), lambda i: (0, i, 0)),
                   pl.BlockSpec((ATT_Q_HEADS + ATT_KV_HEADS, HEAD_DIM, tm), lambda i: (0, 0, i))],
        out_shape=[jax.ShapeDtypeStruct((N_SLOTS, m, LANES), BF16),
                   jax.ShapeDtypeStruct((ATT_Q_HEADS + ATT_KV_HEADS, HEAD_DIM, m), BF16)],
        compiler_params=_cparams(("arbitrary",)),
        name="norm_proj",
    )(x2, shift, scale, g, w, cosf, sinf, qg, kg)


ATT_KEY_BLOCK = 128


def _attn_kernel(*refs, n_kv, reps):
    n_cast = len(reps)
    q_ref = refs[0]
    kv_refs = refs[1:1 + 2 * n_kv]
    w_refs = refs[1 + 2 * n_kv:1 + 2 * n_kv + n_cast]
    o_ref = refs[1 + 2 * n_kv + n_cast]
    wo_refs = refs[2 + 2 * n_kv + n_cast:]
    step = (pl.program_id(0) * pl.num_programs(1) + pl.program_id(1)) * pl.num_programs(2) + pl.program_id(2)

    def convert(w_ref, wo_ref):
        if len(wo_ref.shape) == 3:
            tn = wo_ref.shape[2]
            for t in range(wo_ref.shape[0]):
                wo_ref[t] = w_ref[0, :, t * tn:(t + 1) * tn].astype(BF16)
        else:
            wo_ref[...] = w_ref[0].astype(BF16)

    for w_ref, wo_ref, rep in zip(w_refs, wo_refs, reps):
        if rep == 1:
            convert(w_ref, wo_ref)
        else:
            pl.when(step % rep == 0)(functools.partial(convert, w_ref, wo_ref))
    m, l, acc = [None] * GQA, [None] * GQA, [None] * GQA
    units = [(s, j0, g) for s in range(n_kv) for j0 in range(0, kv_refs[2 * s].shape[1], ATT_KEY_BLOCK)
             for g in range(GQA)]

    def scores(unit):
        s, j0, g = unit
        return _dot(kv_refs[2 * s][0, j0:j0 + ATT_KEY_BLOCK, :], q_ref[g])

    st_next = scores(units[0])
    for u, (s, j0, g) in enumerate(units):
        st = st_next
        if u + 1 < len(units):
            st_next = scores(units[u + 1])
        vt = kv_refs[2 * s + 1][0, :, j0:j0 + ATT_KEY_BLOCK]
        mb = jnp.max(st, axis=0, keepdims=True)
        if m[g] is None:
            m[g] = mb
            p = jnp.exp2(st - mb)
            l[g] = jnp.sum(p, axis=0, keepdims=True)
            acc[g] = _dot(vt, p.astype(BF16))
        else:
            m_new = jnp.maximum(m[g], mb)
            alpha = jnp.exp2(m[g] - m_new)
            p = jnp.exp2(st - m_new)
            l[g] = alpha * l[g] + jnp.sum(p, axis=0, keepdims=True)
            acc[g] = alpha * acc[g] + _dot(vt, p.astype(BF16))
            m[g] = m_new
    for g in range(GQA):
        o_ref[:, g * HEAD_DIM:(g + 1) * HEAD_DIM] = (acc[g] / l[g]).T.astype(BF16)


def _attn(qt, kv_sources, *, batch, q_len, tq, casts=()):
    nq = q_len // tq
    steps = batch * ATT_KV_HEADS * nq
    in_specs = [pl.BlockSpec((GQA, HEAD_DIM, tq), lambda b, k, i: (k, 0, b * nq + i))]
    args = [qt]
    for z, vt, kv_len in kv_sources:
        in_specs.append(pl.BlockSpec((1, kv_len, HEAD_DIM), lambda b, k, i: (SLOT_AK + k, b, 0)))
        in_specs.append(pl.BlockSpec((1, HEAD_DIM, kv_len), lambda b, k, i: (ATT_Q_HEADS + k, 0, b)))
        args += [z, vt]
    out_specs = [pl.BlockSpec((tq, GQA * HEAD_DIM), lambda b, k, i: (b * nq + i, k))]
    out_shape = [jax.ShapeDtypeStruct((batch * q_len, ATT_Q_HEADS * HEAD_DIM), BF16)]
    step = lambda b, k, i: (b * ATT_KV_HEADS + k) * nq + i
    reps = []
    for w, layer, tile in casts:
        _, rows, cols = w.shape
        rep = next(r for r in (1, 2, 4, 8, 16) if rows % (steps // r) == 0 and rows // (steps // r) % 16 == 0)
        slab = rows // (steps // rep)
        reps.append(rep)
        in_specs.append(pl.BlockSpec((1, slab, cols),
                                     lambda b, k, i, rep=rep, layer=layer: (layer, step(b, k, i) // rep, 0)))
        args.append(w)
        if tile is None:
            out_specs.append(pl.BlockSpec((slab, cols), lambda b, k, i, rep=rep: (step(b, k, i) // rep, 0)))
            out_shape.append(jax.ShapeDtypeStruct((rows, cols), BF16))
        else:
            out_specs.append(pl.BlockSpec((cols // tile, slab, tile),
                                          lambda b, k, i, rep=rep: (0, step(b, k, i) // rep, 0)))
            out_shape.append(jax.ShapeDtypeStruct((cols // tile, rows, tile), BF16))
    outs = pl.pallas_call(
        functools.partial(_attn_kernel, n_kv=len(kv_sources), reps=tuple(reps)),
        grid=(batch, ATT_KV_HEADS, nq),
        in_specs=in_specs,
        out_specs=out_specs,
        out_shape=out_shape,
        compiler_params=_cparams(("arbitrary", "arbitrary", "arbitrary")),
        name="attention",
    )(*args)
    return outs[0], list(outs[1:])


def _ret_kernel(dec_ref, qc_ref, kc_ref, vc_ref, gc_ref, ql_ref, kl_ref, vl_ref, gl_ref, ng_ref,
                oc_ref, ol_ref, kv_ref, st_ref, *, nc_ctx, nc_lat):
    h = pl.program_id(1)
    C = RET_CHUNK
    lg_f, lg_b = dec_ref[0, h], dec_ref[1, h]
    rel = lax.broadcasted_iota(jnp.int32, (C, C), 0) - lax.broadcasted_iota(jnp.int32, (C, C), 1)
    dmat = (jnp.where(rel >= 0, jnp.exp(lg_f * jnp.maximum(rel, 0).astype(F32)), 0.0)
            + jnp.where(rel <= 0, jnp.exp(lg_b * jnp.maximum(-rel, 0).astype(F32)), 0.0))
    row = lax.broadcasted_iota(jnp.int32, (C, 1), 0).astype(F32)
    qd_f, qd_b = jnp.exp(lg_f * (row + 1.0)), jnp.exp(lg_b * (C - row))
    kd_f, kd_b = jnp.exp(lg_f * (C - 1.0 - row)), jnp.exp(lg_b * row)
    cd_f, cd_b = jnp.exp(lg_f * C), jnp.exp(lg_b * C)

    chunks = [(qc_ref, kc_ref, vc_ref, gc_ref, oc_ref, n) for n in range(nc_ctx)]
    chunks += [(ql_ref, kl_ref, vl_ref, gl_ref, ol_ref, n) for n in range(nc_lat)]
    order_b = list(range(nc_ctx - 1, -1, -1)) + list(range(nc_ctx + nc_lat - 1, nc_ctx - 1, -1))

    def decayed(x_ref, n, dec_f, dec_b):
        x = x_ref[0, n * C:(n + 1) * C, :].astype(F32)
        return jnp.concatenate([(x * dec_f).astype(BF16), (x * dec_b).astype(BF16)], axis=1)

    for c, (_, k_ref, v_ref, _, _, n) in enumerate(chunks):
        kv_ref[c] = _dot_tn(decayed(k_ref, n, kd_f, kd_b), v_ref[0, n * C:(n + 1) * C, :])

    s = jnp.zeros((C, RET_DIM), F32)
    for c in range(len(chunks)):
        st_ref[c, 0:C, :] = s.astype(BF16)
        s = cd_f * s + kv_ref[c, 0:C, :]
    s = jnp.zeros((C, RET_DIM), F32)
    for c in order_b:
        st_ref[c, C:2 * C, :] = s.astype(BF16)
        s = cd_b * s + kv_ref[c, C:2 * C, :]

    for c, (q_ref, k_ref, v_ref, g_ref, o_ref, n) in enumerate(chunks):
        sl = slice(n * C, (n + 1) * C)
        v = v_ref[0, sl, :]
        att = _dot_nt(q_ref[0, sl, :], k_ref[0, sl, :]) * dmat
        o = _dot(att.astype(BF16), v) + _dot(decayed(q_ref, n, qd_f, qd_b), st_ref[c])
        y = _rms(o) * ng_ref[...]
        o_ref[sl, :] = (y * _silu(g_ref[0, sl, :].astype(F32))).astype(BF16)


def _ret(zc, zl, decay, ng, *, batch, ctx_len, lat_len):
    def slot(base, n):
        return pl.BlockSpec((1, n, RET_DIM), lambda b, h: (base + h, b, 0))

    n_chunks = (ctx_len + lat_len) // RET_CHUNK
    specs = [pl.BlockSpec(memory_space=pltpu.SMEM)]
    specs += [slot(s, ctx_len) for s in (SLOT_RQ, SLOT_RK, SLOT_RV, SLOT_RG)]
    specs += [slot(s, lat_len) for s in (SLOT_RQ, SLOT_RK, SLOT_RV, SLOT_RG)]
    specs += [pl.BlockSpec((1, RET_DIM), lambda b, h: (0, 0))]
    return pl.pallas_call(
        functools.partial(_ret_kernel, nc_ctx=ctx_len // RET_CHUNK, nc_lat=lat_len // RET_CHUNK),
        grid=(batch, RET_HEADS),
        in_specs=specs,
        out_specs=[pl.BlockSpec((ctx_len, RET_DIM), lambda b, h: (b, h)),
                   pl.BlockSpec((lat_len, RET_DIM), lambda b, h: (b, h))],
        out_shape=[jax.ShapeDtypeStruct((batch * ctx_len, RET_HEADS * RET_DIM), BF16),
                   jax.ShapeDtypeStruct((batch * lat_len, RET_HEADS * RET_DIM), BF16)],
        scratch_shapes=[pltpu.VMEM((n_chunks, 2 * RET_DIM, RET_DIM), F32),
                        pltpu.VMEM((n_chunks, 2 * RET_DIM, RET_DIM), BF16)],
        compiler_params=_cparams(("arbitrary", "arbitrary")),
        name="retention",
    )(decay, zc, zc, zc, zc, zl, zl, zl, zl, ng)


GLA_SLAB = 256
GLA_UNROLL = 4


def _gla_level_table():
    i = np.arange(GLA_CHUNK)[:, None]
    j = np.arange(GLA_CHUNK)[None, :]
    x = np.maximum(i ^ j, 1)
    lvl = np.floor(np.log2(x)).astype(np.int32)
    lvl = np.where(i == j, GLA_LEVELS, np.where(i > j, lvl, -1)).astype(np.int32)
    return np.concatenate([lvl, lvl.T], axis=1)


def _gla_role_table():
    t = np.arange(GLA_SLAB)
    fwd = np.arange(LANES)[None, :] < GLA_DK
    return np.stack([np.where(((t[:, None] >> lev) & 1).astype(bool) == fwd, LOG2_E, -LOG2_E)
                     for lev in range(GLA_LEVELS)]).astype(np.float32)


def _bcast_block_row(x, blk, r):
    rows = x.shape[0]
    if blk >= 8:
        x3 = x.reshape(rows // blk, blk, LANES)
        return jnp.broadcast_to(x3[:, r:r + 1, :], x3.shape).reshape(rows, LANES)
    pos = lax.broadcasted_iota(jnp.int32, x.shape, 0) & (blk - 1)
    out = x
    for p in range(blk):
        if p != r:
            out = jnp.where(pos == p, pltpu.roll(x, (p - r) % rows, 0), out)
    return out


def _gla_kernel(lvl_ref, role_ref, bdm_ref, up_ref, gb_ref, ng_ref,
                qc_ref, kc_ref, vc_ref, gc_ref, ac_ref, ql_ref, kl_ref, vl_ref, gl_ref, al_ref,
                oc_ref, ol_ref, *scratch, ctx_len, lat_len):
    C = GLA_CHUNK
    streams = ((qc_ref, kc_ref, vc_ref, ac_ref, 0, ctx_len), (ql_ref, kl_ref, vl_ref, al_ref, ctx_len, lat_len))
    nc_ctx, nc_lat = ctx_len // C, lat_len // C
    n_chunks = nc_ctx + nc_lat
    cpos = lax.broadcasted_iota(jnp.int32, (GLA_SLAB, LANES), 0) & (C - 1)
    fwd = lax.broadcasted_iota(jnp.int32, (GLA_SLAB, LANES), 1) < GLA_DK
    lvl = lvl_ref[...]

    passes = []
    for hh in range(2):
        own = fwd if hh == 0 else jnp.logical_not(fwd)
        xs_ref, qd_ref, xst_ref, qh_ref, kh_ref, tot_ref, vv_ref, kv_ref, st_ref = (r.at[hh] for r in scratch)

        def prep(i, between, q_ref, k_ref, v_ref, a_ref, off, hh=hh, own=own, xs_ref=xs_ref, qd_ref=qd_ref,
                 xst_ref=xst_ref, qh_ref=qh_ref, kh_ref=kh_ref, tot_ref=tot_ref, vv_ref=vv_ref):
            between = list(between)
            src = _rows(i * GLA_SLAB, GLA_SLAB, GLA_SLAB)
            dst = _rows(off + i * GLA_SLAB, GLA_SLAB, C)
            chunk0 = off // C + i * (GLA_SLAB // C)

            def store_keys(lev, x):
                y = jnp.concatenate([x[c * C:(c + 1) * C] for c in range(GLA_SLAB // C) for _ in range(2)], axis=0)
                yt = y.T.astype(BF16) * bdm_ref[...]
                for c in range(GLA_SLAB // C):
                    xst_ref[lev, chunk0 + c] = yt[:, c * 2 * C:(c + 1) * 2 * C]

            logit = _dot(a_ref[0, src, :], up_ref[hh]) + gb_ref[hh]
            la = (jnp.minimum(logit, 0.0) - jnp.log1p(jnp.exp(-jnp.abs(logit)))) * (1.0 / GLA_TAU)
            pre = la
            for step in range(GLA_LEVELS):
                sh = 1 << step
                pre = pre + jnp.where(cpos >= sh, pltpu.roll(pre, sh, 0), 0.0)
            tot = _bcast_block_row(pre, C, C - 1)
            cum = jnp.where(fwd, pre, tot - pre + la)
            q2 = q_ref[0, src, :].astype(F32)
            k2 = k_ref[0, src, :].astype(F32)
            q = jnp.where(own, q2, pltpu.roll(q2, GLA_DK, 1))
            k = jnp.where(own, k2, pltpu.roll(k2, GLA_DK, 1))
            qd_ref[dst, :] = q.astype(BF16)
            store_keys(GLA_LEVELS, k)
            qh_ref[dst, :] = (q * jnp.exp(cum)).astype(BF16)
            kh_ref[dst, :] = (k * jnp.exp(tot - cum)).astype(BF16)
            tot_ref[dst, :] = tot
            vv_ref[dst, :] = v_ref[hh, src, :]
            for lev in range(GLA_LEVELS):
                if between and lev % 2 == 0:
                    between.pop(0)()
                s = 1 << lev
                edge = jnp.where(fwd, _bcast_block_row(cum, 2 * s, s - 1), _bcast_block_row(cum, 2 * s, s))
                role = role_ref[lev]
                x = jnp.where(role > 0, q, k) * jnp.exp2((cum - edge) * role)
                xs_ref[lev, dst, :] = x.astype(BF16)
                store_keys(lev, x)
            for item in between:
                item()

        def states(kv_ref=kv_ref, st_ref=st_ref, vv_ref=vv_ref, kh_ref=kh_ref, tot_ref=tot_ref):
            def kv_body(c, carry):
                rows = _rows(c * C, C, C)
                kv_ref[c] = _dot_tn(vv_ref[rows, :], kh_ref[rows, :])
                return carry

            lax.fori_loop(0, n_chunks, kv_body, 0, unroll=GLA_UNROLL)

            def step(c, s, lanes):
                st_ref[c, :, lanes] = s[:, lanes].astype(BF16)
                return s * jnp.exp(tot_ref[pl.ds(c * C, 1), :]) + kv_ref[c]

            def body(n, carry):
                s_f, s_b = carry
                c_b = jnp.where(n < nc_ctx, nc_ctx - 1 - n, n_chunks + nc_ctx - 1 - n)
                return step(n, s_f, slice(0, GLA_DK)), step(c_b, s_b, slice(GLA_DK, LANES))

            zero = jnp.zeros((GLA_DV, LANES), F32)
            lax.fori_loop(0, n_chunks, body, (zero, zero))

        def out_chunk(c, g_ref, o_ref, local, hh=hh, xs_ref=xs_ref, qd_ref=qd_ref, xst_ref=xst_ref, qh_ref=qh_ref,
                      vv_ref=vv_ref, st_ref=st_ref):
            rows = _rows(c * C, C, C)
            a = jnp.where(lvl == GLA_LEVELS, _dot(qd_ref[rows, :], xst_ref[GLA_LEVELS, c]), 0.0)
            for lev in range(GLA_LEVELS):
                a = jnp.where(lvl == lev, _dot(xs_ref[lev, rows, :], xst_ref[lev, c]), a)
            v = vv_ref[rows, :]
            o = _dot(a.astype(BF16), jnp.concatenate([v, v], axis=0)) + _dot_nt(qh_ref[rows, :], st_ref[c])
            y = _rms(o) * ng_ref[...]
            out_rows = _rows(local * C, C, C)
            y = y * _silu(g_ref[hh, out_rows, :].astype(F32))
            o_ref[out_rows, hh * GLA_DV:(hh + 1) * GLA_DV] = y.astype(BF16)

        passes.append((prep, states, out_chunk))

    per_slab = GLA_SLAB // C
    (prep0, states0, out0), (prep1, states1, out1) = passes

    def slab(prep, out, stream, i, g_ref, o_ref):
        q_ref, k_ref, v_ref, a_ref, off, _ = stream
        chunks = []
        if out is not None:
            chunks = [functools.partial(out, off // C + i * per_slab + j, g_ref, o_ref, i * per_slab + j)
                      for j in range(per_slab)]
        if prep is not None:
            prep(i, chunks, q_ref=q_ref, k_ref=k_ref, v_ref=v_ref, a_ref=a_ref, off=off)
        else:
            for item in chunks:
                item()

    def run(prep, out):
        for stream, g_ref, o_ref in ((streams[0], gc_ref, oc_ref), (streams[1], gl_ref, ol_ref)):
            n_slabs = stream[5] // GLA_SLAB
            if n_slabs == 1:
                slab(prep, out, stream, 0, g_ref, o_ref)
            else:
                def body(i, carry, stream=stream, g_ref=g_ref, o_ref=o_ref):
                    slab(prep, out, stream, i, g_ref, o_ref)
                    return carry
                lax.fori_loop(0, n_slabs, body, 0, unroll=2 if n_slabs % 2 == 0 else 1)

    run(prep0, None)
    states0()
    run(prep1, out0)
    states1()
    run(None, out1)


def _gla_gate_params(gate_up, gate_b):
    r = GLA_GATE_RANK
    gu = gate_up.astype(BF16).reshape(2, r, GLA_HEADS, GLA_DK).transpose(2, 0, 1, 3)
    up = jnp.zeros((GLA_HEADS, LANES, 2 * GLA_DK), BF16)
    up = up.at[:, 0:r, 0:GLA_DK].set(gu[:, 0]).at[:, r:2 * r, GLA_DK:].set(gu[:, 1])
    gb = gate_b.reshape(2, GLA_HEADS, GLA_DK).transpose(1, 0, 2).reshape(GLA_HEADS, 1, 2 * GLA_DK)
    return up, gb


def _gla(zc, zl, up_heads, gate_b_heads, ng, *, batch, ctx_len, lat_len):
    total = ctx_len + lat_len
    n_chunks = total // GLA_CHUNK

    def pair(base, n):
        return pl.BlockSpec((1, n, LANES), lambda b, p: (base + p, b, 0))

    def two(base, n):
        return pl.BlockSpec((2, n, LANES), lambda b, p: (base // 2 + p, b, 0))

    def one(n):
        return pl.BlockSpec((1, n, LANES), lambda b, p: (SLOT_GA, b, 0))

    role = _gla_role_table()
    feat_fwd = np.arange(LANES)[:, None] < GLA_DK
    key_fwd = (np.arange(2 * GLA_SLAB)[None, :] % (2 * GLA_CHUNK)) < GLA_CHUNK
    bd_mask = (feat_fwd == key_fwd).astype(np.float32)
    specs = [
        pl.BlockSpec((GLA_CHUNK, 2 * GLA_CHUNK), lambda b, p: (0, 0)),
        pl.BlockSpec(role.shape, lambda b, p: (0, 0, 0)),
        pl.BlockSpec(bd_mask.shape, lambda b, p: (0, 0)),
        pl.BlockSpec((2, LANES, LANES), lambda b, p: (p, 0, 0)),
        pl.BlockSpec((2, 1, LANES), lambda b, p: (p, 0, 0)),
        pl.BlockSpec((1, GLA_DV), lambda b, p: (0, 0)),
    ]
    for n in (ctx_len, lat_len):
        specs += [pair(SLOT_GQ, n), pair(SLOT_GK, n), two(SLOT_GV, n), two(SLOT_GR, n), one(n)]
    return pl.pallas_call(
        functools.partial(_gla_kernel, ctx_len=ctx_len, lat_len=lat_len),
        grid=(batch, GLA_HEADS // 2),
        in_specs=specs,
        out_specs=[pl.BlockSpec((ctx_len, 2 * GLA_DV), lambda b, p: (b, p)),
                   pl.BlockSpec((lat_len, 2 * GLA_DV), lambda b, p: (b, p))],
        out_shape=[jax.ShapeDtypeStruct((batch * ctx_len, GLA_HEADS * GLA_DV), BF16),
                   jax.ShapeDtypeStruct((batch * lat_len, GLA_HEADS * GLA_DV), BF16)],
        scratch_shapes=[
            pltpu.VMEM((2, GLA_LEVELS, total, LANES), BF16),
            pltpu.VMEM((2, total, LANES), BF16),
            pltpu.VMEM((2, GLA_LEVELS + 1, n_chunks, LANES, 2 * GLA_CHUNK), BF16),
            pltpu.VMEM((2, total, LANES), BF16),
            pltpu.VMEM((2, total, LANES), BF16),
            pltpu.VMEM((2, total, LANES), F32),
            pltpu.VMEM((2, total, GLA_DV), BF16),
            pltpu.VMEM((2, n_chunks, GLA_DV, LANES), F32),
            pltpu.VMEM((2, n_chunks, GLA_DV, LANES), BF16),
        ],
        compiler_params=_cparams(("arbitrary", "arbitrary")),
        name="gla",
    )(jnp.asarray(_gla_level_table()), jnp.asarray(role), jnp.asarray(bd_mask, BF16), up_heads, gate_b_heads, ng,
      zc, zc, zc, zc, zc, zl, zl, zl, zl, zl)


def _outproj_kernel(att_ref, ret_ref, gla_ref, w_ref, x_ref, gate_ref, sh_ref, sc_ref, g_ref, xo_ref, ho_ref):
    na, nr = att_ref.shape[1], ret_ref.shape[1]
    y = (_dot(att_ref[...], w_ref[0, 0:na, :]) + _dot(ret_ref[...], w_ref[0, na:na + nr, :])
         + _dot(gla_ref[...], w_ref[0, na + nr:, :]))
    xn = x_ref[...] + gate_ref[0] * y
    xo_ref[...] = xn
    ho_ref[...] = (_rms(xn) * g_ref[...] * (1.0 + sc_ref[0]) + sh_ref[0]).astype(BF16)


def _outproj(att, ret, gla, w, layer, x2, gate, shift, scale, g, *, tm, rows_per_mod):
    m, d = x2.shape

    def rows(n):
        return pl.BlockSpec((tm, n), lambda i: (i, 0))

    def mod():
        return pl.BlockSpec((1, 1, d), lambda i: ((i * tm) // rows_per_mod, 0, 0))

    return pl.pallas_call(
        _outproj_kernel,
        grid=(m // tm,),
        in_specs=[rows(att.shape[1]), rows(ret.shape[1]), rows(gla.shape[1]),
                  pl.BlockSpec((1,) + w.shape[1:], lambda i: (layer, 0, 0), pipeline_mode=pl.Buffered(1)),
                  rows(d), mod(), mod(), mod(),
                  pl.BlockSpec((1, d), lambda i: (0, 0))],
        out_specs=[rows(d), rows(d)],
        out_shape=[jax.ShapeDtypeStruct((m, d), F32), jax.ShapeDtypeStruct((m, d), BF16)],
        compiler_params=_cparams(("arbitrary",)),
        name="out_proj",
    )(att, ret, gla, w, x2, gate, shift, scale, g)


FFN_HALO = 16
FFN_TF = 512
FFN_TM = 512


def _ffn_kernel(h_ref, hn_ref, hp_ref, wa_ref, wv_ref, cw_ref, cb_ref, wd_ref, x_ref, gate_ref, fg_ref,
                o_ref, hs_ref, *, tm, seq_len, final_norm):
    i = pl.program_id(0)
    f = pl.program_id(1)
    ext = tm + 2 * FFN_HALO

    @pl.when(f == 0)
    def _():
        hs_ref[0:tm, :] = h_ref[...]
        hs_ref[tm:tm + FFN_HALO, :] = hn_ref[...]
        hs_ref[tm + FFN_HALO:ext, :] = hp_ref[...]
        o_ref[...] = jnp.zeros_like(o_ref)

    a = _dot(hs_ref[...], wa_ref[0])
    pos = (i * tm + lax.broadcasted_iota(jnp.int32, (ext, 1), 0)) % seq_len
    prev = jnp.where(pos == 0, 0.0, pltpu.roll(a, 1, 0))
    nxt = jnp.where(pos == seq_len - 1, 0.0, pltpu.roll(a, ext - 1, 0))
    conv = prev * cw_ref[0:1, :] + a * cw_ref[1:2, :] + nxt * cw_ref[2:3, :] + cb_ref[...]
    v = _dot(h_ref[...], wv_ref[0])
    u = (_silu(conv[0:tm]) * v).astype(BF16)
    o_ref[...] += _dot(u, wd_ref[0])

    @pl.when(f == pl.num_programs(1) - 1)
    def _():
        xn = x_ref[...] + gate_ref[0] * o_ref[...]
        if final_norm:
            xn = _rms(xn) * fg_ref[...]
        o_ref[...] = xn


def _ffn(h2, w_up, conv_w, conv_b, w_down, x2, gate, fg, *, tm, seq_len, rows_per_mod, final_norm):
    m, d = x2.shape
    tf = w_up.shape[2]
    nf = D_FF // tf
    hb = tm // FFN_HALO
    last = m // FFN_HALO - 1
    return pl.pallas_call(
        functools.partial(_ffn_kernel, tm=tm, seq_len=seq_len, final_norm=final_norm),
        grid=(m // tm, nf),
        in_specs=[
            pl.BlockSpec((tm, d), lambda i, f: (i, 0)),
            pl.BlockSpec((FFN_HALO, d), lambda i, f: (jnp.minimum((i + 1) * hb, last), 0)),
            pl.BlockSpec((FFN_HALO, d), lambda i, f: (jnp.maximum(i * hb - 1, 0), 0)),
            pl.BlockSpec((1, d, tf), lambda i, f: (f, 0, 0)),
            pl.BlockSpec((1, d, tf), lambda i, f: (nf + f, 0, 0)),
            pl.BlockSpec((3, tf), lambda i, f: (0, f)),
            pl.BlockSpec((1, tf), lambda i, f: (0, f)),
            pl.BlockSpec((1, tf, d), lambda i, f: (0, f, 0)),
            pl.BlockSpec((tm, d), lambda i, f: (i, 0)),
            pl.BlockSpec((1, 1, d), lambda i, f: ((i * tm) // rows_per_mod, 0, 0)),
            pl.BlockSpec((1, d), lambda i, f: (0, 0)),
        ],
        out_specs=pl.BlockSpec((tm, d), lambda i, f: (i, 0)),
        out_shape=jax.ShapeDtypeStruct((m, d), F32),
        scratch_shapes=[pltpu.VMEM((tm + 2 * FFN_HALO, d), BF16)],
        compiler_params=_cparams(("arbitrary", "arbitrary")),
        name="conv_glu",
    )(h2, h2, h2, w_up, w_up, conv_w, conv_b, w_down, x2, gate, fg)


def _rope_tables(n_tokens):
    rows = n_tokens // GRID_W
    row = jnp.repeat(jnp.arange(rows, dtype=F32), GRID_W)
    col = jnp.tile(jnp.arange(GRID_W, dtype=F32), rows)
    n_freq = HEAD_DIM // 4
    inv_freq = ROPE_THETA ** (-jnp.arange(n_freq, dtype=F32) / n_freq)
    ang = jnp.concatenate([row[:, None] * inv_freq, col[:, None] * inv_freq], axis=-1)
    cos, sin = jnp.cos(ang), jnp.sin(ang)
    return jnp.concatenate([cos, cos], axis=-1), jnp.concatenate([-sin, sin], axis=-1)


def kernel(x, c, ctx, c_ctx, ada_w, ada_b, norm1_g, w_in, q_norm_g, k_norm_g, ret_log_decay, ret_norm_g,
           gla_gate_up, gla_gate_b, gla_norm_g, w_out, norm2_g, w_up, conv_w, conv_b, w_down, final_norm_g):
    batch, lat_len, d = x.shape
    ctx_len = ctx.shape[1]
    depth = ada_w.shape[0]
    mod_rows = 16
    cc = jnp.concatenate([c, c_ctx[None], jnp.zeros((mod_rows - batch - 1, d), F32)], axis=0)
    mod = _ada(cc, ada_w, ada_b).reshape(depth, mod_rows, N_MOD, d)

    cos_l, sin_l = _rope_tables(lat_len)
    proj_tm = PROJ_TM
    cos_c = jnp.ones((proj_tm, LANES), F32)
    sin_c = jnp.zeros((proj_tm, LANES), F32)

    xl = x.reshape(batch * lat_len, d)
    xc = ctx.reshape(batch * ctx_len, d)
    row = lambda v: v.reshape(1, -1)
    w_in_b = w_in.astype(BF16)
    w_out_b = w_out[0].astype(BF16)[None]

    for l in range(depth):
        last = l == depth - 1
        ml = [mod[l, :batch, k].reshape(batch, 1, d) for k in range(N_MOD)]
        mc = [mod[l, batch, k].reshape(1, 1, d) for k in range(N_MOD)]
        up_heads, gate_b_heads = _gla_gate_params(gla_gate_up[l], gla_gate_b[l])

        zl, vtl = _proj(xl, ml[0], ml[1], row(norm1_g[l]), w_in_b, l, cos_l, sin_l, row(q_norm_g[l]), row(k_norm_g[l]),
                        tm=proj_tm, rows_per_mod=lat_len, rope_tiles=lat_len // proj_tm)
        zc, vtc = _proj(xc, mc[0], mc[1], row(norm1_g[l]), w_in_b, l, cos_c, sin_c, row(q_norm_g[l]), row(k_norm_g[l]),
                        tm=proj_tm, rows_per_mod=batch * ctx_len, rope_tiles=1)

        casts = [(w_up, l, FFN_TF), (w_down, l, None)]
        if not last:
            casts += [(w_out, l + 1, None)]
        att_l, cast = _attn(vtl, [(zc, vtc, ctx_len), (zl, vtl, lat_len)], batch=batch, q_len=lat_len, tq=ATT_TQ,
                            casts=casts)
        w_up_b, w_down_b = cast[0], cast[1][None]
        ret_c, ret_l = _ret(zc, zl, ret_log_decay[l], row(ret_norm_g[l]),
                            batch=batch, ctx_len=ctx_len, lat_len=lat_len)
        gla_c, gla_l = _gla(zc, zl, up_heads, gate_b_heads, row(gla_norm_g[l]),
                            batch=batch, ctx_len=ctx_len, lat_len=lat_len)

        xl, h2 = _outproj(att_l, ret_l, gla_l, w_out_b, 0, xl, ml[2], ml[3], ml[4], row(norm2_g[l]),
                          tm=OUTPROJ_TM, rows_per_mod=lat_len)
        xl = _ffn(h2, w_up_b, conv_w[l], row(conv_b[l]), w_down_b, xl, ml[5], row(final_norm_g),
                  tm=FFN_TM, seq_len=lat_len, rows_per_mod=lat_len, final_norm=last)

        if not last:
            att_c, _ = _attn(vtc, [(zc, vtc, ctx_len)], batch=batch, q_len=ctx_len, tq=ctx_len)
            xc, hc2 = _outproj(att_c, ret_c, gla_c, w_out_b, 0, xc, mc[2], mc[3], mc[4], row(norm2_g[l]),
                               tm=OUTPROJ_TM, rows_per_mod=batch * ctx_len)
            xc = _ffn(hc2, w_up_b, conv_w[l], row(conv_b[l]), w_down_b, xc, mc[5], row(final_norm_g),
                      tm=FFN_TM, seq_len=ctx_len, rows_per_mod=batch * ctx_len, final_norm=False)
            w_out_b = cast[2][None]

    return xl.reshape(batch, lat_len, d)
```

```python
import functools

import numpy as np
import jax
import jax.numpy as jnp
from jax import lax
from jax.experimental import pallas as pl
from jax.experimental.pallas import tpu as pltpu

F32 = jnp.float32
BF16 = jnp.bfloat16

HEAD_DIM = 128
GRID_W = 64
ATT_Q_HEADS = 8
ATT_KV_HEADS = 2
GQA = ATT_Q_HEADS // ATT_KV_HEADS
RET_HEADS = 4
RET_DIM = 128
RET_CHUNK = 128
GLA_HEADS = 4
GLA_DK = 64
GLA_DV = 128
GLA_GATE_RANK = 16
GLA_TAU = 16.0
GLA_CHUNK = 64
GLA_LEVELS = 6
D_FF = 5632
ROPE_THETA = 10000.0
N_MOD = 6
EPS = 1e-6

LOG2_E = 1.4426950408889634

LANES = 128
N_IN = 5152
PROJ_TN = 256
N_SLOTS = -(-N_IN // LANES)
SLOT_AQ, SLOT_AK, SLOT_AV = 0, 8, 10
SLOT_RQ, SLOT_RK, SLOT_RV, SLOT_RG = 12, 16, 20, 24
SLOT_GQ, SLOT_GK, SLOT_GV, SLOT_GR, SLOT_GA = 28, 30, 32, 36, 40

VMEM_LIMIT = 52 * 1024 * 1024
ADA_TN = 1024
PROJ_TM = 512
ATT_TQ = 256
OUTPROJ_TM = 256

_NT = (((1,), (1,)), ((), ()))
_TN = (((0,), (0,)), ((), ()))


def _cparams(sem, vmem_limit=VMEM_LIMIT):
    return pltpu.CompilerParams(dimension_semantics=sem, vmem_limit_bytes=vmem_limit)


def _dot(a, b):
    return jnp.dot(a, b, preferred_element_type=F32)


def _dot_nt(a, b):
    return lax.dot_general(a, b, _NT, preferred_element_type=F32)


def _dot_tn(a, b):
    return lax.dot_general(a, b, _TN, preferred_element_type=F32)


def _rows(start, size, align):
    if isinstance(start, int):
        return pl.ds(start, size)
    return pl.ds(pl.multiple_of(start, align), size)


def _rms(t):
    return t * lax.rsqrt(jnp.mean(t * t, axis=-1, keepdims=True) + EPS)


def _silu(t):
    return t * jax.nn.sigmoid(t)


def _ada_kernel(c_ref, w_ref, b_ref, o_ref):
    s = _silu(c_ref[...]).astype(BF16)
    o_ref[0] = _dot(s, w_ref[0].astype(BF16)) + b_ref[0]


def _ada(cc, ada_w, ada_b):
    depth, d, n = ada_w.shape
    rows = cc.shape[0]
    tn = ADA_TN
    return pl.pallas_call(
        _ada_kernel,
        grid=(depth, n // tn),
        in_specs=[
            pl.BlockSpec((rows, d), lambda l, j: (0, 0)),
            pl.BlockSpec((1, d, tn), lambda l, j: (l, 0, j)),
            pl.BlockSpec((1, 1, tn), lambda l, j: (l, 0, j)),
        ],
        out_specs=pl.BlockSpec((1, rows, tn), lambda l, j: (l, 0, j)),
        out_shape=jax.ShapeDtypeStruct((depth, rows, n), F32),
        compiler_params=_cparams(("arbitrary", "arbitrary")),
        name="ada_mod",
    )(cc, ada_w, ada_b.reshape(depth, 1, n))


def _proj_kernel(x_ref, sh_ref, sc_ref, g_ref, w_ref, cos_ref, sin_ref, qg_ref, kg_ref, o_ref, vt_ref):
    y = _rms(x_ref[...]) * g_ref[...]
    h = (y * (1.0 + sc_ref[0]) + sh_ref[0]).astype(BF16)
    cos, sin = cos_ref[...], sin_ref[...]
    qg = qg_ref[...] * (HEAD_DIM ** -0.5 * LOG2_E)
    kg = kg_ref[...]

    def rope(t):
        return t * cos + pltpu.roll(t, HEAD_DIM // 2, 1) * sin

    def transform(slot, t):
        if slot < SLOT_AK:
            return rope(_rms(t) * qg)
        if slot < SLOT_AV:
            return rope(_rms(t) * kg)
        if SLOT_RQ <= slot < SLOT_RK:
            return rope(t)
        if SLOT_RK <= slot < SLOT_RV:
            return rope(t * (RET_DIM ** -0.5))
        if SLOT_GQ <= slot < SLOT_GK:
            return t * (GLA_DK ** -0.5)
        return t

    for c0 in range(0, N_IN, PROJ_TN):
        c1 = min(c0 + PROJ_TN, N_IN)
        z = _dot(h, w_ref[0, :, c0:c1])
        for s0 in range(c0, c1, LANES):
            slot, n = s0 // LANES, min(LANES, c1 - s0)
            t = transform(slot, z[:, s0 - c0:s0 - c0 + n])
            if slot < SLOT_AK:
                vt_ref[slot] = t.T.astype(BF16)
            elif SLOT_AV <= slot < SLOT_RQ:
                vt_ref[ATT_Q_HEADS + slot - SLOT_AV] = t.T.astype(BF16)
            t = t.astype(BF16)
            if n < LANES:
                o_ref[slot] = jnp.zeros(o_ref.shape[1:], BF16)
                o_ref[slot, :, 0:n] = t
            else:
                o_ref[slot] = t


def _proj(x2, shift, scale, g, w, layer, cosf, sinf, qg, kg, *, tm, rows_per_mod, rope_tiles):
    m, d = x2.shape
    return pl.pallas_call(
        _proj_kernel,
        grid=(m // tm,),
        in_specs=[
            pl.BlockSpec((tm, d), lambda i: (i, 0)),
            pl.BlockSpec((1, 1, d), lambda i: ((i * tm) // rows_per_mod, 0, 0)),
            pl.BlockSpec((1, 1, d), lambda i: ((i * tm) // rows_per_mod, 0, 0)),
            pl.BlockSpec((1, d), lambda i: (0, 0)),
            pl.BlockSpec((1, d, N_IN), lambda i: (layer, 0, 0), pipeline_mode=pl.Buffered(1)),
            pl.BlockSpec((tm, LANES), lambda i: (i % rope_tiles, 0)),
            pl.BlockSpec((tm, LANES), lambda i: (i % rope_tiles, 0)),
            pl.BlockSpec((1, LANES), lambda i: (0, 0)),
            pl.BlockSpec((1, LANES), lambda i: (0, 0)),
        ],
        out_specs=[pl.BlockSpec((N_SLOTS, tm, LANES), lambda i: (0, i, 0)),
                   pl.BlockSpec((ATT_Q_HEADS + ATT_KV_HEADS, HEAD_DIM, tm), lambda i: (0, 0, i))],
        out_shape=[jax.ShapeDtypeStruct((N_SLOTS, m, LANES), BF16),
                   jax.ShapeDtypeStruct((ATT_Q_HEADS + ATT_KV_HEADS, HEAD_DIM, m), BF16)],
        compiler_params=_cparams(("arbitrary",)),
        name="norm_proj",
    )(x2, shift, scale, g, w, cosf, sinf, qg, kg)


ATT_KEY_BLOCK = 128


def _attn_kernel(*refs, n_kv, reps):
    n_cast = len(reps)
    q_ref = refs[0]
    kv_refs = refs[1:1 + 2 * n_kv]
    w_refs = refs[1 + 2 * n_kv:1 + 2 * n_kv + n_cast]
    o_ref = refs[1 + 2 * n_kv + n_cast]
    wo_refs = refs[2 + 2 * n_kv + n_cast:]
    step = (pl.program_id(0) * pl.num_programs(1) + pl.program_id(1)) * pl.num_programs(2) + pl.program_id(2)

    def convert(w_ref, wo_ref):
        if len(wo_ref.shape) == 3:
            tn = wo_ref.shape[2]
            for t in range(wo_ref.shape[0]):
                wo_ref[t] = w_ref[0, :, t * tn:(t + 1) * tn].astype(BF16)
        else:
            wo_ref[...] = w_ref[0].astype(BF16)

    for w_ref, wo_ref, rep in zip(w_refs, wo_refs, reps):
        if rep == 1:
            convert(w_ref, wo_ref)
        else:
            pl.when(step % rep == 0)(functools.partial(convert, w_ref, wo_ref))
    m, l, acc = [None] * GQA, [None] * GQA, [None] * GQA
    units = [(s, j0, g) for s in range(n_kv) for j0 in range(0, kv_refs[2 * s].shape[1], ATT_KEY_BLOCK)
             for g in range(GQA)]

    def scores(unit):
        s, j0, g = unit
        return _dot(kv_refs[2 * s][0, j0:j0 + ATT_KEY_BLOCK, :], q_ref[g])

    st_next = scores(units[0])
    for u, (s, j0, g) in enumerate(units):
        st = st_next
        if u + 1 < len(units):
            st_next = scores(units[u + 1])
        vt = kv_refs[2 * s + 1][0, :, j0:j0 + ATT_KEY_BLOCK]
        mb = jnp.max(st, axis=0, keepdims=True)
        if m[g] is None:
            m[g] = mb
            p = jnp.exp2(st - mb)
            l[g] = jnp.sum(p, axis=0, keepdims=True)
            acc[g] = _dot(vt, p.astype(BF16))
        else:
            m_new = jnp.maximum(m[g], mb)
            alpha = jnp.exp2(m[g] - m_new)
            p = jnp.exp2(st - m_new)
            l[g] = alpha * l[g] + jnp.sum(p, axis=0, keepdims=True)
            acc[g] = alpha * acc[g] + _dot(vt, p.astype(BF16))
            m[g] = m_new
    for g in range(GQA):
        o_ref[:, g * HEAD_DIM:(g + 1) * HEAD_DIM] = (acc[g] / l[g]).T.astype(BF16)


def _attn(qt, kv_sources, *, batch, q_len, tq, casts=()):
    nq = q_len // tq
    steps = batch * ATT_KV_HEADS * nq
    in_specs = [pl.BlockSpec((GQA, HEAD_DIM, tq), lambda b, k, i: (k, 0, b * nq + i))]
    args = [qt]
    for z, vt, kv_len in kv_sources:
        in_specs.append(pl.BlockSpec((1, kv_len, HEAD_DIM), lambda b, k, i: (SLOT_AK + k, b, 0)))
        in_specs.append(pl.BlockSpec((1, HEAD_DIM, kv_len), lambda b, k, i: (ATT_Q_HEADS + k, 0, b)))
        args += [z, vt]
    out_specs = [pl.BlockSpec((tq, GQA * HEAD_DIM), lambda b, k, i: (b * nq + i, k))]
    out_shape = [jax.ShapeDtypeStruct((batch * q_len, ATT_Q_HEADS * HEAD_DIM), BF16)]
    step = lambda b, k, i: (b * ATT_KV_HEADS + k) * nq + i
    reps = []
    for w, layer, tile in casts:
        _, rows, cols = w.shape
        rep = next(r for r in (1, 2, 4, 8, 16) if rows % (steps // r) == 0 and rows // (steps // r) % 16 == 0)
        slab = rows // (steps // rep)
        reps.append(rep)
        in_specs.append(pl.BlockSpec((1, slab, cols),
                                     lambda b, k, i, rep=rep, layer=layer: (layer, step(b, k, i) // rep, 0)))
        args.append(w)
        if tile is None:
            out_specs.append(pl.BlockSpec((slab, cols), lambda b, k, i, rep=rep: (step(b, k, i) // rep, 0)))
            out_shape.append(jax.ShapeDtypeStruct((rows, cols), BF16))
        else:
            out_specs.append(pl.BlockSpec((cols // tile, slab, tile),
                                          lambda b, k, i, rep=rep: (0, step(b, k, i) // rep, 0)))
            out_shape.append(jax.ShapeDtypeStruct((cols // tile, rows, tile), BF16))
    outs = pl.pallas_call(
        functools.partial(_attn_kernel, n_kv=len(kv_sources), reps=tuple(reps)),
        grid=(batch, ATT_KV_HEADS, nq),
        in_specs=in_specs,
        out_specs=out_specs,
        out_shape=out_shape,
        compiler_params=_cparams(("arbitrary", "arbitrary", "arbitrary")),
        name="attention",
    )(*args)
    return outs[0], list(outs[1:])


def _ret_kernel(dec_ref, qc_ref, kc_ref, vc_ref, gc_ref, ql_ref, kl_ref, vl_ref, gl_ref, ng_ref,
                oc_ref, ol_ref, kv_ref, st_ref, *, nc_ctx, nc_lat):
    h = pl.program_id(1)
    C = RET_CHUNK
    lg_f, lg_b = dec_ref[0, h], dec_ref[1, h]
    rel = lax.broadcasted_iota(jnp.int32, (C, C), 0) - lax.broadcasted_iota(jnp.int32, (C, C), 1)
    dmat = (jnp.where(rel >= 0, jnp.exp(lg_f * jnp.maximum(rel, 0).astype(F32)), 0.0)
            + jnp.where(rel <= 0, jnp.exp(lg_b * jnp.maximum(-rel, 0).astype(F32)), 0.0))
    row = lax.broadcasted_iota(jnp.int32, (C, 1), 0).astype(F32)
    qd_f, qd_b = jnp.exp(lg_f * (row + 1.0)), jnp.exp(lg_b * (C - row))
    kd_f, kd_b = jnp.exp(lg_f * (C - 1.0 - row)), jnp.exp(lg_b * row)
    cd_f, cd_b = jnp.exp(lg_f * C), jnp.exp(lg_b * C)

    chunks = [(qc_ref, kc_ref, vc_ref, gc_ref, oc_ref, n) for n in range(nc_ctx)]
    chunks += [(ql_ref, kl_ref, vl_ref, gl_ref, ol_ref, n) for n in range(nc_lat)]
    order_b = list(range(nc_ctx - 1, -1, -1)) + list(range(nc_ctx + nc_lat - 1, nc_ctx - 1, -1))

    def decayed(x_ref, n, dec_f, dec_b):
        x = x_ref[0, n * C:(n + 1) * C, :].astype(F32)
        return jnp.concatenate([(x * dec_f).astype(BF16), (x * dec_b).astype(BF16)], axis=1)

    for c, (_, k_ref, v_ref, _, _, n) in enumerate(chunks):
        kv_ref[c] = _dot_tn(decayed(k_ref, n, kd_f, kd_b), v_ref[0, n * C:(n + 1) * C, :])

    s = jnp.zeros((C, RET_DIM), F32)
    for c in range(len(chunks)):
        st_ref[c, 0:C, :] = s.astype(BF16)
        s = cd_f * s + kv_ref[c, 0:C, :]
    s = jnp.zeros((C, RET_DIM), F32)
    for c in order_b:
        st_ref[c, C:2 * C, :] = s.astype(BF16)
        s = cd_b * s + kv_ref[c, C:2 * C, :]

    for c, (q_ref, k_ref, v_ref, g_ref, o_ref, n) in enumerate(chunks):
        sl = slice(n * C, (n + 1) * C)
        v = v_ref[0, sl, :]
        att = _dot_nt(q_ref[0, sl, :], k_ref[0, sl, :]) * dmat
        o = _dot(att.astype(BF16), v) + _dot(decayed(q_ref, n, qd_f, qd_b), st_ref[c])
        y = _rms(o) * ng_ref[...]
        o_ref[sl, :] = (y * _silu(g_ref[0, sl, :].astype(F32))).astype(BF16)


def _ret(zc, zl, decay, ng, *, batch, ctx_len, lat_len):
    def slot(base, n):
        return pl.BlockSpec((1, n, RET_DIM), lambda b, h: (base + h, b, 0))

    n_chunks = (ctx_len + lat_len) // RET_CHUNK
    specs = [pl.BlockSpec(memory_space=pltpu.SMEM)]
    specs += [slot(s, ctx_len) for s in (SLOT_RQ, SLOT_RK, SLOT_RV, SLOT_RG)]
    specs += [slot(s, lat_len) for s in (SLOT_RQ, SLOT_RK, SLOT_RV, SLOT_RG)]
    specs += [pl.BlockSpec((1, RET_DIM), lambda b, h: (0, 0))]
    return pl.pallas_call(
        functools.partial(_ret_kernel, nc_ctx=ctx_len // RET_CHUNK, nc_lat=lat_len // RET_CHUNK),
        grid=(batch, RET_HEADS),
        in_specs=specs,
        out_specs=[pl.BlockSpec((ctx_len, RET_DIM), lambda b, h: (b, h)),
                   pl.BlockSpec((lat_len, RET_DIM), lambda b, h: (b, h))],
        out_shape=[jax.ShapeDtypeStruct((batch * ctx_len, RET_HEADS * RET_DIM), BF16),
                   jax.ShapeDtypeStruct((batch * lat_len, RET_HEADS * RET_DIM), BF16)],
        scratch_shapes=[pltpu.VMEM((n_chunks, 2 * RET_DIM, RET_DIM), F32),
                        pltpu.VMEM((n_chunks, 2 * RET_DIM, RET_DIM), BF16)],
        compiler_params=_cparams(("arbitrary", "arbitrary")),
        name="retention",
    )(decay, zc, zc, zc, zc, zl, zl, zl, zl, ng)


GLA_SLAB = 256
GLA_UNROLL = 4


def _gla_level_table():
    i = np.arange(GLA_CHUNK)[:, None]
    j = np.arange(GLA_CHUNK)[None, :]
    x = np.maximum(i ^ j, 1)
    lvl = np.floor(np.log2(x)).astype(np.int32)
    lvl = np.where(i == j, GLA_LEVELS, np.where(i > j, lvl, -1)).astype(np.int32)
    return np.concatenate([lvl, lvl.T], axis=1)


def _gla_role_table():
    t = np.arange(GLA_SLAB)
    fwd = np.arange(LANES)[None, :] < GLA_DK
    return np.stack([np.where(((t[:, None] >> lev) & 1).astype(bool) == fwd, LOG2_E, -LOG2_E)
                     for lev in range(GLA_LEVELS)]).astype(np.float32)


def _bcast_block_row(x, blk, r):
    rows = x.shape[0]
    if blk >= 8:
        x3 = x.reshape(rows // blk, blk, LANES)
        return jnp.broadcast_to(x3[:, r:r + 1, :], x3.shape).reshape(rows, LANES)
    pos = lax.broadcasted_iota(jnp.int32, x.shape, 0) & (blk - 1)
    out = x
    for p in range(blk):
        if p != r:
            out = jnp.where(pos == p, pltpu.roll(x, (p - r) % rows, 0), out)
    return out


def _gla_kernel(lvl_ref, role_ref, bdm_ref, up_ref, gb_ref, ng_ref,
                qc_ref, kc_ref, vc_ref, gc_ref, ac_ref, ql_ref, kl_ref, vl_ref, gl_ref, al_ref,
                oc_ref, ol_ref, *scratch, ctx_len, lat_len):
    C = GLA_CHUNK
    streams = ((qc_ref, kc_ref, vc_ref, ac_ref, 0, ctx_len), (ql_ref, kl_ref, vl_ref, al_ref, ctx_len, lat_len))
    nc_ctx, nc_lat = ctx_len // C, lat_len // C
    n_chunks = nc_ctx + nc_lat
    cpos = lax.broadcasted_iota(jnp.int32, (GLA_SLAB, LANES), 0) & (C - 1)
    fwd = lax.broadcasted_iota(jnp.int32, (GLA_SLAB, LANES), 1) < GLA_DK
    lvl = lvl_ref[...]

    passes = []
    for hh in range(2):
        own = fwd if hh == 0 else jnp.logical_not(fwd)
        xs_ref, qd_ref, xst_ref, qh_ref, kh_ref, tot_ref, vv_ref, kv_ref, st_ref = (r.at[hh] for r in scratch)

        def prep(i, between, q_ref, k_ref, v_ref, a_ref, off, hh=hh, own=own, xs_ref=xs_ref, qd_ref=qd_ref,
                 xst_ref=xst_ref, qh_ref=qh_ref, kh_ref=kh_ref, tot_ref=tot_ref, vv_ref=vv_ref):
            between = list(between)
            src = _rows(i * GLA_SLAB, GLA_SLAB, GLA_SLAB)
            dst = _rows(off + i * GLA_SLAB, GLA_SLAB, C)
            chunk0 = off // C + i * (GLA_SLAB // C)

            def store_keys(lev, x):
                y = jnp.concatenate([x[c * C:(c + 1) * C] for c in range(GLA_SLAB // C) for _ in range(2)], axis=0)
                yt = y.T.astype(BF16) * bdm_ref[...]
                for c in range(GLA_SLAB // C):
                    xst_ref[lev, chunk0 + c] = yt[:, c * 2 * C:(c + 1) * 2 * C]

            logit = _dot(a_ref[0, src, :], up_ref[hh]) + gb_ref[hh]
            la = (jnp.minimum(logit, 0.0) - jnp.log(1.0 + jnp.exp(-jnp.abs(logit)))) * (1.0 / GLA_TAU)
            pre = la
            for step in range(GLA_LEVELS):
                sh = 1 << step
                pre = pre + jnp.where(cpos >= sh, pltpu.roll(pre, sh, 0), 0.0)
            tot = _bcast_block_row(pre, C, C - 1)
            cum = jnp.where(fwd, pre, tot - pre + la)
            q2 = q_ref[0, src, :].astype(F32)
            k2 = k_ref[0, src, :].astype(F32)
            q = jnp.where(own, q2, pltpu.roll(q2, GLA_DK, 1))
            k = jnp.where(own, k2, pltpu.roll(k2, GLA_DK, 1))
            qd_ref[dst, :] = q.astype(BF16)
            store_keys(GLA_LEVELS, k)
            qh_ref[dst, :] = (q * jnp.exp(cum)).astype(BF16)
            kh_ref[dst, :] = (k * jnp.exp(tot - cum)).astype(BF16)
            tot_ref[dst, :] = tot
            vv_ref[dst, :] = v_ref[hh, src, :]
            for lev in range(GLA_LEVELS):
                if between and lev % 2 == 0:
                    between.pop(0)()
                s = 1 << lev
                d = pre - _bcast_block_row(pre, 2 * s, s - 1)
                dist = jnp.where(fwd, d, la - d)
                role = role_ref[lev]
                x = jnp.where(role > 0, q, k) * jnp.exp2(dist * role)
                xs_ref[lev, dst, :] = x.astype(BF16)
                store_keys(lev, x)
            for item in between:
                item()

        def states(kv_ref=kv_ref, st_ref=st_ref, vv_ref=vv_ref, kh_ref=kh_ref, tot_ref=tot_ref):
            def kv_body(c, carry):
                rows = _rows(c * C, C, C)
                kv_ref[c] = _dot_tn(vv_ref[rows, :], kh_ref[rows, :])
                return carry

            lax.fori_loop(0, n_chunks, kv_body, 0, unroll=GLA_UNROLL)

            def step(c, s, lanes):
                st_ref[c, :, lanes] = s[:, lanes].astype(BF16)
                return s * jnp.exp(tot_ref[pl.ds(c * C, 1), :]) + kv_ref[c]

            def body(n, carry):
                s_f, s_b = carry
                c_b = jnp.where(n < nc_ctx, nc_ctx - 1 - n, n_chunks + nc_ctx - 1 - n)
                return step(n, s_f, slice(0, GLA_DK)), step(c_b, s_b, slice(GLA_DK, LANES))

            zero = jnp.zeros((GLA_DV, LANES), F32)
            lax.fori_loop(0, n_chunks, body, (zero, zero))

        def out_chunk(c, g_ref, o_ref, local, hh=hh, xs_ref=xs_ref, qd_ref=qd_ref, xst_ref=xst_ref, qh_ref=qh_ref,
                      vv_ref=vv_ref, st_ref=st_ref):
            rows = _rows(c * C, C, C)
            a = jnp.where(lvl == GLA_LEVELS, _dot(qd_ref[rows, :], xst_ref[GLA_LEVELS, c]), 0.0)
            for lev in range(GLA_LEVELS):
                a = jnp.where(lvl == lev, _dot(xs_ref[lev, rows, :], xst_ref[lev, c]), a)
            v = vv_ref[rows, :]
            o = _dot(a.astype(BF16), jnp.concatenate([v, v], axis=0)) + _dot_nt(qh_ref[rows, :], st_ref[c])
            y = _rms(o) * ng_ref[...]
            out_rows = _rows(local * C, C, C)
            y = y * _silu(g_ref[hh, out_rows, :].astype(F32))
            o_ref[out_rows, hh * GLA_DV:(hh + 1) * GLA_DV] = y.astype(BF16)

        passes.append((prep, states, out_chunk))

    per_slab = GLA_SLAB // C
    (prep0, states0, out0), (prep1, states1, out1) = passes

    def slab(prep, out, stream, i, g_ref, o_ref):
        q_ref, k_ref, v_ref, a_ref, off, _ = stream
        chunks = []
        if out is not None:
            chunks = [functools.partial(out, off // C + i * per_slab + j, g_ref, o_ref, i * per_slab + j)
                      for j in range(per_slab)]
        if prep is not None:
            prep(i, chunks, q_ref=q_ref, k_ref=k_ref, v_ref=v_ref, a_ref=a_ref, off=off)
        else:
            for item in chunks:
                item()

    def run(prep, out):
        for stream, g_ref, o_ref in ((streams[0], gc_ref, oc_ref), (streams[1], gl_ref, ol_ref)):
            n_slabs = stream[5] // GLA_SLAB
            if n_slabs == 1:
                slab(prep, out, stream, 0, g_ref, o_ref)
            else:
                def body(i, carry, stream=stream, g_ref=g_ref, o_ref=o_ref):
                    slab(prep, out, stream, i, g_ref, o_ref)
                    return carry
                lax.fori_loop(0, n_slabs, body, 0, unroll=2 if n_slabs % 2 == 0 else 1)

    run(prep0, None)
    states0()
    run(prep1, out0)
    states1()
    run(None, out1)


def _gla_gate_params(gate_up, gate_b):
    r = GLA_GATE_RANK
    gu = gate_up.astype(BF16).reshape(2, r, GLA_HEADS, GLA_DK).transpose(2, 0, 1, 3)
    up = jnp.zeros((GLA_HEADS, LANES, 2 * GLA_DK), BF16)
    up = up.at[:, 0:r, 0:GLA_DK].set(gu[:, 0]).at[:, r:2 * r, GLA_DK:].set(gu[:, 1])
    gb = gate_b.reshape(2, GLA_HEADS, GLA_DK).transpose(1, 0, 2).reshape(GLA_HEADS, 1, 2 * GLA_DK)
    return up, gb


def _gla(zc, zl, up_heads, gate_b_heads, ng, *, batch, ctx_len, lat_len):
    total = ctx_len + lat_len
    n_chunks = total // GLA_CHUNK

    def pair(base, n):
        return pl.BlockSpec((1, n, LANES), lambda b, p: (base + p, b, 0))

    def two(base, n):
        return pl.BlockSpec((2, n, LANES), lambda b, p: (base // 2 + p, b, 0))

    def one(n):
        return pl.BlockSpec((1, n, LANES), lambda b, p: (SLOT_GA, b, 0))

    role = _gla_role_table()
    feat_fwd = np.arange(LANES)[:, None] < GLA_DK
    key_fwd = (np.arange(2 * GLA_SLAB)[None, :] % (2 * GLA_CHUNK)) < GLA_CHUNK
    bd_mask = (feat_fwd == key_fwd).astype(np.float32)
    specs = [
        pl.BlockSpec((GLA_CHUNK, 2 * GLA_CHUNK), lambda b, p: (0, 0)),
        pl.BlockSpec(role.shape, lambda b, p: (0, 0, 0)),
        pl.BlockSpec(bd_mask.shape, lambda b, p: (0, 0)),
        pl.BlockSpec((2, LANES, LANES), lambda b, p: (p, 0, 0)),
        pl.BlockSpec((2, 1, LANES), lambda b, p: (p, 0, 0)),
        pl.BlockSpec((1, GLA_DV), lambda b, p: (0, 0)),
    ]
    for n in (ctx_len, lat_len):
        specs += [pair(SLOT_GQ, n), pair(SLOT_GK, n), two(SLOT_GV, n), two(SLOT_GR, n), one(n)]
    return pl.pallas_call(
        functools.partial(_gla_kernel, ctx_len=ctx_len, lat_len=lat_len),
        grid=(batch, GLA_HEADS // 2),
        in_specs=specs,
        out_specs=[pl.BlockSpec((ctx_len, 2 * GLA_DV), lambda b, p: (b, p)),
                   pl.BlockSpec((lat_len, 2 * GLA_DV), lambda b, p: (b, p))],
        out_shape=[jax.ShapeDtypeStruct((batch * ctx_len, GLA_HEADS * GLA_DV), BF16),
                   jax.ShapeDtypeStruct((batch * lat_len, GLA_HEADS * GLA_DV), BF16)],
        scratch_shapes=[
            pltpu.VMEM((2, GLA_LEVELS, total, LANES), BF16),
            pltpu.VMEM((2, total, LANES), BF16),
            pltpu.VMEM((2, GLA_LEVELS + 1, n_chunks, LANES, 2 * GLA_CHUNK), BF16),
            pltpu.VMEM((2, total, LANES), BF16),
            pltpu.VMEM((2, total, LANES), BF16),
            pltpu.VMEM((2, total, LANES), F32),
            pltpu.VMEM((2, total, GLA_DV), BF16),
            pltpu.VMEM((2, n_chunks, GLA_DV, LANES), F32),
            pltpu.VMEM((2, n_chunks, GLA_DV, LANES), BF16),
        ],
        compiler_params=_cparams(("arbitrary", "arbitrary")),
        name="gla",
    )(jnp.asarray(_gla_level_table()), jnp.asarray(role), jnp.asarray(bd_mask, BF16), up_heads, gate_b_heads, ng,
      zc, zc, zc, zc, zc, zl, zl, zl, zl, zl)


def _outproj_kernel(att_ref, ret_ref, gla_ref, w_ref, x_ref, gate_ref, sh_ref, sc_ref, g_ref, xo_ref, ho_ref):
    na, nr = att_ref.shape[1], ret_ref.shape[1]
    y = (_dot(att_ref[...], w_ref[0, 0:na, :]) + _dot(ret_ref[...], w_ref[0, na:na + nr, :])
         + _dot(gla_ref[...], w_ref[0, na + nr:, :]))
    xn = x_ref[...] + gate_ref[0] * y
    xo_ref[...] = xn
    ho_ref[...] = (_rms(xn) * g_ref[...] * (1.0 + sc_ref[0]) + sh_ref[0]).astype(BF16)


def _outproj(att, ret, gla, w, layer, x2, gate, shift, scale, g, *, tm, rows_per_mod):
    m, d = x2.shape

    def rows(n):
        return pl.BlockSpec((tm, n), lambda i: (i, 0))

    def mod():
        return pl.BlockSpec((1, 1, d), lambda i: ((i * tm) // rows_per_mod, 0, 0))

    return pl.pallas_call(
        _outproj_kernel,
        grid=(m // tm,),
        in_specs=[rows(att.shape[1]), rows(ret.shape[1]), rows(gla.shape[1]),
                  pl.BlockSpec((1,) + w.shape[1:], lambda i: (layer, 0, 0), pipeline_mode=pl.Buffered(1)),
                  rows(d), mod(), mod(), mod(),
                  pl.BlockSpec((1, d), lambda i: (0, 0))],
        out_specs=[rows(d), rows(d)],
        out_shape=[jax.ShapeDtypeStruct((m, d), F32), jax.ShapeDtypeStruct((m, d), BF16)],
        compiler_params=_cparams(("arbitrary",)),
        name="out_proj",
    )(att, ret, gla, w, x2, gate, shift, scale, g)


FFN_HALO = 16
FFN_TF = 512
FFN_TM = 512


def _ffn_kernel(h_ref, hn_ref, hp_ref, wa_ref, wv_ref, cw_ref, cb_ref, wd_ref, x_ref, gate_ref, fg_ref,
                o_ref, hs_ref, *, tm, seq_len, final_norm):
    i = pl.program_id(0)
    f = pl.program_id(1)
    ext = tm + 2 * FFN_HALO

    @pl.when(f == 0)
    def _():
        hs_ref[0:tm, :] = h_ref[...]
        hs_ref[tm:tm + FFN_HALO, :] = hn_ref[...]
        hs_ref[tm + FFN_HALO:ext, :] = hp_ref[...]
        o_ref[...] = jnp.zeros_like(o_ref)

    a = _dot(hs_ref[...], wa_ref[0])
    pos = (i * tm + lax.broadcasted_iota(jnp.int32, (ext, 1), 0)) % seq_len
    prev = jnp.where(pos == 0, 0.0, pltpu.roll(a, 1, 0))
    nxt = jnp.where(pos == seq_len - 1, 0.0, pltpu.roll(a, ext - 1, 0))
    conv = prev * cw_ref[0:1, :] + a * cw_ref[1:2, :] + nxt * cw_ref[2:3, :] + cb_ref[...]
    v = _dot(h_ref[...], wv_ref[0])
    u = (_silu(conv[0:tm]) * v).astype(BF16)
    o_ref[...] += _dot(u, wd_ref[0])

    @pl.when(f == pl.num_programs(1) - 1)
    def _():
        xn = x_ref[...] + gate_ref[0] * o_ref[...]
        if final_norm:
            xn = _rms(xn) * fg_ref[...]
        o_ref[...] = xn


def _ffn(h2, w_up, conv_w, conv_b, w_down, x2, gate, fg, *, tm, seq_len, rows_per_mod, final_norm):
    m, d = x2.shape
    tf = w_up.shape[2]
    nf = D_FF // tf
    hb = tm // FFN_HALO
    last = m // FFN_HALO - 1
    return pl.pallas_call(
        functools.partial(_ffn_kernel, tm=tm, seq_len=seq_len, final_norm=final_norm),
        grid=(m // tm, nf),
        in_specs=[
            pl.BlockSpec((tm, d), lambda i, f: (i, 0)),
            pl.BlockSpec((FFN_HALO, d), lambda i, f: (jnp.minimum((i + 1) * hb, last), 0)),
            pl.BlockSpec((FFN_HALO, d), lambda i, f: (jnp.maximum(i * hb - 1, 0), 0)),
            pl.BlockSpec((1, d, tf), lambda i, f: (f, 0, 0)),
            pl.BlockSpec((1, d, tf), lambda i, f: (nf + f, 0, 0)),
            pl.BlockSpec((3, tf), lambda i, f: (0, f)),
            pl.BlockSpec((1, tf), lambda i, f: (0, f)),
            pl.BlockSpec((1, tf, d), lambda i, f: (0, f, 0)),
            pl.BlockSpec((tm, d), lambda i, f: (i, 0)),
            pl.BlockSpec((1, 1, d), lambda i, f: ((i * tm) // rows_per_mod, 0, 0)),
            pl.BlockSpec((1, d), lambda i, f: (0, 0)),
        ],
        out_specs=pl.BlockSpec((tm, d), lambda i, f: (i, 0)),
        out_shape=jax.ShapeDtypeStruct((m, d), F32),
        scratch_shapes=[pltpu.VMEM((tm + 2 * FFN_HALO, d), BF16)],
        compiler_params=_cparams(("arbitrary", "arbitrary")),
        name="conv_glu",
    )(h2, h2, h2, w_up, w_up, conv_w, conv_b, w_down, x2, gate, fg)


def _rope_tables(n_tokens):
    rows = n_tokens // GRID_W
    row = jnp.repeat(jnp.arange(rows, dtype=F32), GRID_W)
    col = jnp.tile(jnp.arange(GRID_W, dtype=F32), rows)
    n_freq = HEAD_DIM // 4
    inv_freq = ROPE_THETA ** (-jnp.arange(n_freq, dtype=F32) / n_freq)
    ang = jnp.concatenate([row[:, None] * inv_freq, col[:, None] * inv_freq], axis=-1)
    cos, sin = jnp.cos(ang), jnp.sin(ang)
    return jnp.concatenate([cos, cos], axis=-1), jnp.concatenate([-sin, sin], axis=-1)


def kernel(x, c, ctx, c_ctx, ada_w, ada_b, norm1_g, w_in, q_norm_g, k_norm_g, ret_log_decay, ret_norm_g,
           gla_gate_up, gla_gate_b, gla_norm_g, w_out, norm2_g, w_up, conv_w, conv_b, w_down, final_norm_g):
    batch, lat_len, d = x.shape
    ctx_len = ctx.shape[1]
    depth = ada_w.shape[0]
    mod_rows = 16
    cc = jnp.concatenate([c, c_ctx[None], jnp.zeros((mod_rows - batch - 1, d), F32)], axis=0)
    mod = _ada(cc, ada_w, ada_b).reshape(depth, mod_rows, N_MOD, d)

    cos_l, sin_l = _rope_tables(lat_len)
    proj_tm = PROJ_TM
    cos_c = jnp.ones((proj_tm, LANES), F32)
    sin_c = jnp.zeros((proj_tm, LANES), F32)

    xl = x.reshape(batch * lat_len, d)
    xc = ctx.reshape(batch * ctx_len, d)
    row = lambda v: v.reshape(1, -1)
    w_in_b = w_in.astype(BF16)
    w_out_b = w_out[0].astype(BF16)[None]

    for l in range(depth):
        last = l == depth - 1
        ml = [mod[l, :batch, k].reshape(batch, 1, d) for k in range(N_MOD)]
        mc = [mod[l, batch, k].reshape(1, 1, d) for k in range(N_MOD)]
        up_heads, gate_b_heads = _gla_gate_params(gla_gate_up[l], gla_gate_b[l])

        zl, vtl = _proj(xl, ml[0], ml[1], row(norm1_g[l]), w_in_b, l, cos_l, sin_l, row(q_norm_g[l]), row(k_norm_g[l]),
                        tm=proj_tm, rows_per_mod=lat_len, rope_tiles=lat_len // proj_tm)
        zc, vtc = _proj(xc, mc[0], mc[1], row(norm1_g[l]), w_in_b, l, cos_c, sin_c, row(q_norm_g[l]), row(k_norm_g[l]),
                        tm=proj_tm, rows_per_mod=batch * ctx_len, rope_tiles=1)

        casts = [(w_up, l, FFN_TF), (w_down, l, None)]
        if not last:
            casts += [(w_out, l + 1, None)]
        att_l, cast = _attn(vtl, [(zc, vtc, ctx_len), (zl, vtl, lat_len)], batch=batch, q_len=lat_len, tq=ATT_TQ,
                            casts=casts)
        w_up_b, w_down_b = cast[0], cast[1][None]
        ret_c, ret_l = _ret(zc, zl, ret_log_decay[l], row(ret_norm_g[l]),
                            batch=batch, ctx_len=ctx_len, lat_len=lat_len)
        gla_c, gla_l = _gla(zc, zl, up_heads, gate_b_heads, row(gla_norm_g[l]),
                            batch=batch, ctx_len=ctx_len, lat_len=lat_len)

        xl, h2 = _outproj(att_l, ret_l, gla_l, w_out_b, 0, xl, ml[2], ml[3], ml[4], row(norm2_g[l]),
                          tm=OUTPROJ_TM, rows_per_mod=lat_len)
        xl = _ffn(h2, w_up_b, conv_w[l], row(conv_b[l]), w_down_b, xl, ml[5], row(final_norm_g),
                  tm=FFN_TM, seq_len=lat_len, rows_per_mod=lat_len, final_norm=last)

        if not last:
            att_c, _ = _attn(vtc, [(zc, vtc, ctx_len)], batch=batch, q_len=ctx_len, tq=ctx_len)
            xc, hc2 = _outproj(att_c, ret_c, gla_c, w_out_b, 0, xc, mc[2], mc[3], mc[4], row(norm2_g[l]),
                               tm=OUTPROJ_TM, rows_per_mod=batch * ctx_len)
            xc = _ffn(hc2, w_up_b, conv_w[l], row(conv_b[l]), w_down_b, xc, mc[5], row(final_norm_g),
                      tm=FFN_TM, seq_len=ctx_len, rows_per_mod=batch * ctx_len, final_norm=False)
            w_out_b = cast[2][None]

    return xl.reshape(batch, lat_len, d)
```

```python
import functools

import numpy as np
import jax
import jax.numpy as jnp
from jax import lax
from jax.experimental import pallas as pl
from jax.experimental.pallas import tpu as pltpu

F32 = jnp.float32
BF16 = jnp.bfloat16

HEAD_DIM = 128
GRID_W = 64
ATT_Q_HEADS = 8
ATT_KV_HEADS = 2
GQA = ATT_Q_HEADS // ATT_KV_HEADS
RET_HEADS = 4
RET_DIM = 128
RET_CHUNK = 128
GLA_HEADS = 4
GLA_DK = 64
GLA_DV = 128
GLA_GATE_RANK = 16
GLA_TAU = 16.0
GLA_CHUNK = 64
GLA_LEVELS = 6
D_FF = 5632
ROPE_THETA = 10000.0
N_MOD = 6
EPS = 1e-6

LOG2_E = 1.4426950408889634

LANES = 128
N_IN = 5152
PROJ_TN = 256
N_SLOTS = -(-N_IN // LANES)
SLOT_AQ, SLOT_AK, SLOT_AV = 0, 8, 10
SLOT_RQ, SLOT_RK, SLOT_RV, SLOT_RG = 12, 16, 20, 24
SLOT_GQ, SLOT_GK, SLOT_GV, SLOT_GR, SLOT_GA = 28, 30, 32, 36, 40

VMEM_LIMIT = 52 * 1024 * 1024
ADA_TN = 1024
PROJ_TM = 512
ATT_TQ = 256
OUTPROJ_TM = 256

_NT = (((1,), (1,)), ((), ()))
_TN = (((0,), (0,)), ((), ()))


def _cparams(sem, vmem_limit=VMEM_LIMIT):
    return pltpu.CompilerParams(dimension_semantics=sem, vmem_limit_bytes=vmem_limit)


def _dot(a, b):
    return jnp.dot(a, b, preferred_element_type=F32)


def _dot_nt(a, b):
    return lax.dot_general(a, b, _NT, preferred_element_type=F32)


def _dot_tn(a, b):
    return lax.dot_general(a, b, _TN, preferred_element_type=F32)


def _rows(start, size, align):
    if isinstance(start, int):
        return pl.ds(start, size)
    return pl.ds(pl.multiple_of(start, align), size)


def _rms(t):
    return t * lax.rsqrt(jnp.mean(t * t, axis=-1, keepdims=True) + EPS)


def _silu(t):
    return t * jax.nn.sigmoid(t)


def _ada_kernel(c_ref, w_ref, b_ref, o_ref):
    s = _silu(c_ref[...]).astype(BF16)
    o_ref[0] = _dot(s, w_ref[0].astype(BF16)) + b_ref[0]


def _ada(cc, ada_w, ada_b):
    depth, d, n = ada_w.shape
    rows = cc.shape[0]
    tn = ADA_TN
    return pl.pallas_call(
        _ada_kernel,
        grid=(depth, n // tn),
        in_specs=[
            pl.BlockSpec((rows, d), lambda l, j: (0, 0)),
            pl.BlockSpec((1, d, tn), lambda l, j: (l, 0, j)),
            pl.BlockSpec((1, 1, tn), lambda l, j: (l, 0, j)),
        ],
        out_specs=pl.BlockSpec((1, rows, tn), lambda l, j: (l, 0, j)),
        out_shape=jax.ShapeDtypeStruct((depth, rows, n), F32),
        compiler_params=_cparams(("arbitrary", "arbitrary")),
        name="ada_mod",
    )(cc, ada_w, ada_b.reshape(depth, 1, n))


def _proj_kernel(x_ref, sh_ref, sc_ref, g_ref, w_ref, cos_ref, sin_ref, qg_ref, kg_ref, o_ref, vt_ref):
    y = _rms(x_ref[...]) * g_ref[...]
    h = (y * (1.0 + sc_ref[0]) + sh_ref[0]).astype(BF16)
    cos, sin = cos_ref[...], sin_ref[...]
    qg = qg_ref[...] * (HEAD_DIM ** -0.5 * LOG2_E)
    kg = kg_ref[...]

    def rope(t):
        return t * cos + pltpu.roll(t, HEAD_DIM // 2, 1) * sin

    def transform(slot, t):
        if slot < SLOT_AK:
            return rope(_rms(t) * qg)
        if slot < SLOT_AV:
            return rope(_rms(t) * kg)
        if SLOT_RQ <= slot < SLOT_RK:
            return rope(t)
        if SLOT_RK <= slot < SLOT_RV:
            return rope(t * (RET_DIM ** -0.5))
        if SLOT_GQ <= slot < SLOT_GK:
            return t * (GLA_DK ** -0.5)
        return t

    for c0 in range(0, N_IN, PROJ_TN):
        c1 = min(c0 + PROJ_TN, N_IN)
        z = _dot(h, w_ref[0, :, c0:c1])
        for s0 in range(c0, c1, LANES):
            slot, n = s0 // LANES, min(LANES, c1 - s0)
            t = transform(slot, z[:, s0 - c0:s0 - c0 + n])
            if slot < SLOT_AK:
                vt_ref[slot] = t.T.astype(BF16)
            elif SLOT_AV <= slot < SLOT_RQ:
                vt_ref[ATT_Q_HEADS + slot - SLOT_AV] = t.T.astype(BF16)
            t = t.astype(BF16)
            if n < LANES:
                o_ref[slot] = jnp.zeros(o_ref.shape[1:], BF16)
                o_ref[slot, :, 0:n] = t
            else:
                o_ref[slot] = t


def _proj(x2, shift, scale, g, w, layer, cosf, sinf, qg, kg, *, tm, rows_per_mod, rope_tiles):
    m, d = x2.shape
    return pl.pallas_call(
        _proj_kernel,
        grid=(m // tm,),
        in_specs=[
            pl.BlockSpec((tm, d), lambda i: (i, 0)),
            pl.BlockSpec((1, 1, d), lambda i: ((i * tm) // rows_per_mod, 0, 0)),
            pl.BlockSpec((1, 1, d), lambda i: ((i * tm) // rows_per_mod, 0, 0)),
            pl.BlockSpec((1, d), lambda i: (0, 0)),
            pl.BlockSpec((1, d, N_IN), lambda i: (layer, 0, 0), pipeline_mode=pl.Buffered(1)),
            pl.BlockSpec((tm, LANES), lambda i: (i % rope_tiles, 0)),
            pl.BlockSpec((tm, LANES), lambda i: (i % rope_tiles, 0)),
            pl.BlockSpec((1, LANES), lambda i: (0, 0)),
            pl.BlockSpec((1, LANES), lambda i: (0, 0)),
        ],
        out_specs=[pl.BlockSpec((N_SLOTS, tm, LANES), lambda i: (0, i, 0)),
                   pl.BlockSpec((ATT_Q_HEADS + ATT_KV_HEADS, HEAD_DIM, tm), lambda i: (0, 0, i))],
        out_shape=[jax.ShapeDtypeStruct((N_SLOTS, m, LANES), BF16),
                   jax.ShapeDtypeStruct((ATT_Q_HEADS + ATT_KV_HEADS, HEAD_DIM, m), BF16)],
        compiler_params=_cparams(("arbitrary",)),
        name="norm_proj",
    )(x2, shift, scale, g, w, cosf, sinf, qg, kg)


ATT_KEY_BLOCK = 128


def _attn_kernel(*refs, n_kv, reps):
    n_cast = len(reps)
    q_ref = refs[0]
    kv_refs = refs[1:1 + 2 * n_kv]
    w_refs = refs[1 + 2 * n_kv:1 + 2 * n_kv + n_cast]
    o_ref = refs[1 + 2 * n_kv + n_cast]
    wo_refs = refs[2 + 2 * n_kv + n_cast:]
    step = (pl.program_id(0) * pl.num_programs(1) + pl.program_id(1)) * pl.num_programs(2) + pl.program_id(2)

    def convert(w_ref, wo_ref):
        if len(wo_ref.shape) == 3:
            tn = wo_ref.shape[2]
            for t in range(wo_ref.shape[0]):
                wo_ref[t] = w_ref[0, :, t * tn:(t + 1) * tn].astype(BF16)
        else:
            wo_ref[...] = w_ref[0].astype(BF16)

    for w_ref, wo_ref, rep in zip(w_refs, wo_refs, reps):
        if rep == 1:
            convert(w_ref, wo_ref)
        else:
            pl.when(step % rep == 0)(functools.partial(convert, w_ref, wo_ref))
    m, l, acc = [None] * GQA, [None] * GQA, [None] * GQA
    units = [(s, j0, g) for s in range(n_kv) for j0 in range(0, kv_refs[2 * s].shape[1], ATT_KEY_BLOCK)
             for g in range(GQA)]

    def scores(unit):
        s, j0, g = unit
        return _dot(kv_refs[2 * s][0, j0:j0 + ATT_KEY_BLOCK, :], q_ref[g])

    st_next = scores(units[0])
    for u, (s, j0, g) in enumerate(units):
        st = st_next
        if u + 1 < len(units):
            st_next = scores(units[u + 1])
        vt = kv_refs[2 * s + 1][0, :, j0:j0 + ATT_KEY_BLOCK]
        mb = jnp.max(st, axis=0, keepdims=True)
        if m[g] is None:
            m[g] = mb
            p = jnp.exp2(st - mb)
            l[g] = jnp.sum(p, axis=0, keepdims=True)
            acc[g] = _dot(vt, p.astype(BF16))
        else:
            m_new = jnp.maximum(m[g], mb)
            alpha = jnp.exp2(m[g] - m_new)
            p = jnp.exp2(st - m_new)
            l[g] = alpha * l[g] + jnp.sum(p, axis=0, keepdims=True)
            acc[g] = alpha * acc[g] + _dot(vt, p.astype(BF16))
            m[g] = m_new
    for g in range(GQA):
        o_ref[:, g * HEAD_DIM:(g + 1) * HEAD_DIM] = (acc[g] / l[g]).T.astype(BF16)


def _attn(qt, kv_sources, *, batch, q_len, tq, casts=()):
    nq = q_len // tq
    steps = batch * ATT_KV_HEADS * nq
    in_specs = [pl.BlockSpec((GQA, HEAD_DIM, tq), lambda b, k, i: (k, 0, b * nq + i))]
    args = [qt]
    for z, vt, kv_len in kv_sources:
        in_specs.append(pl.BlockSpec((1, kv_len, HEAD_DIM), lambda b, k, i: (SLOT_AK + k, b, 0)))
        in_specs.append(pl.BlockSpec((1, HEAD_DIM, kv_len), lambda b, k, i: (ATT_Q_HEADS + k, 0, b)))
        args += [z, vt]
    out_specs = [pl.BlockSpec((tq, GQA * HEAD_DIM), lambda b, k, i: (b * nq + i, k))]
    out_shape = [jax.ShapeDtypeStruct((batch * q_len, ATT_Q_HEADS * HEAD_DIM), BF16)]
    step = lambda b, k, i: (b * ATT_KV_HEADS + k) * nq + i
    reps = []
    for w, layer, tile in casts:
        _, rows, cols = w.shape
        rep = next(r for r in (1, 2, 4, 8, 16) if rows % (steps // r) == 0 and rows // (steps // r) % 16 == 0)
        slab = rows // (steps // rep)
        reps.append(rep)
        in_specs.append(pl.BlockSpec((1, slab, cols),
                                     lambda b, k, i, rep=rep, layer=layer: (layer, step(b, k, i) // rep, 0)))
        args.append(w)
        if tile is None:
            out_specs.append(pl.BlockSpec((slab, cols), lambda b, k, i, rep=rep: (step(b, k, i) // rep, 0)))
            out_shape.append(jax.ShapeDtypeStruct((rows, cols), BF16))
        else:
            out_specs.append(pl.BlockSpec((cols // tile, slab, tile),
                                          lambda b, k, i, rep=rep: (0, step(b, k, i) // rep, 0)))
            out_shape.append(jax.ShapeDtypeStruct((cols // tile, rows, tile), BF16))
    outs = pl.pallas_call(
        functools.partial(_attn_kernel, n_kv=len(kv_sources), reps=tuple(reps)),
        grid=(batch, ATT_KV_HEADS, nq),
        in_specs=in_specs,
        out_specs=out_specs,
        out_shape=out_shape,
        compiler_params=_cparams(("arbitrary", "arbitrary", "arbitrary")),
        name="attention",
    )(*args)
    return outs[0], list(outs[1:])


def _ret_kernel(dec_ref, qc_ref, kc_ref, vc_ref, gc_ref, ql_ref, kl_ref, vl_ref, gl_ref, ng_ref,
                oc_ref, ol_ref, kv_ref, st_ref, *, nc_ctx, nc_lat):
    h = pl.program_id(1)
    C = RET_CHUNK
    lg_f, lg_b = dec_ref[0, h], dec_ref[1, h]
    rel = lax.broadcasted_iota(jnp.int32, (C, C), 0) - lax.broadcasted_iota(jnp.int32, (C, C), 1)
    dmat = (jnp.where(rel >= 0, jnp.exp(lg_f * jnp.maximum(rel, 0).astype(F32)), 0.0)
            + jnp.where(rel <= 0, jnp.exp(lg_b * jnp.maximum(-rel, 0).astype(F32)), 0.0))
    row = lax.broadcasted_iota(jnp.int32, (C, 1), 0).astype(F32)
    qd_f, qd_b = jnp.exp(lg_f * (row + 1.0)), jnp.exp(lg_b * (C - row))
    kd_f, kd_b = jnp.exp(lg_f * (C - 1.0 - row)), jnp.exp(lg_b * row)
    cd_f, cd_b = jnp.exp(lg_f * C), jnp.exp(lg_b * C)

    chunks = [(qc_ref, kc_ref, vc_ref, gc_ref, oc_ref, n) for n in range(nc_ctx)]
    chunks += [(ql_ref, kl_ref, vl_ref, gl_ref, ol_ref, n) for n in range(nc_lat)]
    order_b = list(range(nc_ctx - 1, -1, -1)) + list(range(nc_ctx + nc_lat - 1, nc_ctx - 1, -1))

    def decayed(x_ref, n, dec_f, dec_b):
        x = x_ref[0, n * C:(n + 1) * C, :].astype(F32)
        return jnp.concatenate([(x * dec_f).astype(BF16), (x * dec_b).astype(BF16)], axis=1)

    for c, (_, k_ref, v_ref, _, _, n) in enumerate(chunks):
        kv_ref[c] = _dot_tn(decayed(k_ref, n, kd_f, kd_b), v_ref[0, n * C:(n + 1) * C, :])

    s = jnp.zeros((C, RET_DIM), F32)
    for c in range(len(chunks)):
        st_ref[c, 0:C, :] = s.astype(BF16)
        s = cd_f * s + kv_ref[c, 0:C, :]
    s = jnp.zeros((C, RET_DIM), F32)
    for c in order_b:
        st_ref[c, C:2 * C, :] = s.astype(BF16)
        s = cd_b * s + kv_ref[c, C:2 * C, :]

    for c, (q_ref, k_ref, v_ref, g_ref, o_ref, n) in enumerate(chunks):
        sl = slice(n * C, (n + 1) * C)
        v = v_ref[0, sl, :]
        att = _dot_nt(q_ref[0, sl, :], k_ref[0, sl, :]) * dmat
        o = _dot(att.astype(BF16), v) + _dot(decayed(q_ref, n, qd_f, qd_b), st_ref[c])
        y = _rms(o) * ng_ref[...]
        o_ref[sl, :] = (y * _silu(g_ref[0, sl, :].astype(F32))).astype(BF16)


def _ret(zc, zl, decay, ng, *, batch, ctx_len, lat_len):
    def slot(base, n):
        return pl.BlockSpec((1, n, RET_DIM), lambda b, h: (base + h, b, 0))

    n_chunks = (ctx_len + lat_len) // RET_CHUNK
    specs = [pl.BlockSpec(memory_space=pltpu.SMEM)]
    specs += [slot(s, ctx_len) for s in (SLOT_RQ, SLOT_RK, SLOT_RV, SLOT_RG)]
    specs += [slot(s, lat_len) for s in (SLOT_RQ, SLOT_RK, SLOT_RV, SLOT_RG)]
    specs += [pl.BlockSpec((1, RET_DIM), lambda b, h: (0, 0))]
    return pl.pallas_call(
        functools.partial(_ret_kernel, nc_ctx=ctx_len // RET_CHUNK, nc_lat=lat_len // RET_CHUNK),
        grid=(batch, RET_HEADS),
        in_specs=specs,
        out_specs=[pl.BlockSpec((ctx_len, RET_DIM), lambda b, h: (b, h)),
                   pl.BlockSpec((lat_len, RET_DIM), lambda b, h: (b, h))],
        out_shape=[jax.ShapeDtypeStruct((batch * ctx_len, RET_HEADS * RET_DIM), BF16),
                   jax.ShapeDtypeStruct((batch * lat_len, RET_HEADS * RET_DIM), BF16)],
        scratch_shapes=[pltpu.VMEM((n_chunks, 2 * RET_DIM, RET_DIM), F32),
                        pltpu.VMEM((n_chunks, 2 * RET_DIM, RET_DIM), BF16)],
        compiler_params=_cparams(("arbitrary", "arbitrary")),
        name="retention",
    )(decay, zc, zc, zc, zc, zl, zl, zl, zl, ng)


GLA_SLAB = 256


def _gla_level_table():
    i = np.arange(GLA_CHUNK)[:, None]
    j = np.arange(GLA_CHUNK)[None, :]
    x = np.maximum(i ^ j, 1)
    lvl = np.floor(np.log2(x)).astype(np.int32)
    lvl = np.where(i == j, GLA_LEVELS, np.where(i > j, lvl, -1)).astype(np.int32)
    return np.concatenate([lvl, lvl.T], axis=1)


def _gla_role_table():
    t = np.arange(GLA_SLAB)
    fwd = np.arange(LANES)[None, :] < GLA_DK
    return np.stack([np.where(((t[:, None] >> lev) & 1).astype(bool) == fwd, LOG2_E, -LOG2_E)
                     for lev in range(GLA_LEVELS)]).astype(np.float32)


def _bcast_block_row(x, blk, r):
    rows = x.shape[0]
    if blk >= 8:
        x3 = x.reshape(rows // blk, blk, LANES)
        return jnp.broadcast_to(x3[:, r:r + 1, :], x3.shape).reshape(rows, LANES)
    pos = lax.broadcasted_iota(jnp.int32, x.shape, 0) & (blk - 1)
    out = x
    for p in range(blk):
        if p != r:
            out = jnp.where(pos == p, pltpu.roll(x, (p - r) % rows, 0), out)
    return out


def _gla_kernel(lvl_ref, role_ref, bdm_ref, up_ref, gb_ref, ng_ref,
                qc_ref, kc_ref, vc_ref, gc_ref, ac_ref, ql_ref, kl_ref, vl_ref, gl_ref, al_ref,
                oc_ref, ol_ref, *scratch, ctx_len, lat_len):
    C = GLA_CHUNK
    streams = ((qc_ref, kc_ref, vc_ref, ac_ref, 0, ctx_len), (ql_ref, kl_ref, vl_ref, al_ref, ctx_len, lat_len))
    nc_ctx, nc_lat = ctx_len // C, lat_len // C
    n_chunks = nc_ctx + nc_lat
    cpos = lax.broadcasted_iota(jnp.int32, (GLA_SLAB, LANES), 0) & (C - 1)
    fwd = lax.broadcasted_iota(jnp.int32, (GLA_SLAB, LANES), 1) < GLA_DK
    lvl = lvl_ref[...]

    passes = []
    for hh in range(2):
        own = fwd if hh == 0 else jnp.logical_not(fwd)
        xs_ref, qd_ref, xst_ref, qh_ref, tot_ref, vv_ref, kv_ref, st_ref = (r.at[hh] for r in scratch)

        def prep(i, between, q_ref, k_ref, v_ref, a_ref, off, hh=hh, own=own, xs_ref=xs_ref, qd_ref=qd_ref,
                 xst_ref=xst_ref, qh_ref=qh_ref, tot_ref=tot_ref, vv_ref=vv_ref, kv_ref=kv_ref):
            between = list(between)
            src = _rows(i * GLA_SLAB, GLA_SLAB, GLA_SLAB)
            dst = _rows(off + i * GLA_SLAB, GLA_SLAB, C)
            chunk0 = off // C + i * (GLA_SLAB // C)

            def store_keys(lev, x):
                y = jnp.concatenate([x[c * C:(c + 1) * C] for c in range(GLA_SLAB // C) for _ in range(2)], axis=0)
                yt = y.T.astype(BF16) * bdm_ref[...]
                for c in range(GLA_SLAB // C):
                    xst_ref[lev, chunk0 + c] = yt[:, c * 2 * C:(c + 1) * 2 * C]

            logit = _dot(a_ref[0, src, :], up_ref[hh]) + gb_ref[hh]
            la = (jnp.minimum(logit, 0.0) - jnp.log(1.0 + jnp.exp(-jnp.abs(logit)))) * (1.0 / GLA_TAU)
            pre = la
            for step in range(GLA_LEVELS):
                sh = 1 << step
                pre = pre + jnp.where(cpos >= sh, pltpu.roll(pre, sh, 0), 0.0)
            tot = _bcast_block_row(pre, C, C - 1)
            cum = jnp.where(fwd, pre, tot - pre + la)
            q2 = q_ref[0, src, :].astype(F32)
            k2 = k_ref[0, src, :].astype(F32)
            q = jnp.where(own, q2, pltpu.roll(q2, GLA_DK, 1))
            k = jnp.where(own, k2, pltpu.roll(k2, GLA_DK, 1))
            qd_ref[dst, :] = q.astype(BF16)
            store_keys(GLA_LEVELS, k)
            qh_ref[dst, :] = (q * jnp.exp(cum)).astype(BF16)
            kh = (k * jnp.exp(tot - cum)).astype(BF16)
            tot_ref[dst, :] = tot
            v = v_ref[hh, src, :]
            vv_ref[dst, :] = v
            for c in range(GLA_SLAB // C):
                kv_ref[chunk0 + c] = _dot_tn(v[c * C:(c + 1) * C], kh[c * C:(c + 1) * C])
            for lev in range(GLA_LEVELS):
                if between and lev % 2 == 0:
                    between.pop(0)()
                s = 1 << lev
                d = pre - _bcast_block_row(pre, 2 * s, s - 1)
                dist = jnp.where(fwd, d, la - d)
                role = role_ref[lev]
                x = jnp.where(role > 0, q, k) * jnp.exp2(dist * role)
                xs_ref[lev, dst, :] = x.astype(BF16)
                store_keys(lev, x)
            for item in between:
                item()

        def states(kv_ref=kv_ref, st_ref=st_ref, tot_ref=tot_ref):
            def step(c, s, lanes):
                st_ref[c, :, lanes] = s[:, lanes].astype(BF16)
                return s * jnp.exp(tot_ref[pl.ds(c * C, 1), :]) + kv_ref[c]

            def body(n, carry):
                s_f, s_b = carry
                c_b = jnp.where(n < nc_ctx, nc_ctx - 1 - n, n_chunks + nc_ctx - 1 - n)
                return step(n, s_f, slice(0, GLA_DK)), step(c_b, s_b, slice(GLA_DK, LANES))

            zero = jnp.zeros((GLA_DV, LANES), F32)
            lax.fori_loop(0, n_chunks, body, (zero, zero))

        def out_chunk(c, g_ref, o_ref, local, hh=hh, xs_ref=xs_ref, qd_ref=qd_ref, xst_ref=xst_ref, qh_ref=qh_ref,
                      vv_ref=vv_ref, st_ref=st_ref):
            rows = _rows(c * C, C, C)
            a = jnp.where(lvl == GLA_LEVELS, _dot(qd_ref[rows, :], xst_ref[GLA_LEVELS, c]), 0.0)
            for lev in range(GLA_LEVELS):
                a = jnp.where(lvl == lev, _dot(xs_ref[lev, rows, :], xst_ref[lev, c]), a)
            v = vv_ref[rows, :]
            o = _dot(a.astype(BF16), jnp.concatenate([v, v], axis=0)) + _dot_nt(qh_ref[rows, :], st_ref[c])
            y = _rms(o) * ng_ref[...]
            out_rows = _rows(local * C, C, C)
            y = y * _silu(g_ref[hh, out_rows, :].astype(F32))
            o_ref[out_rows, hh * GLA_DV:(hh + 1) * GLA_DV] = y.astype(BF16)

        passes.append((prep, states, out_chunk))

    per_slab = GLA_SLAB // C
    (prep0, states0, out0), (prep1, states1, out1) = passes

    def slab(prep, out, stream, i, g_ref, o_ref):
        q_ref, k_ref, v_ref, a_ref, off, _ = stream
        chunks = []
        if out is not None:
            chunks = [functools.partial(out, off // C + i * per_slab + j, g_ref, o_ref, i * per_slab + j)
                      for j in range(per_slab)]
        if prep is not None:
            prep(i, chunks, q_ref=q_ref, k_ref=k_ref, v_ref=v_ref, a_ref=a_ref, off=off)
        else:
            for item in chunks:
                item()

    def run(prep, out):
        for stream, g_ref, o_ref in ((streams[0], gc_ref, oc_ref), (streams[1], gl_ref, ol_ref)):
            n_slabs = stream[5] // GLA_SLAB
            if n_slabs == 1:
                slab(prep, out, stream, 0, g_ref, o_ref)
            else:
                def body(i, carry, stream=stream, g_ref=g_ref, o_ref=o_ref):
                    slab(prep, out, stream, i, g_ref, o_ref)
                    return carry
                lax.fori_loop(0, n_slabs, body, 0, unroll=2 if n_slabs % 2 == 0 else 1)

    run(prep0, None)
    states0()
    run(prep1, out0)
    states1()
    run(None, out1)


def _gla_gate_params(gate_up, gate_b):
    r = GLA_GATE_RANK
    gu = gate_up.astype(BF16).reshape(2, r, GLA_HEADS, GLA_DK).transpose(2, 0, 1, 3)
    up = jnp.zeros((GLA_HEADS, LANES, 2 * GLA_DK), BF16)
    up = up.at[:, 0:r, 0:GLA_DK].set(gu[:, 0]).at[:, r:2 * r, GLA_DK:].set(gu[:, 1])
    gb = gate_b.reshape(2, GLA_HEADS, GLA_DK).transpose(1, 0, 2).reshape(GLA_HEADS, 1, 2 * GLA_DK)
    return up, gb


def _gla(zc, zl, up_heads, gate_b_heads, ng, *, batch, ctx_len, lat_len):
    total = ctx_len + lat_len
    n_chunks = total // GLA_CHUNK

    def pair(base, n):
        return pl.BlockSpec((1, n, LANES), lambda b, p: (base + p, b, 0))

    def two(base, n):
        return pl.BlockSpec((2, n, LANES), lambda b, p: (base // 2 + p, b, 0))

    def one(n):
        return pl.BlockSpec((1, n, LANES), lambda b, p: (SLOT_GA, b, 0))

    role = _gla_role_table()
    feat_fwd = np.arange(LANES)[:, None] < GLA_DK
    key_fwd = (np.arange(2 * GLA_SLAB)[None, :] % (2 * GLA_CHUNK)) < GLA_CHUNK
    bd_mask = (feat_fwd == key_fwd).astype(np.float32)
    specs = [
        pl.BlockSpec((GLA_CHUNK, 2 * GLA_CHUNK), lambda b, p: (0, 0)),
        pl.BlockSpec(role.shape, lambda b, p: (0, 0, 0)),
        pl.BlockSpec(bd_mask.shape, lambda b, p: (0, 0)),
        pl.BlockSpec((2, LANES, LANES), lambda b, p: (p, 0, 0)),
        pl.BlockSpec((2, 1, LANES), lambda b, p: (p, 0, 0)),
        pl.BlockSpec((1, GLA_DV), lambda b, p: (0, 0)),
    ]
    for n in (ctx_len, lat_len):
        specs += [pair(SLOT_GQ, n), pair(SLOT_GK, n), two(SLOT_GV, n), two(SLOT_GR, n), one(n)]
    return pl.pallas_call(
        functools.partial(_gla_kernel, ctx_len=ctx_len, lat_len=lat_len),
        grid=(batch, GLA_HEADS // 2),
        in_specs=specs,
        out_specs=[pl.BlockSpec((ctx_len, 2 * GLA_DV), lambda b, p: (b, p)),
                   pl.BlockSpec((lat_len, 2 * GLA_DV), lambda b, p: (b, p))],
        out_shape=[jax.ShapeDtypeStruct((batch * ctx_len, GLA_HEADS * GLA_DV), BF16),
                   jax.ShapeDtypeStruct((batch * lat_len, GLA_HEADS * GLA_DV), BF16)],
        scratch_shapes=[
            pltpu.VMEM((2, GLA_LEVELS, total, LANES), BF16),
            pltpu.VMEM((2, total, LANES), BF16),
            pltpu.VMEM((2, GLA_LEVELS + 1, n_chunks, LANES, 2 * GLA_CHUNK), BF16),
            pltpu.VMEM((2, total, LANES), BF16),
            pltpu.VMEM((2, total, LANES), F32),
            pltpu.VMEM((2, total, GLA_DV), BF16),
            pltpu.VMEM((2, n_chunks, GLA_DV, LANES), F32),
            pltpu.VMEM((2, n_chunks, GLA_DV, LANES), BF16),
        ],
        compiler_params=_cparams(("arbitrary", "arbitrary")),
        name="gla",
    )(jnp.asarray(_gla_level_table()), jnp.asarray(role), jnp.asarray(bd_mask, BF16), up_heads, gate_b_heads, ng,
      zc, zc, zc, zc, zc, zl, zl, zl, zl, zl)


def _outproj_kernel(att_ref, ret_ref, gla_ref, w_ref, x_ref, gate_ref, sh_ref, sc_ref, g_ref, xo_ref, ho_ref):
    na, nr = att_ref.shape[1], ret_ref.shape[1]
    y = (_dot(att_ref[...], w_ref[0, 0:na, :]) + _dot(ret_ref[...], w_ref[0, na:na + nr, :])
         + _dot(gla_ref[...], w_ref[0, na + nr:, :]))
    xn = x_ref[...] + gate_ref[0] * y
    xo_ref[...] = xn
    ho_ref[...] = (_rms(xn) * g_ref[...] * (1.0 + sc_ref[0]) + sh_ref[0]).astype(BF16)


def _outproj(att, ret, gla, w, layer, x2, gate, shift, scale, g, *, tm, rows_per_mod):
    m, d = x2.shape

    def rows(n):
        return pl.BlockSpec((tm, n), lambda i: (i, 0))

    def mod():
        return pl.BlockSpec((1, 1, d), lambda i: ((i * tm) // rows_per_mod, 0, 0))

    return pl.pallas_call(
        _outproj_kernel,
        grid=(m // tm,),
        in_specs=[rows(att.shape[1]), rows(ret.shape[1]), rows(gla.shape[1]),
                  pl.BlockSpec((1,) + w.shape[1:], lambda i: (layer, 0, 0), pipeline_mode=pl.Buffered(1)),
                  rows(d), mod(), mod(), mod(),
                  pl.BlockSpec((1, d), lambda i: (0, 0))],
        out_specs=[rows(d), rows(d)],
        out_shape=[jax.ShapeDtypeStruct((m, d), F32), jax.ShapeDtypeStruct((m, d), BF16)],
        compiler_params=_cparams(("arbitrary",)),
        name="out_proj",
    )(att, ret, gla, w, x2, gate, shift, scale, g)


FFN_HALO = 16
FFN_TF = 512
FFN_TM = 512


def _ffn_kernel(h_ref, hn_ref, hp_ref, wa_ref, wv_ref, cw_ref, cb_ref, wd_ref, x_ref, gate_ref, fg_ref,
                o_ref, hs_ref, *, tm, seq_len, final_norm):
    i = pl.program_id(0)
    f = pl.program_id(1)
    ext = tm + 2 * FFN_HALO

    @pl.when(f == 0)
    def _():
        hs_ref[0:tm, :] = h_ref[...]
        hs_ref[tm:tm + FFN_HALO, :] = hn_ref[...]
        hs_ref[tm + FFN_HALO:ext, :] = hp_ref[...]
        o_ref[...] = jnp.zeros_like(o_ref)

    a = _dot(hs_ref[...], wa_ref[0])
    pos = (i * tm + lax.broadcasted_iota(jnp.int32, (ext, 1), 0)) % seq_len
    prev = jnp.where(pos == 0, 0.0, pltpu.roll(a, 1, 0))
    nxt = jnp.where(pos == seq_len - 1, 0.0, pltpu.roll(a, ext - 1, 0))
    conv = prev * cw_ref[0:1, :] + a * cw_ref[1:2, :] + nxt * cw_ref[2:3, :] + cb_ref[...]
    v = _dot(h_ref[...], wv_ref[0])
    u = (_silu(conv[0:tm]) * v).astype(BF16)
    o_ref[...] += _dot(u, wd_ref[0])

    @pl.when(f == pl.num_programs(1) - 1)
    def _():
        xn = x_ref[...] + gate_ref[0] * o_ref[...]
        if final_norm:
            xn = _rms(xn) * fg_ref[...]
        o_ref[...] = xn


def _ffn(h2, w_up, conv_w, conv_b, w_down, x2, gate, fg, *, tm, seq_len, rows_per_mod, final_norm):
    m, d = x2.shape
    tf = w_up.shape[2]
    nf = D_FF // tf
    hb = tm // FFN_HALO
    last = m // FFN_HALO - 1
    return pl.pallas_call(
        functools.partial(_ffn_kernel, tm=tm, seq_len=seq_len, final_norm=final_norm),
        grid=(m // tm, nf),
        in_specs=[
            pl.BlockSpec((tm, d), lambda i, f: (i, 0)),
            pl.BlockSpec((FFN_HALO, d), lambda i, f: (jnp.minimum((i + 1) * hb, last), 0)),
            pl.BlockSpec((FFN_HALO, d), lambda i, f: (jnp.maximum(i * hb - 1, 0), 0)),
            pl.BlockSpec((1, d, tf), lambda i, f: (f, 0, 0)),
            pl.BlockSpec((1, d, tf), lambda i, f: (nf + f, 0, 0)),
            pl.BlockSpec((3, tf), lambda i, f: (0, f)),
            pl.BlockSpec((1, tf), lambda i, f: (0, f)),
            pl.BlockSpec((1, tf, d), lambda i, f: (0, f, 0)),
            pl.BlockSpec((tm, d), lambda i, f: (i, 0)),
            pl.BlockSpec((1, 1, d), lambda i, f: ((i * tm) // rows_per_mod, 0, 0)),
            pl.BlockSpec((1, d), lambda i, f: (0, 0)),
        ],
        out_specs=pl.BlockSpec((tm, d), lambda i, f: (i, 0)),
        out_shape=jax.ShapeDtypeStruct((m, d), F32),
        scratch_shapes=[pltpu.VMEM((tm + 2 * FFN_HALO, d), BF16)],
        compiler_params=_cparams(("arbitrary", "arbitrary")),
        name="conv_glu",
    )(h2, h2, h2, w_up, w_up, conv_w, conv_b, w_down, x2, gate, fg)


def _rope_tables(n_tokens):
    rows = n_tokens // GRID_W
    row = jnp.repeat(jnp.arange(rows, dtype=F32), GRID_W)
    col = jnp.tile(jnp.arange(GRID_W, dtype=F32), rows)
    n_freq = HEAD_DIM // 4
    inv_freq = ROPE_THETA ** (-jnp.arange(n_freq, dtype=F32) / n_freq)
    ang = jnp.concatenate([row[:, None] * inv_freq, col[:, None] * inv_freq], axis=-1)
    cos, sin = jnp.cos(ang), jnp.sin(ang)
    return jnp.concatenate([cos, cos], axis=-1), jnp.concatenate([-sin, sin], axis=-1)


def kernel(x, c, ctx, c_ctx, ada_w, ada_b, norm1_g, w_in, q_norm_g, k_norm_g, ret_log_decay, ret_norm_g,
           gla_gate_up, gla_gate_b, gla_norm_g, w_out, norm2_g, w_up, conv_w, conv_b, w_down, final_norm_g):
    batch, lat_len, d = x.shape
    ctx_len = ctx.shape[1]
    depth = ada_w.shape[0]
    mod_rows = 16
    cc = jnp.concatenate([c, c_ctx[None], jnp.zeros((mod_rows - batch - 1, d), F32)], axis=0)
    mod = _ada(cc, ada_w, ada_b).reshape(depth, mod_rows, N_MOD, d)

    cos_l, sin_l = _rope_tables(lat_len)
    proj_tm = PROJ_TM
    cos_c = jnp.ones((proj_tm, LANES), F32)
    sin_c = jnp.zeros((proj_tm, LANES), F32)

    xl = x.reshape(batch * lat_len, d)
    xc = ctx.reshape(batch * ctx_len, d)
    row = lambda v: v.reshape(1, -1)
    w_in_b = w_in.astype(BF16)
    w_out_b = w_out[0].astype(BF16)[None]

    for l in range(depth):
        last = l == depth - 1
        ml = [mod[l, :batch, k].reshape(batch, 1, d) for k in range(N_MOD)]
        mc = [mod[l, batch, k].reshape(1, 1, d) for k in range(N_MOD)]
        up_heads, gate_b_heads = _gla_gate_params(gla_gate_up[l], gla_gate_b[l])

        zl, vtl = _proj(xl, ml[0], ml[1], row(norm1_g[l]), w_in_b, l, cos_l, sin_l, row(q_norm_g[l]), row(k_norm_g[l]),
                        tm=proj_tm, rows_per_mod=lat_len, rope_tiles=lat_len // proj_tm)
        zc, vtc = _proj(xc, mc[0], mc[1], row(norm1_g[l]), w_in_b, l, cos_c, sin_c, row(q_norm_g[l]), row(k_norm_g[l]),
                        tm=proj_tm, rows_per_mod=batch * ctx_len, rope_tiles=1)

        casts = [(w_up, l, FFN_TF), (w_down, l, None)]
        if not last:
            casts += [(w_out, l + 1, None)]
        att_l, cast = _attn(vtl, [(zc, vtc, ctx_len), (zl, vtl, lat_len)], batch=batch, q_len=lat_len, tq=ATT_TQ,
                            casts=casts)
        w_up_b, w_down_b = cast[0], cast[1][None]
        ret_c, ret_l = _ret(zc, zl, ret_log_decay[l], row(ret_norm_g[l]),
                            batch=batch, ctx_len=ctx_len, lat_len=lat_len)
        gla_c, gla_l = _gla(zc, zl, up_heads, gate_b_heads, row(gla_norm_g[l]),
                            batch=batch, ctx_len=ctx_len, lat_len=lat_len)

        xl, h2 = _outproj(att_l, ret_l, gla_l, w_out_b, 0, xl, ml[2], ml[3], ml[4], row(norm2_g[l]),
                          tm=OUTPROJ_TM, rows_per_mod=lat_len)
        xl = _ffn(h2, w_up_b, conv_w[l], row(conv_b[l]), w_down_b, xl, ml[5], row(final_norm_g),
                  tm=FFN_TM, seq_len=lat_len, rows_per_mod=lat_len, final_norm=last)

        if not last:
            att_c, _ = _attn(vtc, [(zc, vtc, ctx_len)], batch=batch, q_len=ctx_len, tq=ctx_len)
            xc, hc2 = _outproj(att_c, ret_c, gla_c, w_out_b, 0, xc, mc[2], mc[3], mc[4], row(norm2_g[l]),
                               tm=OUTPROJ_TM, rows_per_mod=batch * ctx_len)
            xc = _ffn(hc2, w_up_b, conv_w[l], row(conv_b[l]), w_down_b, xc, mc[5], row(final_norm_g),
                      tm=FFN_TM, seq_len=ctx_len, rows_per_mod=batch * ctx_len, final_norm=False)
            w_out_b = cast[2][None]

    return xl.reshape(batch, lat_len, d)
```

```python
import functools

import numpy as np
import jax
import jax.numpy as jnp
from jax import lax
from jax.experimental import pallas as pl
from jax.experimental.pallas import tpu as pltpu

F32 = jnp.float32
BF16 = jnp.bfloat16

HEAD_DIM = 128
GRID_W = 64
ATT_Q_HEADS = 8
ATT_KV_HEADS = 2
GQA = ATT_Q_HEADS // ATT_KV_HEADS
RET_HEADS = 4
RET_DIM = 128
RET_CHUNK = 128
GLA_HEADS = 4
GLA_DK = 64
GLA_DV = 128
GLA_GATE_RANK = 16
GLA_TAU = 16.0
GLA_CHUNK = 64
GLA_LEVELS = 6
D_FF = 5632
ROPE_THETA = 10000.0
N_MOD = 6
EPS = 1e-6

LOG2_E = 1.4426950408889634

LANES = 128
N_IN = 5152
PROJ_TN = 256
N_SLOTS = -(-N_IN // LANES)
SLOT_AQ, SLOT_AK, SLOT_AV = 0, 8, 10
SLOT_RQ, SLOT_RK, SLOT_RV, SLOT_RG = 12, 16, 20, 24
SLOT_GQ, SLOT_GK, SLOT_GV, SLOT_GR, SLOT_GA = 28, 30, 32, 36, 40

VMEM_LIMIT = 52 * 1024 * 1024
ADA_TN = 1024
PROJ_TM = 512
ATT_TQ = 256
OUTPROJ_TM = 512

_NT = (((1,), (1,)), ((), ()))
_TN = (((0,), (0,)), ((), ()))


def _cparams(sem, vmem_limit=VMEM_LIMIT):
    return pltpu.CompilerParams(dimension_semantics=sem, vmem_limit_bytes=vmem_limit)


def _dot(a, b):
    return jnp.dot(a, b, preferred_element_type=F32)


def _dot_nt(a, b):
    return lax.dot_general(a, b, _NT, preferred_element_type=F32)


def _dot_tn(a, b):
    return lax.dot_general(a, b, _TN, preferred_element_type=F32)


def _rows(start, size, align):
    if isinstance(start, int):
        return pl.ds(start, size)
    return pl.ds(pl.multiple_of(start, align), size)


def _rms(t):
    return t * lax.rsqrt(jnp.mean(t * t, axis=-1, keepdims=True) + EPS)


def _silu(t):
    return t * jax.nn.sigmoid(t)


def _ada_kernel(c_ref, w_ref, b_ref, o_ref):
    s = _silu(c_ref[...]).astype(BF16)
    o_ref[0] = _dot(s, w_ref[0].astype(BF16)) + b_ref[0]


def _ada(cc, ada_w, ada_b):
    depth, d, n = ada_w.shape
    rows = cc.shape[0]
    tn = ADA_TN
    return pl.pallas_call(
        _ada_kernel,
        grid=(depth, n // tn),
        in_specs=[
            pl.BlockSpec((rows, d), lambda l, j: (0, 0)),
            pl.BlockSpec((1, d, tn), lambda l, j: (l, 0, j)),
            pl.BlockSpec((1, 1, tn), lambda l, j: (l, 0, j)),
        ],
        out_specs=pl.BlockSpec((1, rows, tn), lambda l, j: (l, 0, j)),
        out_shape=jax.ShapeDtypeStruct((depth, rows, n), F32),
        compiler_params=_cparams(("arbitrary", "arbitrary")),
        name="ada_mod",
    )(cc, ada_w, ada_b.reshape(depth, 1, n))


def _proj_kernel(x_ref, sh_ref, sc_ref, g_ref, w_ref, cos_ref, sin_ref, qg_ref, kg_ref, o_ref, vt_ref):
    y = _rms(x_ref[...]) * g_ref[...]
    h = (y * (1.0 + sc_ref[0]) + sh_ref[0]).astype(BF16)
    cos, sin = cos_ref[...], sin_ref[...]
    qg = qg_ref[...] * (HEAD_DIM ** -0.5 * LOG2_E)
    kg = kg_ref[...]

    def rope(t):
        return t * cos + pltpu.roll(t, HEAD_DIM // 2, 1) * sin

    def transform(slot, t):
        if slot < SLOT_AK:
            return rope(_rms(t) * qg)
        if slot < SLOT_AV:
            return rope(_rms(t) * kg)
        if SLOT_RQ <= slot < SLOT_RK:
            return rope(t)
        if SLOT_RK <= slot < SLOT_RV:
            return rope(t * (RET_DIM ** -0.5))
        if SLOT_GQ <= slot < SLOT_GK:
            return t * (GLA_DK ** -0.5)
        return t

    for c0 in range(0, N_IN, PROJ_TN):
        c1 = min(c0 + PROJ_TN, N_IN)
        z = _dot(h, w_ref[0, :, c0:c1])
        for s0 in range(c0, c1, LANES):
            slot, n = s0 // LANES, min(LANES, c1 - s0)
            t = transform(slot, z[:, s0 - c0:s0 - c0 + n])
            if slot < SLOT_AK:
                vt_ref[slot] = t.T.astype(BF16)
            elif SLOT_AV <= slot < SLOT_RQ:
                vt_ref[ATT_Q_HEADS + slot - SLOT_AV] = t.T.astype(BF16)
            t = t.astype(BF16)
            if n < LANES:
                o_ref[slot] = jnp.zeros(o_ref.shape[1:], BF16)
                o_ref[slot, :, 0:n] = t
            else:
                o_ref[slot] = t


def _proj(x2, shift, scale, g, w, layer, cosf, sinf, qg, kg, *, tm, rows_per_mod, rope_tiles):
    m, d = x2.shape
    return pl.pallas_call(
        _proj_kernel,
        grid=(m // tm,),
        in_specs=[
            pl.BlockSpec((tm, d), lambda i: (i, 0)),
            pl.BlockSpec((1, 1, d), lambda i: ((i * tm) // rows_per_mod, 0, 0)),
            pl.BlockSpec((1, 1, d), lambda i: ((i * tm) // rows_per_mod, 0, 0)),
            pl.BlockSpec((1, d), lambda i: (0, 0)),
            pl.BlockSpec((1, d, N_IN), lambda i: (layer, 0, 0), pipeline_mode=pl.Buffered(1)),
            pl.BlockSpec((tm, LANES), lambda i: (i % rope_tiles, 0)),
            pl.BlockSpec((tm, LANES), lambda i: (i % rope_tiles, 0)),
            pl.BlockSpec((1, LANES), lambda i: (0, 0)),
            pl.BlockSpec((1, LANES), lambda i: (0, 0)),
        ],
        out_specs=[pl.BlockSpec((N_SLOTS, tm, LANES), lambda i: (0, i, 0)),
                   pl.BlockSpec((ATT_Q_HEADS + ATT_KV_HEADS, HEAD_DIM, tm), lambda i: (0, 0, i))],
        out_shape=[jax.ShapeDtypeStruct((N_SLOTS, m, LANES), BF16),
                   jax.ShapeDtypeStruct((ATT_Q_HEADS + ATT_KV_HEADS, HEAD_DIM, m), BF16)],
        compiler_params=_cparams(("arbitrary",)),
        name="norm_proj",
    )(x2, shift, scale, g, w, cosf, sinf, qg, kg)


ATT_KEY_BLOCK = 128


def _attn_kernel(*refs, n_kv, reps):
    n_cast = len(reps)
    q_ref = refs[0]
    kv_refs = refs[1:1 + 2 * n_kv]
    w_refs = refs[1 + 2 * n_kv:1 + 2 * n_kv + n_cast]
    o_ref = refs[1 + 2 * n_kv + n_cast]
    wo_refs = refs[2 + 2 * n_kv + n_cast:]
    step = (pl.program_id(0) * pl.num_programs(1) + pl.program_id(1)) * pl.num_programs(2) + pl.program_id(2)

    def convert(w_ref, wo_ref):
        if len(wo_ref.shape) == 3:
            tn = wo_ref.shape[2]
            for t in range(wo_ref.shape[0]):
                wo_ref[t] = w_ref[0, :, t * tn:(t + 1) * tn].astype(BF16)
        else:
            wo_ref[...] = w_ref[0].astype(BF16)

    for w_ref, wo_ref, rep in zip(w_refs, wo_refs, reps):
        if rep == 1:
            convert(w_ref, wo_ref)
        else:
            pl.when(step % rep == 0)(functools.partial(convert, w_ref, wo_ref))
    m, l, acc = [None] * GQA, [None] * GQA, [None] * GQA
    units = [(s, j0, g) for s in range(n_kv) for j0 in range(0, kv_refs[2 * s].shape[1], ATT_KEY_BLOCK)
             for g in range(GQA)]

    def scores(unit):
        s, j0, g = unit
        return _dot(kv_refs[2 * s][0, j0:j0 + ATT_KEY_BLOCK, :], q_ref[g])

    st_next = scores(units[0])
    for u, (s, j0, g) in enumerate(units):
        st = st_next
        if u + 1 < len(units):
            st_next = scores(units[u + 1])
        vt = kv_refs[2 * s + 1][0, :, j0:j0 + ATT_KEY_BLOCK]
        mb = jnp.max(st, axis=0, keepdims=True)
        if m[g] is None:
            m[g] = mb
            p = jnp.exp2(st - mb)
            l[g] = jnp.sum(p, axis=0, keepdims=True)
            acc[g] = _dot(vt, p.astype(BF16))
        else:
            m_new = jnp.maximum(m[g], mb)
            alpha = jnp.exp2(m[g] - m_new)
            p = jnp.exp2(st - m_new)
            l[g] = alpha * l[g] + jnp.sum(p, axis=0, keepdims=True)
            acc[g] = alpha * acc[g] + _dot(vt, p.astype(BF16))
            m[g] = m_new
    for g in range(GQA):
        o_ref[:, g * HEAD_DIM:(g + 1) * HEAD_DIM] = (acc[g] / l[g]).T.astype(BF16)


def _attn(qt, kv_sources, *, batch, q_len, tq, casts=()):
    nq = q_len // tq
    steps = batch * ATT_KV_HEADS * nq
    in_specs = [pl.BlockSpec((GQA, HEAD_DIM, tq), lambda b, k, i: (k, 0, b * nq + i))]
    args = [qt]
    for z, vt, kv_len in kv_sources:
        in_specs.append(pl.BlockSpec((1, kv_len, HEAD_DIM), lambda b, k, i: (SLOT_AK + k, b, 0)))
        in_specs.append(pl.BlockSpec((1, HEAD_DIM, kv_len), lambda b, k, i: (ATT_Q_HEADS + k, 0, b)))
        args += [z, vt]
    out_specs = [pl.BlockSpec((tq, GQA * HEAD_DIM), lambda b, k, i: (b * nq + i, k))]
    out_shape = [jax.ShapeDtypeStruct((batch * q_len, ATT_Q_HEADS * HEAD_DIM), BF16)]
    step = lambda b, k, i: (b * ATT_KV_HEADS + k) * nq + i
    reps = []
    for w, layer, tile in casts:
        _, rows, cols = w.shape
        rep = next(r for r in (1, 2, 4, 8, 16) if rows % (steps // r) == 0 and rows // (steps // r) % 16 == 0)
        slab = rows // (steps // rep)
        reps.append(rep)
        in_specs.append(pl.BlockSpec((1, slab, cols),
                                     lambda b, k, i, rep=rep, layer=layer: (layer, step(b, k, i) // rep, 0)))
        args.append(w)
        if tile is None:
            out_specs.append(pl.BlockSpec((slab, cols), lambda b, k, i, rep=rep: (step(b, k, i) // rep, 0)))
            out_shape.append(jax.ShapeDtypeStruct((rows, cols), BF16))
        else:
            out_specs.append(pl.BlockSpec((cols // tile, slab, tile),
                                          lambda b, k, i, rep=rep: (0, step(b, k, i) // rep, 0)))
            out_shape.append(jax.ShapeDtypeStruct((cols // tile, rows, tile), BF16))
    outs = pl.pallas_call(
        functools.partial(_attn_kernel, n_kv=len(kv_sources), reps=tuple(reps)),
        grid=(batch, ATT_KV_HEADS, nq),
        in_specs=in_specs,
        out_specs=out_specs,
        out_shape=out_shape,
        compiler_params=_cparams(("arbitrary", "arbitrary", "arbitrary")),
        name="attention",
    )(*args)
    return outs[0], list(outs[1:])


def _ret_kernel(dec_ref, qc_ref, kc_ref, vc_ref, gc_ref, ql_ref, kl_ref, vl_ref, gl_ref, ng_ref,
                oc_ref, ol_ref, kv_ref, st_ref, *, nc_ctx, nc_lat):
    h = pl.program_id(1)
    C = RET_CHUNK
    lg_f, lg_b = dec_ref[0, h], dec_ref[1, h]
    rel = lax.broadcasted_iota(jnp.int32, (C, C), 0) - lax.broadcasted_iota(jnp.int32, (C, C), 1)
    dmat = (jnp.where(rel >= 0, jnp.exp(lg_f * jnp.maximum(rel, 0).astype(F32)), 0.0)
            + jnp.where(rel <= 0, jnp.exp(lg_b * jnp.maximum(-rel, 0).astype(F32)), 0.0))
    row = lax.broadcasted_iota(jnp.int32, (C, 1), 0).astype(F32)
    qd_f, qd_b = jnp.exp(lg_f * (row + 1.0)), jnp.exp(lg_b * (C - row))
    kd_f, kd_b = jnp.exp(lg_f * (C - 1.0 - row)), jnp.exp(lg_b * row)
    cd_f, cd_b = jnp.exp(lg_f * C), jnp.exp(lg_b * C)

    chunks = [(qc_ref, kc_ref, vc_ref, gc_ref, oc_ref, n) for n in range(nc_ctx)]
    chunks += [(ql_ref, kl_ref, vl_ref, gl_ref, ol_ref, n) for n in range(nc_lat)]
    order_b = list(range(nc_ctx - 1, -1, -1)) + list(range(nc_ctx + nc_lat - 1, nc_ctx - 1, -1))

    def decayed(x_ref, n, dec_f, dec_b):
        x = x_ref[0, n * C:(n + 1) * C, :].astype(F32)
        return jnp.concatenate([(x * dec_f).astype(BF16), (x * dec_b).astype(BF16)], axis=1)

    for c, (_, k_ref, v_ref, _, _, n) in enumerate(chunks):
        kv_ref[c] = _dot_tn(decayed(k_ref, n, kd_f, kd_b), v_ref[0, n * C:(n + 1) * C, :])

    s = jnp.zeros((C, RET_DIM), F32)
    for c in range(len(chunks)):
        st_ref[c, 0:C, :] = s.astype(BF16)
        s = cd_f * s + kv_ref[c, 0:C, :]
    s = jnp.zeros((C, RET_DIM), F32)
    for c in order_b:
        st_ref[c, C:2 * C, :] = s.astype(BF16)
        s = cd_b * s + kv_ref[c, C:2 * C, :]

    for c, (q_ref, k_ref, v_ref, g_ref, o_ref, n) in enumerate(chunks):
        sl = slice(n * C, (n + 1) * C)
        v = v_ref[0, sl, :]
        att = _dot_nt(q_ref[0, sl, :], k_ref[0, sl, :]) * dmat
        o = _dot(att.astype(BF16), v) + _dot(decayed(q_ref, n, qd_f, qd_b), st_ref[c])
        y = _rms(o) * ng_ref[...]
        o_ref[sl, :] = (y * _silu(g_ref[0, sl, :].astype(F32))).astype(BF16)


def _ret(zc, zl, decay, ng, *, batch, ctx_len, lat_len):
    def slot(base, n):
        return pl.BlockSpec((1, n, RET_DIM), lambda b, h: (base + h, b, 0))

    n_chunks = (ctx_len + lat_len) // RET_CHUNK
    specs = [pl.BlockSpec(memory_space=pltpu.SMEM)]
    specs += [slot(s, ctx_len) for s in (SLOT_RQ, SLOT_RK, SLOT_RV, SLOT_RG)]
    specs += [slot(s, lat_len) for s in (SLOT_RQ, SLOT_RK, SLOT_RV, SLOT_RG)]
    specs += [pl.BlockSpec((1, RET_DIM), lambda b, h: (0, 0))]
    return pl.pallas_call(
        functools.partial(_ret_kernel, nc_ctx=ctx_len // RET_CHUNK, nc_lat=lat_len // RET_CHUNK),
        grid=(batch, RET_HEADS),
        in_specs=specs,
        out_specs=[pl.BlockSpec((ctx_len, RET_DIM), lambda b, h: (b, h)),
                   pl.BlockSpec((lat_len, RET_DIM), lambda b, h: (b, h))],
        out_shape=[jax.ShapeDtypeStruct((batch * ctx_len, RET_HEADS * RET_DIM), BF16),
                   jax.ShapeDtypeStruct((batch * lat_len, RET_HEADS * RET_DIM), BF16)],
        scratch_shapes=[pltpu.VMEM((n_chunks, 2 * RET_DIM, RET_DIM), F32),
                        pltpu.VMEM((n_chunks, 2 * RET_DIM, RET_DIM), BF16)],
        compiler_params=_cparams(("arbitrary", "arbitrary")),
        name="retention",
    )(decay, zc, zc, zc, zc, zl, zl, zl, zl, ng)


GLA_SLAB = 256


def _gla_level_table():
    i = np.arange(GLA_CHUNK)[:, None]
    j = np.arange(GLA_CHUNK)[None, :]
    x = np.maximum(i ^ j, 1)
    lvl = np.floor(np.log2(x)).astype(np.int32)
    lvl = np.where(i == j, GLA_LEVELS, np.where(i > j, lvl, -1)).astype(np.int32)
    return np.concatenate([lvl, lvl.T], axis=1)


def _gla_role_table():
    t = np.arange(GLA_SLAB)
    fwd = np.arange(LANES)[None, :] < GLA_DK
    return np.stack([np.where(((t[:, None] >> lev) & 1).astype(bool) == fwd, LOG2_E, -LOG2_E)
                     for lev in range(GLA_LEVELS)]).astype(np.float32)


def _bcast_block_row(x, blk, r):
    rows = x.shape[0]
    if blk >= 8:
        x3 = x.reshape(rows // blk, blk, LANES)
        return jnp.broadcast_to(x3[:, r:r + 1, :], x3.shape).reshape(rows, LANES)
    pos = lax.broadcasted_iota(jnp.int32, x.shape, 0) & (blk - 1)
    out = x
    for p in range(blk):
        if p != r:
            out = jnp.where(pos == p, pltpu.roll(x, (p - r) % rows, 0), out)
    return out


def _gla_kernel(lvl_ref, role_ref, bdm_ref, up_ref, gb_ref, ng_ref,
                qc_ref, kc_ref, vc_ref, gc_ref, ac_ref, ql_ref, kl_ref, vl_ref, gl_ref, al_ref,
                oc_ref, ol_ref, *scratch, ctx_len, lat_len):
    C = GLA_CHUNK
    streams = ((qc_ref, kc_ref, vc_ref, ac_ref, 0, ctx_len), (ql_ref, kl_ref, vl_ref, al_ref, ctx_len, lat_len))
    nc_ctx, nc_lat = ctx_len // C, lat_len // C
    n_chunks = nc_ctx + nc_lat
    cpos = lax.broadcasted_iota(jnp.int32, (GLA_SLAB, LANES), 0) & (C - 1)
    fwd = lax.broadcasted_iota(jnp.int32, (GLA_SLAB, LANES), 1) < GLA_DK
    lvl = lvl_ref[...]

    passes = []
    for hh in range(2):
        own = fwd if hh == 0 else jnp.logical_not(fwd)
        xs_ref, qd_ref, xst_ref, qh_ref, tot_ref, vv_ref, kv_ref, st_ref = (r.at[hh] for r in scratch)

        def prep(i, between, q_ref, k_ref, v_ref, a_ref, off, hh=hh, own=own, xs_ref=xs_ref, qd_ref=qd_ref,
                 xst_ref=xst_ref, qh_ref=qh_ref, tot_ref=tot_ref, vv_ref=vv_ref, kv_ref=kv_ref):
            between = list(between)
            src = _rows(i * GLA_SLAB, GLA_SLAB, GLA_SLAB)
            dst = _rows(off + i * GLA_SLAB, GLA_SLAB, C)
            chunk0 = off // C + i * (GLA_SLAB // C)

            def store_keys(lev, x):
                y = jnp.concatenate([x[c * C:(c + 1) * C] for c in range(GLA_SLAB // C) for _ in range(2)], axis=0)
                yt = y.T.astype(BF16) * bdm_ref[...]
                for c in range(GLA_SLAB // C):
                    xst_ref[lev, chunk0 + c] = yt[:, c * 2 * C:(c + 1) * 2 * C]

            logit = _dot(a_ref[0, src, :], up_ref[hh]) + gb_ref[hh]
            la = (jnp.minimum(logit, 0.0) - jnp.log(1.0 + jnp.exp(-jnp.abs(logit)))) * (1.0 / GLA_TAU)
            pre = la
            for step in range(GLA_LEVELS):
                sh = 1 << step
                pre = pre + jnp.where(cpos >= sh, pltpu.roll(pre, sh, 0), 0.0)
            tot = _bcast_block_row(pre, C, C - 1)
            cum = jnp.where(fwd, pre, tot - pre + la)
            q2 = q_ref[0, src, :].astype(F32)
            k2 = k_ref[0, src, :].astype(F32)
            q = jnp.where(own, q2, pltpu.roll(q2, GLA_DK, 1))
            k = jnp.where(own, k2, pltpu.roll(k2, GLA_DK, 1))
            qd_ref[dst, :] = q.astype(BF16)
            store_keys(GLA_LEVELS, k)
            qh_ref[dst, :] = (q * jnp.exp(cum)).astype(BF16)
            kh = (k * jnp.exp(tot - cum)).astype(BF16)
            tot_ref[dst, :] = tot
            v = v_ref[hh, src, :]
            vv_ref[dst, :] = v
            for c in range(GLA_SLAB // C):
                kv_ref[chunk0 + c] = _dot_tn(v[c * C:(c + 1) * C], kh[c * C:(c + 1) * C])
            for lev in range(GLA_LEVELS):
                if between and lev % 2 == 0:
                    between.pop(0)()
                s = 1 << lev
                d = pre - _bcast_block_row(pre, 2 * s, s - 1)
                dist = jnp.where(fwd, d, la - d)
                role = role_ref[lev]
                x = jnp.where(role > 0, q, k) * jnp.exp2(dist * role)
                xs_ref[lev, dst, :] = x.astype(BF16)
                store_keys(lev, x)
            for item in between:
                item()

        def states(kv_ref=kv_ref, st_ref=st_ref, tot_ref=tot_ref):
            def step(c, s, lanes):
                st_ref[c, :, lanes] = s[:, lanes].astype(BF16)
                return s * jnp.exp(tot_ref[pl.ds(c * C, 1), :]) + kv_ref[c]

            def body(n, carry):
                s_f, s_b = carry
                c_b = jnp.where(n < nc_ctx, nc_ctx - 1 - n, n_chunks + nc_ctx - 1 - n)
                return step(n, s_f, slice(0, GLA_DK)), step(c_b, s_b, slice(GLA_DK, LANES))

            zero = jnp.zeros((GLA_DV, LANES), F32)
            lax.fori_loop(0, n_chunks, body, (zero, zero))

        def out_chunk(c, g_ref, o_ref, local, hh=hh, xs_ref=xs_ref, qd_ref=qd_ref, xst_ref=xst_ref, qh_ref=qh_ref,
                      vv_ref=vv_ref, st_ref=st_ref):
            rows = _rows(c * C, C, C)
            a = jnp.where(lvl == GLA_LEVELS, _dot(qd_ref[rows, :], xst_ref[GLA_LEVELS, c]), 0.0)
            for lev in range(GLA_LEVELS):
                a = jnp.where(lvl == lev, _dot(xs_ref[lev, rows, :], xst_ref[lev, c]), a)
            v = vv_ref[rows, :]
            o = _dot(a.astype(BF16), jnp.concatenate([v, v], axis=0)) + _dot_nt(qh_ref[rows, :], st_ref[c])
            y = _rms(o) * ng_ref[...]
            out_rows = _rows(local * C, C, C)
            y = y * _silu(g_ref[hh, out_rows, :].astype(F32))
            o_ref[out_rows, hh * GLA_DV:(hh + 1) * GLA_DV] = y.astype(BF16)

        passes.append((prep, states, out_chunk))

    per_slab = GLA_SLAB // C
    (prep0, states0, out0), (prep1, states1, out1) = passes

    def slab(prep, out, stream, i, g_ref, o_ref):
        q_ref, k_ref, v_ref, a_ref, off, _ = stream
        chunks = []
        if out is not None:
            chunks = [functools.partial(out, off // C + i * per_slab + j, g_ref, o_ref, i * per_slab + j)
                      for j in range(per_slab)]
        if prep is not None:
            prep(i, chunks, q_ref=q_ref, k_ref=k_ref, v_ref=v_ref, a_ref=a_ref, off=off)
        else:
            for item in chunks:
                item()

    def run(prep, out):
        for stream, g_ref, o_ref in ((streams[0], gc_ref, oc_ref), (streams[1], gl_ref, ol_ref)):
            n_slabs = stream[5] // GLA_SLAB
            if n_slabs == 1:
                slab(prep, out, stream, 0, g_ref, o_ref)
            else:
                def body(i, carry, stream=stream, g_ref=g_ref, o_ref=o_ref):
                    slab(prep, out, stream, i, g_ref, o_ref)
                    return carry
                lax.fori_loop(0, n_slabs, body, 0, unroll=2 if n_slabs % 2 == 0 else 1)

    run(prep0, None)
    states0()
    run(prep1, out0)
    states1()
    run(None, out1)


def _gla_gate_params(gate_up, gate_b):
    r = GLA_GATE_RANK
    gu = gate_up.astype(BF16).reshape(2, r, GLA_HEADS, GLA_DK).transpose(2, 0, 1, 3)
    up = jnp.zeros((GLA_HEADS, LANES, 2 * GLA_DK), BF16)
    up = up.at[:, 0:r, 0:GLA_DK].set(gu[:, 0]).at[:, r:2 * r, GLA_DK:].set(gu[:, 1])
    gb = gate_b.reshape(2, GLA_HEADS, GLA_DK).transpose(1, 0, 2).reshape(GLA_HEADS, 1, 2 * GLA_DK)
    return up, gb


def _gla(zc, zl, up_heads, gate_b_heads, ng, *, batch, ctx_len, lat_len):
    total = ctx_len + lat_len
    n_chunks = total // GLA_CHUNK

    def pair(base, n):
        return pl.BlockSpec((1, n, LANES), lambda b, p: (base + p, b, 0))

    def two(base, n):
        return pl.BlockSpec((2, n, LANES), lambda b, p: (base // 2 + p, b, 0))

    def one(n):
        return pl.BlockSpec((1, n, LANES), lambda b, p: (SLOT_GA, b, 0))

    role = _gla_role_table()
    feat_fwd = np.arange(LANES)[:, None] < GLA_DK
    key_fwd = (np.arange(2 * GLA_SLAB)[None, :] % (2 * GLA_CHUNK)) < GLA_CHUNK
    bd_mask = (feat_fwd == key_fwd).astype(np.float32)
    specs = [
        pl.BlockSpec((GLA_CHUNK, 2 * GLA_CHUNK), lambda b, p: (0, 0)),
        pl.BlockSpec(role.shape, lambda b, p: (0, 0, 0)),
        pl.BlockSpec(bd_mask.shape, lambda b, p: (0, 0)),
        pl.BlockSpec((2, LANES, LANES), lambda b, p: (p, 0, 0)),
        pl.BlockSpec((2, 1, LANES), lambda b, p: (p, 0, 0)),
        pl.BlockSpec((1, GLA_DV), lambda b, p: (0, 0)),
    ]
    for n in (ctx_len, lat_len):
        specs += [pair(SLOT_GQ, n), pair(SLOT_GK, n), two(SLOT_GV, n), two(SLOT_GR, n), one(n)]
    return pl.pallas_call(
        functools.partial(_gla_kernel, ctx_len=ctx_len, lat_len=lat_len),
        grid=(batch, GLA_HEADS // 2),
        in_specs=specs,
        out_specs=[pl.BlockSpec((ctx_len, 2 * GLA_DV), lambda b, p: (b, p)),
                   pl.BlockSpec((lat_len, 2 * GLA_DV), lambda b, p: (b, p))],
        out_shape=[jax.ShapeDtypeStruct((batch * ctx_len, GLA_HEADS * GLA_DV), BF16),
                   jax.ShapeDtypeStruct((batch * lat_len, GLA_HEADS * GLA_DV), BF16)],
        scratch_shapes=[
            pltpu.VMEM((2, GLA_LEVELS, total, LANES), BF16),
            pltpu.VMEM((2, total, LANES), BF16),
            pltpu.VMEM((2, GLA_LEVELS + 1, n_chunks, LANES, 2 * GLA_CHUNK), BF16),
            pltpu.VMEM((2, total, LANES), BF16),
            pltpu.VMEM((2, total, LANES), F32),
            pltpu.VMEM((2, total, GLA_DV), BF16),
            pltpu.VMEM((2, n_chunks, GLA_DV, LANES), F32),
            pltpu.VMEM((2, n_chunks, GLA_DV, LANES), BF16),
        ],
        compiler_params=_cparams(("arbitrary", "arbitrary")),
        name="gla",
    )(jnp.asarray(_gla_level_table()), jnp.asarray(role), jnp.asarray(bd_mask, BF16), up_heads, gate_b_heads, ng,
      zc, zc, zc, zc, zc, zl, zl, zl, zl, zl)


def _outproj_kernel(att_ref, ret_ref, gla_ref, w_ref, x_ref, gate_ref, sh_ref, sc_ref, g_ref, xo_ref, ho_ref):
    na, nr = att_ref.shape[1], ret_ref.shape[1]
    y = (_dot(att_ref[...], w_ref[0, 0:na, :]) + _dot(ret_ref[...], w_ref[0, na:na + nr, :])
         + _dot(gla_ref[...], w_ref[0, na + nr:, :]))
    xn = x_ref[...] + gate_ref[0] * y
    xo_ref[...] = xn
    ho_ref[...] = (_rms(xn) * g_ref[...] * (1.0 + sc_ref[0]) + sh_ref[0]).astype(BF16)


def _outproj(att, ret, gla, w, layer, x2, gate, shift, scale, g, *, tm, rows_per_mod):
    m, d = x2.shape

    def rows(n):
        return pl.BlockSpec((tm, n), lambda i: (i, 0))

    def mod():
        return pl.BlockSpec((1, 1, d), lambda i: ((i * tm) // rows_per_mod, 0, 0))

    return pl.pallas_call(
        _outproj_kernel,
        grid=(m // tm,),
        in_specs=[rows(att.shape[1]), rows(ret.shape[1]), rows(gla.shape[1]),
                  pl.BlockSpec((1,) + w.shape[1:], lambda i: (layer, 0, 0), pipeline_mode=pl.Buffered(1)),
                  rows(d), mod(), mod(), mod(),
                  pl.BlockSpec((1, d), lambda i: (0, 0))],
        out_specs=[rows(d), rows(d)],
        out_shape=[jax.ShapeDtypeStruct((m, d), F32), jax.ShapeDtypeStruct((m, d), BF16)],
        compiler_params=_cparams(("arbitrary",)),
        name="out_proj",
    )(att, ret, gla, w, x2, gate, shift, scale, g)


FFN_HALO = 16
FFN_TF = 512
FFN_TM = 512


def _ffn_kernel(h_ref, hn_ref, hp_ref, wa_ref, wv_ref, cw_ref, cb_ref, wd_ref, x_ref, gate_ref, fg_ref,
                o_ref, hs_ref, *, tm, seq_len, final_norm):
    i = pl.program_id(0)
    f = pl.program_id(1)
    ext = tm + 2 * FFN_HALO

    @pl.when(f == 0)
    def _():
        hs_ref[0:tm, :] = h_ref[...]
        hs_ref[tm:tm + FFN_HALO, :] = hn_ref[...]
        hs_ref[tm + FFN_HALO:ext, :] = hp_ref[...]
        o_ref[...] = jnp.zeros_like(o_ref)

    a = _dot(hs_ref[...], wa_ref[0])
    pos = (i * tm + lax.broadcasted_iota(jnp.int32, (ext, 1), 0)) % seq_len
    prev = jnp.where(pos == 0, 0.0, pltpu.roll(a, 1, 0))
    nxt = jnp.where(pos == seq_len - 1, 0.0, pltpu.roll(a, ext - 1, 0))
    conv = prev * cw_ref[0:1, :] + a * cw_ref[1:2, :] + nxt * cw_ref[2:3, :] + cb_ref[...]
    v = _dot(h_ref[...], wv_ref[0])
    u = (_silu(conv[0:tm]) * v).astype(BF16)
    o_ref[...] += _dot(u, wd_ref[0])

    @pl.when(f == pl.num_programs(1) - 1)
    def _():
        xn = x_ref[...] + gate_ref[0] * o_ref[...]
        if final_norm:
            xn = _rms(xn) * fg_ref[...]
        o_ref[...] = xn


def _ffn(h2, w_up, conv_w, conv_b, w_down, x2, gate, fg, *, tm, seq_len, rows_per_mod, final_norm):
    m, d = x2.shape
    tf = w_up.shape[2]
    nf = D_FF // tf
    hb = tm // FFN_HALO
    last = m // FFN_HALO - 1
    return pl.pallas_call(
        functools.partial(_ffn_kernel, tm=tm, seq_len=seq_len, final_norm=final_norm),
        grid=(m // tm, nf),
        in_specs=[
            pl.BlockSpec((tm, d), lambda i, f: (i, 0)),
            pl.BlockSpec((FFN_HALO, d), lambda i, f: (jnp.minimum((i + 1) * hb, last), 0)),
            pl.BlockSpec((FFN_HALO, d), lambda i, f: (jnp.maximum(i * hb - 1, 0), 0)),
            pl.BlockSpec((1, d, tf), lambda i, f: (f, 0, 0)),
            pl.BlockSpec((1, d, tf), lambda i, f: (nf + f, 0, 0)),
            pl.BlockSpec((3, tf), lambda i, f: (0, f)),
            pl.BlockSpec((1, tf), lambda i, f: (0, f)),
            pl.BlockSpec((1, tf, d), lambda i, f: (0, f, 0)),
            pl.BlockSpec((tm, d), lambda i, f: (i, 0)),
            pl.BlockSpec((1, 1, d), lambda i, f: ((i * tm) // rows_per_mod, 0, 0)),
            pl.BlockSpec((1, d), lambda i, f: (0, 0)),
        ],
        out_specs=pl.BlockSpec((tm, d), lambda i, f: (i, 0)),
        out_shape=jax.ShapeDtypeStruct((m, d), F32),
        scratch_shapes=[pltpu.VMEM((tm + 2 * FFN_HALO, d), BF16)],
        compiler_params=_cparams(("arbitrary", "arbitrary")),
        name="conv_glu",
    )(h2, h2, h2, w_up, w_up, conv_w, conv_b, w_down, x2, gate, fg)


def _rope_tables(n_tokens):
    rows = n_tokens // GRID_W
    row = jnp.repeat(jnp.arange(rows, dtype=F32), GRID_W)
    col = jnp.tile(jnp.arange(GRID_W, dtype=F32), rows)
    n_freq = HEAD_DIM // 4
    inv_freq = ROPE_THETA ** (-jnp.arange(n_freq, dtype=F32) / n_freq)
    ang = jnp.concatenate([row[:, None] * inv_freq, col[:, None] * inv_freq], axis=-1)
    cos, sin = jnp.cos(ang), jnp.sin(ang)
    return jnp.concatenate([cos, cos], axis=-1), jnp.concatenate([-sin, sin], axis=-1)


def kernel(x, c, ctx, c_ctx, ada_w, ada_b, norm1_g, w_in, q_norm_g, k_norm_g, ret_log_decay, ret_norm_g,
           gla_gate_up, gla_gate_b, gla_norm_g, w_out, norm2_g, w_up, conv_w, conv_b, w_down, final_norm_g):
    batch, lat_len, d = x.shape
    ctx_len = ctx.shape[1]
    depth = ada_w.shape[0]
    mod_rows = 16
    cc = jnp.concatenate([c, c_ctx[None], jnp.zeros((mod_rows - batch - 1, d), F32)], axis=0)
    mod = _ada(cc, ada_w, ada_b).reshape(depth, mod_rows, N_MOD, d)

    cos_l, sin_l = _rope_tables(lat_len)
    proj_tm = PROJ_TM
    cos_c = jnp.ones((proj_tm, LANES), F32)
    sin_c = jnp.zeros((proj_tm, LANES), F32)

    xl = x.reshape(batch * lat_len, d)
    xc = ctx.reshape(batch * ctx_len, d)
    row = lambda v: v.reshape(1, -1)
    w_in_b = w_in.astype(BF16)
    w_out_b = w_out[0].astype(BF16)[None]

    for l in range(depth):
        last = l == depth - 1
        ml = [mod[l, :batch, k].reshape(batch, 1, d) for k in range(N_MOD)]
        mc = [mod[l, batch, k].reshape(1, 1, d) for k in range(N_MOD)]
        up_heads, gate_b_heads = _gla_gate_params(gla_gate_up[l], gla_gate_b[l])

        zl, vtl = _proj(xl, ml[0], ml[1], row(norm1_g[l]), w_in_b, l, cos_l, sin_l, row(q_norm_g[l]), row(k_norm_g[l]),
                        tm=proj_tm, rows_per_mod=lat_len, rope_tiles=lat_len // proj_tm)
        zc, vtc = _proj(xc, mc[0], mc[1], row(norm1_g[l]), w_in_b, l, cos_c, sin_c, row(q_norm_g[l]), row(k_norm_g[l]),
                        tm=proj_tm, rows_per_mod=batch * ctx_len, rope_tiles=1)

        casts = [(w_up, l, FFN_TF), (w_down, l, None)]
        if not last:
            casts += [(w_out, l + 1, None)]
        att_l, cast = _attn(vtl, [(zc, vtc, ctx_len), (zl, vtl, lat_len)], batch=batch, q_len=lat_len, tq=ATT_TQ,
                            casts=casts)
        w_up_b, w_down_b = cast[0], cast[1][None]
        ret_c, ret_l = _ret(zc, zl, ret_log_decay[l], row(ret_norm_g[l]),
                            batch=batch, ctx_len=ctx_len, lat_len=lat_len)
        gla_c, gla_l = _gla(zc, zl, up_heads, gate_b_heads, row(gla_norm_g[l]),
                            batch=batch, ctx_len=ctx_len, lat_len=lat_len)

        xl, h2 = _outproj(att_l, ret_l, gla_l, w_out_b, 0, xl, ml[2], ml[3], ml[4], row(norm2_g[l]),
                          tm=OUTPROJ_TM, rows_per_mod=lat_len)
        xl = _ffn(h2, w_up_b, conv_w[l], row(conv_b[l]), w_down_b, xl, ml[5], row(final_norm_g),
                  tm=FFN_TM, seq_len=lat_len, rows_per_mod=lat_len, final_norm=last)

        if not last:
            att_c, _ = _attn(vtc, [(zc, vtc, ctx_len)], batch=batch, q_len=ctx_len, tq=ctx_len)
            xc, hc2 = _outproj(att_c, ret_c, gla_c, w_out_b, 0, xc, mc[2], mc[3], mc[4], row(norm2_g[l]),
                               tm=OUTPROJ_TM, rows_per_mod=batch * ctx_len)
            xc = _ffn(hc2, w_up_b, conv_w[l], row(conv_b[l]), w_down_b, xc, mc[5], row(final_norm_g),
                      tm=FFN_TM, seq_len=ctx_len, rows_per_mod=batch * ctx_len, final_norm=False)
            w_out_b = cast[2][None]

    return xl.reshape(batch, lat_len, d)
```

```python
import functools

import numpy as np
import jax
import jax.numpy as jnp
from jax import lax
from jax.experimental import pallas as pl
from jax.experimental.pallas import tpu as pltpu

F32 = jnp.float32
BF16 = jnp.bfloat16

HEAD_DIM = 128
GRID_W = 64
ATT_Q_HEADS = 8
ATT_KV_HEADS = 2
GQA = ATT_Q_HEADS // ATT_KV_HEADS
RET_HEADS = 4
RET_DIM = 128
RET_CHUNK = 128
GLA_HEADS = 4
GLA_DK = 64
GLA_DV = 128
GLA_GATE_RANK = 16
GLA_TAU = 16.0
GLA_CHUNK = 64
GLA_LEVELS = 6
D_FF = 5632
ROPE_THETA = 10000.0
N_MOD = 6
EPS = 1e-6

LOG2_E = 1.4426950408889634

LANES = 128
N_IN = 5152
PROJ_TN = 256
N_SLOTS = -(-N_IN // LANES)
SLOT_AQ, SLOT_AK, SLOT_AV = 0, 8, 10
SLOT_RQ, SLOT_RK, SLOT_RV, SLOT_RG = 12, 16, 20, 24
SLOT_GQ, SLOT_GK, SLOT_GV, SLOT_GR, SLOT_GA = 28, 30, 32, 36, 40

VMEM_LIMIT = 52 * 1024 * 1024
ADA_TN = 1024
PROJ_TM = 512
ATT_TQ = 512
ATT_Q_SUB = 256
OUTPROJ_TM = 512

_NT = (((1,), (1,)), ((), ()))
_TN = (((0,), (0,)), ((), ()))


def _cparams(sem, vmem_limit=VMEM_LIMIT):
    return pltpu.CompilerParams(dimension_semantics=sem, vmem_limit_bytes=vmem_limit)


def _dot(a, b):
    return jnp.dot(a, b, preferred_element_type=F32)


def _dot_nt(a, b):
    return lax.dot_general(a, b, _NT, preferred_element_type=F32)


def _dot_tn(a, b):
    return lax.dot_general(a, b, _TN, preferred_element_type=F32)


def _rows(start, size, align):
    if isinstance(start, int):
        return pl.ds(start, size)
    return pl.ds(pl.multiple_of(start, align), size)


def _rms(t):
    return t * lax.rsqrt(jnp.mean(t * t, axis=-1, keepdims=True) + EPS)


def _silu(t):
    return t * jax.nn.sigmoid(t)


def _ada_kernel(c_ref, w_ref, b_ref, o_ref):
    s = _silu(c_ref[...]).astype(BF16)
    o_ref[0] = _dot(s, w_ref[0].astype(BF16)) + b_ref[0]


def _ada(cc, ada_w, ada_b):
    depth, d, n = ada_w.shape
    rows = cc.shape[0]
    tn = ADA_TN
    return pl.pallas_call(
        _ada_kernel,
        grid=(depth, n // tn),
        in_specs=[
            pl.BlockSpec((rows, d), lambda l, j: (0, 0)),
            pl.BlockSpec((1, d, tn), lambda l, j: (l, 0, j)),
            pl.BlockSpec((1, 1, tn), lambda l, j: (l, 0, j)),
        ],
        out_specs=pl.BlockSpec((1, rows, tn), lambda l, j: (l, 0, j)),
        out_shape=jax.ShapeDtypeStruct((depth, rows, n), F32),
        compiler_params=_cparams(("arbitrary", "arbitrary")),
        name="ada_mod",
    )(cc, ada_w, ada_b.reshape(depth, 1, n))


def _proj_kernel(x_ref, sh_ref, sc_ref, g_ref, w_ref, cos_ref, sin_ref, qg_ref, kg_ref, o_ref, vt_ref):
    y = _rms(x_ref[...]) * g_ref[...]
    h = (y * (1.0 + sc_ref[0]) + sh_ref[0]).astype(BF16)
    cos, sin = cos_ref[...], sin_ref[...]
    qg = qg_ref[...] * (HEAD_DIM ** -0.5 * LOG2_E)
    kg = kg_ref[...]

    def rope(t):
        return t * cos + pltpu.roll(t, HEAD_DIM // 2, 1) * sin

    def transform(slot, t):
        if slot < SLOT_AK:
            return rope(_rms(t) * qg)
        if slot < SLOT_AV:
            return rope(_rms(t) * kg)
        if SLOT_RQ <= slot < SLOT_RK:
            return rope(t)
        if SLOT_RK <= slot < SLOT_RV:
            return rope(t * (RET_DIM ** -0.5))
        if SLOT_GQ <= slot < SLOT_GK:
            return t * (GLA_DK ** -0.5)
        return t

    for c0 in range(0, N_IN, PROJ_TN):
        c1 = min(c0 + PROJ_TN, N_IN)
        z = _dot(h, w_ref[0, :, c0:c1])
        for s0 in range(c0, c1, LANES):
            slot, n = s0 // LANES, min(LANES, c1 - s0)
            t = transform(slot, z[:, s0 - c0:s0 - c0 + n])
            if slot < SLOT_AK:
                vt_ref[slot] = t.T.astype(BF16)
            elif SLOT_AV <= slot < SLOT_RQ:
                vt_ref[ATT_Q_HEADS + slot - SLOT_AV] = t.T.astype(BF16)
            t = t.astype(BF16)
            if n < LANES:
                o_ref[slot] = jnp.zeros(o_ref.shape[1:], BF16)
                o_ref[slot, :, 0:n] = t
            else:
                o_ref[slot] = t


def _proj(x2, shift, scale, g, w, layer, cosf, sinf, qg, kg, *, tm, rows_per_mod, rope_tiles):
    m, d = x2.shape
    return pl.pallas_call(
        _proj_kernel,
        grid=(m // tm,),
        in_specs=[
            pl.BlockSpec((tm, d), lambda i: (i, 0)),
            pl.BlockSpec((1, 1, d), lambda i: ((i * tm) // rows_per_mod, 0, 0)),
            pl.BlockSpec((1, 1, d), lambda i: ((i * tm) // rows_per_mod, 0, 0)),
            pl.BlockSpec((1, d), lambda i: (0, 0)),
            pl.BlockSpec((1, d, N_IN), lambda i: (layer, 0, 0), pipeline_mode=pl.Buffered(1)),
            pl.BlockSpec((tm, LANES), lambda i: (i % rope_tiles, 0)),
            pl.BlockSpec((tm, LANES), lambda i: (i % rope_tiles, 0)),
            pl.BlockSpec((1, LANES), lambda i: (0, 0)),
            pl.BlockSpec((1, LANES), lambda i: (0, 0)),
        ],
        out_specs=[pl.BlockSpec((N_SLOTS, tm, LANES), lambda i: (0, i, 0)),
                   pl.BlockSpec((ATT_Q_HEADS + ATT_KV_HEADS, HEAD_DIM, tm), lambda i: (0, 0, i))],
        out_shape=[jax.ShapeDtypeStruct((N_SLOTS, m, LANES), BF16),
                   jax.ShapeDtypeStruct((ATT_Q_HEADS + ATT_KV_HEADS, HEAD_DIM, m), BF16)],
        compiler_params=_cparams(("arbitrary",)),
        name="norm_proj",
    )(x2, shift, scale, g, w, cosf, sinf, qg, kg)


ATT_KEY_BLOCK = 128


def _attn_kernel(*refs, n_kv, reps):
    n_cast = len(reps)
    q_ref = refs[0]
    kv_refs = refs[1:1 + 2 * n_kv]
    w_refs = refs[1 + 2 * n_kv:1 + 2 * n_kv + n_cast]
    o_ref = refs[1 + 2 * n_kv + n_cast]
    wo_refs = refs[2 + 2 * n_kv + n_cast:]
    step = (pl.program_id(0) * pl.num_programs(1) + pl.program_id(1)) * pl.num_programs(2) + pl.program_id(2)

    def convert(w_ref, wo_ref):
        if len(wo_ref.shape) == 3:
            tn = wo_ref.shape[2]
            for t in range(wo_ref.shape[0]):
                wo_ref[t] = w_ref[0, :, t * tn:(t + 1) * tn].astype(BF16)
        else:
            wo_ref[...] = w_ref[0].astype(BF16)

    for w_ref, wo_ref, rep in zip(w_refs, wo_refs, reps):
        if rep == 1:
            convert(w_ref, wo_ref)
        else:
            pl.when(step % rep == 0)(functools.partial(convert, w_ref, wo_ref))
    sub = min(ATT_Q_SUB, q_ref.shape[2])
    chains = [(g, t0) for t0 in range(0, q_ref.shape[2], sub) for g in range(GQA)]
    m, l, acc = {}, {}, {}
    units = [(s, j0, ch) for s in range(n_kv) for j0 in range(0, kv_refs[2 * s].shape[1], ATT_KEY_BLOCK)
             for ch in chains]

    def scores(unit):
        s, j0, (g, t0) = unit
        return _dot(kv_refs[2 * s][0, j0:j0 + ATT_KEY_BLOCK, :], q_ref[g, :, t0:t0 + sub])

    st_next = scores(units[0])
    for u, (s, j0, ch) in enumerate(units):
        st = st_next
        if u + 1 < len(units):
            st_next = scores(units[u + 1])
        vt = kv_refs[2 * s + 1][0, :, j0:j0 + ATT_KEY_BLOCK]
        mb = jnp.max(st, axis=0, keepdims=True)
        if ch not in m:
            m[ch] = mb
            p = jnp.exp2(st - mb)
            l[ch] = jnp.sum(p, axis=0, keepdims=True)
            acc[ch] = _dot(vt, p.astype(BF16))
        else:
            m_new = jnp.maximum(m[ch], mb)
            alpha = jnp.exp2(m[ch] - m_new)
            p = jnp.exp2(st - m_new)
            l[ch] = alpha * l[ch] + jnp.sum(p, axis=0, keepdims=True)
            acc[ch] = alpha * acc[ch] + _dot(vt, p.astype(BF16))
            m[ch] = m_new
    for g, t0 in chains:
        o_ref[t0:t0 + sub, g * HEAD_DIM:(g + 1) * HEAD_DIM] = (acc[g, t0] / l[g, t0]).T.astype(BF16)


def _attn(qt, kv_sources, *, batch, q_len, tq, casts=()):
    nq = q_len // tq
    steps = batch * ATT_KV_HEADS * nq
    in_specs = [pl.BlockSpec((GQA, HEAD_DIM, tq), lambda b, k, i: (k, 0, b * nq + i))]
    args = [qt]
    for z, vt, kv_len in kv_sources:
        in_specs.append(pl.BlockSpec((1, kv_len, HEAD_DIM), lambda b, k, i: (SLOT_AK + k, b, 0)))
        in_specs.append(pl.BlockSpec((1, HEAD_DIM, kv_len), lambda b, k, i: (ATT_Q_HEADS + k, 0, b)))
        args += [z, vt]
    out_specs = [pl.BlockSpec((tq, GQA * HEAD_DIM), lambda b, k, i: (b * nq + i, k))]
    out_shape = [jax.ShapeDtypeStruct((batch * q_len, ATT_Q_HEADS * HEAD_DIM), BF16)]
    step = lambda b, k, i: (b * ATT_KV_HEADS + k) * nq + i
    reps = []
    for w, layer, tile in casts:
        _, rows, cols = w.shape
        rep = next(r for r in (1, 2, 4, 8, 16) if rows % (steps // r) == 0 and rows // (steps // r) % 16 == 0)
        slab = rows // (steps // rep)
        reps.append(rep)
        in_specs.append(pl.BlockSpec((1, slab, cols),
                                     lambda b, k, i, rep=rep, layer=layer: (layer, step(b, k, i) // rep, 0)))
        args.append(w)
        if tile is None:
            out_specs.append(pl.BlockSpec((slab, cols), lambda b, k, i, rep=rep: (step(b, k, i) // rep, 0)))
            out_shape.append(jax.ShapeDtypeStruct((rows, cols), BF16))
        else:
            out_specs.append(pl.BlockSpec((cols // tile, slab, tile),
                                          lambda b, k, i, rep=rep: (0, step(b, k, i) // rep, 0)))
            out_shape.append(jax.ShapeDtypeStruct((cols // tile, rows, tile), BF16))
    outs = pl.pallas_call(
        functools.partial(_attn_kernel, n_kv=len(kv_sources), reps=tuple(reps)),
        grid=(batch, ATT_KV_HEADS, nq),
        in_specs=in_specs,
        out_specs=out_specs,
        out_shape=out_shape,
        compiler_params=_cparams(("arbitrary", "arbitrary", "arbitrary")),
        name="attention",
    )(*args)
    return outs[0], list(outs[1:])


def _ret_kernel(dec_ref, qc_ref, kc_ref, vc_ref, gc_ref, ql_ref, kl_ref, vl_ref, gl_ref, ng_ref,
                oc_ref, ol_ref, kv_ref, st_ref, *, nc_ctx, nc_lat):
    h = pl.program_id(1)
    C = RET_CHUNK
    lg_f, lg_b = dec_ref[0, h], dec_ref[1, h]
    rel = lax.broadcasted_iota(jnp.int32, (C, C), 0) - lax.broadcasted_iota(jnp.int32, (C, C), 1)
    dmat = (jnp.where(rel >= 0, jnp.exp(lg_f * jnp.maximum(rel, 0).astype(F32)), 0.0)
            + jnp.where(rel <= 0, jnp.exp(lg_b * jnp.maximum(-rel, 0).astype(F32)), 0.0))
    row = lax.broadcasted_iota(jnp.int32, (C, 1), 0).astype(F32)
    qd_f, qd_b = jnp.exp(lg_f * (row + 1.0)), jnp.exp(lg_b * (C - row))
    kd_f, kd_b = jnp.exp(lg_f * (C - 1.0 - row)), jnp.exp(lg_b * row)
    cd_f, cd_b = jnp.exp(lg_f * C), jnp.exp(lg_b * C)

    chunks = [(qc_ref, kc_ref, vc_ref, gc_ref, oc_ref, n) for n in range(nc_ctx)]
    chunks += [(ql_ref, kl_ref, vl_ref, gl_ref, ol_ref, n) for n in range(nc_lat)]
    order_b = list(range(nc_ctx - 1, -1, -1)) + list(range(nc_ctx + nc_lat - 1, nc_ctx - 1, -1))

    def decayed(x_ref, n, dec_f, dec_b):
        x = x_ref[0, n * C:(n + 1) * C, :].astype(F32)
        return jnp.concatenate([(x * dec_f).astype(BF16), (x * dec_b).astype(BF16)], axis=1)

    for c, (_, k_ref, v_ref, _, _, n) in enumerate(chunks):
        kv_ref[c] = _dot_tn(decayed(k_ref, n, kd_f, kd_b), v_ref[0, n * C:(n + 1) * C, :])

    s = jnp.zeros((C, RET_DIM), F32)
    for c in range(len(chunks)):
        st_ref[c, 0:C, :] = s.astype(BF16)
        s = cd_f * s + kv_ref[c, 0:C, :]
    s = jnp.zeros((C, RET_DIM), F32)
    for c in order_b:
        st_ref[c, C:2 * C, :] = s.astype(BF16)
        s = cd_b * s + kv_ref[c, C:2 * C, :]

    for c, (q_ref, k_ref, v_ref, g_ref, o_ref, n) in enumerate(chunks):
        sl = slice(n * C, (n + 1) * C)
        v = v_ref[0, sl, :]
        att = _dot_nt(q_ref[0, sl, :], k_ref[0, sl, :]) * dmat
        o = _dot(att.astype(BF16), v) + _dot(decayed(q_ref, n, qd_f, qd_b), st_ref[c])
        y = _rms(o) * ng_ref[...]
        o_ref[sl, :] = (y * _silu(g_ref[0, sl, :].astype(F32))).astype(BF16)


def _ret(zc, zl, decay, ng, *, batch, ctx_len, lat_len):
    def slot(base, n):
        return pl.BlockSpec((1, n, RET_DIM), lambda b, h: (base + h, b, 0))

    n_chunks = (ctx_len + lat_len) // RET_CHUNK
    specs = [pl.BlockSpec(memory_space=pltpu.SMEM)]
    specs += [slot(s, ctx_len) for s in (SLOT_RQ, SLOT_RK, SLOT_RV, SLOT_RG)]
    specs += [slot(s, lat_len) for s in (SLOT_RQ, SLOT_RK, SLOT_RV, SLOT_RG)]
    specs += [pl.BlockSpec((1, RET_DIM), lambda b, h: (0, 0))]
    return pl.pallas_call(
        functools.partial(_ret_kernel, nc_ctx=ctx_len // RET_CHUNK, nc_lat=lat_len // RET_CHUNK),
        grid=(batch, RET_HEADS),
        in_specs=specs,
        out_specs=[pl.BlockSpec((ctx_len, RET_DIM), lambda b, h: (b, h)),
                   pl.BlockSpec((lat_len, RET_DIM), lambda b, h: (b, h))],
        out_shape=[jax.ShapeDtypeStruct((batch * ctx_len, RET_HEADS * RET_DIM), BF16),
                   jax.ShapeDtypeStruct((batch * lat_len, RET_HEADS * RET_DIM), BF16)],
        scratch_shapes=[pltpu.VMEM((n_chunks, 2 * RET_DIM, RET_DIM), F32),
                        pltpu.VMEM((n_chunks, 2 * RET_DIM, RET_DIM), BF16)],
        compiler_params=_cparams(("arbitrary", "arbitrary")),
        name="retention",
    )(decay, zc, zc, zc, zc, zl, zl, zl, zl, ng)


GLA_SLAB = 256


def _gla_level_table():
    i = np.arange(GLA_CHUNK)[:, None]
    j = np.arange(GLA_CHUNK)[None, :]
    x = np.maximum(i ^ j, 1)
    lvl = np.floor(np.log2(x)).astype(np.int32)
    lvl = np.where(i == j, GLA_LEVELS, np.where(i > j, lvl, -1)).astype(np.int32)
    return np.concatenate([lvl, lvl.T], axis=1)


def _gla_role_table():
    t = np.arange(GLA_SLAB)
    fwd = np.arange(LANES)[None, :] < GLA_DK
    return np.stack([np.where(((t[:, None] >> lev) & 1).astype(bool) == fwd, LOG2_E, -LOG2_E)
                     for lev in range(GLA_LEVELS)]).astype(np.float32)


def _bcast_block_row(x, blk, r):
    rows = x.shape[0]
    if blk >= 8:
        x3 = x.reshape(rows // blk, blk, LANES)
        return jnp.broadcast_to(x3[:, r:r + 1, :], x3.shape).reshape(rows, LANES)
    pos = lax.broadcasted_iota(jnp.int32, x.shape, 0) & (blk - 1)
    out = x
    for p in range(blk):
        if p != r:
            out = jnp.where(pos == p, pltpu.roll(x, (p - r) % rows, 0), out)
    return out


def _gla_kernel(lvl_ref, role_ref, bdm_ref, up_ref, gb_ref, ng_ref,
                qc_ref, kc_ref, vc_ref, gc_ref, ac_ref, ql_ref, kl_ref, vl_ref, gl_ref, al_ref,
                oc_ref, ol_ref, *scratch, ctx_len, lat_len):
    C = GLA_CHUNK
    streams = ((qc_ref, kc_ref, vc_ref, ac_ref, 0, ctx_len), (ql_ref, kl_ref, vl_ref, al_ref, ctx_len, lat_len))
    nc_ctx, nc_lat = ctx_len // C, lat_len // C
    n_chunks = nc_ctx + nc_lat
    cpos = lax.broadcasted_iota(jnp.int32, (GLA_SLAB, LANES), 0) & (C - 1)
    fwd = lax.broadcasted_iota(jnp.int32, (GLA_SLAB, LANES), 1) < GLA_DK
    lvl = lvl_ref[...]

    passes = []
    for hh in range(2):
        own = fwd if hh == 0 else jnp.logical_not(fwd)
        xs_ref, qd_ref, xst_ref, qh_ref, tot_ref, vv_ref, kv_ref, st_ref = (r.at[hh] for r in scratch)

        def prep(i, between, q_ref, k_ref, v_ref, a_ref, off, hh=hh, own=own, xs_ref=xs_ref, qd_ref=qd_ref,
                 xst_ref=xst_ref, qh_ref=qh_ref, tot_ref=tot_ref, vv_ref=vv_ref, kv_ref=kv_ref):
            between = list(between)
            src = _rows(i * GLA_SLAB, GLA_SLAB, GLA_SLAB)
            dst = _rows(off + i * GLA_SLAB, GLA_SLAB, C)
            chunk0 = off // C + i * (GLA_SLAB // C)

            def store_keys(lev, x):
                y = jnp.concatenate([x[c * C:(c + 1) * C] for c in range(GLA_SLAB // C) for _ in range(2)], axis=0)
                yt = y.T.astype(BF16) * bdm_ref[...]
                for c in range(GLA_SLAB // C):
                    xst_ref[lev, chunk0 + c] = yt[:, c * 2 * C:(c + 1) * 2 * C]

            logit = _dot(a_ref[0, src, :], up_ref[hh]) + gb_ref[hh]
            la = (jnp.minimum(logit, 0.0) - jnp.log(1.0 + jnp.exp(-jnp.abs(logit)))) * (1.0 / GLA_TAU)
            pre = la
            for step in range(GLA_LEVELS):
                sh = 1 << step
                pre = pre + jnp.where(cpos >= sh, pltpu.roll(pre, sh, 0), 0.0)
            tot = _bcast_block_row(pre, C, C - 1)
            cum = jnp.where(fwd, pre, tot - pre + la)
            q2 = q_ref[0, src, :].astype(F32)
            k2 = k_ref[0, src, :].astype(F32)
            q = jnp.where(own, q2, pltpu.roll(q2, GLA_DK, 1))
            k = jnp.where(own, k2, pltpu.roll(k2, GLA_DK, 1))
            qd_ref[dst, :] = q.astype(BF16)
            store_keys(GLA_LEVELS, k)
            qh_ref[dst, :] = (q * jnp.exp(cum)).astype(BF16)
            kh = (k * jnp.exp(tot - cum)).astype(BF16)
            tot_ref[dst, :] = tot
            v = v_ref[hh, src, :]
            vv_ref[dst, :] = v
            for c in range(GLA_SLAB // C):
                kv_ref[chunk0 + c] = _dot_tn(v[c * C:(c + 1) * C], kh[c * C:(c + 1) * C])
            for lev in range(GLA_LEVELS):
                if between and lev % 2 == 0:
                    between.pop(0)()
                s = 1 << lev
                d = pre - _bcast_block_row(pre, 2 * s, s - 1)
                dist = jnp.where(fwd, d, la - d)
                role = role_ref[lev]
                x = jnp.where(role > 0, q, k) * jnp.exp2(dist * role)
                xs_ref[lev, dst, :] = x.astype(BF16)
                store_keys(lev, x)
            for item in between:
                item()

        def states(kv_ref=kv_ref, st_ref=st_ref, tot_ref=tot_ref):
            def step(c, s, lanes):
                st_ref[c, :, lanes] = s[:, lanes].astype(BF16)
                return s * jnp.exp(tot_ref[pl.ds(c * C, 1), :]) + kv_ref[c]

            def body(n, carry):
                s_f, s_b = carry
                c_b = jnp.where(n < nc_ctx, nc_ctx - 1 - n, n_chunks + nc_ctx - 1 - n)
                return step(n, s_f, slice(0, GLA_DK)), step(c_b, s_b, slice(GLA_DK, LANES))

            zero = jnp.zeros((GLA_DV, LANES), F32)
            lax.fori_loop(0, n_chunks, body, (zero, zero))

        def out_chunk(c, g_ref, o_ref, local, hh=hh, xs_ref=xs_ref, qd_ref=qd_ref, xst_ref=xst_ref, qh_ref=qh_ref,
                      vv_ref=vv_ref, st_ref=st_ref):
            rows = _rows(c * C, C, C)
            a = jnp.where(lvl == GLA_LEVELS, _dot(qd_ref[rows, :], xst_ref[GLA_LEVELS, c]), 0.0)
            for lev in range(GLA_LEVELS):
                a = jnp.where(lvl == lev, _dot(xs_ref[lev, rows, :], xst_ref[lev, c]), a)
            v = vv_ref[rows, :]
            o = _dot(a.astype(BF16), jnp.concatenate([v, v], axis=0)) + _dot_nt(qh_ref[rows, :], st_ref[c])
            y = _rms(o) * ng_ref[...]
            out_rows = _rows(local * C, C, C)
            y = y * _silu(g_ref[hh, out_rows, :].astype(F32))
            o_ref[out_rows, hh * GLA_DV:(hh + 1) * GLA_DV] = y.astype(BF16)

        passes.append((prep, states, out_chunk))

    per_slab = GLA_SLAB // C
    (prep0, states0, out0), (prep1, states1, out1) = passes

    def slab(prep, out, stream, i, g_ref, o_ref):
        q_ref, k_ref, v_ref, a_ref, off, _ = stream
        chunks = []
        if out is not None:
            chunks = [functools.partial(out, off // C + i * per_slab + j, g_ref, o_ref, i * per_slab + j)
                      for j in range(per_slab)]
        if prep is not None:
            prep(i, chunks, q_ref=q_ref, k_ref=k_ref, v_ref=v_ref, a_ref=a_ref, off=off)
        else:
            for item in chunks:
                item()

    def run(prep, out):
        for stream, g_ref, o_ref in ((streams[0], gc_ref, oc_ref), (streams[1], gl_ref, ol_ref)):
            n_slabs = stream[5] // GLA_SLAB
            if n_slabs == 1:
                slab(prep, out, stream, 0, g_ref, o_ref)
            else:
                def body(i, carry, stream=stream, g_ref=g_ref, o_ref=o_ref):
                    slab(prep, out, stream, i, g_ref, o_ref)
                    return carry
                lax.fori_loop(0, n_slabs, body, 0, unroll=2 if n_slabs % 2 == 0 else 1)

    run(prep0, None)
    states0()
    run(prep1, out0)
    states1()
    run(None, out1)


def _gla_gate_params(gate_up, gate_b):
    r = GLA_GATE_RANK
    gu = gate_up.astype(BF16).reshape(2, r, GLA_HEADS, GLA_DK).transpose(2, 0, 1, 3)
    up = jnp.zeros((GLA_HEADS, LANES, 2 * GLA_DK), BF16)
    up = up.at[:, 0:r, 0:GLA_DK].set(gu[:, 0]).at[:, r:2 * r, GLA_DK:].set(gu[:, 1])
    gb = gate_b.reshape(2, GLA_HEADS, GLA_DK).transpose(1, 0, 2).reshape(GLA_HEADS, 1, 2 * GLA_DK)
    return up, gb


def _gla(zc, zl, up_heads, gate_b_heads, ng, *, batch, ctx_len, lat_len):
    total = ctx_len + lat_len
    n_chunks = total // GLA_CHUNK

    def pair(base, n):
        return pl.BlockSpec((1, n, LANES), lambda b, p: (base + p, b, 0))

    def two(base, n):
        return pl.BlockSpec((2, n, LANES), lambda b, p: (base // 2 + p, b, 0))

    def one(n):
        return pl.BlockSpec((1, n, LANES), lambda b, p: (SLOT_GA, b, 0))

    role = _gla_role_table()
    feat_fwd = np.arange(LANES)[:, None] < GLA_DK
    key_fwd = (np.arange(2 * GLA_SLAB)[None, :] % (2 * GLA_CHUNK)) < GLA_CHUNK
    bd_mask = (feat_fwd == key_fwd).astype(np.float32)
    specs = [
        pl.BlockSpec((GLA_CHUNK, 2 * GLA_CHUNK), lambda b, p: (0, 0)),
        pl.BlockSpec(role.shape, lambda b, p: (0, 0, 0)),
        pl.BlockSpec(bd_mask.shape, lambda b, p: (0, 0)),
        pl.BlockSpec((2, LANES, LANES), lambda b, p: (p, 0, 0)),
        pl.BlockSpec((2, 1, LANES), lambda b, p: (p, 0, 0)),
        pl.BlockSpec((1, GLA_DV), lambda b, p: (0, 0)),
    ]
    for n in (ctx_len, lat_len):
        specs += [pair(SLOT_GQ, n), pair(SLOT_GK, n), two(SLOT_GV, n), two(SLOT_GR, n), one(n)]
    return pl.pallas_call(
        functools.partial(_gla_kernel, ctx_len=ctx_len, lat_len=lat_len),
        grid=(batch, GLA_HEADS // 2),
        in_specs=specs,
        out_specs=[pl.BlockSpec((ctx_len, 2 * GLA_DV), lambda b, p: (b, p)),
                   pl.BlockSpec((lat_len, 2 * GLA_DV), lambda b, p: (b, p))],
        out_shape=[jax.ShapeDtypeStruct((batch * ctx_len, GLA_HEADS * GLA_DV), BF16),
                   jax.ShapeDtypeStruct((batch * lat_len, GLA_HEADS * GLA_DV), BF16)],
        scratch_shapes=[
            pltpu.VMEM((2, GLA_LEVELS, total, LANES), BF16),
            pltpu.VMEM((2, total, LANES), BF16),
            pltpu.VMEM((2, GLA_LEVELS + 1, n_chunks, LANES, 2 * GLA_CHUNK), BF16),
            pltpu.VMEM((2, total, LANES), BF16),
            pltpu.VMEM((2, total, LANES), F32),
            pltpu.VMEM((2, total, GLA_DV), BF16),
            pltpu.VMEM((2, n_chunks, GLA_DV, LANES), F32),
            pltpu.VMEM((2, n_chunks, GLA_DV, LANES), BF16),
        ],
        compiler_params=_cparams(("arbitrary", "arbitrary")),
        name="gla",
    )(jnp.asarray(_gla_level_table()), jnp.asarray(role), jnp.asarray(bd_mask, BF16), up_heads, gate_b_heads, ng,
      zc, zc, zc, zc, zc, zl, zl, zl, zl, zl)


def _outproj_kernel(att_ref, ret_ref, gla_ref, w_ref, x_ref, gate_ref, sh_ref, sc_ref, g_ref, xo_ref, ho_ref):
    na, nr = att_ref.shape[1], ret_ref.shape[1]
    y = (_dot(att_ref[...], w_ref[0, 0:na, :]) + _dot(ret_ref[...], w_ref[0, na:na + nr, :])
         + _dot(gla_ref[...], w_ref[0, na + nr:, :]))
    xn = x_ref[...] + gate_ref[0] * y
    xo_ref[...] = xn
    ho_ref[...] = (_rms(xn) * g_ref[...] * (1.0 + sc_ref[0]) + sh_ref[0]).astype(BF16)


def _outproj(att, ret, gla, w, layer, x2, gate, shift, scale, g, *, tm, rows_per_mod):
    m, d = x2.shape

    def rows(n):
        return pl.BlockSpec((tm, n), lambda i: (i, 0))

    def mod():
        return pl.BlockSpec((1, 1, d), lambda i: ((i * tm) // rows_per_mod, 0, 0))

    return pl.pallas_call(
        _outproj_kernel,
        grid=(m // tm,),
        in_specs=[rows(att.shape[1]), rows(ret.shape[1]), rows(gla.shape[1]),
                  pl.BlockSpec((1,) + w.shape[1:], lambda i: (layer, 0, 0), pipeline_mode=pl.Buffered(1)),
                  rows(d), mod(), mod(), mod(),
                  pl.BlockSpec((1, d), lambda i: (0, 0))],
        out_specs=[rows(d), rows(d)],
        out_shape=[jax.ShapeDtypeStruct((m, d), F32), jax.ShapeDtypeStruct((m, d), BF16)],
        compiler_params=_cparams(("arbitrary",)),
        name="out_proj",
    )(att, ret, gla, w, x2, gate, shift, scale, g)


FFN_HALO = 16
FFN_TF = 512
FFN_TM = 512


def _ffn_kernel(h_ref, hn_ref, hp_ref, wa_ref, wv_ref, cw_ref, cb_ref, wd_ref, x_ref, gate_ref, fg_ref,
                o_ref, hs_ref, *, tm, seq_len, final_norm):
    i = pl.program_id(0)
    f = pl.program_id(1)
    ext = tm + 2 * FFN_HALO

    @pl.when(f == 0)
    def _():
        hs_ref[0:tm, :] = h_ref[...]
        hs_ref[tm:tm + FFN_HALO, :] = hn_ref[...]
        hs_ref[tm + FFN_HALO:ext, :] = hp_ref[...]
        o_ref[...] = jnp.zeros_like(o_ref)

    a = _dot(hs_ref[...], wa_ref[0])
    pos = (i * tm + lax.broadcasted_iota(jnp.int32, (ext, 1), 0)) % seq_len
    prev = jnp.where(pos == 0, 0.0, pltpu.roll(a, 1, 0))
    nxt = jnp.where(pos == seq_len - 1, 0.0, pltpu.roll(a, ext - 1, 0))
    conv = prev * cw_ref[0:1, :] + a * cw_ref[1:2, :] + nxt * cw_ref[2:3, :] + cb_ref[...]
    v = _dot(h_ref[...], wv_ref[0])
    u = (_silu(conv[0:tm]) * v).astype(BF16)
    o_ref[...] += _dot(u, wd_ref[0])

    @pl.when(f == pl.num_programs(1) - 1)
    def _():
        xn = x_ref[...] + gate_ref[0] * o_ref[...]
        if final_norm:
            xn = _rms(xn) * fg_ref[...]
        o_ref[...] = xn


def _ffn(h2, w_up, conv_w, conv_b, w_down, x2, gate, fg, *, tm, seq_len, rows_per_mod, final_norm):
    m, d = x2.shape
    tf = w_up.shape[2]
    nf = D_FF // tf
    hb = tm // FFN_HALO
    last = m // FFN_HALO - 1
    return pl.pallas_call(
        functools.partial(_ffn_kernel, tm=tm, seq_len=seq_len, final_norm=final_norm),
        grid=(m // tm, nf),
        in_specs=[
            pl.BlockSpec((tm, d), lambda i, f: (i, 0)),
            pl.BlockSpec((FFN_HALO, d), lambda i, f: (jnp.minimum((i + 1) * hb, last), 0)),
            pl.BlockSpec((FFN_HALO, d), lambda i, f: (jnp.maximum(i * hb - 1, 0), 0)),
            pl.BlockSpec((1, d, tf), lambda i, f: (f, 0, 0)),
            pl.BlockSpec((1, d, tf), lambda i, f: (nf + f, 0, 0)),
            pl.BlockSpec((3, tf), lambda i, f: (0, f)),
            pl.BlockSpec((1, tf), lambda i, f: (0, f)),
            pl.BlockSpec((1, tf, d), lambda i, f: (0, f, 0)),
            pl.BlockSpec((tm, d), lambda i, f: (i, 0)),
            pl.BlockSpec((1, 1, d), lambda i, f: ((i * tm) // rows_per_mod, 0, 0)),
            pl.BlockSpec((1, d), lambda i, f: (0, 0)),
        ],
        out_specs=pl.BlockSpec((tm, d), lambda i, f: (i, 0)),
        out_shape=jax.ShapeDtypeStruct((m, d), F32),
        scratch_shapes=[pltpu.VMEM((tm + 2 * FFN_HALO, d), BF16)],
        compiler_params=_cparams(("arbitrary", "arbitrary")),
        name="conv_glu",
    )(h2, h2, h2, w_up, w_up, conv_w, conv_b, w_down, x2, gate, fg)


def _rope_tables(n_tokens):
    rows = n_tokens // GRID_W
    row = jnp.repeat(jnp.arange(rows, dtype=F32), GRID_W)
    col = jnp.tile(jnp.arange(GRID_W, dtype=F32), rows)
    n_freq = HEAD_DIM // 4
    inv_freq = ROPE_THETA ** (-jnp.arange(n_freq, dtype=F32) / n_freq)
    ang = jnp.concatenate([row[:, None] * inv_freq, col[:, None] * inv_freq], axis=-1)
    cos, sin = jnp.cos(ang), jnp.sin(ang)
    return jnp.concatenate([cos, cos], axis=-1), jnp.concatenate([-sin, sin], axis=-1)


def kernel(x, c, ctx, c_ctx, ada_w, ada_b, norm1_g, w_in, q_norm_g, k_norm_g, ret_log_decay, ret_norm_g,
           gla_gate_up, gla_gate_b, gla_norm_g, w_out, norm2_g, w_up, conv_w, conv_b, w_down, final_norm_g):
    batch, lat_len, d = x.shape
    ctx_len = ctx.shape[1]
    depth = ada_w.shape[0]
    mod_rows = 16
    cc = jnp.concatenate([c, c_ctx[None], jnp.zeros((mod_rows - batch - 1, d), F32)], axis=0)
    mod = _ada(cc, ada_w, ada_b).reshape(depth, mod_rows, N_MOD, d)

    cos_l, sin_l = _rope_tables(lat_len)
    proj_tm = PROJ_TM
    cos_c = jnp.ones((proj_tm, LANES), F32)
    sin_c = jnp.zeros((proj_tm, LANES), F32)

    xl = x.reshape(batch * lat_len, d)
    xc = ctx.reshape(batch * ctx_len, d)
    row = lambda v: v.reshape(1, -1)
    w_in_b = w_in.astype(BF16)
    w_out_b = w_out[0].astype(BF16)[None]

    for l in range(depth):
        last = l == depth - 1
        ml = [mod[l, :batch, k].reshape(batch, 1, d) for k in range(N_MOD)]
        mc = [mod[l, batch, k].reshape(1, 1, d) for k in range(N_MOD)]
        up_heads, gate_b_heads = _gla_gate_params(gla_gate_up[l], gla_gate_b[l])

        zl, vtl = _proj(xl, ml[0], ml[1], row(norm1_g[l]), w_in_b, l, cos_l, sin_l, row(q_norm_g[l]), row(k_norm_g[l]),
                        tm=proj_tm, rows_per_mod=lat_len, rope_tiles=lat_len // proj_tm)
        zc, vtc = _proj(xc, mc[0], mc[1], row(norm1_g[l]), w_in_b, l, cos_c, sin_c, row(q_norm_g[l]), row(k_norm_g[l]),
                        tm=proj_tm, rows_per_mod=batch * ctx_len, rope_tiles=1)

        casts = [(w_up, l, FFN_TF), (w_down, l, None)]
        if not last:
            casts += [(w_out, l + 1, None)]
        att_l, cast = _attn(vtl, [(zc, vtc, ctx_len), (zl, vtl, lat_len)], batch=batch, q_len=lat_len, tq=ATT_TQ,
                            casts=casts)
        w_up_b, w_down_b = cast[0], cast[1][None]
        ret_c, ret_l = _ret(zc, zl, ret_log_decay[l], row(ret_norm_g[l]),
                            batch=batch, ctx_len=ctx_len, lat_len=lat_len)
        gla_c, gla_l = _gla(zc, zl, up_heads, gate_b_heads, row(gla_norm_g[l]),
                            batch=batch, ctx_len=ctx_len, lat_len=lat_len)

        xl, h2 = _outproj(att_l, ret_l, gla_l, w_out_b, 0, xl, ml[2], ml[3], ml[4], row(norm2_g[l]),
                          tm=OUTPROJ_TM, rows_per_mod=lat_len)
        xl = _ffn(h2, w_up_b, conv_w[l], row(conv_b[l]), w_down_b, xl, ml[5], row(final_norm_g),
                  tm=FFN_TM, seq_len=lat_len, rows_per_mod=lat_len, final_norm=last)

        if not last:
            att_c, _ = _attn(vtc, [(zc, vtc, ctx_len)], batch=batch, q_len=ctx_len, tq=ctx_len)
            xc, hc2 = _outproj(att_c, ret_c, gla_c, w_out_b, 0, xc, mc[2], mc[3], mc[4], row(norm2_g[l]),
                               tm=OUTPROJ_TM, rows_per_mod=batch * ctx_len)
            xc = _ffn(hc2, w_up_b, conv_w[l], row(conv_b[l]), w_down_b, xc, mc[5], row(final_norm_g),
                      tm=FFN_TM, seq_len=ctx_len, rows_per_mod=batch * ctx_len, final_norm=False)
            w_out_b = cast[2][None]

    return xl.reshape(batch, lat_len, d)
```

```python
import functools

import numpy as np
import jax
import jax.numpy as jnp
from jax import lax
from jax.experimental import pallas as pl
from jax.experimental.pallas import tpu as pltpu

F32 = jnp.float32
BF16 = jnp.bfloat16

HEAD_DIM = 128
GRID_W = 64
ATT_Q_HEADS = 8
ATT_KV_HEADS = 2
GQA = ATT_Q_HEADS // ATT_KV_HEADS
RET_HEADS = 4
RET_DIM = 128
RET_CHUNK = 128
GLA_HEADS = 4
GLA_DK = 64
GLA_DV = 128
GLA_GATE_RANK = 16
GLA_TAU = 16.0
GLA_CHUNK = 64
GLA_LEVELS = 6
D_FF = 5632
ROPE_THETA = 10000.0
N_MOD = 6
EPS = 1e-6

LOG2_E = 1.4426950408889634

LANES = 128
N_IN = 5152
PROJ_TN = 256
N_SLOTS = -(-N_IN // LANES)
SLOT_AQ, SLOT_AK, SLOT_AV = 0, 8, 10
SLOT_RQ, SLOT_RK, SLOT_RV, SLOT_RG = 12, 16, 20, 24
SLOT_GQ, SLOT_GK, SLOT_GV, SLOT_GR, SLOT_GA = 28, 30, 32, 36, 40

VMEM_LIMIT = 52 * 1024 * 1024
ADA_TN = 1024
PROJ_TM = 512
ATT_TQ = 1024
ATT_Q_SUB = 256
OUTPROJ_TM = 512

_NT = (((1,), (1,)), ((), ()))
_TN = (((0,), (0,)), ((), ()))


def _cparams(sem, vmem_limit=VMEM_LIMIT):
    return pltpu.CompilerParams(dimension_semantics=sem, vmem_limit_bytes=vmem_limit)


def _dot(a, b):
    return jnp.dot(a, b, preferred_element_type=F32)


def _dot_nt(a, b):
    return lax.dot_general(a, b, _NT, preferred_element_type=F32)


def _dot_tn(a, b):
    return lax.dot_general(a, b, _TN, preferred_element_type=F32)


def _rows(start, size, align):
    if isinstance(start, int):
        return pl.ds(start, size)
    return pl.ds(pl.multiple_of(start, align), size)


def _rms(t):
    return t * lax.rsqrt(jnp.mean(t * t, axis=-1, keepdims=True) + EPS)


def _silu(t):
    return t * jax.nn.sigmoid(t)


def _ada_kernel(c_ref, w_ref, b_ref, o_ref):
    s = _silu(c_ref[...]).astype(BF16)
    o_ref[0] = _dot(s, w_ref[0].astype(BF16)) + b_ref[0]


def _ada(cc, ada_w, ada_b):
    depth, d, n = ada_w.shape
    rows = cc.shape[0]
    tn = ADA_TN
    return pl.pallas_call(
        _ada_kernel,
        grid=(depth, n // tn),
        in_specs=[
            pl.BlockSpec((rows, d), lambda l, j: (0, 0)),
            pl.BlockSpec((1, d, tn), lambda l, j: (l, 0, j)),
            pl.BlockSpec((1, 1, tn), lambda l, j: (l, 0, j)),
        ],
        out_specs=pl.BlockSpec((1, rows, tn), lambda l, j: (l, 0, j)),
        out_shape=jax.ShapeDtypeStruct((depth, rows, n), F32),
        compiler_params=_cparams(("arbitrary", "arbitrary")),
        name="ada_mod",
    )(cc, ada_w, ada_b.reshape(depth, 1, n))


def _proj_kernel(x_ref, sh_ref, sc_ref, g_ref, w_ref, cos_ref, sin_ref, qg_ref, kg_ref, o_ref, vt_ref):
    y = _rms(x_ref[...]) * g_ref[...]
    h = (y * (1.0 + sc_ref[0]) + sh_ref[0]).astype(BF16)
    cos, sin = cos_ref[...], sin_ref[...]
    qg = qg_ref[...] * (HEAD_DIM ** -0.5 * LOG2_E)
    kg = kg_ref[...]

    def rope(t):
        return t * cos + pltpu.roll(t, HEAD_DIM // 2, 1) * sin

    def transform(slot, t):
        if slot < SLOT_AK:
            return rope(_rms(t) * qg)
        if slot < SLOT_AV:
            return rope(_rms(t) * kg)
        if SLOT_RQ <= slot < SLOT_RK:
            return rope(t)
        if SLOT_RK <= slot < SLOT_RV:
            return rope(t * (RET_DIM ** -0.5))
        if SLOT_GQ <= slot < SLOT_GK:
            return t * (GLA_DK ** -0.5)
        return t

    for c0 in range(0, N_IN, PROJ_TN):
        c1 = min(c0 + PROJ_TN, N_IN)
        z = _dot(h, w_ref[0, :, c0:c1])
        for s0 in range(c0, c1, LANES):
            slot, n = s0 // LANES, min(LANES, c1 - s0)
            t = transform(slot, z[:, s0 - c0:s0 - c0 + n])
            if slot < SLOT_AK:
                vt_ref[slot] = t.T.astype(BF16)
            elif SLOT_AV <= slot < SLOT_RQ:
                vt_ref[ATT_Q_HEADS + slot - SLOT_AV] = t.T.astype(BF16)
            t = t.astype(BF16)
            if n < LANES:
                o_ref[slot] = jnp.zeros(o_ref.shape[1:], BF16)
                o_ref[slot, :, 0:n] = t
            else:
                o_ref[slot] = t


def _proj(x2, shift, scale, g, w, layer, cosf, sinf, qg, kg, *, tm, rows_per_mod, rope_tiles):
    m, d = x2.shape
    return pl.pallas_call(
        _proj_kernel,
        grid=(m // tm,),
        in_specs=[
            pl.BlockSpec((tm, d), lambda i: (i, 0)),
            pl.BlockSpec((1, 1, d), lambda i: ((i * tm) // rows_per_mod, 0, 0)),
            pl.BlockSpec((1, 1, d), lambda i: ((i * tm) // rows_per_mod, 0, 0)),
            pl.BlockSpec((1, d), lambda i: (0, 0)),
            pl.BlockSpec((1, d, N_IN), lambda i: (layer, 0, 0), pipeline_mode=pl.Buffered(1)),
            pl.BlockSpec((tm, LANES), lambda i: (i % rope_tiles, 0)),
            pl.BlockSpec((tm, LANES), lambda i: (i % rope_tiles, 0)),
            pl.BlockSpec((1, LANES), lambda i: (0, 0)),
            pl.BlockSpec((1, LANES), lambda i: (0, 0)),
        ],
        out_specs=[pl.BlockSpec((N_SLOTS, tm, LANES), lambda i: (0, i, 0)),
                   pl.BlockSpec((ATT_Q_HEADS + ATT_KV_HEADS, HEAD_DIM, tm), lambda i: (0, 0, i))],
        out_shape=[jax.ShapeDtypeStruct((N_SLOTS, m, LANES), BF16),
                   jax.ShapeDtypeStruct((ATT_Q_HEADS + ATT_KV_HEADS, HEAD_DIM, m), BF16)],
        compiler_params=_cparams(("arbitrary",)),
        name="norm_proj",
    )(x2, shift, scale, g, w, cosf, sinf, qg, kg)


ATT_KEY_BLOCK = 128


def _attn_kernel(*refs, n_kv, reps):
    n_cast = len(reps)
    q_ref = refs[0]
    kv_refs = refs[1:1 + 2 * n_kv]
    w_refs = refs[1 + 2 * n_kv:1 + 2 * n_kv + n_cast]
    o_ref = refs[1 + 2 * n_kv + n_cast]
    wo_refs = refs[2 + 2 * n_kv + n_cast:]
    step = (pl.program_id(0) * pl.num_programs(1) + pl.program_id(1)) * pl.num_programs(2) + pl.program_id(2)

    def convert(w_ref, wo_ref):
        if len(wo_ref.shape) == 3:
            tn = wo_ref.shape[2]
            for t in range(wo_ref.shape[0]):
                wo_ref[t] = w_ref[0, :, t * tn:(t + 1) * tn].astype(BF16)
        else:
            wo_ref[...] = w_ref[0].astype(BF16)

    for w_ref, wo_ref, rep in zip(w_refs, wo_refs, reps):
        if rep == 1:
            convert(w_ref, wo_ref)
        else:
            pl.when(step % rep == 0)(functools.partial(convert, w_ref, wo_ref))
    sub = min(ATT_Q_SUB, q_ref.shape[2])
    chains = [(g, t0) for t0 in range(0, q_ref.shape[2], sub) for g in range(GQA)]
    m, l, acc = {}, {}, {}
    units = [(s, j0, ch) for s in range(n_kv) for j0 in range(0, kv_refs[2 * s].shape[1], ATT_KEY_BLOCK)
             for ch in chains]

    def scores(unit):
        s, j0, (g, t0) = unit
        return _dot(kv_refs[2 * s][0, j0:j0 + ATT_KEY_BLOCK, :], q_ref[g, :, t0:t0 + sub])

    st_next = scores(units[0])
    for u, (s, j0, ch) in enumerate(units):
        st = st_next
        if u + 1 < len(units):
            st_next = scores(units[u + 1])
        vt = kv_refs[2 * s + 1][0, :, j0:j0 + ATT_KEY_BLOCK]
        mb = jnp.max(st, axis=0, keepdims=True)
        if ch not in m:
            m[ch] = mb
            p = jnp.exp2(st - mb)
            l[ch] = jnp.sum(p, axis=0, keepdims=True)
            acc[ch] = _dot(vt, p.astype(BF16))
        else:
            m_new = jnp.maximum(m[ch], mb)
            alpha = jnp.exp2(m[ch] - m_new)
            p = jnp.exp2(st - m_new)
            l[ch] = alpha * l[ch] + jnp.sum(p, axis=0, keepdims=True)
            acc[ch] = alpha * acc[ch] + _dot(vt, p.astype(BF16))
            m[ch] = m_new
    for g, t0 in chains:
        o_ref[t0:t0 + sub, g * HEAD_DIM:(g + 1) * HEAD_DIM] = (acc[g, t0] / l[g, t0]).T.astype(BF16)


def _attn(qt, kv_sources, *, batch, q_len, tq, casts=()):
    nq = q_len // tq
    steps = batch * ATT_KV_HEADS * nq
    in_specs = [pl.BlockSpec((GQA, HEAD_DIM, tq), lambda b, k, i: (k, 0, b * nq + i))]
    args = [qt]
    for z, vt, kv_len in kv_sources:
        in_specs.append(pl.BlockSpec((1, kv_len, HEAD_DIM), lambda b, k, i: (SLOT_AK + k, b, 0)))
        in_specs.append(pl.BlockSpec((1, HEAD_DIM, kv_len), lambda b, k, i: (ATT_Q_HEADS + k, 0, b)))
        args += [z, vt]
    out_specs = [pl.BlockSpec((tq, GQA * HEAD_DIM), lambda b, k, i: (b * nq + i, k))]
    out_shape = [jax.ShapeDtypeStruct((batch * q_len, ATT_Q_HEADS * HEAD_DIM), BF16)]
    step = lambda b, k, i: (b * ATT_KV_HEADS + k) * nq + i
    reps = []
    for w, layer, tile in casts:
        _, rows, cols = w.shape
        rep = next(r for r in (1, 2, 4, 8, 16) if rows % (steps // r) == 0 and rows // (steps // r) % 16 == 0)
        slab = rows // (steps // rep)
        reps.append(rep)
        in_specs.append(pl.BlockSpec((1, slab, cols),
                                     lambda b, k, i, rep=rep, layer=layer: (layer, step(b, k, i) // rep, 0)))
        args.append(w)
        if tile is None:
            out_specs.append(pl.BlockSpec((slab, cols), lambda b, k, i, rep=rep: (step(b, k, i) // rep, 0)))
            out_shape.append(jax.ShapeDtypeStruct((rows, cols), BF16))
        else:
            out_specs.append(pl.BlockSpec((cols // tile, slab, tile),
                                          lambda b, k, i, rep=rep: (0, step(b, k, i) // rep, 0)))
            out_shape.append(jax.ShapeDtypeStruct((cols // tile, rows, tile), BF16))
    outs = pl.pallas_call(
        functools.partial(_attn_kernel, n_kv=len(kv_sources), reps=tuple(reps)),
        grid=(batch, ATT_KV_HEADS, nq),
        in_specs=in_specs,
        out_specs=out_specs,
        out_shape=out_shape,
        compiler_params=_cparams(("arbitrary", "arbitrary", "arbitrary")),
        name="attention",
    )(*args)
    return outs[0], list(outs[1:])


def _ret_kernel(dec_ref, qc_ref, kc_ref, vc_ref, gc_ref, ql_ref, kl_ref, vl_ref, gl_ref, ng_ref,
                oc_ref, ol_ref, kv_ref, st_ref, *, nc_ctx, nc_lat):
    h = pl.program_id(1)
    C = RET_CHUNK
    lg_f, lg_b = dec_ref[0, h], dec_ref[1, h]
    rel = lax.broadcasted_iota(jnp.int32, (C, C), 0) - lax.broadcasted_iota(jnp.int32, (C, C), 1)
    dmat = (jnp.where(rel >= 0, jnp.exp(lg_f * jnp.maximum(rel, 0).astype(F32)), 0.0)
            + jnp.where(rel <= 0, jnp.exp(lg_b * jnp.maximum(-rel, 0).astype(F32)), 0.0))
    row = lax.broadcasted_iota(jnp.int32, (C, 1), 0).astype(F32)
    qd_f, qd_b = jnp.exp(lg_f * (row + 1.0)), jnp.exp(lg_b * (C - row))
    kd_f, kd_b = jnp.exp(lg_f * (C - 1.0 - row)), jnp.exp(lg_b * row)
    cd_f, cd_b = jnp.exp(lg_f * C), jnp.exp(lg_b * C)

    chunks = [(qc_ref, kc_ref, vc_ref, gc_ref, oc_ref, n) for n in range(nc_ctx)]
    chunks += [(ql_ref, kl_ref, vl_ref, gl_ref, ol_ref, n) for n in range(nc_lat)]
    order_b = list(range(nc_ctx - 1, -1, -1)) + list(range(nc_ctx + nc_lat - 1, nc_ctx - 1, -1))

    def decayed(x_ref, n, dec_f, dec_b):
        x = x_ref[0, n * C:(n + 1) * C, :].astype(F32)
        return jnp.concatenate([(x * dec_f).astype(BF16), (x * dec_b).astype(BF16)], axis=1)

    for c, (_, k_ref, v_ref, _, _, n) in enumerate(chunks):
        kv_ref[c] = _dot_tn(decayed(k_ref, n, kd_f, kd_b), v_ref[0, n * C:(n + 1) * C, :])

    s = jnp.zeros((C, RET_DIM), F32)
    for c in range(len(chunks)):
        st_ref[c, 0:C, :] = s.astype(BF16)
        s = cd_f * s + kv_ref[c, 0:C, :]
    s = jnp.zeros((C, RET_DIM), F32)
    for c in order_b:
        st_ref[c, C:2 * C, :] = s.astype(BF16)
        s = cd_b * s + kv_ref[c, C:2 * C, :]

    for c, (q_ref, k_ref, v_ref, g_ref, o_ref, n) in enumerate(chunks):
        sl = slice(n * C, (n + 1) * C)
        v = v_ref[0, sl, :]
        att = _dot_nt(q_ref[0, sl, :], k_ref[0, sl, :]) * dmat
        o = _dot(att.astype(BF16), v) + _dot(decayed(q_ref, n, qd_f, qd_b), st_ref[c])
        y = _rms(o) * ng_ref[...]
        o_ref[sl, :] = (y * _silu(g_ref[0, sl, :].astype(F32))).astype(BF16)


def _ret(zc, zl, decay, ng, *, batch, ctx_len, lat_len):
    def slot(base, n):
        return pl.BlockSpec((1, n, RET_DIM), lambda b, h: (base + h, b, 0))

    n_chunks = (ctx_len + lat_len) // RET_CHUNK
    specs = [pl.BlockSpec(memory_space=pltpu.SMEM)]
    specs += [slot(s, ctx_len) for s in (SLOT_RQ, SLOT_RK, SLOT_RV, SLOT_RG)]
    specs += [slot(s, lat_len) for s in (SLOT_RQ, SLOT_RK, SLOT_RV, SLOT_RG)]
    specs += [pl.BlockSpec((1, RET_DIM), lambda b, h: (0, 0))]
    return pl.pallas_call(
        functools.partial(_ret_kernel, nc_ctx=ctx_len // RET_CHUNK, nc_lat=lat_len // RET_CHUNK),
        grid=(batch, RET_HEADS),
        in_specs=specs,
        out_specs=[pl.BlockSpec((ctx_len, RET_DIM), lambda b, h: (b, h)),
                   pl.BlockSpec((lat_len, RET_DIM), lambda b, h: (b, h))],
        out_shape=[jax.ShapeDtypeStruct((batch * ctx_len, RET_HEADS * RET_DIM), BF16),
                   jax.ShapeDtypeStruct((batch * lat_len, RET_HEADS * RET_DIM), BF16)],
        scratch_shapes=[pltpu.VMEM((n_chunks, 2 * RET_DIM, RET_DIM), F32),
                        pltpu.VMEM((n_chunks, 2 * RET_DIM, RET_DIM), BF16)],
        compiler_params=_cparams(("arbitrary", "arbitrary")),
        name="retention",
    )(decay, zc, zc, zc, zc, zl, zl, zl, zl, ng)


GLA_SLAB = 256


def _gla_level_table():
    i = np.arange(GLA_CHUNK)[:, None]
    j = np.arange(GLA_CHUNK)[None, :]
    x = np.maximum(i ^ j, 1)
    lvl = np.floor(np.log2(x)).astype(np.int32)
    lvl = np.where(i == j, GLA_LEVELS, np.where(i > j, lvl, -1)).astype(np.int32)
    return np.concatenate([lvl, lvl.T], axis=1)


def _gla_role_table():
    t = np.arange(GLA_SLAB)
    fwd = np.arange(LANES)[None, :] < GLA_DK
    return np.stack([np.where(((t[:, None] >> lev) & 1).astype(bool) == fwd, LOG2_E, -LOG2_E)
                     for lev in range(GLA_LEVELS)]).astype(np.float32)


def _bcast_block_row(x, blk, r):
    rows = x.shape[0]
    if blk >= 8:
        x3 = x.reshape(rows // blk, blk, LANES)
        return jnp.broadcast_to(x3[:, r:r + 1, :], x3.shape).reshape(rows, LANES)
    pos = lax.broadcasted_iota(jnp.int32, x.shape, 0) & (blk - 1)
    out = x
    for p in range(blk):
        if p != r:
            out = jnp.where(pos == p, pltpu.roll(x, (p - r) % rows, 0), out)
    return out


def _gla_kernel(lvl_ref, role_ref, bdm_ref, up_ref, gb_ref, ng_ref,
                qc_ref, kc_ref, vc_ref, gc_ref, ac_ref, ql_ref, kl_ref, vl_ref, gl_ref, al_ref,
                oc_ref, ol_ref, *scratch, ctx_len, lat_len):
    C = GLA_CHUNK
    streams = ((qc_ref, kc_ref, vc_ref, ac_ref, 0, ctx_len), (ql_ref, kl_ref, vl_ref, al_ref, ctx_len, lat_len))
    nc_ctx, nc_lat = ctx_len // C, lat_len // C
    n_chunks = nc_ctx + nc_lat
    cpos = lax.broadcasted_iota(jnp.int32, (GLA_SLAB, LANES), 0) & (C - 1)
    fwd = lax.broadcasted_iota(jnp.int32, (GLA_SLAB, LANES), 1) < GLA_DK
    lvl = lvl_ref[...]

    passes = []
    for hh in range(2):
        own = fwd if hh == 0 else jnp.logical_not(fwd)
        xs_ref, qd_ref, xst_ref, qh_ref, tot_ref, vv_ref, kv_ref, st_ref = (r.at[hh] for r in scratch)

        def prep(i, between, q_ref, k_ref, v_ref, a_ref, off, hh=hh, own=own, xs_ref=xs_ref, qd_ref=qd_ref,
                 xst_ref=xst_ref, qh_ref=qh_ref, tot_ref=tot_ref, vv_ref=vv_ref, kv_ref=kv_ref):
            between = list(between)
            src = _rows(i * GLA_SLAB, GLA_SLAB, GLA_SLAB)
            dst = _rows(off + i * GLA_SLAB, GLA_SLAB, C)
            chunk0 = off // C + i * (GLA_SLAB // C)

            def store_keys(lev, x):
                y = jnp.concatenate([x[c * C:(c + 1) * C] for c in range(GLA_SLAB // C) for _ in range(2)], axis=0)
                yt = y.T.astype(BF16) * bdm_ref[...]
                for c in range(GLA_SLAB // C):
                    xst_ref[lev, chunk0 + c] = yt[:, c * 2 * C:(c + 1) * 2 * C]

            logit = _dot(a_ref[0, src, :], up_ref[hh]) + gb_ref[hh]
            la = (jnp.minimum(logit, 0.0) - jnp.log(1.0 + jnp.exp(-jnp.abs(logit)))) * (1.0 / GLA_TAU)
            pre = la
            for step in range(GLA_LEVELS):
                sh = 1 << step
                pre = pre + jnp.where(cpos >= sh, pltpu.roll(pre, sh, 0), 0.0)
            tot = _bcast_block_row(pre, C, C - 1)
            cum = jnp.where(fwd, pre, tot - pre + la)
            q2 = q_ref[0, src, :].astype(F32)
            k2 = k_ref[0, src, :].astype(F32)
            q = jnp.where(own, q2, pltpu.roll(q2, GLA_DK, 1))
            k = jnp.where(own, k2, pltpu.roll(k2, GLA_DK, 1))
            qd_ref[dst, :] = q.astype(BF16)
            store_keys(GLA_LEVELS, k)
            qh_ref[dst, :] = (q * jnp.exp(cum)).astype(BF16)
            kh = (k * jnp.exp(tot - cum)).astype(BF16)
            tot_ref[dst, :] = tot
            v = v_ref[hh, src, :]
            vv_ref[dst, :] = v
            for c in range(GLA_SLAB // C):
                kv_ref[chunk0 + c] = _dot_tn(v[c * C:(c + 1) * C], kh[c * C:(c + 1) * C])
            for lev in range(GLA_LEVELS):
                if between and lev % 2 == 0:
                    between.pop(0)()
                s = 1 << lev
                d = pre - _bcast_block_row(pre, 2 * s, s - 1)
                dist = jnp.where(fwd, d, la - d)
                role = role_ref[lev]
                x = jnp.where(role > 0, q, k) * jnp.exp2(dist * role)
                xs_ref[lev, dst, :] = x.astype(BF16)
                store_keys(lev, x)
            for item in between:
                item()

        def states(kv_ref=kv_ref, st_ref=st_ref, tot_ref=tot_ref):
            def step(c, s, lanes):
                st_ref[c, :, lanes] = s[:, lanes].astype(BF16)
                return s * jnp.exp(tot_ref[pl.ds(c * C, 1), :]) + kv_ref[c]

            def body(n, carry):
                s_f, s_b = carry
                c_b = jnp.where(n < nc_ctx, nc_ctx - 1 - n, n_chunks + nc_ctx - 1 - n)
                return step(n, s_f, slice(0, GLA_DK)), step(c_b, s_b, slice(GLA_DK, LANES))

            zero = jnp.zeros((GLA_DV, LANES), F32)
            lax.fori_loop(0, n_chunks, body, (zero, zero))

        def out_chunk(c, g_ref, o_ref, local, hh=hh, xs_ref=xs_ref, qd_ref=qd_ref, xst_ref=xst_ref, qh_ref=qh_ref,
                      vv_ref=vv_ref, st_ref=st_ref):
            rows = _rows(c * C, C, C)
            a = jnp.where(lvl == GLA_LEVELS, _dot(qd_ref[rows, :], xst_ref[GLA_LEVELS, c]), 0.0)
            for lev in range(GLA_LEVELS):
                a = jnp.where(lvl == lev, _dot(xs_ref[lev, rows, :], xst_ref[lev, c]), a)
            v = vv_ref[rows, :]
            o = _dot(a.astype(BF16), jnp.concatenate([v, v], axis=0)) + _dot_nt(qh_ref[rows, :], st_ref[c])
            y = _rms(o) * ng_ref[...]
            out_rows = _rows(local * C, C, C)
            y = y * _silu(g_ref[hh, out_rows, :].astype(F32))
            o_ref[out_rows, hh * GLA_DV:(hh + 1) * GLA_DV] = y.astype(BF16)

        passes.append((prep, states, out_chunk))

    per_slab = GLA_SLAB // C
    (prep0, states0, out0), (prep1, states1, out1) = passes

    def slab(prep, out, stream, i, g_ref, o_ref):
        q_ref, k_ref, v_ref, a_ref, off, _ = stream
        chunks = []
        if out is not None:
            chunks = [functools.partial(out, off // C + i * per_slab + j, g_ref, o_ref, i * per_slab + j)
                      for j in range(per_slab)]
        if prep is not None:
            prep(i, chunks, q_ref=q_ref, k_ref=k_ref, v_ref=v_ref, a_ref=a_ref, off=off)
        else:
            for item in chunks:
                item()

    def run(prep, out):
        for stream, g_ref, o_ref in ((streams[0], gc_ref, oc_ref), (streams[1], gl_ref, ol_ref)):
            n_slabs = stream[5] // GLA_SLAB
            if n_slabs == 1:
                slab(prep, out, stream, 0, g_ref, o_ref)
            else:
                def body(i, carry, stream=stream, g_ref=g_ref, o_ref=o_ref):
                    slab(prep, out, stream, i, g_ref, o_ref)
                    return carry
                lax.fori_loop(0, n_slabs, body, 0, unroll=2 if n_slabs % 2 == 0 else 1)

    run(prep0, None)
    states0()
    run(prep1, out0)
    states1()
    run(None, out1)


def _gla_gate_params(gate_up, gate_b):
    r = GLA_GATE_RANK
    gu = gate_up.astype(BF16).reshape(2, r, GLA_HEADS, GLA_DK).transpose(2, 0, 1, 3)
    up = jnp.zeros((GLA_HEADS, LANES, 2 * GLA_DK), BF16)
    up = up.at[:, 0:r, 0:GLA_DK].set(gu[:, 0]).at[:, r:2 * r, GLA_DK:].set(gu[:, 1])
    gb = gate_b.reshape(2, GLA_HEADS, GLA_DK).transpose(1, 0, 2).reshape(GLA_HEADS, 1, 2 * GLA_DK)
    return up, gb


def _gla(zc, zl, up_heads, gate_b_heads, ng, *, batch, ctx_len, lat_len):
    total = ctx_len + lat_len
    n_chunks = total // GLA_CHUNK

    def pair(base, n):
        return pl.BlockSpec((1, n, LANES), lambda b, p: (base + p, b, 0))

    def two(base, n):
        return pl.BlockSpec((2, n, LANES), lambda b, p: (base // 2 + p, b, 0))

    def one(n):
        return pl.BlockSpec((1, n, LANES), lambda b, p: (SLOT_GA, b, 0))

    role = _gla_role_table()
    feat_fwd = np.arange(LANES)[:, None] < GLA_DK
    key_fwd = (np.arange(2 * GLA_SLAB)[None, :] % (2 * GLA_CHUNK)) < GLA_CHUNK
    bd_mask = (feat_fwd == key_fwd).astype(np.float32)
    specs = [
        pl.BlockSpec((GLA_CHUNK, 2 * GLA_CHUNK), lambda b, p: (0, 0)),
        pl.BlockSpec(role.shape, lambda b, p: (0, 0, 0)),
        pl.BlockSpec(bd_mask.shape, lambda b, p: (0, 0)),
        pl.BlockSpec((2, LANES, LANES), lambda b, p: (p, 0, 0)),
        pl.BlockSpec((2, 1, LANES), lambda b, p: (p, 0, 0)),
        pl.BlockSpec((1, GLA_DV), lambda b, p: (0, 0)),
    ]
    for n in (ctx_len, lat_len):
        specs += [pair(SLOT_GQ, n), pair(SLOT_GK, n), two(SLOT_GV, n), two(SLOT_GR, n), one(n)]
    return pl.pallas_call(
        functools.partial(_gla_kernel, ctx_len=ctx_len, lat_len=lat_len),
        grid=(batch, GLA_HEADS // 2),
        in_specs=specs,
        out_specs=[pl.BlockSpec((ctx_len, 2 * GLA_DV), lambda b, p: (b, p)),
                   pl.BlockSpec((lat_len, 2 * GLA_DV), lambda b, p: (b, p))],
        out_shape=[jax.ShapeDtypeStruct((batch * ctx_len, GLA_HEADS * GLA_DV), BF16),
                   jax.ShapeDtypeStruct((batch * lat_len, GLA_HEADS * GLA_DV), BF16)],
        scratch_shapes=[
            pltpu.VMEM((2, GLA_LEVELS, total, LANES), BF16),
            pltpu.VMEM((2, total, LANES), BF16),
            pltpu.VMEM((2, GLA_LEVELS + 1, n_chunks, LANES, 2 * GLA_CHUNK), BF16),
            pltpu.VMEM((2, total, LANES), BF16),
            pltpu.VMEM((2, total, LANES), F32),
            pltpu.VMEM((2, total, GLA_DV), BF16),
            pltpu.VMEM((2, n_chunks, GLA_DV, LANES), F32),
            pltpu.VMEM((2, n_chunks, GLA_DV, LANES), BF16),
        ],
        compiler_params=_cparams(("arbitrary", "arbitrary")),
        name="gla",
    )(jnp.asarray(_gla_level_table()), jnp.asarray(role), jnp.asarray(bd_mask, BF16), up_heads, gate_b_heads, ng,
      zc, zc, zc, zc, zc, zl, zl, zl, zl, zl)


def _outproj_kernel(att_ref, ret_ref, gla_ref, w_ref, x_ref, gate_ref, sh_ref, sc_ref, g_ref, xo_ref, ho_ref):
    na, nr = att_ref.shape[1], ret_ref.shape[1]
    y = (_dot(att_ref[...], w_ref[0, 0:na, :]) + _dot(ret_ref[...], w_ref[0, na:na + nr, :])
         + _dot(gla_ref[...], w_ref[0, na + nr:, :]))
    xn = x_ref[...] + gate_ref[0] * y
    xo_ref[...] = xn
    ho_ref[...] = (_rms(xn) * g_ref[...] * (1.0 + sc_ref[0]) + sh_ref[0]).astype(BF16)


def _outproj(att, ret, gla, w, layer, x2, gate, shift, scale, g, *, tm, rows_per_mod):
    m, d = x2.shape

    def rows(n):
        return pl.BlockSpec((tm, n), lambda i: (i, 0))

    def mod():
        return pl.BlockSpec((1, 1, d), lambda i: ((i * tm) // rows_per_mod, 0, 0))

    return pl.pallas_call(
        _outproj_kernel,
        grid=(m // tm,),
        in_specs=[rows(att.shape[1]), rows(ret.shape[1]), rows(gla.shape[1]),
                  pl.BlockSpec((1,) + w.shape[1:], lambda i: (layer, 0, 0), pipeline_mode=pl.Buffered(1)),
                  rows(d), mod(), mod(), mod(),
                  pl.BlockSpec((1, d), lambda i: (0, 0))],
        out_specs=[rows(d), rows(d)],
        out_shape=[jax.ShapeDtypeStruct((m, d), F32), jax.ShapeDtypeStruct((m, d), BF16)],
        compiler_params=_cparams(("arbitrary",)),
        name="out_proj",
    )(att, ret, gla, w, x2, gate, shift, scale, g)


FFN_HALO = 16
FFN_TF = 512
FFN_TM = 512


def _ffn_kernel(h_ref, hn_ref, hp_ref, wa_ref, wv_ref, cw_ref, cb_ref, wd_ref, x_ref, gate_ref, fg_ref,
                o_ref, hs_ref, *, tm, seq_len, final_norm):
    i = pl.program_id(0)
    f = pl.program_id(1)
    ext = tm + 2 * FFN_HALO

    @pl.when(f == 0)
    def _():
        hs_ref[0:tm, :] = h_ref[...]
        hs_ref[tm:tm + FFN_HALO, :] = hn_ref[...]
        hs_ref[tm + FFN_HALO:ext, :] = hp_ref[...]
        o_ref[...] = jnp.zeros_like(o_ref)

    a = _dot(hs_ref[...], wa_ref[0])
    pos = (i * tm + lax.broadcasted_iota(jnp.int32, (ext, 1), 0)) % seq_len
    prev = jnp.where(pos == 0, 0.0, pltpu.roll(a, 1, 0))
    nxt = jnp.where(pos == seq_len - 1, 0.0, pltpu.roll(a, ext - 1, 0))
    conv = prev * cw_ref[0:1, :] + a * cw_ref[1:2, :] + nxt * cw_ref[2:3, :] + cb_ref[...]
    v = _dot(h_ref[...], wv_ref[0])
    u = (_silu(conv[0:tm]) * v).astype(BF16)
    o_ref[...] += _dot(u, wd_ref[0])

    @pl.when(f == pl.num_programs(1) - 1)
    def _():
        xn = x_ref[...] + gate_ref[0] * o_ref[...]
        if final_norm:
            xn = _rms(xn) * fg_ref[...]
        o_ref[...] = xn


def _ffn(h2, w_up, conv_w, conv_b, w_down, x2, gate, fg, *, tm, seq_len, rows_per_mod, final_norm):
    m, d = x2.shape
    tf = w_up.shape[2]
    nf = D_FF // tf
    hb = tm // FFN_HALO
    last = m // FFN_HALO - 1
    return pl.pallas_call(
        functools.partial(_ffn_kernel, tm=tm, seq_len=seq_len, final_norm=final_norm),
        grid=(m // tm, nf),
        in_specs=[
            pl.BlockSpec((tm, d), lambda i, f: (i, 0)),
            pl.BlockSpec((FFN_HALO, d), lambda i, f: (jnp.minimum((i + 1) * hb, last), 0)),
            pl.BlockSpec((FFN_HALO, d), lambda i, f: (jnp.maximum(i * hb - 1, 0), 0)),
            pl.BlockSpec((1, d, tf), lambda i, f: (f, 0, 0)),
            pl.BlockSpec((1, d, tf), lambda i, f: (nf + f, 0, 0)),
            pl.BlockSpec((3, tf), lambda i, f: (0, f)),
            pl.BlockSpec((1, tf), lambda i, f: (0, f)),
            pl.BlockSpec((1, tf, d), lambda i, f: (0, f, 0)),
            pl.BlockSpec((tm, d), lambda i, f: (i, 0)),
            pl.BlockSpec((1, 1, d), lambda i, f: ((i * tm) // rows_per_mod, 0, 0)),
            pl.BlockSpec((1, d), lambda i, f: (0, 0)),
        ],
        out_specs=pl.BlockSpec((tm, d), lambda i, f: (i, 0)),
        out_shape=jax.ShapeDtypeStruct((m, d), F32),
        scratch_shapes=[pltpu.VMEM((tm + 2 * FFN_HALO, d), BF16)],
        compiler_params=_cparams(("arbitrary", "arbitrary")),
        name="conv_glu",
    )(h2, h2, h2, w_up, w_up, conv_w, conv_b, w_down, x2, gate, fg)


def _rope_tables(n_tokens):
    rows = n_tokens // GRID_W
    row = jnp.repeat(jnp.arange(rows, dtype=F32), GRID_W)
    col = jnp.tile(jnp.arange(GRID_W, dtype=F32), rows)
    n_freq = HEAD_DIM // 4
    inv_freq = ROPE_THETA ** (-jnp.arange(n_freq, dtype=F32) / n_freq)
    ang = jnp.concatenate([row[:, None] * inv_freq, col[:, None] * inv_freq], axis=-1)
    cos, sin = jnp.cos(ang), jnp.sin(ang)
    return jnp.concatenate([cos, cos], axis=-1), jnp.concatenate([-sin, sin], axis=-1)


def kernel(x, c, ctx, c_ctx, ada_w, ada_b, norm1_g, w_in, q_norm_g, k_norm_g, ret_log_decay, ret_norm_g,
           gla_gate_up, gla_gate_b, gla_norm_g, w_out, norm2_g, w_up, conv_w, conv_b, w_down, final_norm_g):
    batch, lat_len, d = x.shape
    ctx_len = ctx.shape[1]
    depth = ada_w.shape[0]
    mod_rows = 16
    cc = jnp.concatenate([c, c_ctx[None], jnp.zeros((mod_rows - batch - 1, d), F32)], axis=0)
    mod = _ada(cc, ada_w, ada_b).reshape(depth, mod_rows, N_MOD, d)

    cos_l, sin_l = _rope_tables(lat_len)
    proj_tm = PROJ_TM
    cos_c = jnp.ones((proj_tm, LANES), F32)
    sin_c = jnp.zeros((proj_tm, LANES), F32)

    xl = x.reshape(batch * lat_len, d)
    xc = ctx.reshape(batch * ctx_len, d)
    row = lambda v: v.reshape(1, -1)
    w_in_b = w_in.astype(BF16)
    w_out_b = w_out[0].astype(BF16)[None]

    for l in range(depth):
        last = l == depth - 1
        ml = [mod[l, :batch, k].reshape(batch, 1, d) for k in range(N_MOD)]
        mc = [mod[l, batch, k].reshape(1, 1, d) for k in range(N_MOD)]
        up_heads, gate_b_heads = _gla_gate_params(gla_gate_up[l], gla_gate_b[l])

        zl, vtl = _proj(xl, ml[0], ml[1], row(norm1_g[l]), w_in_b, l, cos_l, sin_l, row(q_norm_g[l]), row(k_norm_g[l]),
                        tm=proj_tm, rows_per_mod=lat_len, rope_tiles=lat_len // proj_tm)
        zc, vtc = _proj(xc, mc[0], mc[1], row(norm1_g[l]), w_in_b, l, cos_c, sin_c, row(q_norm_g[l]), row(k_norm_g[l]),
                        tm=proj_tm, rows_per_mod=batch * ctx_len, rope_tiles=1)

        casts = [(w_up, l, FFN_TF), (w_down, l, None)]
        if not last:
            casts += [(w_out, l + 1, None)]
        att_l, cast = _attn(vtl, [(zc, vtc, ctx_len), (zl, vtl, lat_len)], batch=batch, q_len=lat_len, tq=ATT_TQ,
                            casts=casts)
        w_up_b, w_down_b = cast[0], cast[1][None]
        ret_c, ret_l = _ret(zc, zl, ret_log_decay[l], row(ret_norm_g[l]),
                            batch=batch, ctx_len=ctx_len, lat_len=lat_len)
        gla_c, gla_l = _gla(zc, zl, up_heads, gate_b_heads, row(gla_norm_g[l]),
                            batch=batch, ctx_len=ctx_len, lat_len=lat_len)

        xl, h2 = _outproj(att_l, ret_l, gla_l, w_out_b, 0, xl, ml[2], ml[3], ml[4], row(norm2_g[l]),
                          tm=OUTPROJ_TM, rows_per_mod=lat_len)
        xl = _ffn(h2, w_up_b, conv_w[l], row(conv_b[l]), w_down_b, xl, ml[5], row(final_norm_g),
                  tm=FFN_TM, seq_len=lat_len, rows_per_mod=lat_len, final_norm=last)

        if not last:
            att_c, _ = _attn(vtc, [(zc, vtc, ctx_len)], batch=batch, q_len=ctx_len, tq=ctx_len)
            xc, hc2 = _outproj(att_c, ret_c, gla_c, w_out_b, 0, xc, mc[2], mc[3], mc[4], row(norm2_g[l]),
                               tm=OUTPROJ_TM, rows_per_mod=batch * ctx_len)
            xc = _ffn(hc2, w_up_b, conv_w[l], row(conv_b[l]), w_down_b, xc, mc[5], row(final_norm_g),
                      tm=FFN_TM, seq_len=ctx_len, rows_per_mod=batch * ctx_len, final_norm=False)
            w_out_b = cast[2][None]

    return xl.reshape(batch, lat_len, d)
```

```python
import functools

import numpy as np
import jax
import jax.numpy as jnp
from jax import lax
from jax.experimental import pallas as pl
from jax.experimental.pallas import tpu as pltpu

F32 = jnp.float32
BF16 = jnp.bfloat16

HEAD_DIM = 128
GRID_W = 64
ATT_Q_HEADS = 8
ATT_KV_HEADS = 2
GQA = ATT_Q_HEADS // ATT_KV_HEADS
RET_HEADS = 4
RET_DIM = 128
RET_CHUNK = 128
GLA_HEADS = 4
GLA_DK = 64
GLA_DV = 128
GLA_GATE_RANK = 16
GLA_TAU = 16.0
GLA_CHUNK = 64
GLA_LEVELS = 6
D_FF = 5632
ROPE_THETA = 10000.0
N_MOD = 6
EPS = 1e-6

LOG2_E = 1.4426950408889634

LANES = 128
N_IN = 5152
PROJ_TN = 256
N_SLOTS = -(-N_IN // LANES)
SLOT_AQ, SLOT_AK, SLOT_AV = 0, 8, 10
SLOT_RQ, SLOT_RK, SLOT_RV, SLOT_RG = 12, 16, 20, 24
SLOT_GQ, SLOT_GK, SLOT_GV, SLOT_GR, SLOT_GA = 28, 30, 32, 36, 40

VMEM_LIMIT = 52 * 1024 * 1024
ADA_TN = 1024
PROJ_TM = 512
ATT_TQ = 1024
ATT_Q_SUB = 256
OUTPROJ_TM = 512

_NT = (((1,), (1,)), ((), ()))
_TN = (((0,), (0,)), ((), ()))


def _cparams(sem, vmem_limit=VMEM_LIMIT):
    return pltpu.CompilerParams(dimension_semantics=sem, vmem_limit_bytes=vmem_limit)


def _dot(a, b):
    return jnp.dot(a, b, preferred_element_type=F32)


def _dot_nt(a, b):
    return lax.dot_general(a, b, _NT, preferred_element_type=F32)


def _dot_tn(a, b):
    return lax.dot_general(a, b, _TN, preferred_element_type=F32)


def _rows(start, size, align):
    if isinstance(start, int):
        return pl.ds(start, size)
    return pl.ds(pl.multiple_of(start, align), size)


def _rms(t):
    return t * lax.rsqrt(jnp.mean(t * t, axis=-1, keepdims=True) + EPS)


def _silu(t):
    return t * jax.nn.sigmoid(t)


def _ada_kernel(c_ref, w_ref, b_ref, o_ref):
    s = _silu(c_ref[...]).astype(BF16)
    o_ref[0] = _dot(s, w_ref[0].astype(BF16)) + b_ref[0]


def _ada(cc, ada_w, ada_b):
    depth, d, n = ada_w.shape
    rows = cc.shape[0]
    tn = ADA_TN
    return pl.pallas_call(
        _ada_kernel,
        grid=(depth, n // tn),
        in_specs=[
            pl.BlockSpec((rows, d), lambda l, j: (0, 0)),
            pl.BlockSpec((1, d, tn), lambda l, j: (l, 0, j)),
            pl.BlockSpec((1, 1, tn), lambda l, j: (l, 0, j)),
        ],
        out_specs=pl.BlockSpec((1, rows, tn), lambda l, j: (l, 0, j)),
        out_shape=jax.ShapeDtypeStruct((depth, rows, n), F32),
        compiler_params=_cparams(("arbitrary", "arbitrary")),
        name="ada_mod",
    )(cc, ada_w, ada_b.reshape(depth, 1, n))


def _proj_kernel(x_ref, sh_ref, sc_ref, g_ref, w_ref, cos_ref, sin_ref, qg_ref, kg_ref, o_ref, vt_ref):
    y = _rms(x_ref[...]) * g_ref[...]
    h = (y * (1.0 + sc_ref[0]) + sh_ref[0]).astype(BF16)
    cos, sin = cos_ref[...], sin_ref[...]
    qg = qg_ref[...] * (HEAD_DIM ** -0.5 * LOG2_E)
    kg = kg_ref[...]

    def rope(t):
        return t * cos + pltpu.roll(t, HEAD_DIM // 2, 1) * sin

    def transform(slot, t):
        if slot < SLOT_AK:
            return rope(_rms(t) * qg)
        if slot < SLOT_AV:
            return rope(_rms(t) * kg)
        if SLOT_RQ <= slot < SLOT_RK:
            return rope(t)
        if SLOT_RK <= slot < SLOT_RV:
            return rope(t * (RET_DIM ** -0.5))
        if SLOT_GQ <= slot < SLOT_GK:
            return t * (GLA_DK ** -0.5)
        return t

    for c0 in range(0, N_IN, PROJ_TN):
        c1 = min(c0 + PROJ_TN, N_IN)
        z = _dot(h, w_ref[0, :, c0:c1])
        for s0 in range(c0, c1, LANES):
            slot, n = s0 // LANES, min(LANES, c1 - s0)
            t = transform(slot, z[:, s0 - c0:s0 - c0 + n])
            if slot < SLOT_AK:
                vt_ref[slot] = t.T.astype(BF16)
            elif SLOT_AV <= slot < SLOT_RQ:
                vt_ref[ATT_Q_HEADS + slot - SLOT_AV] = t.T.astype(BF16)
            t = t.astype(BF16)
            if n < LANES:
                o_ref[slot] = jnp.zeros(o_ref.shape[1:], BF16)
                o_ref[slot, :, 0:n] = t
            else:
                o_ref[slot] = t


def _proj(x2, shift, scale, g, w, layer, cosf, sinf, qg, kg, *, tm, rows_per_mod, rope_tiles):
    m, d = x2.shape
    return pl.pallas_call(
        _proj_kernel,
        grid=(m // tm,),
        in_specs=[
            pl.BlockSpec((tm, d), lambda i: (i, 0)),
            pl.BlockSpec((1, 1, d), lambda i: ((i * tm) // rows_per_mod, 0, 0)),
            pl.BlockSpec((1, 1, d), lambda i: ((i * tm) // rows_per_mod, 0, 0)),
            pl.BlockSpec((1, d), lambda i: (0, 0)),
            pl.BlockSpec((1, d, N_IN), lambda i: (layer, 0, 0), pipeline_mode=pl.Buffered(1)),
            pl.BlockSpec((tm, LANES), lambda i: (i % rope_tiles, 0)),
            pl.BlockSpec((tm, LANES), lambda i: (i % rope_tiles, 0)),
            pl.BlockSpec((1, LANES), lambda i: (0, 0)),
            pl.BlockSpec((1, LANES), lambda i: (0, 0)),
        ],
        out_specs=[pl.BlockSpec((N_SLOTS, tm, LANES), lambda i: (0, i, 0)),
                   pl.BlockSpec((ATT_Q_HEADS + ATT_KV_HEADS, HEAD_DIM, tm), lambda i: (0, 0, i))],
        out_shape=[jax.ShapeDtypeStruct((N_SLOTS, m, LANES), BF16),
                   jax.ShapeDtypeStruct((ATT_Q_HEADS + ATT_KV_HEADS, HEAD_DIM, m), BF16)],
        compiler_params=_cparams(("arbitrary",)),
        name="norm_proj",
    )(x2, shift, scale, g, w, cosf, sinf, qg, kg)


ATT_KEY_BLOCK = 128


def _attn_kernel(*refs, n_kv, reps):
    n_cast = len(reps)
    q_ref = refs[0]
    kv_refs = refs[1:1 + 2 * n_kv]
    w_refs = refs[1 + 2 * n_kv:1 + 2 * n_kv + n_cast]
    o_ref = refs[1 + 2 * n_kv + n_cast]
    wo_refs = refs[2 + 2 * n_kv + n_cast:]
    step = (pl.program_id(0) * pl.num_programs(1) + pl.program_id(1)) * pl.num_programs(2) + pl.program_id(2)

    def convert(w_ref, wo_ref):
        if len(wo_ref.shape) == 3:
            tn = wo_ref.shape[2]
            for t in range(wo_ref.shape[0]):
                wo_ref[t] = w_ref[0, :, t * tn:(t + 1) * tn].astype(BF16)
        else:
            wo_ref[...] = w_ref[0].astype(BF16)

    for w_ref, wo_ref, rep in zip(w_refs, wo_refs, reps):
        if rep == 1:
            convert(w_ref, wo_ref)
        else:
            pl.when(step % rep == 0)(functools.partial(convert, w_ref, wo_ref))
    sub = min(ATT_Q_SUB, q_ref.shape[2])
    chains = [(g, t0) for t0 in range(0, q_ref.shape[2], sub) for g in range(GQA)]
    m, l, acc = {}, {}, {}
    units = [(s, j0, ch) for s in range(n_kv) for j0 in range(0, kv_refs[2 * s].shape[1], ATT_KEY_BLOCK)
             for ch in chains]

    def scores(unit):
        s, j0, (g, t0) = unit
        return _dot(kv_refs[2 * s][0, j0:j0 + ATT_KEY_BLOCK, :], q_ref[g, :, t0:t0 + sub])

    st_next = scores(units[0])
    for u, (s, j0, ch) in enumerate(units):
        st = st_next
        if u + 1 < len(units):
            st_next = scores(units[u + 1])
        vt = kv_refs[2 * s + 1][0, :, j0:j0 + ATT_KEY_BLOCK]
        mb = jnp.max(st, axis=0, keepdims=True)
        if ch not in m:
            m[ch] = mb
            p = jnp.exp2(st - mb)
            l[ch] = jnp.sum(p, axis=0, keepdims=True)
            acc[ch] = _dot(vt, p.astype(BF16))
        else:
            m_new = jnp.maximum(m[ch], mb)
            alpha = jnp.exp2(m[ch] - m_new)
            p = jnp.exp2(st - m_new)
            l[ch] = alpha * l[ch] + jnp.sum(p, axis=0, keepdims=True)
            acc[ch] = alpha * acc[ch] + _dot(vt, p.astype(BF16))
            m[ch] = m_new
    for g, t0 in chains:
        o_ref[t0:t0 + sub, g * HEAD_DIM:(g + 1) * HEAD_DIM] = (acc[g, t0] / l[g, t0]).T.astype(BF16)


def _attn(qt, kv_sources, *, batch, q_len, tq, casts=()):
    nq = q_len // tq
    steps = batch * ATT_KV_HEADS * nq
    in_specs = [pl.BlockSpec((GQA, HEAD_DIM, tq), lambda b, k, i: (k, 0, b * nq + i))]
    args = [qt]
    for z, vt, kv_len in kv_sources:
        in_specs.append(pl.BlockSpec((1, kv_len, HEAD_DIM), lambda b, k, i: (SLOT_AK + k, b, 0)))
        in_specs.append(pl.BlockSpec((1, HEAD_DIM, kv_len), lambda b, k, i: (ATT_Q_HEADS + k, 0, b)))
        args += [z, vt]
    out_specs = [pl.BlockSpec((tq, GQA * HEAD_DIM), lambda b, k, i: (b * nq + i, k))]
    out_shape = [jax.ShapeDtypeStruct((batch * q_len, ATT_Q_HEADS * HEAD_DIM), BF16)]
    step = lambda b, k, i: (b * ATT_KV_HEADS + k) * nq + i
    reps = []
    for w, layer, tile in casts:
        _, rows, cols = w.shape
        rep = next(r for r in (1, 2, 4, 8, 16) if rows % (steps // r) == 0 and rows // (steps // r) % 16 == 0)
        slab = rows // (steps // rep)
        reps.append(rep)
        in_specs.append(pl.BlockSpec((1, slab, cols),
                                     lambda b, k, i, rep=rep, layer=layer: (layer, step(b, k, i) // rep, 0)))
        args.append(w)
        if tile is None:
            out_specs.append(pl.BlockSpec((slab, cols), lambda b, k, i, rep=rep: (step(b, k, i) // rep, 0)))
            out_shape.append(jax.ShapeDtypeStruct((rows, cols), BF16))
        else:
            out_specs.append(pl.BlockSpec((cols // tile, slab, tile),
                                          lambda b, k, i, rep=rep: (0, step(b, k, i) // rep, 0)))
            out_shape.append(jax.ShapeDtypeStruct((cols // tile, rows, tile), BF16))
    outs = pl.pallas_call(
        functools.partial(_attn_kernel, n_kv=len(kv_sources), reps=tuple(reps)),
        grid=(batch, ATT_KV_HEADS, nq),
        in_specs=in_specs,
        out_specs=out_specs,
        out_shape=out_shape,
        compiler_params=_cparams(("arbitrary", "arbitrary", "arbitrary")),
        name="attention",
    )(*args)
    return outs[0], list(outs[1:])


def _ret_kernel(dec_ref, qc_ref, kc_ref, vc_ref, gc_ref, ql_ref, kl_ref, vl_ref, gl_ref, ng_ref,
                oc_ref, ol_ref, kv_ref, st_ref, *, nc_ctx, nc_lat):
    h = pl.program_id(1)
    C = RET_CHUNK
    lg_f, lg_b = dec_ref[0, h], dec_ref[1, h]
    rel = lax.broadcasted_iota(jnp.int32, (C, C), 0) - lax.broadcasted_iota(jnp.int32, (C, C), 1)
    dmat = (jnp.where(rel >= 0, jnp.exp(lg_f * jnp.maximum(rel, 0).astype(F32)), 0.0)
            + jnp.where(rel <= 0, jnp.exp(lg_b * jnp.maximum(-rel, 0).astype(F32)), 0.0))
    row = lax.broadcasted_iota(jnp.int32, (C, 1), 0).astype(F32)
    qd_f, qd_b = jnp.exp(lg_f * (row + 1.0)), jnp.exp(lg_b * (C - row))
    kd_f, kd_b = jnp.exp(lg_f * (C - 1.0 - row)), jnp.exp(lg_b * row)
    cd_f, cd_b = jnp.exp(lg_f * C), jnp.exp(lg_b * C)

    chunks = [(qc_ref, kc_ref, vc_ref, gc_ref, oc_ref, n) for n in range(nc_ctx)]
    chunks += [(ql_ref, kl_ref, vl_ref, gl_ref, ol_ref, n) for n in range(nc_lat)]
    order_b = list(range(nc_ctx - 1, -1, -1)) + list(range(nc_ctx + nc_lat - 1, nc_ctx - 1, -1))

    def decayed(x_ref, n, dec_f, dec_b):
        x = x_ref[0, n * C:(n + 1) * C, :].astype(F32)
        return jnp.concatenate([(x * dec_f).astype(BF16), (x * dec_b).astype(BF16)], axis=1)

    for c, (_, k_ref, v_ref, _, _, n) in enumerate(chunks):
        kv_ref[c] = _dot_tn(decayed(k_ref, n, kd_f, kd_b), v_ref[0, n * C:(n + 1) * C, :])

    s = jnp.zeros((C, RET_DIM), F32)
    for c in range(len(chunks)):
        st_ref[c, 0:C, :] = s.astype(BF16)
        s = cd_f * s + kv_ref[c, 0:C, :]
    s = jnp.zeros((C, RET_DIM), F32)
    for c in order_b:
        st_ref[c, C:2 * C, :] = s.astype(BF16)
        s = cd_b * s + kv_ref[c, C:2 * C, :]

    for c, (q_ref, k_ref, v_ref, g_ref, o_ref, n) in enumerate(chunks):
        sl = slice(n * C, (n + 1) * C)
        v = v_ref[0, sl, :]
        att = _dot_nt(q_ref[0, sl, :], k_ref[0, sl, :]) * dmat
        o = _dot(att.astype(BF16), v) + _dot(decayed(q_ref, n, qd_f, qd_b), st_ref[c])
        y = _rms(o) * ng_ref[...]
        o_ref[sl, :] = (y * _silu(g_ref[0, sl, :].astype(F32))).astype(BF16)


def _ret(zc, zl, decay, ng, *, batch, ctx_len, lat_len):
    def slot(base, n):
        return pl.BlockSpec((1, n, RET_DIM), lambda b, h: (base + h, b, 0))

    n_chunks = (ctx_len + lat_len) // RET_CHUNK
    specs = [pl.BlockSpec(memory_space=pltpu.SMEM)]
    specs += [slot(s, ctx_len) for s in (SLOT_RQ, SLOT_RK, SLOT_RV, SLOT_RG)]
    specs += [slot(s, lat_len) for s in (SLOT_RQ, SLOT_RK, SLOT_RV, SLOT_RG)]
    specs += [pl.BlockSpec((1, RET_DIM), lambda b, h: (0, 0))]
    return pl.pallas_call(
        functools.partial(_ret_kernel, nc_ctx=ctx_len // RET_CHUNK, nc_lat=lat_len // RET_CHUNK),
        grid=(batch, RET_HEADS),
        in_specs=specs,
        out_specs=[pl.BlockSpec((ctx_len, RET_DIM), lambda b, h: (b, h)),
                   pl.BlockSpec((lat_len, RET_DIM), lambda b, h: (b, h))],
        out_shape=[jax.ShapeDtypeStruct((batch * ctx_len, RET_HEADS * RET_DIM), BF16),
                   jax.ShapeDtypeStruct((batch * lat_len, RET_HEADS * RET_DIM), BF16)],
        scratch_shapes=[pltpu.VMEM((n_chunks, 2 * RET_DIM, RET_DIM), F32),
                        pltpu.VMEM((n_chunks, 2 * RET_DIM, RET_DIM), BF16)],
        compiler_params=_cparams(("arbitrary", "arbitrary")),
        name="retention",
    )(decay, zc, zc, zc, zc, zl, zl, zl, zl, ng)


GLA_SLAB = 256


def _gla_level_table():
    i = np.arange(GLA_CHUNK)[:, None]
    j = np.arange(GLA_CHUNK)[None, :]
    x = np.maximum(i ^ j, 1)
    lvl = np.floor(np.log2(x)).astype(np.int32)
    lvl = np.where(i == j, GLA_LEVELS, np.where(i > j, lvl, -1)).astype(np.int32)
    return np.concatenate([lvl, lvl.T], axis=1)


def _gla_role_table():
    t = np.arange(GLA_SLAB)
    fwd = np.arange(LANES)[None, :] < GLA_DK
    return np.stack([np.where(((t[:, None] >> lev) & 1).astype(bool) == fwd, LOG2_E, -LOG2_E)
                     for lev in range(GLA_LEVELS)]).astype(np.float32)


def _bcast_block_row(x, blk, r):
    rows = x.shape[0]
    if blk >= 8:
        x3 = x.reshape(rows // blk, blk, LANES)
        return jnp.broadcast_to(x3[:, r:r + 1, :], x3.shape).reshape(rows, LANES)
    pos = lax.broadcasted_iota(jnp.int32, x.shape, 0) & (blk - 1)
    out = x
    for p in range(blk):
        if p != r:
            out = jnp.where(pos == p, pltpu.roll(x, (p - r) % rows, 0), out)
    return out


def _gla_kernel(lvl_ref, role_ref, bdm_ref, up_ref, gb_ref, ng_ref,
                qc_ref, kc_ref, vc_ref, gc_ref, ac_ref, ql_ref, kl_ref, vl_ref, gl_ref, al_ref,
                oc_ref, ol_ref, *scratch, ctx_len, lat_len):
    C = GLA_CHUNK
    streams = ((qc_ref, kc_ref, vc_ref, ac_ref, 0, ctx_len), (ql_ref, kl_ref, vl_ref, al_ref, ctx_len, lat_len))
    nc_ctx, nc_lat = ctx_len // C, lat_len // C
    n_chunks = nc_ctx + nc_lat
    cpos = lax.broadcasted_iota(jnp.int32, (GLA_SLAB, LANES), 0) & (C - 1)
    fwd = lax.broadcasted_iota(jnp.int32, (GLA_SLAB, LANES), 1) < GLA_DK
    lvl = lvl_ref[...]

    passes = []
    for hh in range(2):
        own = fwd if hh == 0 else jnp.logical_not(fwd)
        xs_ref, qd_ref, xst_ref, qh_ref, tot_ref, vv_ref, kv_ref, st_ref = (r.at[hh] for r in scratch)

        def prep(i, between, q_ref, k_ref, v_ref, a_ref, off, hh=hh, own=own, xs_ref=xs_ref, qd_ref=qd_ref,
                 xst_ref=xst_ref, qh_ref=qh_ref, tot_ref=tot_ref, vv_ref=vv_ref, kv_ref=kv_ref):
            between = list(between)
            src = _rows(i * GLA_SLAB, GLA_SLAB, GLA_SLAB)
            dst = _rows(off + i * GLA_SLAB, GLA_SLAB, C)
            chunk0 = off // C + i * (GLA_SLAB // C)

            def store_keys(lev, x):
                y = jnp.concatenate([x[c * C:(c + 1) * C] for c in range(GLA_SLAB // C) for _ in range(2)], axis=0)
                yt = y.T.astype(BF16) * bdm_ref[...]
                for c in range(GLA_SLAB // C):
                    xst_ref[lev, chunk0 + c] = yt[:, c * 2 * C:(c + 1) * 2 * C]

            logit = _dot(a_ref[0, src, :], up_ref[hh]) + gb_ref[hh]
            la = (jnp.minimum(logit, 0.0) - jnp.log(1.0 + jnp.exp(-jnp.abs(logit)))) * (1.0 / GLA_TAU)
            pre = la
            for step in range(GLA_LEVELS):
                sh = 1 << step
                pre = pre + jnp.where(cpos >= sh, pltpu.roll(pre, sh, 0), 0.0)
            tot = _bcast_block_row(pre, C, C - 1)
            cum = jnp.where(fwd, pre, tot - pre + la)
            q2 = q_ref[0, src, :].astype(F32)
            k2 = k_ref[0, src, :].astype(F32)
            q = jnp.where(own, q2, pltpu.roll(q2, GLA_DK, 1))
            k = jnp.where(own, k2, pltpu.roll(k2, GLA_DK, 1))
            qd_ref[dst, :] = q.astype(BF16)
            store_keys(GLA_LEVELS, k)
            qh_ref[dst, :] = (q * jnp.exp(cum)).astype(BF16)
            kh = (k * jnp.exp(tot - cum)).astype(BF16)
            tot_ref[dst, :] = tot
            v = v_ref[hh, src, :]
            vv_ref[dst, :] = v
            for c in range(GLA_SLAB // C):
                kv_ref[chunk0 + c] = _dot_tn(v[c * C:(c + 1) * C], kh[c * C:(c + 1) * C])
            for lev in range(GLA_LEVELS):
                if between and lev % 2 == 0:
                    between.pop(0)()
                s = 1 << lev
                d = pre - _bcast_block_row(pre, 2 * s, s - 1)
                dist = jnp.where(fwd, d, la - d)
                role = role_ref[lev]
                x = jnp.where(role > 0, q, k) * jnp.exp2(dist * role)
                xs_ref[lev, dst, :] = x.astype(BF16)
                store_keys(lev, x)
            for item in between:
                item()

        def states(kv_ref=kv_ref, st_ref=st_ref, tot_ref=tot_ref):
            def step(c, s, lanes):
                st_ref[c, :, lanes] = s[:, lanes].astype(BF16)
                return s * jnp.exp(tot_ref[pl.ds(c * C, 1), :]) + kv_ref[c]

            def body(n, carry):
                s_f, s_b = carry
                c_b = jnp.where(n < nc_ctx, nc_ctx - 1 - n, n_chunks + nc_ctx - 1 - n)
                return step(n, s_f, slice(0, GLA_DK)), step(c_b, s_b, slice(GLA_DK, LANES))

            zero = jnp.zeros((GLA_DV, LANES), F32)
            lax.fori_loop(0, n_chunks, body, (zero, zero))

        def out_chunk(c, g_ref, o_ref, local, hh=hh, xs_ref=xs_ref, qd_ref=qd_ref, xst_ref=xst_ref, qh_ref=qh_ref,
                      vv_ref=vv_ref, st_ref=st_ref):
            rows = _rows(c * C, C, C)
            a = jnp.where(lvl == GLA_LEVELS, _dot(qd_ref[rows, :], xst_ref[GLA_LEVELS, c]), 0.0)
            for lev in range(GLA_LEVELS):
                a = jnp.where(lvl == lev, _dot(xs_ref[lev, rows, :], xst_ref[lev, c]), a)
            v = vv_ref[rows, :]
            o = _dot(a.astype(BF16), jnp.concatenate([v, v], axis=0)) + _dot_nt(qh_ref[rows, :], st_ref[c])
            y = _rms(o) * ng_ref[...]
            out_rows = _rows(local * C, C, C)
            y = y * _silu(g_ref[hh, out_rows, :].astype(F32))
            o_ref[out_rows, hh * GLA_DV:(hh + 1) * GLA_DV] = y.astype(BF16)

        passes.append((prep, states, out_chunk))

    per_slab = GLA_SLAB // C
    (prep0, states0, out0), (prep1, states1, out1) = passes

    def slab(prep, out, stream, i, g_ref, o_ref):
        q_ref, k_ref, v_ref, a_ref, off, _ = stream
        chunks = []
        if out is not None:
            chunks = [functools.partial(out, off // C + i * per_slab + j, g_ref, o_ref, i * per_slab + j)
                      for j in range(per_slab)]
        if prep is not None:
            prep(i, chunks, q_ref=q_ref, k_ref=k_ref, v_ref=v_ref, a_ref=a_ref, off=off)
        else:
            for item in chunks:
                item()

    def run(prep, out):
        for stream, g_ref, o_ref in ((streams[0], gc_ref, oc_ref), (streams[1], gl_ref, ol_ref)):
            n_slabs = stream[5] // GLA_SLAB
            if n_slabs == 1:
                slab(prep, out, stream, 0, g_ref, o_ref)
            else:
                def body(i, carry, stream=stream, g_ref=g_ref, o_ref=o_ref):
                    slab(prep, out, stream, i, g_ref, o_ref)
                    return carry
                lax.fori_loop(0, n_slabs, body, 0, unroll=2 if n_slabs % 2 == 0 else 1)

    run(prep0, None)
    states0()
    run(prep1, out0)
    states1()
    run(None, out1)


def _gla_gate_params(gate_up, gate_b):
    r = GLA_GATE_RANK
    gu = gate_up.astype(BF16).reshape(2, r, GLA_HEADS, GLA_DK).transpose(2, 0, 1, 3)
    up = jnp.zeros((GLA_HEADS, LANES, 2 * GLA_DK), BF16)
    up = up.at[:, 0:r, 0:GLA_DK].set(gu[:, 0]).at[:, r:2 * r, GLA_DK:].set(gu[:, 1])
    gb = gate_b.reshape(2, GLA_HEADS, GLA_DK).transpose(1, 0, 2).reshape(GLA_HEADS, 1, 2 * GLA_DK)
    return up, gb


def _gla(zc, zl, up_heads, gate_b_heads, ng, *, batch, ctx_len, lat_len):
    total = ctx_len + lat_len
    n_chunks = total // GLA_CHUNK

    def pair(base, n):
        return pl.BlockSpec((1, n, LANES), lambda b, p: (base + p, b, 0))

    def two(base, n):
        return pl.BlockSpec((2, n, LANES), lambda b, p: (base // 2 + p, b, 0))

    def one(n):
        return pl.BlockSpec((1, n, LANES), lambda b, p: (SLOT_GA, b, 0))

    role = _gla_role_table()
    feat_fwd = np.arange(LANES)[:, None] < GLA_DK
    key_fwd = (np.arange(2 * GLA_SLAB)[None, :] % (2 * GLA_CHUNK)) < GLA_CHUNK
    bd_mask = (feat_fwd == key_fwd).astype(np.float32)
    specs = [
        pl.BlockSpec((GLA_CHUNK, 2 * GLA_CHUNK), lambda b, p: (0, 0)),
        pl.BlockSpec(role.shape, lambda b, p: (0, 0, 0)),
        pl.BlockSpec(bd_mask.shape, lambda b, p: (0, 0)),
        pl.BlockSpec((2, LANES, LANES), lambda b, p: (p, 0, 0)),
        pl.BlockSpec((2, 1, LANES), lambda b, p: (p, 0, 0)),
        pl.BlockSpec((1, GLA_DV), lambda b, p: (0, 0)),
    ]
    for n in (ctx_len, lat_len):
        specs += [pair(SLOT_GQ, n), pair(SLOT_GK, n), two(SLOT_GV, n), two(SLOT_GR, n), one(n)]
    return pl.pallas_call(
        functools.partial(_gla_kernel, ctx_len=ctx_len, lat_len=lat_len),
        grid=(batch, GLA_HEADS // 2),
        in_specs=specs,
        out_specs=[pl.BlockSpec((ctx_len, 2 * GLA_DV), lambda b, p: (b, p)),
                   pl.BlockSpec((lat_len, 2 * GLA_DV), lambda b, p: (b, p))],
        out_shape=[jax.ShapeDtypeStruct((batch * ctx_len, GLA_HEADS * GLA_DV), BF16),
                   jax.ShapeDtypeStruct((batch * lat_len, GLA_HEADS * GLA_DV), BF16)],
        scratch_shapes=[
            pltpu.VMEM((2, GLA_LEVELS, total, LANES), BF16),
            pltpu.VMEM((2, total, LANES), BF16),
            pltpu.VMEM((2, GLA_LEVELS + 1, n_chunks, LANES, 2 * GLA_CHUNK), BF16),
            pltpu.VMEM((2, total, LANES), BF16),
            pltpu.VMEM((2, total, LANES), F32),
            pltpu.VMEM((2, total, GLA_DV), BF16),
            pltpu.VMEM((2, n_chunks, GLA_DV, LANES), F32),
            pltpu.VMEM((2, n_chunks, GLA_DV, LANES), BF16),
        ],
        compiler_params=_cparams(("arbitrary", "arbitrary")),
        name="gla",
    )(jnp.asarray(_gla_level_table()), jnp.asarray(role), jnp.asarray(bd_mask, BF16), up_heads, gate_b_heads, ng,
      zc, zc, zc, zc, zc, zl, zl, zl, zl, zl)


def _outproj_kernel(att_ref, ret_ref, gla_ref, w_ref, x_ref, gate_ref, sh_ref, sc_ref, g_ref, xo_ref, ho_ref):
    na, nr = att_ref.shape[1], ret_ref.shape[1]
    y = (_dot(att_ref[...], w_ref[0, 0:na, :]) + _dot(ret_ref[...], w_ref[0, na:na + nr, :])
         + _dot(gla_ref[...], w_ref[0, na + nr:, :]))
    xn = x_ref[...] + gate_ref[0] * y
    xo_ref[...] = xn
    ho_ref[...] = (_rms(xn) * g_ref[...] * (1.0 + sc_ref[0]) + sh_ref[0]).astype(BF16)


def _outproj(att, ret, gla, w, layer, x2, gate, shift, scale, g, *, tm, rows_per_mod):
    m, d = x2.shape

    def rows(n):
        return pl.BlockSpec((tm, n), lambda i: (i, 0))

    def mod():
        return pl.BlockSpec((1, 1, d), lambda i: ((i * tm) // rows_per_mod, 0, 0))

    return pl.pallas_call(
        _outproj_kernel,
        grid=(m // tm,),
        in_specs=[rows(att.shape[1]), rows(ret.shape[1]), rows(gla.shape[1]),
                  pl.BlockSpec((1,) + w.shape[1:], lambda i: (layer, 0, 0), pipeline_mode=pl.Buffered(1)),
                  rows(d), mod(), mod(), mod(),
                  pl.BlockSpec((1, d), lambda i: (0, 0))],
        out_specs=[rows(d), rows(d)],
        out_shape=[jax.ShapeDtypeStruct((m, d), F32), jax.ShapeDtypeStruct((m, d), BF16)],
        compiler_params=_cparams(("arbitrary",)),
        name="out_proj",
    )(att, ret, gla, w, x2, gate, shift, scale, g)


FFN_HALO = 16
FFN_TF = 512
FFN_TM = 512


def _ffn_kernel(h_ref, hn_ref, hp_ref, wa_ref, wv_ref, cw_ref, cb_ref, wd_ref, x_ref, gate_ref, fg_ref,
                o_ref, hs_ref, *, tm, seq_len, final_norm):
    i = pl.program_id(0)
    f = pl.program_id(1)
    half = FFN_HALO // 2
    ext = tm + FFN_HALO

    @pl.when(f == 0)
    def _():
        hs_ref[0:tm, :] = h_ref[...]
        hs_ref[tm:tm + half, :] = hn_ref[0:half, :]
        hs_ref[tm + half:ext, :] = hp_ref[half:, :]
        o_ref[...] = jnp.zeros_like(o_ref)

    a = _dot(hs_ref[...], wa_ref[0])
    pos = (i * tm + lax.broadcasted_iota(jnp.int32, (ext, 1), 0)) % seq_len
    prev = jnp.where(pos == 0, 0.0, pltpu.roll(a, 1, 0))
    nxt = jnp.where(pos == seq_len - 1, 0.0, pltpu.roll(a, ext - 1, 0))
    conv = prev * cw_ref[0:1, :] + a * cw_ref[1:2, :] + nxt * cw_ref[2:3, :] + cb_ref[...]
    v = _dot(h_ref[...], wv_ref[0])
    u = (_silu(conv[0:tm]) * v).astype(BF16)
    o_ref[...] += _dot(u, wd_ref[0])

    @pl.when(f == pl.num_programs(1) - 1)
    def _():
        xn = x_ref[...] + gate_ref[0] * o_ref[...]
        if final_norm:
            xn = _rms(xn) * fg_ref[...]
        o_ref[...] = xn


def _ffn(h2, w_up, conv_w, conv_b, w_down, x2, gate, fg, *, tm, seq_len, rows_per_mod, final_norm):
    m, d = x2.shape
    tf = w_up.shape[2]
    nf = D_FF // tf
    hb = tm // FFN_HALO
    last = m // FFN_HALO - 1
    return pl.pallas_call(
        functools.partial(_ffn_kernel, tm=tm, seq_len=seq_len, final_norm=final_norm),
        grid=(m // tm, nf),
        in_specs=[
            pl.BlockSpec((tm, d), lambda i, f: (i, 0)),
            pl.BlockSpec((FFN_HALO, d), lambda i, f: (jnp.minimum((i + 1) * hb, last), 0)),
            pl.BlockSpec((FFN_HALO, d), lambda i, f: (jnp.maximum(i * hb - 1, 0), 0)),
            pl.BlockSpec((1, d, tf), lambda i, f: (f, 0, 0)),
            pl.BlockSpec((1, d, tf), lambda i, f: (nf + f, 0, 0)),
            pl.BlockSpec((3, tf), lambda i, f: (0, f)),
            pl.BlockSpec((1, tf), lambda i, f: (0, f)),
            pl.BlockSpec((1, tf, d), lambda i, f: (0, f, 0)),
            pl.BlockSpec((tm, d), lambda i, f: (i, 0)),
            pl.BlockSpec((1, 1, d), lambda i, f: ((i * tm) // rows_per_mod, 0, 0)),
            pl.BlockSpec((1, d), lambda i, f: (0, 0)),
        ],
        out_specs=pl.BlockSpec((tm, d), lambda i, f: (i, 0)),
        out_shape=jax.ShapeDtypeStruct((m, d), F32),
        scratch_shapes=[pltpu.VMEM((tm + FFN_HALO, d), BF16)],
        compiler_params=_cparams(("arbitrary", "arbitrary")),
        name="conv_glu",
    )(h2, h2, h2, w_up, w_up, conv_w, conv_b, w_down, x2, gate, fg)


def _rope_tables(n_tokens):
    rows = n_tokens // GRID_W
    row = jnp.repeat(jnp.arange(rows, dtype=F32), GRID_W)
    col = jnp.tile(jnp.arange(GRID_W, dtype=F32), rows)
    n_freq = HEAD_DIM // 4
    inv_freq = ROPE_THETA ** (-jnp.arange(n_freq, dtype=F32) / n_freq)
    ang = jnp.concatenate([row[:, None] * inv_freq, col[:, None] * inv_freq], axis=-1)
    cos, sin = jnp.cos(ang), jnp.sin(ang)
    return jnp.concatenate([cos, cos], axis=-1), jnp.concatenate([-sin, sin], axis=-1)


def kernel(x, c, ctx, c_ctx, ada_w, ada_b, norm1_g, w_in, q_norm_g, k_norm_g, ret_log_decay, ret_norm_g,
           gla_gate_up, gla_gate_b, gla_norm_g, w_out, norm2_g, w_up, conv_w, conv_b, w_down, final_norm_g):
    batch, lat_len, d = x.shape
    ctx_len = ctx.shape[1]
    depth = ada_w.shape[0]
    mod_rows = 16
    cc = jnp.concatenate([c, c_ctx[None], jnp.zeros((mod_rows - batch - 1, d), F32)], axis=0)
    mod = _ada(cc, ada_w, ada_b).reshape(depth, mod_rows, N_MOD, d)

    cos_l, sin_l = _rope_tables(lat_len)
    proj_tm = PROJ_TM
    cos_c = jnp.ones((proj_tm, LANES), F32)
    sin_c = jnp.zeros((proj_tm, LANES), F32)

    xl = x.reshape(batch * lat_len, d)
    xc = ctx.reshape(batch * ctx_len, d)
    row = lambda v: v.reshape(1, -1)
    w_in_b = w_in.astype(BF16)
    w_out_b = w_out[0].astype(BF16)[None]

    for l in range(depth):
        last = l == depth - 1
        ml = [mod[l, :batch, k].reshape(batch, 1, d) for k in range(N_MOD)]
        mc = [mod[l, batch, k].reshape(1, 1, d) for k in range(N_MOD)]
        up_heads, gate_b_heads = _gla_gate_params(gla_gate_up[l], gla_gate_b[l])

        zl, vtl = _proj(xl, ml[0], ml[1], row(norm1_g[l]), w_in_b, l, cos_l, sin_l, row(q_norm_g[l]), row(k_norm_g[l]),
                        tm=proj_tm, rows_per_mod=lat_len, rope_tiles=lat_len // proj_tm)
        zc, vtc = _proj(xc, mc[0], mc[1], row(norm1_g[l]), w_in_b, l, cos_c, sin_c, row(q_norm_g[l]), row(k_norm_g[l]),
                        tm=proj_tm, rows_per_mod=batch * ctx_len, rope_tiles=1)

        casts = [(w_up, l, FFN_TF), (w_down, l, None)]
        if not last:
            casts += [(w_out, l + 1, None)]
        att_l, cast = _attn(vtl, [(zc, vtc, ctx_len), (zl, vtl, lat_len)], batch=batch, q_len=lat_len, tq=ATT_TQ,
                            casts=casts)
        w_up_b, w_down_b = cast[0], cast[1][None]
        ret_c, ret_l = _ret(zc, zl, ret_log_decay[l], row(ret_norm_g[l]),
                            batch=batch, ctx_len=ctx_len, lat_len=lat_len)
        gla_c, gla_l = _gla(zc, zl, up_heads, gate_b_heads, row(gla_norm_g[l]),
                            batch=batch, ctx_len=ctx_len, lat_len=lat_len)

        xl, h2 = _outproj(att_l, ret_l, gla_l, w_out_b, 0, xl, ml[2], ml[3], ml[4], row(norm2_g[l]),
                          tm=OUTPROJ_TM, rows_per_mod=lat_len)
        xl = _ffn(h2, w_up_b, conv_w[l], row(conv_b[l]), w_down_b, xl, ml[5], row(final_norm_g),
                  tm=FFN_TM, seq_len=lat_len, rows_per_mod=lat_len, final_norm=last)

        if not last:
            att_c, _ = _attn(vtc, [(zc, vtc, ctx_len)], batch=batch, q_len=ctx_len, tq=ctx_len)
            xc, hc2 = _outproj(att_c, ret_c, gla_c, w_out_b, 0, xc, mc[2], mc[3], mc[4], row(norm2_g[l]),
                               tm=OUTPROJ_TM, rows_per_mod=batch * ctx_len)
            xc = _ffn(hc2, w_up_b, conv_w[l], row(conv_b[l]), w_down_b, xc, mc[5], row(final_norm_g),
                      tm=FFN_TM, seq_len=ctx_len, rows_per_mod=batch * ctx_len, final_norm=False)
            w_out_b = cast[2][None]

    return xl.reshape(batch, lat_len, d)
```

```python
import functools

import numpy as np
import jax
import jax.numpy as jnp
from jax import lax
from jax.experimental import pallas as pl
from jax.experimental.pallas import tpu as pltpu

F32 = jnp.float32
BF16 = jnp.bfloat16

HEAD_DIM = 128
GRID_W = 64
ATT_Q_HEADS = 8
ATT_KV_HEADS = 2
GQA = ATT_Q_HEADS // ATT_KV_HEADS
RET_HEADS = 4
RET_DIM = 128
RET_CHUNK = 128
GLA_HEADS = 4
GLA_DK = 64
GLA_DV = 128
GLA_GATE_RANK = 16
GLA_TAU = 16.0
GLA_CHUNK = 64
GLA_LEVELS = 6
D_FF = 5632
ROPE_THETA = 10000.0
N_MOD = 6
EPS = 1e-6

LOG2_E = 1.4426950408889634

LANES = 128
N_IN = 5152
PROJ_TN = 256
N_SLOTS = -(-N_IN // LANES)
SLOT_AQ, SLOT_AK, SLOT_AV = 0, 8, 10
SLOT_RQ, SLOT_RK, SLOT_RV, SLOT_RG = 12, 16, 20, 24
SLOT_GQ, SLOT_GK, SLOT_GV, SLOT_GR, SLOT_GA = 28, 30, 32, 36, 40

VMEM_LIMIT = 52 * 1024 * 1024
ADA_TN = 1024
PROJ_TM = 512
ATT_TQ = 1024
ATT_Q_SUB = 256
OUTPROJ_TM = 512

_NT = (((1,), (1,)), ((), ()))
_TN = (((0,), (0,)), ((), ()))


def _cparams(sem, vmem_limit=VMEM_LIMIT):
    return pltpu.CompilerParams(dimension_semantics=sem, vmem_limit_bytes=vmem_limit)


def _dot(a, b):
    return jnp.dot(a, b, preferred_element_type=F32)


def _dot_nt(a, b):
    return lax.dot_general(a, b, _NT, preferred_element_type=F32)


def _dot_tn(a, b):
    return lax.dot_general(a, b, _TN, preferred_element_type=F32)


def _rows(start, size, align):
    if isinstance(start, int):
        return pl.ds(start, size)
    return pl.ds(pl.multiple_of(start, align), size)


def _rms(t):
    return t * lax.rsqrt(jnp.mean(t * t, axis=-1, keepdims=True) + EPS)


def _silu(t):
    return t * jax.nn.sigmoid(t)


def _ada_kernel(c_ref, w_ref, b_ref, o_ref):
    s = _silu(c_ref[...]).astype(BF16)
    o_ref[0] = _dot(s, w_ref[0].astype(BF16)) + b_ref[0]


def _ada(cc, ada_w, ada_b):
    depth, d, n = ada_w.shape
    rows = cc.shape[0]
    tn = ADA_TN
    return pl.pallas_call(
        _ada_kernel,
        grid=(depth, n // tn),
        in_specs=[
            pl.BlockSpec((rows, d), lambda l, j: (0, 0)),
            pl.BlockSpec((1, d, tn), lambda l, j: (l, 0, j)),
            pl.BlockSpec((1, 1, tn), lambda l, j: (l, 0, j)),
        ],
        out_specs=pl.BlockSpec((1, rows, tn), lambda l, j: (l, 0, j)),
        out_shape=jax.ShapeDtypeStruct((depth, rows, n), F32),
        compiler_params=_cparams(("arbitrary", "arbitrary")),
        name="ada_mod",
    )(cc, ada_w, ada_b.reshape(depth, 1, n))


def _proj_kernel(x_ref, sh_ref, sc_ref, g_ref, w_ref, cos_ref, sin_ref, qg_ref, kg_ref, o_ref, vt_ref):
    y = _rms(x_ref[...]) * g_ref[...]
    h = (y * (1.0 + sc_ref[0]) + sh_ref[0]).astype(BF16)
    cos, sin = cos_ref[...], sin_ref[...]
    qg = qg_ref[...] * (HEAD_DIM ** -0.5 * LOG2_E)
    kg = kg_ref[...]

    def rope(t):
        return t * cos + pltpu.roll(t, HEAD_DIM // 2, 1) * sin

    def transform(slot, t):
        if slot < SLOT_AK:
            return rope(_rms(t) * qg)
        if slot < SLOT_AV:
            return rope(_rms(t) * kg)
        if SLOT_RQ <= slot < SLOT_RK:
            return rope(t)
        if SLOT_RK <= slot < SLOT_RV:
            return rope(t * (RET_DIM ** -0.5))
        if SLOT_GQ <= slot < SLOT_GK:
            return t * (GLA_DK ** -0.5)
        return t

    for c0 in range(0, N_IN, PROJ_TN):
        c1 = min(c0 + PROJ_TN, N_IN)
        z = _dot(h, w_ref[0, :, c0:c1])
        for s0 in range(c0, c1, LANES):
            slot, n = s0 // LANES, min(LANES, c1 - s0)
            t = transform(slot, z[:, s0 - c0:s0 - c0 + n])
            if slot < SLOT_AK:
                vt_ref[slot] = t.T.astype(BF16)
            elif SLOT_AV <= slot < SLOT_RQ:
                vt_ref[ATT_Q_HEADS + slot - SLOT_AV] = t.T.astype(BF16)
            t = t.astype(BF16)
            if n < LANES:
                o_ref[slot] = jnp.zeros(o_ref.shape[1:], BF16)
                o_ref[slot, :, 0:n] = t
            else:
                o_ref[slot] = t


def _proj(x2, shift, scale, g, w, layer, cosf, sinf, qg, kg, *, tm, rows_per_mod, rope_tiles):
    m, d = x2.shape
    return pl.pallas_call(
        _proj_kernel,
        grid=(m // tm,),
        in_specs=[
            pl.BlockSpec((tm, d), lambda i: (i, 0)),
            pl.BlockSpec((1, 1, d), lambda i: ((i * tm) // rows_per_mod, 0, 0)),
            pl.BlockSpec((1, 1, d), lambda i: ((i * tm) // rows_per_mod, 0, 0)),
            pl.BlockSpec((1, d), lambda i: (0, 0)),
            pl.BlockSpec((1, d, N_IN), lambda i: (layer, 0, 0), pipeline_mode=pl.Buffered(1)),
            pl.BlockSpec((tm, LANES), lambda i: (i % rope_tiles, 0)),
            pl.BlockSpec((tm, LANES), lambda i: (i % rope_tiles, 0)),
            pl.BlockSpec((1, LANES), lambda i: (0, 0)),
            pl.BlockSpec((1, LANES), lambda i: (0, 0)),
        ],
        out_specs=[pl.BlockSpec((N_SLOTS, tm, LANES), lambda i: (0, i, 0)),
                   pl.BlockSpec((ATT_Q_HEADS + ATT_KV_HEADS, HEAD_DIM, tm), lambda i: (0, 0, i))],
        out_shape=[jax.ShapeDtypeStruct((N_SLOTS, m, LANES), BF16),
                   jax.ShapeDtypeStruct((ATT_Q_HEADS + ATT_KV_HEADS, HEAD_DIM, m), BF16)],
        compiler_params=_cparams(("arbitrary",)),
        name="norm_proj",
    )(x2, shift, scale, g, w, cosf, sinf, qg, kg)


ATT_KEY_BLOCK = 128


def _attn_kernel(*refs, n_kv, reps):
    n_cast = len(reps)
    q_ref = refs[0]
    kv_refs = refs[1:1 + 2 * n_kv]
    w_refs = refs[1 + 2 * n_kv:1 + 2 * n_kv + n_cast]
    o_ref = refs[1 + 2 * n_kv + n_cast]
    wo_refs = refs[2 + 2 * n_kv + n_cast:]
    step = (pl.program_id(0) * pl.num_programs(1) + pl.program_id(1)) * pl.num_programs(2) + pl.program_id(2)

    def convert(w_ref, wo_ref):
        if len(wo_ref.shape) == 3:
            tn = wo_ref.shape[2]
            for t in range(wo_ref.shape[0]):
                wo_ref[t] = w_ref[0, :, t * tn:(t + 1) * tn].astype(BF16)
        else:
            wo_ref[...] = w_ref[0].astype(BF16)

    for w_ref, wo_ref, rep in zip(w_refs, wo_refs, reps):
        if rep == 1:
            convert(w_ref, wo_ref)
        else:
            pl.when(step % rep == 0)(functools.partial(convert, w_ref, wo_ref))
    sub = min(ATT_Q_SUB, q_ref.shape[2])
    chains = [(g, t0) for t0 in range(0, q_ref.shape[2], sub) for g in range(GQA)]
    m, l, acc = {}, {}, {}
    units = [(s, j0, ch) for s in range(n_kv) for j0 in range(0, kv_refs[2 * s].shape[1], ATT_KEY_BLOCK)
             for ch in chains]

    def scores(unit):
        s, j0, (g, t0) = unit
        return _dot(kv_refs[2 * s][0, j0:j0 + ATT_KEY_BLOCK, :], q_ref[g, :, t0:t0 + sub])

    st_next = scores(units[0])
    for u, (s, j0, ch) in enumerate(units):
        st = st_next
        if u + 1 < len(units):
            st_next = scores(units[u + 1])
        vt = kv_refs[2 * s + 1][0, :, j0:j0 + ATT_KEY_BLOCK]
        mb = jnp.max(st, axis=0, keepdims=True)
        if ch not in m:
            m[ch] = mb
            p = jnp.exp2(st - mb)
            l[ch] = jnp.sum(p, axis=0, keepdims=True)
            acc[ch] = _dot(vt, p.astype(BF16))
        else:
            m_new = jnp.maximum(m[ch], mb)
            alpha = jnp.exp2(m[ch] - m_new)
            p = jnp.exp2(st - m_new)
            l[ch] = alpha * l[ch] + jnp.sum(p, axis=0, keepdims=True)
            acc[ch] = alpha * acc[ch] + _dot(vt, p.astype(BF16))
            m[ch] = m_new
    for g, t0 in chains:
        o_ref[t0:t0 + sub, g * HEAD_DIM:(g + 1) * HEAD_DIM] = (acc[g, t0] / l[g, t0]).T.astype(BF16)


def _attn(qt, kv_sources, *, batch, q_len, tq, casts=()):
    nq = q_len // tq
    steps = batch * ATT_KV_HEADS * nq
    in_specs = [pl.BlockSpec((GQA, HEAD_DIM, tq), lambda b, k, i: (k, 0, b * nq + i))]
    args = [qt]
    for z, vt, kv_len in kv_sources:
        in_specs.append(pl.BlockSpec((1, kv_len, HEAD_DIM), lambda b, k, i: (SLOT_AK + k, b, 0)))
        in_specs.append(pl.BlockSpec((1, HEAD_DIM, kv_len), lambda b, k, i: (ATT_Q_HEADS + k, 0, b)))
        args += [z, vt]
    out_specs = [pl.BlockSpec((tq, GQA * HEAD_DIM), lambda b, k, i: (b * nq + i, k))]
    out_shape = [jax.ShapeDtypeStruct((batch * q_len, ATT_Q_HEADS * HEAD_DIM), BF16)]
    step = lambda b, k, i: (b * ATT_KV_HEADS + k) * nq + i
    reps = []
    for w, layer, tile in casts:
        _, rows, cols = w.shape
        rep = next(r for r in (1, 2, 4, 8, 16) if rows % (steps // r) == 0 and rows // (steps // r) % 16 == 0)
        slab = rows // (steps // rep)
        reps.append(rep)
        in_specs.append(pl.BlockSpec((1, slab, cols),
                                     lambda b, k, i, rep=rep, layer=layer: (layer, step(b, k, i) // rep, 0)))
        args.append(w)
        if tile is None:
            out_specs.append(pl.BlockSpec((slab, cols), lambda b, k, i, rep=rep: (step(b, k, i) // rep, 0)))
            out_shape.append(jax.ShapeDtypeStruct((rows, cols), BF16))
        else:
            out_specs.append(pl.BlockSpec((cols // tile, slab, tile),
                                          lambda b, k, i, rep=rep: (0, step(b, k, i) // rep, 0)))
            out_shape.append(jax.ShapeDtypeStruct((cols // tile, rows, tile), BF16))
    outs = pl.pallas_call(
        functools.partial(_attn_kernel, n_kv=len(kv_sources), reps=tuple(reps)),
        grid=(batch, ATT_KV_HEADS, nq),
        in_specs=in_specs,
        out_specs=out_specs,
        out_shape=out_shape,
        compiler_params=_cparams(("arbitrary", "arbitrary", "arbitrary")),
        name="attention",
    )(*args)
    return outs[0], list(outs[1:])


def _ret_kernel(dec_ref, qc_ref, kc_ref, vc_ref, gc_ref, ql_ref, kl_ref, vl_ref, gl_ref, ng_ref,
                oc_ref, ol_ref, kv_ref, st_ref, *, nc_ctx, nc_lat):
    h = pl.program_id(1)
    C = RET_CHUNK
    lg_f, lg_b = dec_ref[0, h], dec_ref[1, h]
    rel = lax.broadcasted_iota(jnp.int32, (C, C), 0) - lax.broadcasted_iota(jnp.int32, (C, C), 1)
    dmat = (jnp.where(rel >= 0, jnp.exp(lg_f * jnp.maximum(rel, 0).astype(F32)), 0.0)
            + jnp.where(rel <= 0, jnp.exp(lg_b * jnp.maximum(-rel, 0).astype(F32)), 0.0))
    row = lax.broadcasted_iota(jnp.int32, (C, 1), 0).astype(F32)
    qd_f, qd_b = jnp.exp(lg_f * (row + 1.0)), jnp.exp(lg_b * (C - row))
    kd_f, kd_b = jnp.exp(lg_f * (C - 1.0 - row)), jnp.exp(lg_b * row)
    cd_f, cd_b = jnp.exp(lg_f * C), jnp.exp(lg_b * C)

    chunks = [(qc_ref, kc_ref, vc_ref, gc_ref, oc_ref, n) for n in range(nc_ctx)]
    chunks += [(ql_ref, kl_ref, vl_ref, gl_ref, ol_ref, n) for n in range(nc_lat)]
    order_b = list(range(nc_ctx - 1, -1, -1)) + list(range(nc_ctx + nc_lat - 1, nc_ctx - 1, -1))

    def decayed(x_ref, n, dec_f, dec_b):
        x = x_ref[0, n * C:(n + 1) * C, :].astype(F32)
        return jnp.concatenate([(x * dec_f).astype(BF16), (x * dec_b).astype(BF16)], axis=1)

    for c, (_, k_ref, v_ref, _, _, n) in enumerate(chunks):
        kv_ref[c] = _dot_tn(decayed(k_ref, n, kd_f, kd_b), v_ref[0, n * C:(n + 1) * C, :])

    s = jnp.zeros((C, RET_DIM), F32)
    for c in range(len(chunks)):
        st_ref[c, 0:C, :] = s.astype(BF16)
        s = cd_f * s + kv_ref[c, 0:C, :]
    s = jnp.zeros((C, RET_DIM), F32)
    for c in order_b:
        st_ref[c, C:2 * C, :] = s.astype(BF16)
        s = cd_b * s + kv_ref[c, C:2 * C, :]

    for c, (q_ref, k_ref, v_ref, g_ref, o_ref, n) in enumerate(chunks):
        sl = slice(n * C, (n + 1) * C)
        v = v_ref[0, sl, :]
        att = _dot_nt(q_ref[0, sl, :], k_ref[0, sl, :]) * dmat
        o = _dot(att.astype(BF16), v) + _dot(decayed(q_ref, n, qd_f, qd_b), st_ref[c])
        y = _rms(o) * ng_ref[...]
        o_ref[sl, :] = (y * _silu(g_ref[0, sl, :].astype(F32))).astype(BF16)


def _ret(zc, zl, decay, ng, *, batch, ctx_len, lat_len):
    def slot(base, n):
        return pl.BlockSpec((1, n, RET_DIM), lambda b, h: (base + h, b, 0))

    n_chunks = (ctx_len + lat_len) // RET_CHUNK
    specs = [pl.BlockSpec(memory_space=pltpu.SMEM)]
    specs += [slot(s, ctx_len) for s in (SLOT_RQ, SLOT_RK, SLOT_RV, SLOT_RG)]
    specs += [slot(s, lat_len) for s in (SLOT_RQ, SLOT_RK, SLOT_RV, SLOT_RG)]
    specs += [pl.BlockSpec((1, RET_DIM), lambda b, h: (0, 0))]
    return pl.pallas_call(
        functools.partial(_ret_kernel, nc_ctx=ctx_len // RET_CHUNK, nc_lat=lat_len // RET_CHUNK),
        grid=(batch, RET_HEADS),
        in_specs=specs,
        out_specs=[pl.BlockSpec((ctx_len, RET_DIM), lambda b, h: (b, h)),
                   pl.BlockSpec((lat_len, RET_DIM), lambda b, h: (b, h))],
        out_shape=[jax.ShapeDtypeStruct((batch * ctx_len, RET_HEADS * RET_DIM), BF16),
                   jax.ShapeDtypeStruct((batch * lat_len, RET_HEADS * RET_DIM), BF16)],
        scratch_shapes=[pltpu.VMEM((n_chunks, 2 * RET_DIM, RET_DIM), F32),
                        pltpu.VMEM((n_chunks, 2 * RET_DIM, RET_DIM), BF16)],
        compiler_params=_cparams(("arbitrary", "arbitrary")),
        name="retention",
    )(decay, zc, zc, zc, zc, zl, zl, zl, zl, ng)


GLA_SLAB = 256


def _gla_level_table():
    i = np.arange(GLA_CHUNK)[:, None]
    j = np.arange(GLA_CHUNK)[None, :]
    x = np.maximum(i ^ j, 1)
    lvl = np.floor(np.log2(x)).astype(np.int32)
    lvl = np.where(i == j, GLA_LEVELS, np.where(i > j, lvl, -1)).astype(np.int32)
    return np.concatenate([lvl, lvl.T], axis=1)


def _gla_role_table():
    t = np.arange(GLA_SLAB)
    fwd = np.arange(LANES)[None, :] < GLA_DK
    return np.stack([np.where(((t[:, None] >> lev) & 1).astype(bool) == fwd, LOG2_E, -LOG2_E)
                     for lev in range(GLA_LEVELS)]).astype(np.float32)


def _bcast_block_row(x, blk, r):
    rows = x.shape[0]
    if blk >= 8:
        x3 = x.reshape(rows // blk, blk, LANES)
        return jnp.broadcast_to(x3[:, r:r + 1, :], x3.shape).reshape(rows, LANES)
    pos = lax.broadcasted_iota(jnp.int32, x.shape, 0) & (blk - 1)
    out = x
    for p in range(blk):
        if p != r:
            out = jnp.where(pos == p, pltpu.roll(x, (p - r) % rows, 0), out)
    return out


def _gla_kernel(lvl_ref, role_ref, bdm_ref, up_ref, gb_ref, ng_ref,
                qc_ref, kc_ref, vc_ref, gc_ref, ac_ref, ql_ref, kl_ref, vl_ref, gl_ref, al_ref,
                oc_ref, ol_ref, *scratch, ctx_len, lat_len):
    C = GLA_CHUNK
    streams = ((qc_ref, kc_ref, vc_ref, ac_ref, 0, ctx_len), (ql_ref, kl_ref, vl_ref, al_ref, ctx_len, lat_len))
    nc_ctx, nc_lat = ctx_len // C, lat_len // C
    n_chunks = nc_ctx + nc_lat
    cpos = lax.broadcasted_iota(jnp.int32, (GLA_SLAB, LANES), 0) & (C - 1)
    fwd = lax.broadcasted_iota(jnp.int32, (GLA_SLAB, LANES), 1) < GLA_DK
    lvl = lvl_ref[...]

    passes = []
    for hh in range(2):
        own = fwd if hh == 0 else jnp.logical_not(fwd)
        xs_ref, qd_ref, xst_ref, qh_ref, tot_ref, vv_ref, kv_ref, st_ref = (r.at[hh] for r in scratch)

        def prep(i, between, q_ref, k_ref, v_ref, a_ref, off, hh=hh, own=own, xs_ref=xs_ref, qd_ref=qd_ref,
                 xst_ref=xst_ref, qh_ref=qh_ref, tot_ref=tot_ref, vv_ref=vv_ref, kv_ref=kv_ref):
            between = list(between)
            src = _rows(i * GLA_SLAB, GLA_SLAB, GLA_SLAB)
            dst = _rows(off + i * GLA_SLAB, GLA_SLAB, C)
            chunk0 = off // C + i * (GLA_SLAB // C)

            def store_keys(lev, x):
                y = jnp.concatenate([x[c * C:(c + 1) * C] for c in range(GLA_SLAB // C) for _ in range(2)], axis=0)
                yt = y.T.astype(BF16) * bdm_ref[...]
                for c in range(GLA_SLAB // C):
                    xst_ref[lev, chunk0 + c] = yt[:, c * 2 * C:(c + 1) * 2 * C]

            logit = _dot(a_ref[0, src, :], up_ref[hh]) + gb_ref[hh]
            la = (jnp.minimum(logit, 0.0) - jnp.log(1.0 + jnp.exp(-jnp.abs(logit)))) * (1.0 / GLA_TAU)
            pre = la
            for step in range(GLA_LEVELS):
                sh = 1 << step
                pre = pre + jnp.where(cpos >= sh, pltpu.roll(pre, sh, 0), 0.0)
            tot = _bcast_block_row(pre, C, C - 1)
            cum = jnp.where(fwd, pre, tot - pre + la)
            q2 = q_ref[0, src, :].astype(F32)
            k2 = k_ref[0, src, :].astype(F32)
            q = jnp.where(own, q2, pltpu.roll(q2, GLA_DK, 1))
            k = jnp.where(own, k2, pltpu.roll(k2, GLA_DK, 1))
            qd_ref[dst, :] = q.astype(BF16)
            store_keys(GLA_LEVELS, k)
            qh_ref[dst, :] = (q * jnp.exp(cum)).astype(BF16)
            kh = (k * jnp.exp(tot - cum)).astype(BF16)
            tot_ref[dst, :] = tot
            v = v_ref[hh, src, :]
            vv_ref[dst, :] = v
            for c in range(GLA_SLAB // C):
                kv_ref[chunk0 + c] = _dot_tn(v[c * C:(c + 1) * C], kh[c * C:(c + 1) * C])
            for lev in range(GLA_LEVELS):
                if between and lev % 2 == 0:
                    between.pop(0)()
                s = 1 << lev
                d = pre - _bcast_block_row(pre, 2 * s, s - 1)
                dist = jnp.where(fwd, d, la - d)
                role = role_ref[lev]
                x = jnp.where(role > 0, q, k) * jnp.exp2(dist * role)
                xs_ref[lev, dst, :] = x.astype(BF16)
                store_keys(lev, x)
            for item in between:
                item()

        def states(kv_ref=kv_ref, st_ref=st_ref, tot_ref=tot_ref):
            def step(c, s, lanes):
                st_ref[c, :, lanes] = s[:, lanes].astype(BF16)
                return s * jnp.exp(tot_ref[pl.ds(c * C, 1), :]) + kv_ref[c]

            def body(n, carry):
                s_f, s_b = carry
                c_b = jnp.where(n < nc_ctx, nc_ctx - 1 - n, n_chunks + nc_ctx - 1 - n)
                return step(n, s_f, slice(0, GLA_DK)), step(c_b, s_b, slice(GLA_DK, LANES))

            zero = jnp.zeros((GLA_DV, LANES), F32)
            lax.fori_loop(0, n_chunks, body, (zero, zero))

        def out_chunk(c, g_ref, o_ref, local, hh=hh, xs_ref=xs_ref, qd_ref=qd_ref, xst_ref=xst_ref, qh_ref=qh_ref,
                      vv_ref=vv_ref, st_ref=st_ref):
            rows = _rows(c * C, C, C)
            a = jnp.where(lvl == GLA_LEVELS, _dot(qd_ref[rows, :], xst_ref[GLA_LEVELS, c]), 0.0)
            for lev in range(GLA_LEVELS):
                a = jnp.where(lvl == lev, _dot(xs_ref[lev, rows, :], xst_ref[lev, c]), a)
            v = vv_ref[rows, :]
            o = _dot(a.astype(BF16), jnp.concatenate([v, v], axis=0)) + _dot_nt(qh_ref[rows, :], st_ref[c])
            y = _rms(o) * ng_ref[...]
            out_rows = _rows(local * C, C, C)
            y = y * _silu(g_ref[hh, out_rows, :].astype(F32))
            o_ref[out_rows, hh * GLA_DV:(hh + 1) * GLA_DV] = y.astype(BF16)

        passes.append((prep, states, out_chunk))

    per_slab = GLA_SLAB // C
    (prep0, states0, out0), (prep1, states1, out1) = passes

    def slab(prep, out, stream, i, g_ref, o_ref):
        q_ref, k_ref, v_ref, a_ref, off, _ = stream
        chunks = []
        if out is not None:
            chunks = [functools.partial(out, off // C + i * per_slab + j, g_ref, o_ref, i * per_slab + j)
                      for j in range(per_slab)]
        if prep is not None:
            prep(i, chunks, q_ref=q_ref, k_ref=k_ref, v_ref=v_ref, a_ref=a_ref, off=off)
        else:
            for item in chunks:
                item()

    def run(prep, out):
        for stream, g_ref, o_ref in ((streams[0], gc_ref, oc_ref), (streams[1], gl_ref, ol_ref)):
            n_slabs = stream[5] // GLA_SLAB
            if n_slabs == 1:
                slab(prep, out, stream, 0, g_ref, o_ref)
            else:
                def body(i, carry, stream=stream, g_ref=g_ref, o_ref=o_ref):
                    slab(prep, out, stream, i, g_ref, o_ref)
                    return carry
                lax.fori_loop(0, n_slabs, body, 0, unroll=2 if n_slabs % 2 == 0 else 1)

    run(prep0, None)
    states0()
    run(prep1, out0)
    states1()
    run(None, out1)


def _gla_gate_params(gate_up, gate_b):
    r = GLA_GATE_RANK
    gu = gate_up.astype(BF16).reshape(2, r, GLA_HEADS, GLA_DK).transpose(2, 0, 1, 3)
    up = jnp.zeros((GLA_HEADS, LANES, 2 * GLA_DK), BF16)
    up = up.at[:, 0:r, 0:GLA_DK].set(gu[:, 0]).at[:, r:2 * r, GLA_DK:].set(gu[:, 1])
    gb = gate_b.reshape(2, GLA_HEADS, GLA_DK).transpose(1, 0, 2).reshape(GLA_HEADS, 1, 2 * GLA_DK)
    return up, gb


def _gla(zc, zl, up_heads, gate_b_heads, ng, *, batch, ctx_len, lat_len):
    total = ctx_len + lat_len
    n_chunks = total // GLA_CHUNK

    def pair(base, n):
        return pl.BlockSpec((1, n, LANES), lambda b, p: (base + p, b, 0))

    def two(base, n):
        return pl.BlockSpec((2, n, LANES), lambda b, p: (base // 2 + p, b, 0))

    def one(n):
        return pl.BlockSpec((1, n, LANES), lambda b, p: (SLOT_GA, b, 0))

    role = _gla_role_table()
    feat_fwd = np.arange(LANES)[:, None] < GLA_DK
    key_fwd = (np.arange(2 * GLA_SLAB)[None, :] % (2 * GLA_CHUNK)) < GLA_CHUNK
    bd_mask = (feat_fwd == key_fwd).astype(np.float32)
    specs = [
        pl.BlockSpec((GLA_CHUNK, 2 * GLA_CHUNK), lambda b, p: (0, 0)),
        pl.BlockSpec(role.shape, lambda b, p: (0, 0, 0)),
        pl.BlockSpec(bd_mask.shape, lambda b, p: (0, 0)),
        pl.BlockSpec((2, LANES, LANES), lambda b, p: (p, 0, 0)),
        pl.BlockSpec((2, 1, LANES), lambda b, p: (p, 0, 0)),
        pl.BlockSpec((1, GLA_DV), lambda b, p: (0, 0)),
    ]
    for n in (ctx_len, lat_len):
        specs += [pair(SLOT_GQ, n), pair(SLOT_GK, n), two(SLOT_GV, n), two(SLOT_GR, n), one(n)]
    return pl.pallas_call(
        functools.partial(_gla_kernel, ctx_len=ctx_len, lat_len=lat_len),
        grid=(batch, GLA_HEADS // 2),
        in_specs=specs,
        out_specs=[pl.BlockSpec((ctx_len, 2 * GLA_DV), lambda b, p: (b, p)),
                   pl.BlockSpec((lat_len, 2 * GLA_DV), lambda b, p: (b, p))],
        out_shape=[jax.ShapeDtypeStruct((batch * ctx_len, GLA_HEADS * GLA_DV), BF16),
                   jax.ShapeDtypeStruct((batch * lat_len, GLA_HEADS * GLA_DV), BF16)],
        scratch_shapes=[
            pltpu.VMEM((2, GLA_LEVELS, total, LANES), BF16),
            pltpu.VMEM((2, total, LANES), BF16),
            pltpu.VMEM((2, GLA_LEVELS + 1, n_chunks, LANES, 2 * GLA_CHUNK), BF16),
            pltpu.VMEM((2, total, LANES), BF16),
            pltpu.VMEM((2, total, LANES), F32),
            pltpu.VMEM((2, total, GLA_DV), BF16),
            pltpu.VMEM((2, n_chunks, GLA_DV, LANES), F32),
            pltpu.VMEM((2, n_chunks, GLA_DV, LANES), BF16),
        ],
        compiler_params=_cparams(("arbitrary", "arbitrary")),
        name="gla",
    )(jnp.asarray(_gla_level_table()), jnp.asarray(role), jnp.asarray(bd_mask, BF16), up_heads, gate_b_heads, ng,
      zc, zc, zc, zc, zc, zl, zl, zl, zl, zl)


def _outproj_kernel(att_ref, ret_ref, gla_ref, w_ref, x_ref, gate_ref, sh_ref, sc_ref, g_ref, xo_ref, ho_ref):
    na, nr = att_ref.shape[1], ret_ref.shape[1]
    y = (_dot(att_ref[...], w_ref[0, 0:na, :]) + _dot(ret_ref[...], w_ref[0, na:na + nr, :])
         + _dot(gla_ref[...], w_ref[0, na + nr:, :]))
    xn = x_ref[...] + gate_ref[0] * y
    xo_ref[...] = xn
    ho_ref[...] = (_rms(xn) * g_ref[...] * (1.0 + sc_ref[0]) + sh_ref[0]).astype(BF16)


def _outproj(att, ret, gla, w, layer, x2, gate, shift, scale, g, *, tm, rows_per_mod):
    m, d = x2.shape

    def rows(n):
        return pl.BlockSpec((tm, n), lambda i: (i, 0))

    def mod():
        return pl.BlockSpec((1, 1, d), lambda i: ((i * tm) // rows_per_mod, 0, 0))

    return pl.pallas_call(
        _outproj_kernel,
        grid=(m // tm,),
        in_specs=[rows(att.shape[1]), rows(ret.shape[1]), rows(gla.shape[1]),
                  pl.BlockSpec((1,) + w.shape[1:], lambda i: (layer, 0, 0), pipeline_mode=pl.Buffered(1)),
                  rows(d), mod(), mod(), mod(),
                  pl.BlockSpec((1, d), lambda i: (0, 0))],
        out_specs=[rows(d), rows(d)],
        out_shape=[jax.ShapeDtypeStruct((m, d), F32), jax.ShapeDtypeStruct((m, d), BF16)],
        compiler_params=_cparams(("arbitrary",)),
        name="out_proj",
    )(att, ret, gla, w, x2, gate, shift, scale, g)


FFN_HALO = 16
FFN_TF = 512
FFN_RING = 3
FFN_TM = 512


def _ffn_kernel(h_ref, hn_ref, hp_ref, wup_hbm, cw_ref, cb_ref, wdn_hbm, x_ref, gate_ref, fg_ref,
                o_ref, hs_ref, wa_buf, wv_buf, wd_buf, sem, *, tm, seq_len, final_norm):
    i = pl.program_id(0)
    f = pl.program_id(1)
    nf = pl.num_programs(1)
    tf = wa_buf.shape[2]
    half = FFN_HALO // 2
    ext = tm + FFN_HALO
    step = i * nf + f
    n_steps = pl.num_programs(0) * nf

    def tile_copies(s):
        ft = s % nf
        slot = s % FFN_RING
        return (pltpu.make_async_copy(wup_hbm.at[ft], wa_buf.at[slot], sem.at[slot, 0]),
                pltpu.make_async_copy(wup_hbm.at[nf + ft], wv_buf.at[slot], sem.at[slot, 1]),
                pltpu.make_async_copy(wdn_hbm.at[0, pl.ds(ft * tf, tf)], wd_buf.at[slot], sem.at[slot, 2]))

    def start(s):
        for cp in tile_copies(s):
            cp.start()

    @pl.when(step == 0)
    def _():
        for s in range(FFN_RING - 1):
            start(s)

    @pl.when(step + FFN_RING - 1 < n_steps)
    def _():
        start(step + FFN_RING - 1)

    for cp in tile_copies(step):
        cp.wait()
    slot = step % FFN_RING

    @pl.when(f == 0)
    def _():
        hs_ref[0:tm, :] = h_ref[...]
        hs_ref[tm:tm + half, :] = hn_ref[0:half, :]
        hs_ref[tm + half:ext, :] = hp_ref[half:, :]
        o_ref[...] = jnp.zeros_like(o_ref)

    a = _dot(hs_ref[...], wa_buf[slot])
    pos = (i * tm + lax.broadcasted_iota(jnp.int32, (ext, 1), 0)) % seq_len
    prev = jnp.where(pos == 0, 0.0, pltpu.roll(a, 1, 0))
    nxt = jnp.where(pos == seq_len - 1, 0.0, pltpu.roll(a, ext - 1, 0))
    conv = prev * cw_ref[0:1, :] + a * cw_ref[1:2, :] + nxt * cw_ref[2:3, :] + cb_ref[...]
    v = _dot(h_ref[...], wv_buf[slot])
    u = (_silu(conv[0:tm]) * v).astype(BF16)
    o_ref[...] += _dot(u, wd_buf[slot])

    @pl.when(f == pl.num_programs(1) - 1)
    def _():
        xn = x_ref[...] + gate_ref[0] * o_ref[...]
        if final_norm:
            xn = _rms(xn) * fg_ref[...]
        o_ref[...] = xn


def _ffn(h2, w_up, conv_w, conv_b, w_down, x2, gate, fg, *, tm, seq_len, rows_per_mod, final_norm):
    m, d = x2.shape
    tf = w_up.shape[2]
    nf = D_FF // tf
    hb = tm // FFN_HALO
    last = m // FFN_HALO - 1
    return pl.pallas_call(
        functools.partial(_ffn_kernel, tm=tm, seq_len=seq_len, final_norm=final_norm),
        grid=(m // tm, nf),
        in_specs=[
            pl.BlockSpec((tm, d), lambda i, f: (i, 0)),
            pl.BlockSpec((FFN_HALO, d), lambda i, f: (jnp.minimum((i + 1) * hb, last), 0)),
            pl.BlockSpec((FFN_HALO, d), lambda i, f: (jnp.maximum(i * hb - 1, 0), 0)),
            pl.BlockSpec(memory_space=pl.ANY),
            pl.BlockSpec((3, tf), lambda i, f: (0, f)),
            pl.BlockSpec((1, tf), lambda i, f: (0, f)),
            pl.BlockSpec(memory_space=pl.ANY),
            pl.BlockSpec((tm, d), lambda i, f: (i, 0)),
            pl.BlockSpec((1, 1, d), lambda i, f: ((i * tm) // rows_per_mod, 0, 0)),
            pl.BlockSpec((1, d), lambda i, f: (0, 0)),
        ],
        out_specs=pl.BlockSpec((tm, d), lambda i, f: (i, 0)),
        out_shape=jax.ShapeDtypeStruct((m, d), F32),
        scratch_shapes=[pltpu.VMEM((tm + FFN_HALO, d), BF16),
                        pltpu.VMEM((FFN_RING, d, tf), BF16),
                        pltpu.VMEM((FFN_RING, d, tf), BF16),
                        pltpu.VMEM((FFN_RING, tf, d), BF16),
                        pltpu.SemaphoreType.DMA((FFN_RING, 3))],
        compiler_params=_cparams(("arbitrary", "arbitrary")),
        name="conv_glu",
    )(h2, h2, h2, w_up, conv_w, conv_b, w_down, x2, gate, fg)


def _rope_tables(n_tokens):
    rows = n_tokens // GRID_W
    row = jnp.repeat(jnp.arange(rows, dtype=F32), GRID_W)
    col = jnp.tile(jnp.arange(GRID_W, dtype=F32), rows)
    n_freq = HEAD_DIM // 4
    inv_freq = ROPE_THETA ** (-jnp.arange(n_freq, dtype=F32) / n_freq)
    ang = jnp.concatenate([row[:, None] * inv_freq, col[:, None] * inv_freq], axis=-1)
    cos, sin = jnp.cos(ang), jnp.sin(ang)
    return jnp.concatenate([cos, cos], axis=-1), jnp.concatenate([-sin, sin], axis=-1)


def kernel(x, c, ctx, c_ctx, ada_w, ada_b, norm1_g, w_in, q_norm_g, k_norm_g, ret_log_decay, ret_norm_g,
           gla_gate_up, gla_gate_b, gla_norm_g, w_out, norm2_g, w_up, conv_w, conv_b, w_down, final_norm_g):
    batch, lat_len, d = x.shape
    ctx_len = ctx.shape[1]
    depth = ada_w.shape[0]
    mod_rows = 16
    cc = jnp.concatenate([c, c_ctx[None], jnp.zeros((mod_rows - batch - 1, d), F32)], axis=0)
    mod = _ada(cc, ada_w, ada_b).reshape(depth, mod_rows, N_MOD, d)

    cos_l, sin_l = _rope_tables(lat_len)
    proj_tm = PROJ_TM
    cos_c = jnp.ones((proj_tm, LANES), F32)
    sin_c = jnp.zeros((proj_tm, LANES), F32)

    xl = x.reshape(batch * lat_len, d)
    xc = ctx.reshape(batch * ctx_len, d)
    row = lambda v: v.reshape(1, -1)
    w_in_b = w_in.astype(BF16)
    w_out_b = w_out[0].astype(BF16)[None]

    for l in range(depth):
        last = l == depth - 1
        ml = [mod[l, :batch, k].reshape(batch, 1, d) for k in range(N_MOD)]
        mc = [mod[l, batch, k].reshape(1, 1, d) for k in range(N_MOD)]
        up_heads, gate_b_heads = _gla_gate_params(gla_gate_up[l], gla_gate_b[l])

        zl, vtl = _proj(xl, ml[0], ml[1], row(norm1_g[l]), w_in_b, l, cos_l, sin_l, row(q_norm_g[l]), row(k_norm_g[l]),
                        tm=proj_tm, rows_per_mod=lat_len, rope_tiles=lat_len // proj_tm)
        zc, vtc = _proj(xc, mc[0], mc[1], row(norm1_g[l]), w_in_b, l, cos_c, sin_c, row(q_norm_g[l]), row(k_norm_g[l]),
                        tm=proj_tm, rows_per_mod=batch * ctx_len, rope_tiles=1)

        casts = [(w_up, l, FFN_TF), (w_down, l, None)]
        if not last:
            casts += [(w_out, l + 1, None)]
        att_l, cast = _attn(vtl, [(zc, vtc, ctx_len), (zl, vtl, lat_len)], batch=batch, q_len=lat_len, tq=ATT_TQ,
                            casts=casts)
        w_up_b, w_down_b = cast[0], cast[1][None]
        ret_c, ret_l = _ret(zc, zl, ret_log_decay[l], row(ret_norm_g[l]),
                            batch=batch, ctx_len=ctx_len, lat_len=lat_len)
        gla_c, gla_l = _gla(zc, zl, up_heads, gate_b_heads, row(gla_norm_g[l]),
                            batch=batch, ctx_len=ctx_len, lat_len=lat_len)

        xl, h2 = _outproj(att_l, ret_l, gla_l, w_out_b, 0, xl, ml[2], ml[3], ml[4], row(norm2_g[l]),
                          tm=OUTPROJ_TM, rows_per_mod=lat_len)
        xl = _ffn(h2, w_up_b, conv_w[l], row(conv_b[l]), w_down_b, xl, ml[5], row(final_norm_g),
                  tm=FFN_TM, seq_len=lat_len, rows_per_mod=lat_len, final_norm=last)

        if not last:
            att_c, _ = _attn(vtc, [(zc, vtc, ctx_len)], batch=batch, q_len=ctx_len, tq=ctx_len)
            xc, hc2 = _outproj(att_c, ret_c, gla_c, w_out_b, 0, xc, mc[2], mc[3], mc[4], row(norm2_g[l]),
                               tm=OUTPROJ_TM, rows_per_mod=batch * ctx_len)
            xc = _ffn(hc2, w_up_b, conv_w[l], row(conv_b[l]), w_down_b, xc, mc[5], row(final_norm_g),
                      tm=FFN_TM, seq_len=ctx_len, rows_per_mod=batch * ctx_len, final_norm=False)
            w_out_b = cast[2][None]

    return xl.reshape(batch, lat_len, d)
```

```python
import functools

import numpy as np
import jax
import jax.numpy as jnp
from jax import lax
from jax.experimental import pallas as pl
from jax.experimental.pallas import tpu as pltpu

F32 = jnp.float32
BF16 = jnp.bfloat16

HEAD_DIM = 128
GRID_W = 64
ATT_Q_HEADS = 8
ATT_KV_HEADS = 2
GQA = ATT_Q_HEADS // ATT_KV_HEADS
RET_HEADS = 4
RET_DIM = 128
RET_CHUNK = 128
GLA_HEADS = 4
GLA_DK = 64
GLA_DV = 128
GLA_GATE_RANK = 16
GLA_TAU = 16.0
GLA_CHUNK = 64
GLA_LEVELS = 6
D_FF = 5632
ROPE_THETA = 10000.0
N_MOD = 6
EPS = 1e-6

LOG2_E = 1.4426950408889634

LANES = 128
N_IN = 5152
PROJ_TN = 256
N_SLOTS = -(-N_IN // LANES)
SLOT_AQ, SLOT_AK, SLOT_AV = 0, 8, 10
SLOT_RQ, SLOT_RK, SLOT_RV, SLOT_RG = 12, 16, 20, 24
SLOT_GQ, SLOT_GK, SLOT_GV, SLOT_GR, SLOT_GA = 28, 30, 32, 36, 40

VMEM_LIMIT = 52 * 1024 * 1024
ADA_TN = 1024
PROJ_TM = 512
ATT_TQ = 1024
ATT_Q_SUB = 256
OUTPROJ_TM = 512

_NT = (((1,), (1,)), ((), ()))
_TN = (((0,), (0,)), ((), ()))


def _cparams(sem, vmem_limit=VMEM_LIMIT):
    return pltpu.CompilerParams(dimension_semantics=sem, vmem_limit_bytes=vmem_limit)


def _dot(a, b):
    return jnp.dot(a, b, preferred_element_type=F32)


def _dot_nt(a, b):
    return lax.dot_general(a, b, _NT, preferred_element_type=F32)


def _dot_tn(a, b):
    return lax.dot_general(a, b, _TN, preferred_element_type=F32)


def _rows(start, size, align):
    if isinstance(start, int):
        return pl.ds(start, size)
    return pl.ds(pl.multiple_of(start, align), size)


def _rms(t):
    return t * lax.rsqrt(jnp.mean(t * t, axis=-1, keepdims=True) + EPS)


def _silu(t):
    return t * jax.nn.sigmoid(t)


def _ada_kernel(c_ref, w_ref, b_ref, o_ref):
    s = _silu(c_ref[...]).astype(BF16)
    o_ref[0] = _dot(s, w_ref[0].astype(BF16)) + b_ref[0]


def _ada(cc, ada_w, ada_b):
    depth, d, n = ada_w.shape
    rows = cc.shape[0]
    tn = ADA_TN
    return pl.pallas_call(
        _ada_kernel,
        grid=(depth, n // tn),
        in_specs=[
            pl.BlockSpec((rows, d), lambda l, j: (0, 0)),
            pl.BlockSpec((1, d, tn), lambda l, j: (l, 0, j)),
            pl.BlockSpec((1, 1, tn), lambda l, j: (l, 0, j)),
        ],
        out_specs=pl.BlockSpec((1, rows, tn), lambda l, j: (l, 0, j)),
        out_shape=jax.ShapeDtypeStruct((depth, rows, n), F32),
        compiler_params=_cparams(("arbitrary", "arbitrary")),
        name="ada_mod",
    )(cc, ada_w, ada_b.reshape(depth, 1, n))


def _proj_kernel(x_ref, sh_ref, sc_ref, g_ref, w_ref, cos_ref, sin_ref, qg_ref, kg_ref, o_ref, vt_ref):
    y = _rms(x_ref[...]) * g_ref[...]
    h = (y * (1.0 + sc_ref[0]) + sh_ref[0]).astype(BF16)
    cos, sin = cos_ref[...], sin_ref[...]
    qg = qg_ref[...] * (HEAD_DIM ** -0.5 * LOG2_E)
    kg = kg_ref[...]

    def rope(t):
        return t * cos + pltpu.roll(t, HEAD_DIM // 2, 1) * sin

    def transform(slot, t):
        if slot < SLOT_AK:
            return rope(_rms(t) * qg)
        if slot < SLOT_AV:
            return rope(_rms(t) * kg)
        if SLOT_RQ <= slot < SLOT_RK:
            return rope(t)
        if SLOT_RK <= slot < SLOT_RV:
            return rope(t * (RET_DIM ** -0.5))
        if SLOT_GQ <= slot < SLOT_GK:
            return t * (GLA_DK ** -0.5)
        return t

    for c0 in range(0, N_IN, PROJ_TN):
        c1 = min(c0 + PROJ_TN, N_IN)
        z = _dot(h, w_ref[0, :, c0:c1])
        for s0 in range(c0, c1, LANES):
            slot, n = s0 // LANES, min(LANES, c1 - s0)
            t = transform(slot, z[:, s0 - c0:s0 - c0 + n])
            if slot < SLOT_AK:
                vt_ref[slot] = t.T.astype(BF16)
            elif SLOT_AV <= slot < SLOT_RQ:
                vt_ref[ATT_Q_HEADS + slot - SLOT_AV] = t.T.astype(BF16)
            t = t.astype(BF16)
            if n < LANES:
                o_ref[slot] = jnp.zeros(o_ref.shape[1:], BF16)
                o_ref[slot, :, 0:n] = t
            else:
                o_ref[slot] = t


def _proj(x2, shift, scale, g, w, layer, cosf, sinf, qg, kg, *, tm, rows_per_mod, rope_tiles):
    m, d = x2.shape
    return pl.pallas_call(
        _proj_kernel,
        grid=(m // tm,),
        in_specs=[
            pl.BlockSpec((tm, d), lambda i: (i, 0)),
            pl.BlockSpec((1, 1, d), lambda i: ((i * tm) // rows_per_mod, 0, 0)),
            pl.BlockSpec((1, 1, d), lambda i: ((i * tm) // rows_per_mod, 0, 0)),
            pl.BlockSpec((1, d), lambda i: (0, 0)),
            pl.BlockSpec((1, d, N_IN), lambda i: (layer, 0, 0), pipeline_mode=pl.Buffered(1)),
            pl.BlockSpec((tm, LANES), lambda i: (i % rope_tiles, 0)),
            pl.BlockSpec((tm, LANES), lambda i: (i % rope_tiles, 0)),
            pl.BlockSpec((1, LANES), lambda i: (0, 0)),
            pl.BlockSpec((1, LANES), lambda i: (0, 0)),
        ],
        out_specs=[pl.BlockSpec((N_SLOTS, tm, LANES), lambda i: (0, i, 0)),
                   pl.BlockSpec((ATT_Q_HEADS + ATT_KV_HEADS, HEAD_DIM, tm), lambda i: (0, 0, i))],
        out_shape=[jax.ShapeDtypeStruct((N_SLOTS, m, LANES), BF16),
                   jax.ShapeDtypeStruct((ATT_Q_HEADS + ATT_KV_HEADS, HEAD_DIM, m), BF16)],
        compiler_params=_cparams(("arbitrary",)),
        name="norm_proj",
    )(x2, shift, scale, g, w, cosf, sinf, qg, kg)


ATT_KEY_BLOCK = 128


def _attn_kernel(*refs, n_kv, reps):
    n_cast = len(reps)
    q_ref = refs[0]
    kv_refs = refs[1:1 + 2 * n_kv]
    w_refs = refs[1 + 2 * n_kv:1 + 2 * n_kv + n_cast]
    o_ref = refs[1 + 2 * n_kv + n_cast]
    wo_refs = refs[2 + 2 * n_kv + n_cast:]
    step = (pl.program_id(0) * pl.num_programs(1) + pl.program_id(1)) * pl.num_programs(2) + pl.program_id(2)

    def convert(w_ref, wo_ref):
        if len(wo_ref.shape) == 3:
            tn = wo_ref.shape[2]
            for t in range(wo_ref.shape[0]):
                wo_ref[t] = w_ref[0, :, t * tn:(t + 1) * tn].astype(BF16)
        else:
            wo_ref[...] = w_ref[0].astype(BF16)

    for w_ref, wo_ref, rep in zip(w_refs, wo_refs, reps):
        if rep == 1:
            convert(w_ref, wo_ref)
        else:
            pl.when(step % rep == 0)(functools.partial(convert, w_ref, wo_ref))
    sub = min(ATT_Q_SUB, q_ref.shape[2])
    chains = [(g, t0) for t0 in range(0, q_ref.shape[2], sub) for g in range(GQA)]
    m, l, acc = {}, {}, {}
    units = [(s, j0, ch) for s in range(n_kv) for j0 in range(0, kv_refs[2 * s].shape[1], ATT_KEY_BLOCK)
             for ch in chains]

    def scores(unit):
        s, j0, (g, t0) = unit
        return _dot(kv_refs[2 * s][0, j0:j0 + ATT_KEY_BLOCK, :], q_ref[g, :, t0:t0 + sub])

    st_next = scores(units[0])
    for u, (s, j0, ch) in enumerate(units):
        st = st_next
        if u + 1 < len(units):
            st_next = scores(units[u + 1])
        vt = kv_refs[2 * s + 1][0, :, j0:j0 + ATT_KEY_BLOCK]
        mb = jnp.max(st, axis=0, keepdims=True)
        if ch not in m:
            m[ch] = mb
            p = jnp.exp2(st - mb)
            l[ch] = jnp.sum(p, axis=0, keepdims=True)
            acc[ch] = _dot(vt, p.astype(BF16))
        else:
            m_new = jnp.maximum(m[ch], mb)
            alpha = jnp.exp2(m[ch] - m_new)
            p = jnp.exp2(st - m_new)
            l[ch] = alpha * l[ch] + jnp.sum(p, axis=0, keepdims=True)
            acc[ch] = alpha * acc[ch] + _dot(vt, p.astype(BF16))
            m[ch] = m_new
    for g, t0 in chains:
        o_ref[t0:t0 + sub, g * HEAD_DIM:(g + 1) * HEAD_DIM] = (acc[g, t0] / l[g, t0]).T.astype(BF16)


def _attn(qt, kv_sources, *, batch, q_len, tq, casts=()):
    nq = q_len // tq
    steps = batch * ATT_KV_HEADS * nq
    in_specs = [pl.BlockSpec((GQA, HEAD_DIM, tq), lambda b, k, i: (k, 0, b * nq + i))]
    args = [qt]
    for z, vt, kv_len in kv_sources:
        in_specs.append(pl.BlockSpec((1, kv_len, HEAD_DIM), lambda b, k, i: (SLOT_AK + k, b, 0)))
        in_specs.append(pl.BlockSpec((1, HEAD_DIM, kv_len), lambda b, k, i: (ATT_Q_HEADS + k, 0, b)))
        args += [z, vt]
    out_specs = [pl.BlockSpec((tq, GQA * HEAD_DIM), lambda b, k, i: (b * nq + i, k))]
    out_shape = [jax.ShapeDtypeStruct((batch * q_len, ATT_Q_HEADS * HEAD_DIM), BF16)]
    step = lambda b, k, i: (b * ATT_KV_HEADS + k) * nq + i
    reps = []
    for w, layer, tile in casts:
        _, rows, cols = w.shape
        rep = next(r for r in (1, 2, 4, 8, 16) if rows % (steps // r) == 0 and rows // (steps // r) % 16 == 0)
        slab = rows // (steps // rep)
        reps.append(rep)
        in_specs.append(pl.BlockSpec((1, slab, cols),
                                     lambda b, k, i, rep=rep, layer=layer: (layer, step(b, k, i) // rep, 0)))
        args.append(w)
        if tile is None:
            out_specs.append(pl.BlockSpec((slab, cols), lambda b, k, i, rep=rep: (step(b, k, i) // rep, 0)))
            out_shape.append(jax.ShapeDtypeStruct((rows, cols), BF16))
        else:
            out_specs.append(pl.BlockSpec((cols // tile, slab, tile),
                                          lambda b, k, i, rep=rep: (0, step(b, k, i) // rep, 0)))
            out_shape.append(jax.ShapeDtypeStruct((cols // tile, rows, tile), BF16))
    outs = pl.pallas_call(
        functools.partial(_attn_kernel, n_kv=len(kv_sources), reps=tuple(reps)),
        grid=(batch, ATT_KV_HEADS, nq),
        in_specs=in_specs,
        out_specs=out_specs,
        out_shape=out_shape,
        compiler_params=_cparams(("arbitrary", "arbitrary", "arbitrary")),
        name="attention",
    )(*args)
    return outs[0], list(outs[1:])


def _ret_kernel(dec_ref, qc_ref, kc_ref, vc_ref, gc_ref, ql_ref, kl_ref, vl_ref, gl_ref, ng_ref,
                oc_ref, ol_ref, kv_ref, st_ref, *, nc_ctx, nc_lat):
    h = pl.program_id(1)
    C = RET_CHUNK
    lg_f, lg_b = dec_ref[0, h], dec_ref[1, h]
    rel = lax.broadcasted_iota(jnp.int32, (C, C), 0) - lax.broadcasted_iota(jnp.int32, (C, C), 1)
    dmat = (jnp.where(rel >= 0, jnp.exp(lg_f * jnp.maximum(rel, 0).astype(F32)), 0.0)
            + jnp.where(rel <= 0, jnp.exp(lg_b * jnp.maximum(-rel, 0).astype(F32)), 0.0))
    row = lax.broadcasted_iota(jnp.int32, (C, 1), 0).astype(F32)
    qd_f, qd_b = jnp.exp(lg_f * (row + 1.0)), jnp.exp(lg_b * (C - row))
    kd_f, kd_b = jnp.exp(lg_f * (C - 1.0 - row)), jnp.exp(lg_b * row)
    cd_f, cd_b = jnp.exp(lg_f * C), jnp.exp(lg_b * C)

    chunks = [(qc_ref, kc_ref, vc_ref, gc_ref, oc_ref, n) for n in range(nc_ctx)]
    chunks += [(ql_ref, kl_ref, vl_ref, gl_ref, ol_ref, n) for n in range(nc_lat)]
    order_b = list(range(nc_ctx - 1, -1, -1)) + list(range(nc_ctx + nc_lat - 1, nc_ctx - 1, -1))

    def decayed(x_ref, n, dec_f, dec_b):
        x = x_ref[0, n * C:(n + 1) * C, :].astype(F32)
        return jnp.concatenate([(x * dec_f).astype(BF16), (x * dec_b).astype(BF16)], axis=1)

    for c, (_, k_ref, v_ref, _, _, n) in enumerate(chunks):
        kv_ref[c] = _dot_tn(decayed(k_ref, n, kd_f, kd_b), v_ref[0, n * C:(n + 1) * C, :])

    s = jnp.zeros((C, RET_DIM), F32)
    for c in range(len(chunks)):
        st_ref[c, 0:C, :] = s.astype(BF16)
        s = cd_f * s + kv_ref[c, 0:C, :]
    s = jnp.zeros((C, RET_DIM), F32)
    for c in order_b:
        st_ref[c, C:2 * C, :] = s.astype(BF16)
        s = cd_b * s + kv_ref[c, C:2 * C, :]

    for c, (q_ref, k_ref, v_ref, g_ref, o_ref, n) in enumerate(chunks):
        sl = slice(n * C, (n + 1) * C)
        v = v_ref[0, sl, :]
        att = _dot_nt(q_ref[0, sl, :], k_ref[0, sl, :]) * dmat
        o = _dot(att.astype(BF16), v) + _dot(decayed(q_ref, n, qd_f, qd_b), st_ref[c])
        y = _rms(o) * ng_ref[...]
        o_ref[sl, :] = (y * _silu(g_ref[0, sl, :].astype(F32))).astype(BF16)


def _ret(zc, zl, decay, ng, *, batch, ctx_len, lat_len):
    def slot(base, n):
        return pl.BlockSpec((1, n, RET_DIM), lambda b, h: (base + h, b, 0))

    n_chunks = (ctx_len + lat_len) // RET_CHUNK
    specs = [pl.BlockSpec(memory_space=pltpu.SMEM)]
    specs += [slot(s, ctx_len) for s in (SLOT_RQ, SLOT_RK, SLOT_RV, SLOT_RG)]
    specs += [slot(s, lat_len) for s in (SLOT_RQ, SLOT_RK, SLOT_RV, SLOT_RG)]
    specs += [pl.BlockSpec((1, RET_DIM), lambda b, h: (0, 0))]
    return pl.pallas_call(
        functools.partial(_ret_kernel, nc_ctx=ctx_len // RET_CHUNK, nc_lat=lat_len // RET_CHUNK),
        grid=(batch, RET_HEADS),
        in_specs=specs,
        out_specs=[pl.BlockSpec((ctx_len, RET_DIM), lambda b, h: (b, h)),
                   pl.BlockSpec((lat_len, RET_DIM), lambda b, h: (b, h))],
        out_shape=[jax.ShapeDtypeStruct((batch * ctx_len, RET_HEADS * RET_DIM), BF16),
                   jax.ShapeDtypeStruct((batch * lat_len, RET_HEADS * RET_DIM), BF16)],
        scratch_shapes=[pltpu.VMEM((n_chunks, 2 * RET_DIM, RET_DIM), F32),
                        pltpu.VMEM((n_chunks, 2 * RET_DIM, RET_DIM), BF16)],
        compiler_params=_cparams(("arbitrary", "arbitrary")),
        name="retention",
    )(decay, zc, zc, zc, zc, zl, zl, zl, zl, ng)


GLA_SLAB = 256


def _gla_level_table():
    i = np.arange(GLA_CHUNK)[:, None]
    j = np.arange(GLA_CHUNK)[None, :]
    x = np.maximum(i ^ j, 1)
    lvl = np.floor(np.log2(x)).astype(np.int32)
    lvl = np.where(i == j, GLA_LEVELS, np.where(i > j, lvl, -1)).astype(np.int32)
    return np.concatenate([lvl, lvl.T], axis=1)


def _gla_role_table():
    t = np.arange(GLA_SLAB)
    fwd = np.arange(LANES)[None, :] < GLA_DK
    return np.stack([np.where(((t[:, None] >> lev) & 1).astype(bool) == fwd, LOG2_E, -LOG2_E)
                     for lev in range(GLA_LEVELS)]).astype(np.float32)


def _bcast_block_row(x, blk, r):
    rows = x.shape[0]
    if blk >= 8:
        x3 = x.reshape(rows // blk, blk, LANES)
        return jnp.broadcast_to(x3[:, r:r + 1, :], x3.shape).reshape(rows, LANES)
    pos = lax.broadcasted_iota(jnp.int32, x.shape, 0) & (blk - 1)
    out = x
    for p in range(blk):
        if p != r:
            out = jnp.where(pos == p, pltpu.roll(x, (p - r) % rows, 0), out)
    return out


def _gla_kernel(lvl_ref, role_ref, bdm_ref, up_ref, gb_ref, ng_ref,
                qc_ref, kc_ref, vc_ref, gc_ref, ac_ref, ql_ref, kl_ref, vl_ref, gl_ref, al_ref,
                oc_ref, ol_ref, *scratch, ctx_len, lat_len):
    C = GLA_CHUNK
    streams = ((qc_ref, kc_ref, vc_ref, ac_ref, 0, ctx_len), (ql_ref, kl_ref, vl_ref, al_ref, ctx_len, lat_len))
    nc_ctx, nc_lat = ctx_len // C, lat_len // C
    n_chunks = nc_ctx + nc_lat
    cpos = lax.broadcasted_iota(jnp.int32, (GLA_SLAB, LANES), 0) & (C - 1)
    fwd = lax.broadcasted_iota(jnp.int32, (GLA_SLAB, LANES), 1) < GLA_DK
    lvl = lvl_ref[...]

    passes = []
    for hh in range(2):
        own = fwd if hh == 0 else jnp.logical_not(fwd)
        xs_ref, qd_ref, xst_ref, qh_ref, tot_ref, vv_ref, kv_ref, st_ref = (r.at[hh] for r in scratch)

        def prep(i, between, q_ref, k_ref, v_ref, a_ref, off, hh=hh, own=own, xs_ref=xs_ref, qd_ref=qd_ref,
                 xst_ref=xst_ref, qh_ref=qh_ref, tot_ref=tot_ref, vv_ref=vv_ref, kv_ref=kv_ref):
            between = list(between)
            src = _rows(i * GLA_SLAB, GLA_SLAB, GLA_SLAB)
            dst = _rows(off + i * GLA_SLAB, GLA_SLAB, C)
            chunk0 = off // C + i * (GLA_SLAB // C)

            def store_keys(lev, x):
                y = jnp.concatenate([x[c * C:(c + 1) * C] for c in range(GLA_SLAB // C) for _ in range(2)], axis=0)
                yt = y.T.astype(BF16) * bdm_ref[...]
                for c in range(GLA_SLAB // C):
                    xst_ref[lev, chunk0 + c] = yt[:, c * 2 * C:(c + 1) * 2 * C]

            logit = _dot(a_ref[0, src, :], up_ref[hh]) + gb_ref[hh]
            la = (jnp.minimum(logit, 0.0) - jnp.log(1.0 + jnp.exp(-jnp.abs(logit)))) * (1.0 / GLA_TAU)
            pre = la
            for step in range(GLA_LEVELS):
                sh = 1 << step
                pre = pre + jnp.where(cpos >= sh, pltpu.roll(pre, sh, 0), 0.0)
            tot = _bcast_block_row(pre, C, C - 1)
            cum = jnp.where(fwd, pre, tot - pre + la)
            q2 = q_ref[0, src, :].astype(F32)
            k2 = k_ref[0, src, :].astype(F32)
            q = jnp.where(own, q2, pltpu.roll(q2, GLA_DK, 1))
            k = jnp.where(own, k2, pltpu.roll(k2, GLA_DK, 1))
            qd_ref[dst, :] = q.astype(BF16)
            store_keys(GLA_LEVELS, k)
            qh_ref[dst, :] = (q * jnp.exp(cum)).astype(BF16)
            kh = (k * jnp.exp(tot - cum)).astype(BF16)
            tot_ref[dst, :] = tot
            v = v_ref[hh, src, :]
            vv_ref[dst, :] = v
            for c in range(GLA_SLAB // C):
                kv_ref[chunk0 + c] = _dot_tn(v[c * C:(c + 1) * C], kh[c * C:(c + 1) * C])
            for lev in range(GLA_LEVELS):
                if between and lev % 2 == 0:
                    between.pop(0)()
                s = 1 << lev
                d = pre - _bcast_block_row(pre, 2 * s, s - 1)
                dist = jnp.where(fwd, d, la - d)
                role = role_ref[lev]
                x = jnp.where(role > 0, q, k) * jnp.exp2(dist * role)
                xs_ref[lev, dst, :] = x.astype(BF16)
                store_keys(lev, x)
            for item in between:
                item()

        def states(kv_ref=kv_ref, st_ref=st_ref, tot_ref=tot_ref):
            def step(c, s, lanes):
                st_ref[c, :, lanes] = s[:, lanes].astype(BF16)
                return s * jnp.exp(tot_ref[pl.ds(c * C, 1), :]) + kv_ref[c]

            def body(n, carry):
                s_f, s_b = carry
                c_b = jnp.where(n < nc_ctx, nc_ctx - 1 - n, n_chunks + nc_ctx - 1 - n)
                return step(n, s_f, slice(0, GLA_DK)), step(c_b, s_b, slice(GLA_DK, LANES))

            zero = jnp.zeros((GLA_DV, LANES), F32)
            lax.fori_loop(0, n_chunks, body, (zero, zero))

        def out_chunk(c, g_ref, o_ref, local, hh=hh, xs_ref=xs_ref, qd_ref=qd_ref, xst_ref=xst_ref, qh_ref=qh_ref,
                      vv_ref=vv_ref, st_ref=st_ref):
            rows = _rows(c * C, C, C)
            a = jnp.where(lvl == GLA_LEVELS, _dot(qd_ref[rows, :], xst_ref[GLA_LEVELS, c]), 0.0)
            for lev in range(GLA_LEVELS):
                a = jnp.where(lvl == lev, _dot(xs_ref[lev, rows, :], xst_ref[lev, c]), a)
            v = vv_ref[rows, :]
            o = _dot(a.astype(BF16), jnp.concatenate([v, v], axis=0)) + _dot_nt(qh_ref[rows, :], st_ref[c])
            y = _rms(o) * ng_ref[...]
            out_rows = _rows(local * C, C, C)
            y = y * _silu(g_ref[hh, out_rows, :].astype(F32))
            o_ref[out_rows, hh * GLA_DV:(hh + 1) * GLA_DV] = y.astype(BF16)

        passes.append((prep, states, out_chunk))

    per_slab = GLA_SLAB // C
    (prep0, states0, out0), (prep1, states1, out1) = passes

    def slab(prep, out, stream, i, g_ref, o_ref):
        q_ref, k_ref, v_ref, a_ref, off, _ = stream
        chunks = []
        if out is not None:
            chunks = [functools.partial(out, off // C + i * per_slab + j, g_ref, o_ref, i * per_slab + j)
                      for j in range(per_slab)]
        if prep is not None:
            prep(i, chunks, q_ref=q_ref, k_ref=k_ref, v_ref=v_ref, a_ref=a_ref, off=off)
        else:
            for item in chunks:
                item()

    def run(prep, out):
        for stream, g_ref, o_ref in ((streams[0], gc_ref, oc_ref), (streams[1], gl_ref, ol_ref)):
            n_slabs = stream[5] // GLA_SLAB
            if n_slabs == 1:
                slab(prep, out, stream, 0, g_ref, o_ref)
            else:
                def body(i, carry, stream=stream, g_ref=g_ref, o_ref=o_ref):
                    slab(prep, out, stream, i, g_ref, o_ref)
                    return carry
                lax.fori_loop(0, n_slabs, body, 0, unroll=2 if n_slabs % 2 == 0 else 1)

    run(prep0, None)
    states0()
    run(prep1, out0)
    states1()
    run(None, out1)


def _gla_gate_params(gate_up, gate_b):
    r = GLA_GATE_RANK
    gu = gate_up.astype(BF16).reshape(2, r, GLA_HEADS, GLA_DK).transpose(2, 0, 1, 3)
    up = jnp.zeros((GLA_HEADS, LANES, 2 * GLA_DK), BF16)
    up = up.at[:, 0:r, 0:GLA_DK].set(gu[:, 0]).at[:, r:2 * r, GLA_DK:].set(gu[:, 1])
    gb = gate_b.reshape(2, GLA_HEADS, GLA_DK).transpose(1, 0, 2).reshape(GLA_HEADS, 1, 2 * GLA_DK)
    return up, gb


def _gla(zc, zl, up_heads, gate_b_heads, ng, *, batch, ctx_len, lat_len):
    total = ctx_len + lat_len
    n_chunks = total // GLA_CHUNK

    def pair(base, n):
        return pl.BlockSpec((1, n, LANES), lambda b, p: (base + p, b, 0))

    def two(base, n):
        return pl.BlockSpec((2, n, LANES), lambda b, p: (base // 2 + p, b, 0))

    def one(n):
        return pl.BlockSpec((1, n, LANES), lambda b, p: (SLOT_GA, b, 0))

    role = _gla_role_table()
    feat_fwd = np.arange(LANES)[:, None] < GLA_DK
    key_fwd = (np.arange(2 * GLA_SLAB)[None, :] % (2 * GLA_CHUNK)) < GLA_CHUNK
    bd_mask = (feat_fwd == key_fwd).astype(np.float32)
    specs = [
        pl.BlockSpec((GLA_CHUNK, 2 * GLA_CHUNK), lambda b, p: (0, 0)),
        pl.BlockSpec(role.shape, lambda b, p: (0, 0, 0)),
        pl.BlockSpec(bd_mask.shape, lambda b, p: (0, 0)),
        pl.BlockSpec((2, LANES, LANES), lambda b, p: (p, 0, 0)),
        pl.BlockSpec((2, 1, LANES), lambda b, p: (p, 0, 0)),
        pl.BlockSpec((1, GLA_DV), lambda b, p: (0, 0)),
    ]
    for n in (ctx_len, lat_len):
        specs += [pair(SLOT_GQ, n), pair(SLOT_GK, n), two(SLOT_GV, n), two(SLOT_GR, n), one(n)]
    return pl.pallas_call(
        functools.partial(_gla_kernel, ctx_len=ctx_len, lat_len=lat_len),
        grid=(batch, GLA_HEADS // 2),
        in_specs=specs,
        out_specs=[pl.BlockSpec((ctx_len, 2 * GLA_DV), lambda b, p: (b, p)),
                   pl.BlockSpec((lat_len, 2 * GLA_DV), lambda b, p: (b, p))],
        out_shape=[jax.ShapeDtypeStruct((batch * ctx_len, GLA_HEADS * GLA_DV), BF16),
                   jax.ShapeDtypeStruct((batch * lat_len, GLA_HEADS * GLA_DV), BF16)],
        scratch_shapes=[
            pltpu.VMEM((2, GLA_LEVELS, total, LANES), BF16),
            pltpu.VMEM((2, total, LANES), BF16),
            pltpu.VMEM((2, GLA_LEVELS + 1, n_chunks, LANES, 2 * GLA_CHUNK), BF16),
            pltpu.VMEM((2, total, LANES), BF16),
            pltpu.VMEM((2, total, LANES), F32),
            pltpu.VMEM((2, total, GLA_DV), BF16),
            pltpu.VMEM((2, n_chunks, GLA_DV, LANES), F32),
            pltpu.VMEM((2, n_chunks, GLA_DV, LANES), BF16),
        ],
        compiler_params=_cparams(("arbitrary", "arbitrary")),
        name="gla",
    )(jnp.asarray(_gla_level_table()), jnp.asarray(role), jnp.asarray(bd_mask, BF16), up_heads, gate_b_heads, ng,
      zc, zc, zc, zc, zc, zl, zl, zl, zl, zl)


def _outproj_kernel(att_ref, ret_ref, gla_ref, w_ref, x_ref, gate_ref, sh_ref, sc_ref, g_ref, xo_ref, ho_ref):
    na, nr = att_ref.shape[1], ret_ref.shape[1]
    y = (_dot(att_ref[...], w_ref[0, 0:na, :]) + _dot(ret_ref[...], w_ref[0, na:na + nr, :])
         + _dot(gla_ref[...], w_ref[0, na + nr:, :]))
    xn = x_ref[...] + gate_ref[0] * y
    xo_ref[...] = xn
    ho_ref[...] = (_rms(xn) * g_ref[...] * (1.0 + sc_ref[0]) + sh_ref[0]).astype(BF16)


def _outproj(att, ret, gla, w, layer, x2, gate, shift, scale, g, *, tm, rows_per_mod):
    m, d = x2.shape

    def rows(n):
        return pl.BlockSpec((tm, n), lambda i: (i, 0))

    def mod():
        return pl.BlockSpec((1, 1, d), lambda i: ((i * tm) // rows_per_mod, 0, 0))

    return pl.pallas_call(
        _outproj_kernel,
        grid=(m // tm,),
        in_specs=[rows(att.shape[1]), rows(ret.shape[1]), rows(gla.shape[1]),
                  pl.BlockSpec((1,) + w.shape[1:], lambda i: (layer, 0, 0), pipeline_mode=pl.Buffered(1)),
                  rows(d), mod(), mod(), mod(),
                  pl.BlockSpec((1, d), lambda i: (0, 0))],
        out_specs=[rows(d), rows(d)],
        out_shape=[jax.ShapeDtypeStruct((m, d), F32), jax.ShapeDtypeStruct((m, d), BF16)],
        compiler_params=_cparams(("arbitrary",)),
        name="out_proj",
    )(att, ret, gla, w, x2, gate, shift, scale, g)


FFN_HALO = 16
FFN_TF = 512
FFN_RING = 3
FFN_TM = 512


def _ffn_kernel(h_ref, hn_ref, hp_ref, wup_hbm, cw_ref, cb_ref, wdn_hbm, x_hbm, gate_ref, fg_ref,
                o_ref, hs_ref, wa_buf, wv_buf, wd_buf, sem, x_buf, x_sem, *, tm, seq_len, final_norm):
    i = pl.program_id(0)
    f = pl.program_id(1)
    nf = pl.num_programs(1)
    tf = wa_buf.shape[2]
    half = FFN_HALO // 2
    ext = tm + FFN_HALO
    step = i * nf + f
    n_steps = pl.num_programs(0) * nf

    def tile_copies(s):
        ft = s % nf
        slot = s % FFN_RING
        return (pltpu.make_async_copy(wup_hbm.at[ft], wa_buf.at[slot], sem.at[slot, 0]),
                pltpu.make_async_copy(wup_hbm.at[nf + ft], wv_buf.at[slot], sem.at[slot, 1]),
                pltpu.make_async_copy(wdn_hbm.at[0, pl.ds(ft * tf, tf)], wd_buf.at[slot], sem.at[slot, 2]))

    def start(s):
        for cp in tile_copies(s):
            cp.start()

    def residual_copy():
        return pltpu.make_async_copy(x_hbm.at[pl.ds(pl.multiple_of(i * tm, tm), tm)], x_buf, x_sem.at[0])

    @pl.when(step == 0)
    def _():
        for s in range(FFN_RING - 1):
            start(s)

    @pl.when(step + FFN_RING - 1 < n_steps)
    def _():
        start(step + FFN_RING - 1)

    for cp in tile_copies(step):
        cp.wait()
    slot = step % FFN_RING

    @pl.when(f == 0)
    def _():
        hs_ref[0:tm, :] = h_ref[...]
        hs_ref[tm:tm + half, :] = hn_ref[0:half, :]
        hs_ref[tm + half:ext, :] = hp_ref[half:, :]
        o_ref[...] = jnp.zeros_like(o_ref)
        residual_copy().start()

    a = _dot(hs_ref[...], wa_buf[slot])
    pos = (i * tm + lax.broadcasted_iota(jnp.int32, (ext, 1), 0)) % seq_len
    prev = jnp.where(pos == 0, 0.0, pltpu.roll(a, 1, 0))
    nxt = jnp.where(pos == seq_len - 1, 0.0, pltpu.roll(a, ext - 1, 0))
    conv = prev * cw_ref[0:1, :] + a * cw_ref[1:2, :] + nxt * cw_ref[2:3, :] + cb_ref[...]
    v = _dot(h_ref[...], wv_buf[slot])
    u = (_silu(conv[0:tm]) * v).astype(BF16)
    o_ref[...] += _dot(u, wd_buf[slot])

    @pl.when(f == pl.num_programs(1) - 1)
    def _():
        residual_copy().wait()
        xn = x_buf[...] + gate_ref[0] * o_ref[...]
        if final_norm:
            xn = _rms(xn) * fg_ref[...]
        o_ref[...] = xn


def _ffn(h2, w_up, conv_w, conv_b, w_down, x2, gate, fg, *, tm, seq_len, rows_per_mod, final_norm):
    m, d = x2.shape
    tf = w_up.shape[2]
    nf = D_FF // tf
    hb = tm // FFN_HALO
    last = m // FFN_HALO - 1
    return pl.pallas_call(
        functools.partial(_ffn_kernel, tm=tm, seq_len=seq_len, final_norm=final_norm),
        grid=(m // tm, nf),
        in_specs=[
            pl.BlockSpec((tm, d), lambda i, f: (i, 0)),
            pl.BlockSpec((FFN_HALO, d), lambda i, f: (jnp.minimum((i + 1) * hb, last), 0)),
            pl.BlockSpec((FFN_HALO, d), lambda i, f: (jnp.maximum(i * hb - 1, 0), 0)),
            pl.BlockSpec(memory_space=pl.ANY),
            pl.BlockSpec((3, tf), lambda i, f: (0, f)),
            pl.BlockSpec((1, tf), lambda i, f: (0, f)),
            pl.BlockSpec(memory_space=pl.ANY),
            pl.BlockSpec(memory_space=pl.ANY),
            pl.BlockSpec((1, 1, d), lambda i, f: ((i * tm) // rows_per_mod, 0, 0)),
            pl.BlockSpec((1, d), lambda i, f: (0, 0)),
        ],
        out_specs=pl.BlockSpec((tm, d), lambda i, f: (i, 0)),
        out_shape=jax.ShapeDtypeStruct((m, d), F32),
        scratch_shapes=[pltpu.VMEM((tm + FFN_HALO, d), BF16),
                        pltpu.VMEM((FFN_RING, d, tf), BF16),
                        pltpu.VMEM((FFN_RING, d, tf), BF16),
                        pltpu.VMEM((FFN_RING, tf, d), BF16),
                        pltpu.SemaphoreType.DMA((FFN_RING, 3)),
                        pltpu.VMEM((tm, d), F32),
                        pltpu.SemaphoreType.DMA((1,))],
        compiler_params=_cparams(("arbitrary", "arbitrary")),
        name="conv_glu",
    )(h2, h2, h2, w_up, conv_w, conv_b, w_down, x2, gate, fg)


def _rope_tables(n_tokens):
    rows = n_tokens // GRID_W
    row = jnp.repeat(jnp.arange(rows, dtype=F32), GRID_W)
    col = jnp.tile(jnp.arange(GRID_W, dtype=F32), rows)
    n_freq = HEAD_DIM // 4
    inv_freq = ROPE_THETA ** (-jnp.arange(n_freq, dtype=F32) / n_freq)
    ang = jnp.concatenate([row[:, None] * inv_freq, col[:, None] * inv_freq], axis=-1)
    cos, sin = jnp.cos(ang), jnp.sin(ang)
    return jnp.concatenate([cos, cos], axis=-1), jnp.concatenate([-sin, sin], axis=-1)


def kernel(x, c, ctx, c_ctx, ada_w, ada_b, norm1_g, w_in, q_norm_g, k_norm_g, ret_log_decay, ret_norm_g,
           gla_gate_up, gla_gate_b, gla_norm_g, w_out, norm2_g, w_up, conv_w, conv_b, w_down, final_norm_g):
    batch, lat_len, d = x.shape
    ctx_len = ctx.shape[1]
    depth = ada_w.shape[0]
    mod_rows = 16
    cc = jnp.concatenate([c, c_ctx[None], jnp.zeros((mod_rows - batch - 1, d), F32)], axis=0)
    mod = _ada(cc, ada_w, ada_b).reshape(depth, mod_rows, N_MOD, d)

    cos_l, sin_l = _rope_tables(lat_len)
    proj_tm = PROJ_TM
    cos_c = jnp.ones((proj_tm, LANES), F32)
    sin_c = jnp.zeros((proj_tm, LANES), F32)

    xl = x.reshape(batch * lat_len, d)
    xc = ctx.reshape(batch * ctx_len, d)
    row = lambda v: v.reshape(1, -1)
    w_in_b = w_in.astype(BF16)
    w_out_b = w_out[0].astype(BF16)[None]

    for l in range(depth):
        last = l == depth - 1
        ml = [mod[l, :batch, k].reshape(batch, 1, d) for k in range(N_MOD)]
        mc = [mod[l, batch, k].reshape(1, 1, d) for k in range(N_MOD)]
        up_heads, gate_b_heads = _gla_gate_params(gla_gate_up[l], gla_gate_b[l])

        zl, vtl = _proj(xl, ml[0], ml[1], row(norm1_g[l]), w_in_b, l, cos_l, sin_l, row(q_norm_g[l]), row(k_norm_g[l]),
                        tm=proj_tm, rows_per_mod=lat_len, rope_tiles=lat_len // proj_tm)
        zc, vtc = _proj(xc, mc[0], mc[1], row(norm1_g[l]), w_in_b, l, cos_c, sin_c, row(q_norm_g[l]), row(k_norm_g[l]),
                        tm=proj_tm, rows_per_mod=batch * ctx_len, rope_tiles=1)

        casts = [(w_up, l, FFN_TF), (w_down, l, None)]
        if not last:
            casts += [(w_out, l + 1, None)]
        att_l, cast = _attn(vtl, [(zc, vtc, ctx_len), (zl, vtl, lat_len)], batch=batch, q_len=lat_len, tq=ATT_TQ,
                            casts=casts)
        w_up_b, w_down_b = cast[0], cast[1][None]
        ret_c, ret_l = _ret(zc, zl, ret_log_decay[l], row(ret_norm_g[l]),
                            batch=batch, ctx_len=ctx_len, lat_len=lat_len)
        gla_c, gla_l = _gla(zc, zl, up_heads, gate_b_heads, row(gla_norm_g[l]),
                            batch=batch, ctx_len=ctx_len, lat_len=lat_len)

        xl, h2 = _outproj(att_l, ret_l, gla_l, w_out_b, 0, xl, ml[2], ml[3], ml[4], row(norm2_g[l]),
                          tm=OUTPROJ_TM, rows_per_mod=lat_len)
        xl = _ffn(h2, w_up_b, conv_w[l], row(conv_b[l]), w_down_b, xl, ml[5], row(final_norm_g),
                  tm=FFN_TM, seq_len=lat_len, rows_per_mod=lat_len, final_norm=last)

        if not last:
            att_c, _ = _attn(vtc, [(zc, vtc, ctx_len)], batch=batch, q_len=ctx_len, tq=ctx_len)
            xc, hc2 = _outproj(att_c, ret_c, gla_c, w_out_b, 0, xc, mc[2], mc[3], mc[4], row(norm2_g[l]),
                               tm=OUTPROJ_TM, rows_per_mod=batch * ctx_len)
            xc = _ffn(hc2, w_up_b, conv_w[l], row(conv_b[l]), w_down_b, xc, mc[5], row(final_norm_g),
                      tm=FFN_TM, seq_len=ctx_len, rows_per_mod=batch * ctx_len, final_norm=False)
            w_out_b = cast[2][None]

    return xl.reshape(batch, lat_len, d)
```

```python
import functools

import numpy as np
import jax
import jax.numpy as jnp
from jax import lax
from jax.experimental import pallas as pl
from jax.experimental.pallas import tpu as pltpu

F32 = jnp.float32
BF16 = jnp.bfloat16

HEAD_DIM = 128
GRID_W = 64
ATT_Q_HEADS = 8
ATT_KV_HEADS = 2
GQA = ATT_Q_HEADS // ATT_KV_HEADS
RET_HEADS = 4
RET_DIM = 128
RET_CHUNK = 128
GLA_HEADS = 4
GLA_DK = 64
GLA_DV = 128
GLA_GATE_RANK = 16
GLA_TAU = 16.0
GLA_CHUNK = 64
GLA_LEVELS = 6
D_FF = 5632
ROPE_THETA = 10000.0
N_MOD = 6
EPS = 1e-6

LOG2_E = 1.4426950408889634

LANES = 128
N_IN = 5152
PROJ_TN = 256
N_SLOTS = -(-N_IN // LANES)
SLOT_AQ, SLOT_AK, SLOT_AV = 0, 8, 10
SLOT_RQ, SLOT_RK, SLOT_RV, SLOT_RG = 12, 16, 20, 24
SLOT_GQ, SLOT_GK, SLOT_GV, SLOT_GR, SLOT_GA = 28, 30, 32, 36, 40

VMEM_LIMIT = 52 * 1024 * 1024
ADA_TN = 1024
PROJ_TM = 512
ATT_TQ = 1024
ATT_Q_SUB = 256
OUTPROJ_TM = 512

_NT = (((1,), (1,)), ((), ()))
_TN = (((0,), (0,)), ((), ()))


def _cparams(sem, vmem_limit=VMEM_LIMIT):
    return pltpu.CompilerParams(dimension_semantics=sem, vmem_limit_bytes=vmem_limit)


def _dot(a, b):
    return jnp.dot(a, b, preferred_element_type=F32)


def _dot_nt(a, b):
    return lax.dot_general(a, b, _NT, preferred_element_type=F32)


def _dot_tn(a, b):
    return lax.dot_general(a, b, _TN, preferred_element_type=F32)


def _rows(start, size, align):
    if isinstance(start, int):
        return pl.ds(start, size)
    return pl.ds(pl.multiple_of(start, align), size)


def _rms(t):
    return t * lax.rsqrt(jnp.mean(t * t, axis=-1, keepdims=True) + EPS)


def _silu(t):
    return t * jax.nn.sigmoid(t)


def _ada_kernel(c_ref, w_ref, b_ref, o_ref):
    s = _silu(c_ref[...]).astype(BF16)
    o_ref[0] = _dot(s, w_ref[0].astype(BF16)) + b_ref[0]


def _ada(cc, ada_w, ada_b):
    depth, d, n = ada_w.shape
    rows = cc.shape[0]
    tn = ADA_TN
    return pl.pallas_call(
        _ada_kernel,
        grid=(depth, n // tn),
        in_specs=[
            pl.BlockSpec((rows, d), lambda l, j: (0, 0)),
            pl.BlockSpec((1, d, tn), lambda l, j: (l, 0, j)),
            pl.BlockSpec((1, 1, tn), lambda l, j: (l, 0, j)),
        ],
        out_specs=pl.BlockSpec((1, rows, tn), lambda l, j: (l, 0, j)),
        out_shape=jax.ShapeDtypeStruct((depth, rows, n), F32),
        compiler_params=_cparams(("arbitrary", "arbitrary")),
        name="ada_mod",
    )(cc, ada_w, ada_b.reshape(depth, 1, n))


def _proj_kernel(x_ref, sh_ref, sc_ref, g_ref, w_ref, cos_ref, sin_ref, qg_ref, kg_ref, o_ref, vt_ref):
    y = _rms(x_ref[...]) * g_ref[...]
    h = (y * (1.0 + sc_ref[0]) + sh_ref[0]).astype(BF16)
    cos, sin = cos_ref[...], sin_ref[...]
    qg = qg_ref[...] * (HEAD_DIM ** -0.5 * LOG2_E)
    kg = kg_ref[...]

    def rope(t):
        return t * cos + pltpu.roll(t, HEAD_DIM // 2, 1) * sin

    def transform(slot, t):
        if slot < SLOT_AK:
            return rope(_rms(t) * qg)
        if slot < SLOT_AV:
            return rope(_rms(t) * kg)
        if SLOT_RQ <= slot < SLOT_RK:
            return rope(t)
        if SLOT_RK <= slot < SLOT_RV:
            return rope(t * (RET_DIM ** -0.5))
        if SLOT_GQ <= slot < SLOT_GK:
            return t * (GLA_DK ** -0.5)
        return t

    for c0 in range(0, N_IN, PROJ_TN):
        c1 = min(c0 + PROJ_TN, N_IN)
        z = _dot(h, w_ref[0, :, c0:c1])
        for s0 in range(c0, c1, LANES):
            slot, n = s0 // LANES, min(LANES, c1 - s0)
            t = transform(slot, z[:, s0 - c0:s0 - c0 + n])
            if slot < SLOT_AK:
                vt_ref[slot] = t.T.astype(BF16)
            elif SLOT_AV <= slot < SLOT_RQ:
                vt_ref[ATT_Q_HEADS + slot - SLOT_AV] = t.T.astype(BF16)
            t = t.astype(BF16)
            if n < LANES:
                o_ref[slot] = jnp.zeros(o_ref.shape[1:], BF16)
                o_ref[slot, :, 0:n] = t
            else:
                o_ref[slot] = t


def _proj(x2, shift, scale, g, w, layer, cosf, sinf, qg, kg, *, tm, rows_per_mod, rope_tiles):
    m, d = x2.shape
    return pl.pallas_call(
        _proj_kernel,
        grid=(m // tm,),
        in_specs=[
            pl.BlockSpec((tm, d), lambda i: (i, 0)),
            pl.BlockSpec((1, 1, d), lambda i: ((i * tm) // rows_per_mod, 0, 0)),
            pl.BlockSpec((1, 1, d), lambda i: ((i * tm) // rows_per_mod, 0, 0)),
            pl.BlockSpec((1, d), lambda i: (0, 0)),
            pl.BlockSpec((1, d, N_IN), lambda i: (layer, 0, 0), pipeline_mode=pl.Buffered(1)),
            pl.BlockSpec((tm, LANES), lambda i: (i % rope_tiles, 0)),
            pl.BlockSpec((tm, LANES), lambda i: (i % rope_tiles, 0)),
            pl.BlockSpec((1, LANES), lambda i: (0, 0)),
            pl.BlockSpec((1, LANES), lambda i: (0, 0)),
        ],
        out_specs=[pl.BlockSpec((N_SLOTS, tm, LANES), lambda i: (0, i, 0)),
                   pl.BlockSpec((ATT_Q_HEADS + ATT_KV_HEADS, HEAD_DIM, tm), lambda i: (0, 0, i))],
        out_shape=[jax.ShapeDtypeStruct((N_SLOTS, m, LANES), BF16),
                   jax.ShapeDtypeStruct((ATT_Q_HEADS + ATT_KV_HEADS, HEAD_DIM, m), BF16)],
        compiler_params=_cparams(("arbitrary",)),
        name="norm_proj",
    )(x2, shift, scale, g, w, cosf, sinf, qg, kg)


ATT_KEY_BLOCK = 128


def _attn_kernel(*refs, n_kv, reps):
    n_cast = len(reps)
    q_ref = refs[0]
    kv_refs = refs[1:1 + 2 * n_kv]
    w_refs = refs[1 + 2 * n_kv:1 + 2 * n_kv + n_cast]
    o_ref = refs[1 + 2 * n_kv + n_cast]
    wo_refs = refs[2 + 2 * n_kv + n_cast:]
    step = (pl.program_id(0) * pl.num_programs(1) + pl.program_id(1)) * pl.num_programs(2) + pl.program_id(2)

    def convert(w_ref, wo_ref):
        if len(wo_ref.shape) == 3:
            tn = wo_ref.shape[2]
            for t in range(wo_ref.shape[0]):
                wo_ref[t] = w_ref[0, :, t * tn:(t + 1) * tn].astype(BF16)
        else:
            wo_ref[...] = w_ref[0].astype(BF16)

    for w_ref, wo_ref, rep in zip(w_refs, wo_refs, reps):
        if rep == 1:
            convert(w_ref, wo_ref)
        else:
            pl.when(step % rep == 0)(functools.partial(convert, w_ref, wo_ref))
    sub = min(ATT_Q_SUB, q_ref.shape[2])
    chains = [(g, t0) for t0 in range(0, q_ref.shape[2], sub) for g in range(GQA)]
    m, l, acc = {}, {}, {}
    units = [(s, j0, ch) for s in range(n_kv) for j0 in range(0, kv_refs[2 * s].shape[1], ATT_KEY_BLOCK)
             for ch in chains]

    def scores(unit):
        s, j0, (g, t0) = unit
        return _dot(kv_refs[2 * s][0, j0:j0 + ATT_KEY_BLOCK, :], q_ref[g, :, t0:t0 + sub])

    st_next = scores(units[0])
    for u, (s, j0, ch) in enumerate(units):
        st = st_next
        if u + 1 < len(units):
            st_next = scores(units[u + 1])
        vt = kv_refs[2 * s + 1][0, :, j0:j0 + ATT_KEY_BLOCK]
        mb = jnp.max(st, axis=0, keepdims=True)
        if ch not in m:
            m[ch] = mb
            p = jnp.exp2(st - mb)
            l[ch] = jnp.sum(p, axis=0, keepdims=True)
            acc[ch] = _dot(vt, p.astype(BF16))
        else:
            m_new = jnp.maximum(m[ch], mb)
            alpha = jnp.exp2(m[ch] - m_new)
            p = jnp.exp2(st - m_new)
            l[ch] = alpha * l[ch] + jnp.sum(p, axis=0, keepdims=True)
            acc[ch] = alpha * acc[ch] + _dot(vt, p.astype(BF16))
            m[ch] = m_new
    for g, t0 in chains:
        o_ref[t0:t0 + sub, g * HEAD_DIM:(g + 1) * HEAD_DIM] = (acc[g, t0] / l[g, t0]).T.astype(BF16)


def _attn(qt, kv_sources, *, batch, q_len, tq, casts=()):
    nq = q_len // tq
    steps = batch * ATT_KV_HEADS * nq
    in_specs = [pl.BlockSpec((GQA, HEAD_DIM, tq), lambda b, k, i: (k, 0, b * nq + i))]
    args = [qt]
    for z, vt, kv_len in kv_sources:
        in_specs.append(pl.BlockSpec((1, kv_len, HEAD_DIM), lambda b, k, i: (SLOT_AK + k, b, 0)))
        in_specs.append(pl.BlockSpec((1, HEAD_DIM, kv_len), lambda b, k, i: (ATT_Q_HEADS + k, 0, b)))
        args += [z, vt]
    out_specs = [pl.BlockSpec((tq, GQA * HEAD_DIM), lambda b, k, i: (b * nq + i, k))]
    out_shape = [jax.ShapeDtypeStruct((batch * q_len, ATT_Q_HEADS * HEAD_DIM), BF16)]
    step = lambda b, k, i: (b * ATT_KV_HEADS + k) * nq + i
    reps = []
    for w, layer, tile in casts:
        _, rows, cols = w.shape
        rep = next(r for r in (1, 2, 4, 8, 16) if rows % (steps // r) == 0 and rows // (steps // r) % 16 == 0)
        slab = rows // (steps // rep)
        reps.append(rep)
        in_specs.append(pl.BlockSpec((1, slab, cols),
                                     lambda b, k, i, rep=rep, layer=layer: (layer, step(b, k, i) // rep, 0)))
        args.append(w)
        if tile is None:
            out_specs.append(pl.BlockSpec((slab, cols), lambda b, k, i, rep=rep: (step(b, k, i) // rep, 0)))
            out_shape.append(jax.ShapeDtypeStruct((rows, cols), BF16))
        else:
            out_specs.append(pl.BlockSpec((cols // tile, slab, tile),
                                          lambda b, k, i, rep=rep: (0, step(b, k, i) // rep, 0)))
            out_shape.append(jax.ShapeDtypeStruct((cols // tile, rows, tile), BF16))
    outs = pl.pallas_call(
        functools.partial(_attn_kernel, n_kv=len(kv_sources), reps=tuple(reps)),
        grid=(batch, ATT_KV_HEADS, nq),
        in_specs=in_specs,
        out_specs=out_specs,
        out_shape=out_shape,
        compiler_params=_cparams(("arbitrary", "arbitrary", "arbitrary")),
        name="attention",
    )(*args)
    return outs[0], list(outs[1:])


def _ret_kernel(dec_ref, qc_ref, kc_ref, vc_ref, gc_ref, ql_ref, kl_ref, vl_ref, gl_ref, ng_ref,
                oc_ref, ol_ref, kv_ref, st_ref, *, nc_ctx, nc_lat):
    h = pl.program_id(1)
    C = RET_CHUNK
    lg_f, lg_b = dec_ref[0, h], dec_ref[1, h]
    rel = lax.broadcasted_iota(jnp.int32, (C, C), 0) - lax.broadcasted_iota(jnp.int32, (C, C), 1)
    dmat = (jnp.where(rel >= 0, jnp.exp(lg_f * jnp.maximum(rel, 0).astype(F32)), 0.0)
            + jnp.where(rel <= 0, jnp.exp(lg_b * jnp.maximum(-rel, 0).astype(F32)), 0.0))
    row = lax.broadcasted_iota(jnp.int32, (C, 1), 0).astype(F32)
    qd_f, qd_b = jnp.exp(lg_f * (row + 1.0)), jnp.exp(lg_b * (C - row))
    kd_f, kd_b = jnp.exp(lg_f * (C - 1.0 - row)), jnp.exp(lg_b * row)
    cd_f, cd_b = jnp.exp(lg_f * C), jnp.exp(lg_b * C)

    chunks = [(qc_ref, kc_ref, vc_ref, gc_ref, oc_ref, n) for n in range(nc_ctx)]
    chunks += [(ql_ref, kl_ref, vl_ref, gl_ref, ol_ref, n) for n in range(nc_lat)]
    order_b = list(range(nc_ctx - 1, -1, -1)) + list(range(nc_ctx + nc_lat - 1, nc_ctx - 1, -1))

    def decayed(x_ref, n, dec_f, dec_b):
        x = x_ref[0, n * C:(n + 1) * C, :].astype(F32)
        return jnp.concatenate([(x * dec_f).astype(BF16), (x * dec_b).astype(BF16)], axis=1)

    for c, (_, k_ref, v_ref, _, _, n) in enumerate(chunks):
        kv_ref[c] = _dot_tn(decayed(k_ref, n, kd_f, kd_b), v_ref[0, n * C:(n + 1) * C, :])

    s = jnp.zeros((C, RET_DIM), F32)
    for c in range(len(chunks)):
        st_ref[c, 0:C, :] = s.astype(BF16)
        s = cd_f * s + kv_ref[c, 0:C, :]
    s = jnp.zeros((C, RET_DIM), F32)
    for c in order_b:
        st_ref[c, C:2 * C, :] = s.astype(BF16)
        s = cd_b * s + kv_ref[c, C:2 * C, :]

    for c, (q_ref, k_ref, v_ref, g_ref, o_ref, n) in enumerate(chunks):
        sl = slice(n * C, (n + 1) * C)
        v = v_ref[0, sl, :]
        att = _dot_nt(q_ref[0, sl, :], k_ref[0, sl, :]) * dmat
        o = _dot(att.astype(BF16), v) + _dot(decayed(q_ref, n, qd_f, qd_b), st_ref[c])
        y = _rms(o) * ng_ref[...]
        o_ref[sl, :] = (y * _silu(g_ref[0, sl, :].astype(F32))).astype(BF16)


def _ret(zc, zl, decay, ng, *, batch, ctx_len, lat_len):
    def slot(base, n):
        return pl.BlockSpec((1, n, RET_DIM), lambda b, h: (base + h, b, 0))

    n_chunks = (ctx_len + lat_len) // RET_CHUNK
    specs = [pl.BlockSpec(memory_space=pltpu.SMEM)]
    specs += [slot(s, ctx_len) for s in (SLOT_RQ, SLOT_RK, SLOT_RV, SLOT_RG)]
    specs += [slot(s, lat_len) for s in (SLOT_RQ, SLOT_RK, SLOT_RV, SLOT_RG)]
    specs += [pl.BlockSpec((1, RET_DIM), lambda b, h: (0, 0))]
    return pl.pallas_call(
        functools.partial(_ret_kernel, nc_ctx=ctx_len // RET_CHUNK, nc_lat=lat_len // RET_CHUNK),
        grid=(batch, RET_HEADS),
        in_specs=specs,
        out_specs=[pl.BlockSpec((ctx_len, RET_DIM), lambda b, h: (b, h)),
                   pl.BlockSpec((lat_len, RET_DIM), lambda b, h: (b, h))],
        out_shape=[jax.ShapeDtypeStruct((batch * ctx_len, RET_HEADS * RET_DIM), BF16),
                   jax.ShapeDtypeStruct((batch * lat_len, RET_HEADS * RET_DIM), BF16)],
        scratch_shapes=[pltpu.VMEM((n_chunks, 2 * RET_DIM, RET_DIM), F32),
                        pltpu.VMEM((n_chunks, 2 * RET_DIM, RET_DIM), BF16)],
        compiler_params=_cparams(("arbitrary", "arbitrary")),
        name="retention",
    )(decay, zc, zc, zc, zc, zl, zl, zl, zl, ng)


GLA_SLAB = 256


def _gla_level_table():
    i = np.arange(GLA_CHUNK)[:, None]
    j = np.arange(GLA_CHUNK)[None, :]
    x = np.maximum(i ^ j, 1)
    lvl = np.floor(np.log2(x)).astype(np.int32)
    lvl = np.where(i == j, GLA_LEVELS, np.where(i > j, lvl, -1)).astype(np.int32)
    return np.concatenate([lvl, lvl.T], axis=1)


def _gla_role_table():
    t = np.arange(GLA_SLAB)
    fwd = np.arange(LANES)[None, :] < GLA_DK
    return np.stack([np.where(((t[:, None] >> lev) & 1).astype(bool) == fwd, LOG2_E, -LOG2_E)
                     for lev in range(GLA_LEVELS)]).astype(np.float32)


def _bcast_block_row(x, blk, r):
    rows = x.shape[0]
    if blk >= 8:
        x3 = x.reshape(rows // blk, blk, LANES)
        return jnp.broadcast_to(x3[:, r:r + 1, :], x3.shape).reshape(rows, LANES)
    pos = lax.broadcasted_iota(jnp.int32, x.shape, 0) & (blk - 1)
    out = x
    for p in range(blk):
        if p != r:
            out = jnp.where(pos == p, pltpu.roll(x, (p - r) % rows, 0), out)
    return out


def _gla_kernel(lvl_ref, role_ref, bdm_ref, up_ref, gb_ref, ng_ref,
                qc_ref, kc_ref, vc_ref, gc_ref, ac_ref, ql_ref, kl_ref, vl_ref, gl_ref, al_ref,
                oc_ref, ol_ref, *scratch, ctx_len, lat_len):
    C = GLA_CHUNK
    streams = ((qc_ref, kc_ref, vc_ref, ac_ref, 0, ctx_len), (ql_ref, kl_ref, vl_ref, al_ref, ctx_len, lat_len))
    nc_ctx, nc_lat = ctx_len // C, lat_len // C
    n_chunks = nc_ctx + nc_lat
    cpos = lax.broadcasted_iota(jnp.int32, (GLA_SLAB, LANES), 0) & (C - 1)
    fwd = lax.broadcasted_iota(jnp.int32, (GLA_SLAB, LANES), 1) < GLA_DK
    lvl = lvl_ref[...]

    passes = []
    for hh in range(2):
        own = fwd if hh == 0 else jnp.logical_not(fwd)
        xs_ref, qd_ref, xst_ref, qh_ref, tot_ref, vv_ref, kv_ref, st_ref = (r.at[hh] for r in scratch)

        def prep(i, between, q_ref, k_ref, v_ref, a_ref, off, hh=hh, own=own, xs_ref=xs_ref, qd_ref=qd_ref,
                 xst_ref=xst_ref, qh_ref=qh_ref, tot_ref=tot_ref, vv_ref=vv_ref, kv_ref=kv_ref):
            between = list(between)
            src = _rows(i * GLA_SLAB, GLA_SLAB, GLA_SLAB)
            dst = _rows(off + i * GLA_SLAB, GLA_SLAB, C)
            chunk0 = off // C + i * (GLA_SLAB // C)

            def store_keys(lev, x):
                y = jnp.concatenate([x[c * C:(c + 1) * C] for c in range(GLA_SLAB // C) for _ in range(2)], axis=0)
                yt = y.T.astype(BF16) * bdm_ref[...]
                for c in range(GLA_SLAB // C):
                    xst_ref[lev, chunk0 + c] = yt[:, c * 2 * C:(c + 1) * 2 * C]

            logit = _dot(a_ref[0, src, :], up_ref[hh]) + gb_ref[hh]
            la = (jnp.minimum(logit, 0.0) - jnp.log(1.0 + jnp.exp(-jnp.abs(logit)))) * (1.0 / GLA_TAU)
            pre = la
            for step in range(GLA_LEVELS):
                sh = 1 << step
                pre = pre + jnp.where(cpos >= sh, pltpu.roll(pre, sh, 0), 0.0)
            tot = _bcast_block_row(pre, C, C - 1)
            cum = jnp.where(fwd, pre, tot - pre + la)
            q2 = q_ref[0, src, :].astype(F32)
            k2 = k_ref[0, src, :].astype(F32)
            q = jnp.where(own, q2, pltpu.roll(q2, GLA_DK, 1))
            k = jnp.where(own, k2, pltpu.roll(k2, GLA_DK, 1))
            qd_ref[dst, :] = q.astype(BF16)
            store_keys(GLA_LEVELS, k)
            qh_ref[dst, :] = (q * jnp.exp(cum)).astype(BF16)
            kh = (k * jnp.exp(tot - cum)).astype(BF16)
            tot_ref[dst, :] = tot
            v = v_ref[hh, src, :]
            vv_ref[dst, :] = v
            for c in range(GLA_SLAB // C):
                kv_ref[chunk0 + c] = _dot_tn(v[c * C:(c + 1) * C], kh[c * C:(c + 1) * C])
            for lev in range(GLA_LEVELS):
                if between and lev % 2 == 0:
                    between.pop(0)()
                s = 1 << lev
                d = pre - _bcast_block_row(pre, 2 * s, s - 1)
                dist = jnp.where(fwd, d, la - d)
                role = role_ref[lev]
                x = jnp.where(role > 0, q, k) * jnp.exp2(dist * role)
                xs_ref[lev, dst, :] = x.astype(BF16)
                store_keys(lev, x)
            for item in between:
                item()

        def states(kv_ref=kv_ref, st_ref=st_ref, tot_ref=tot_ref):
            def step(c, s, lanes):
                st_ref[c, :, lanes] = s[:, lanes].astype(BF16)
                return s * jnp.exp(tot_ref[pl.ds(c * C, 1), :]) + kv_ref[c]

            def body(n, carry):
                s_f, s_b = carry
                c_b = jnp.where(n < nc_ctx, nc_ctx - 1 - n, n_chunks + nc_ctx - 1 - n)
                return step(n, s_f, slice(0, GLA_DK)), step(c_b, s_b, slice(GLA_DK, LANES))

            zero = jnp.zeros((GLA_DV, LANES), F32)
            lax.fori_loop(0, n_chunks, body, (zero, zero))

        def out_chunk(c, g_ref, o_ref, local, hh=hh, xs_ref=xs_ref, qd_ref=qd_ref, xst_ref=xst_ref, qh_ref=qh_ref,
                      vv_ref=vv_ref, st_ref=st_ref):
            rows = _rows(c * C, C, C)
            a = jnp.where(lvl == GLA_LEVELS, _dot(qd_ref[rows, :], xst_ref[GLA_LEVELS, c]), 0.0)
            for lev in range(GLA_LEVELS):
                a = jnp.where(lvl == lev, _dot(xs_ref[lev, rows, :], xst_ref[lev, c]), a)
            v = vv_ref[rows, :]
            o = _dot(a.astype(BF16), jnp.concatenate([v, v], axis=0)) + _dot_nt(qh_ref[rows, :], st_ref[c])
            y = _rms(o) * ng_ref[...]
            out_rows = _rows(local * C, C, C)
            y = y * _silu(g_ref[hh, out_rows, :].astype(F32))
            o_ref[out_rows, hh * GLA_DV:(hh + 1) * GLA_DV] = y.astype(BF16)

        passes.append((prep, states, out_chunk))

    per_slab = GLA_SLAB // C
    (prep0, states0, out0), (prep1, states1, out1) = passes

    def slab(prep, out, stream, i, g_ref, o_ref):
        q_ref, k_ref, v_ref, a_ref, off, _ = stream
        chunks = []
        if out is not None:
            chunks = [functools.partial(out, off // C + i * per_slab + j, g_ref, o_ref, i * per_slab + j)
                      for j in range(per_slab)]
        if prep is not None:
            prep(i, chunks, q_ref=q_ref, k_ref=k_ref, v_ref=v_ref, a_ref=a_ref, off=off)
        else:
            for item in chunks:
                item()

    def run(prep, out):
        for stream, g_ref, o_ref in ((streams[0], gc_ref, oc_ref), (streams[1], gl_ref, ol_ref)):
            n_slabs = stream[5] // GLA_SLAB
            if n_slabs == 1:
                slab(prep, out, stream, 0, g_ref, o_ref)
            else:
                def body(i, carry, stream=stream, g_ref=g_ref, o_ref=o_ref):
                    slab(prep, out, stream, i, g_ref, o_ref)
                    return carry
                lax.fori_loop(0, n_slabs, body, 0, unroll=2 if n_slabs % 2 == 0 else 1)

    run(prep0, None)
    states0()
    run(prep1, out0)
    states1()
    run(None, out1)


def _gla_gate_params(gate_up, gate_b):
    r = GLA_GATE_RANK
    gu = gate_up.astype(BF16).reshape(2, r, GLA_HEADS, GLA_DK).transpose(2, 0, 1, 3)
    up = jnp.zeros((GLA_HEADS, LANES, 2 * GLA_DK), BF16)
    up = up.at[:, 0:r, 0:GLA_DK].set(gu[:, 0]).at[:, r:2 * r, GLA_DK:].set(gu[:, 1])
    gb = gate_b.reshape(2, GLA_HEADS, GLA_DK).transpose(1, 0, 2).reshape(GLA_HEADS, 1, 2 * GLA_DK)
    return up, gb


def _gla(zc, zl, up_heads, gate_b_heads, ng, *, batch, ctx_len, lat_len):
    total = ctx_len + lat_len
    n_chunks = total // GLA_CHUNK

    def pair(base, n):
        return pl.BlockSpec((1, n, LANES), lambda b, p: (base + p, b, 0))

    def two(base, n):
        return pl.BlockSpec((2, n, LANES), lambda b, p: (base // 2 + p, b, 0))

    def one(n):
        return pl.BlockSpec((1, n, LANES), lambda b, p: (SLOT_GA, b, 0))

    role = _gla_role_table()
    feat_fwd = np.arange(LANES)[:, None] < GLA_DK
    key_fwd = (np.arange(2 * GLA_SLAB)[None, :] % (2 * GLA_CHUNK)) < GLA_CHUNK
    bd_mask = (feat_fwd == key_fwd).astype(np.float32)
    specs = [
        pl.BlockSpec((GLA_CHUNK, 2 * GLA_CHUNK), lambda b, p: (0, 0)),
        pl.BlockSpec(role.shape, lambda b, p: (0, 0, 0)),
        pl.BlockSpec(bd_mask.shape, lambda b, p: (0, 0)),
        pl.BlockSpec((2, LANES, LANES), lambda b, p: (p, 0, 0)),
        pl.BlockSpec((2, 1, LANES), lambda b, p: (p, 0, 0)),
        pl.BlockSpec((1, GLA_DV), lambda b, p: (0, 0)),
    ]
    for n in (ctx_len, lat_len):
        specs += [pair(SLOT_GQ, n), pair(SLOT_GK, n), two(SLOT_GV, n), two(SLOT_GR, n), one(n)]
    return pl.pallas_call(
        functools.partial(_gla_kernel, ctx_len=ctx_len, lat_len=lat_len),
        grid=(batch, GLA_HEADS // 2),
        in_specs=specs,
        out_specs=[pl.BlockSpec((ctx_len, 2 * GLA_DV), lambda b, p: (b, p)),
                   pl.BlockSpec((lat_len, 2 * GLA_DV), lambda b, p: (b, p))],
        out_shape=[jax.ShapeDtypeStruct((batch * ctx_len, GLA_HEADS * GLA_DV), BF16),
                   jax.ShapeDtypeStruct((batch * lat_len, GLA_HEADS * GLA_DV), BF16)],
        scratch_shapes=[
            pltpu.VMEM((2, GLA_LEVELS, total, LANES), BF16),
            pltpu.VMEM((2, total, LANES), BF16),
            pltpu.VMEM((2, GLA_LEVELS + 1, n_chunks, LANES, 2 * GLA_CHUNK), BF16),
            pltpu.VMEM((2, total, LANES), BF16),
            pltpu.VMEM((2, total, LANES), F32),
            pltpu.VMEM((2, total, GLA_DV), BF16),
            pltpu.VMEM((2, n_chunks, GLA_DV, LANES), F32),
            pltpu.VMEM((2, n_chunks, GLA_DV, LANES), BF16),
        ],
        compiler_params=_cparams(("arbitrary", "arbitrary")),
        name="gla",
    )(jnp.asarray(_gla_level_table()), jnp.asarray(role), jnp.asarray(bd_mask, BF16), up_heads, gate_b_heads, ng,
      zc, zc, zc, zc, zc, zl, zl, zl, zl, zl)


def _outproj_kernel(att_ref, ret_ref, gla_ref, w_ref, x_ref, gate_ref, sh_ref, sc_ref, g_ref, xo_ref, ho_ref):
    na, nr = att_ref.shape[1], ret_ref.shape[1]
    y = (_dot(att_ref[...], w_ref[0, 0:na, :]) + _dot(ret_ref[...], w_ref[0, na:na + nr, :])
         + _dot(gla_ref[...], w_ref[0, na + nr:, :]))
    xn = x_ref[...] + gate_ref[0] * y
    xo_ref[...] = xn
    ho_ref[...] = (_rms(xn) * g_ref[...] * (1.0 + sc_ref[0]) + sh_ref[0]).astype(BF16)


def _outproj(att, ret, gla, w, layer, x2, gate, shift, scale, g, *, tm, rows_per_mod):
    m, d = x2.shape

    def rows(n):
        return pl.BlockSpec((tm, n), lambda i: (i, 0))

    def mod():
        return pl.BlockSpec((1, 1, d), lambda i: ((i * tm) // rows_per_mod, 0, 0))

    return pl.pallas_call(
        _outproj_kernel,
        grid=(m // tm,),
        in_specs=[rows(att.shape[1]), rows(ret.shape[1]), rows(gla.shape[1]),
                  pl.BlockSpec((1,) + w.shape[1:], lambda i: (layer, 0, 0), pipeline_mode=pl.Buffered(1)),
                  rows(d), mod(), mod(), mod(),
                  pl.BlockSpec((1, d), lambda i: (0, 0))],
        out_specs=[rows(d), rows(d)],
        out_shape=[jax.ShapeDtypeStruct((m, d), F32), jax.ShapeDtypeStruct((m, d), BF16)],
        compiler_params=_cparams(("arbitrary",)),
        name="out_proj",
    )(att, ret, gla, w, x2, gate, shift, scale, g)


FFN_HALO = 16
FFN_TF = 512
FFN_RING = 4
FFN_VMEM_LIMIT = 58 * 1024 * 1024
FFN_TM = 512


def _ffn_kernel(h_ref, hn_ref, hp_ref, wup_hbm, cw_ref, cb_ref, wdn_hbm, x_ref, gate_ref, fg_ref,
                o_ref, hs_ref, wa_buf, wv_buf, wd_buf, sem, *, tm, seq_len, final_norm):
    i = pl.program_id(0)
    f = pl.program_id(1)
    nf = pl.num_programs(1)
    tf = wa_buf.shape[2]
    half = FFN_HALO // 2
    ext = tm + FFN_HALO
    step = i * nf + f
    n_steps = pl.num_programs(0) * nf

    def tile_copies(s):
        ft = s % nf
        slot = s % FFN_RING
        return (pltpu.make_async_copy(wup_hbm.at[ft], wa_buf.at[slot], sem.at[slot, 0]),
                pltpu.make_async_copy(wup_hbm.at[nf + ft], wv_buf.at[slot], sem.at[slot, 1]),
                pltpu.make_async_copy(wdn_hbm.at[0, pl.ds(ft * tf, tf)], wd_buf.at[slot], sem.at[slot, 2]))

    def start(s):
        for cp in tile_copies(s):
            cp.start()

    @pl.when(step == 0)
    def _():
        for s in range(FFN_RING - 1):
            start(s)

    @pl.when(step + FFN_RING - 1 < n_steps)
    def _():
        start(step + FFN_RING - 1)

    for cp in tile_copies(step):
        cp.wait()
    slot = step % FFN_RING

    @pl.when(f == 0)
    def _():
        hs_ref[0:tm, :] = h_ref[...]
        hs_ref[tm:tm + half, :] = hn_ref[0:half, :]
        hs_ref[tm + half:ext, :] = hp_ref[half:, :]
        o_ref[...] = jnp.zeros_like(o_ref)

    a = _dot(hs_ref[...], wa_buf[slot])
    pos = (i * tm + lax.broadcasted_iota(jnp.int32, (ext, 1), 0)) % seq_len
    prev = jnp.where(pos == 0, 0.0, pltpu.roll(a, 1, 0))
    nxt = jnp.where(pos == seq_len - 1, 0.0, pltpu.roll(a, ext - 1, 0))
    conv = prev * cw_ref[0:1, :] + a * cw_ref[1:2, :] + nxt * cw_ref[2:3, :] + cb_ref[...]
    v = _dot(h_ref[...], wv_buf[slot])
    u = (_silu(conv[0:tm]) * v).astype(BF16)
    o_ref[...] += _dot(u, wd_buf[slot])

    @pl.when(f == pl.num_programs(1) - 1)
    def _():
        xn = x_ref[...] + gate_ref[0] * o_ref[...]
        if final_norm:
            xn = _rms(xn) * fg_ref[...]
        o_ref[...] = xn


def _ffn(h2, w_up, conv_w, conv_b, w_down, x2, gate, fg, *, tm, seq_len, rows_per_mod, final_norm):
    m, d = x2.shape
    tf = w_up.shape[2]
    nf = D_FF // tf
    hb = tm // FFN_HALO
    last = m // FFN_HALO - 1
    return pl.pallas_call(
        functools.partial(_ffn_kernel, tm=tm, seq_len=seq_len, final_norm=final_norm),
        grid=(m // tm, nf),
        in_specs=[
            pl.BlockSpec((tm, d), lambda i, f: (i, 0)),
            pl.BlockSpec((FFN_HALO, d), lambda i, f: (jnp.minimum((i + 1) * hb, last), 0)),
            pl.BlockSpec((FFN_HALO, d), lambda i, f: (jnp.maximum(i * hb - 1, 0), 0)),
            pl.BlockSpec(memory_space=pl.ANY),
            pl.BlockSpec((3, tf), lambda i, f: (0, f)),
            pl.BlockSpec((1, tf), lambda i, f: (0, f)),
            pl.BlockSpec(memory_space=pl.ANY),
            pl.BlockSpec((tm, d), lambda i, f: (i, 0)),
            pl.BlockSpec((1, 1, d), lambda i, f: ((i * tm) // rows_per_mod, 0, 0)),
            pl.BlockSpec((1, d), lambda i, f: (0, 0)),
        ],
        out_specs=pl.BlockSpec((tm, d), lambda i, f: (i, 0)),
        out_shape=jax.ShapeDtypeStruct((m, d), F32),
        scratch_shapes=[pltpu.VMEM((tm + FFN_HALO, d), BF16),
                        pltpu.VMEM((FFN_RING, d, tf), BF16),
                        pltpu.VMEM((FFN_RING, d, tf), BF16),
                        pltpu.VMEM((FFN_RING, tf, d), BF16),
                        pltpu.SemaphoreType.DMA((FFN_RING, 3))],
        compiler_params=_cparams(("arbitrary", "arbitrary"), FFN_VMEM_LIMIT),
        name="conv_glu",
    )(h2, h2, h2, w_up, conv_w, conv_b, w_down, x2, gate, fg)


def _rope_tables(n_tokens):
    rows = n_tokens // GRID_W
    row = jnp.repeat(jnp.arange(rows, dtype=F32), GRID_W)
    col = jnp.tile(jnp.arange(GRID_W, dtype=F32), rows)
    n_freq = HEAD_DIM // 4
    inv_freq = ROPE_THETA ** (-jnp.arange(n_freq, dtype=F32) / n_freq)
    ang = jnp.concatenate([row[:, None] * inv_freq, col[:, None] * inv_freq], axis=-1)
    cos, sin = jnp.cos(ang), jnp.sin(ang)
    return jnp.concatenate([cos, cos], axis=-1), jnp.concatenate([-sin, sin], axis=-1)


def kernel(x, c, ctx, c_ctx, ada_w, ada_b, norm1_g, w_in, q_norm_g, k_norm_g, ret_log_decay, ret_norm_g,
           gla_gate_up, gla_gate_b, gla_norm_g, w_out, norm2_g, w_up, conv_w, conv_b, w_down, final_norm_g):
    batch, lat_len, d = x.shape
    ctx_len = ctx.shape[1]
    depth = ada_w.shape[0]
    mod_rows = 16
    cc = jnp.concatenate([c, c_ctx[None], jnp.zeros((mod_rows - batch - 1, d), F32)], axis=0)
    mod = _ada(cc, ada_w, ada_b).reshape(depth, mod_rows, N_MOD, d)

    cos_l, sin_l = _rope_tables(lat_len)
    proj_tm = PROJ_TM
    cos_c = jnp.ones((proj_tm, LANES), F32)
    sin_c = jnp.zeros((proj_tm, LANES), F32)

    xl = x.reshape(batch * lat_len, d)
    xc = ctx.reshape(batch * ctx_len, d)
    row = lambda v: v.reshape(1, -1)
    w_in_b = w_in.astype(BF16)
    w_out_b = w_out[0].astype(BF16)[None]

    for l in range(depth):
        last = l == depth - 1
        ml = [mod[l, :batch, k].reshape(batch, 1, d) for k in range(N_MOD)]
        mc = [mod[l, batch, k].reshape(1, 1, d) for k in range(N_MOD)]
        up_heads, gate_b_heads = _gla_gate_params(gla_gate_up[l], gla_gate_b[l])

        zl, vtl = _proj(xl, ml[0], ml[1], row(norm1_g[l]), w_in_b, l, cos_l, sin_l, row(q_norm_g[l]), row(k_norm_g[l]),
                        tm=proj_tm, rows_per_mod=lat_len, rope_tiles=lat_len // proj_tm)
        zc, vtc = _proj(xc, mc[0], mc[1], row(norm1_g[l]), w_in_b, l, cos_c, sin_c, row(q_norm_g[l]), row(k_norm_g[l]),
                        tm=proj_tm, rows_per_mod=batch * ctx_len, rope_tiles=1)

        casts = [(w_up, l, FFN_TF), (w_down, l, None)]
        if not last:
            casts += [(w_out, l + 1, None)]
        att_l, cast = _attn(vtl, [(zc, vtc, ctx_len), (zl, vtl, lat_len)], batch=batch, q_len=lat_len, tq=ATT_TQ,
                            casts=casts)
        w_up_b, w_down_b = cast[0], cast[1][None]
        ret_c, ret_l = _ret(zc, zl, ret_log_decay[l], row(ret_norm_g[l]),
                            batch=batch, ctx_len=ctx_len, lat_len=lat_len)
        gla_c, gla_l = _gla(zc, zl, up_heads, gate_b_heads, row(gla_norm_g[l]),
                            batch=batch, ctx_len=ctx_len, lat_len=lat_len)

        xl, h2 = _outproj(att_l, ret_l, gla_l, w_out_b, 0, xl, ml[2], ml[3], ml[4], row(norm2_g[l]),
                          tm=OUTPROJ_TM, rows_per_mod=lat_len)
        xl = _ffn(h2, w_up_b, conv_w[l], row(conv_b[l]), w_down_b, xl, ml[5], row(final_norm_g),
                  tm=FFN_TM, seq_len=lat_len, rows_per_mod=lat_len, final_norm=last)

        if not last:
            att_c, _ = _attn(vtc, [(zc, vtc, ctx_len)], batch=batch, q_len=ctx_len, tq=ctx_len)
            xc, hc2 = _outproj(att_c, ret_c, gla_c, w_out_b, 0, xc, mc[2], mc[3], mc[4], row(norm2_g[l]),
                               tm=OUTPROJ_TM, rows_per_mod=batch * ctx_len)
            xc = _ffn(hc2, w_up_b, conv_w[l], row(conv_b[l]), w_down_b, xc, mc[5], row(final_norm_g),
                      tm=FFN_TM, seq_len=ctx_len, rows_per_mod=batch * ctx_len, final_norm=False)
            w_out_b = cast[2][None]

    return xl.reshape(batch, lat_len, d)
```
